```python
import jax, jax.numpy as jnp
from jax import lax
import numpy as np

D_MODEL = 1024
BATCH = 2
SEQ = 8192
DEPTH = 1

CHUNK = 64
QBLK = 128
EPS = 1e-6

REC_HEADS = 4
REC_DK = 128
REC_DV = 128
REC_W = REC_HEADS * REC_DV
ATT_HEADS = 8
ATT_DH = 64
ATT_W = ATT_HEADS * ATT_DH
IDX_HEADS = 8
IDX_DIM = 64
TOPK_MAX = 256
NUM_BUCKETS = 32
MAX_DISTANCE = 128
N_GROUPS = 4
EXPERTS_PER_GROUP = 4
N_EXPERTS = N_GROUPS * EXPERTS_PER_GROUP
TOP_K_EXPERTS = 2
D_EXPERT = 512

MIX_W = REC_W + ATT_W
IN_SIZES = (REC_HEADS * REC_DK,
            REC_HEADS * REC_DK,
            REC_W,
            REC_W,
            ATT_W, ATT_W, ATT_W,
            IDX_HEADS * IDX_DIM,
            IDX_DIM,
            IDX_HEADS)
D_IN = int(sum(IN_SIZES))
IN_OFFSETS = tuple(int(o) for o in np.cumsum(IN_SIZES)[:-1])

kernel_name = 'hymba_hgrn2_dsa_hmoe_block'


def _rms(x, g):
    xf = x.astype(jnp.float32)
    y = xf * lax.rsqrt(jnp.mean(xf * xf, axis=-1, keepdims=True) + EPS)
    return (y * g.astype(jnp.float32)).astype(x.dtype)


def _layernorm(x, g, b):
    xf = x.astype(jnp.float32)
    mu = jnp.mean(xf, axis=-1, keepdims=True)
    var = jnp.mean(jnp.square(xf - mu), axis=-1, keepdims=True)
    y = (xf - mu) * lax.rsqrt(var + EPS)
    return (y * g.astype(jnp.float32) + b.astype(jnp.float32)).astype(x.dtype)


def _t5_bucket(rel):
    nb = NUM_BUCKETS // 2
    max_exact = nb // 2
    ret = jnp.where(rel > 0, nb, 0)
    n = jnp.abs(rel)
    nf = jnp.maximum(n, 1).astype(jnp.float32)
    large = max_exact + (jnp.log(nf / max_exact) / np.log(MAX_DISTANCE / max_exact)
                         * (nb - max_exact)).astype(jnp.int32)
    large = jnp.minimum(large, nb - 1)
    return ret + jnp.where(n < max_exact, n, large)


def _hgrn2(q_raw, f_raw, i_raw, g_raw, lb, out_g):
    B, S, _ = q_raw.shape
    nc = S // CHUNK
    f32 = jnp.float32
    q = jax.nn.silu(q_raw.astype(f32))
    f = lb + (1.0 - lb) * jax.nn.sigmoid(f_raw.astype(f32))
    k = 1.0 - f
    lf = jnp.log(f)
    v = i_raw.astype(f32)

    def to_chunks(a, d):
        return a.reshape(B, nc, CHUNK, REC_HEADS, d).transpose(1, 0, 3, 2, 4)

    tril = jnp.tril(jnp.ones((CHUNK, CHUNK), bool))[:, :, None]

    def step(state, inp):
        qc, kc, vc, lfc = inp
        b = jnp.cumsum(lfc, axis=2)
        diff = b[:, :, :, None, :] - b[:, :, None, :, :]
        decay = jnp.where(tril, jnp.exp(jnp.where(tril, diff, 0.0)), 0.0)
        attn = jnp.einsum('bhtk,bhsk,bhtsk->bhts', qc, kc, decay)
        o = (jnp.einsum('bhts,bhsv->bhtv', attn, vc)
             + jnp.einsum('bhtk,bhkv->bhtv', qc * jnp.exp(b), state))
        b_end = b[:, :, -1:, :]
        state = (jnp.exp(b_end[:, :, 0, :])[..., None] * state
                 + jnp.einsum('bhsk,bhsv->bhkv', kc * jnp.exp(b_end - b), vc))
        return state, o

    s0 = jnp.zeros((B, REC_HEADS, REC_DK, REC_DV), f32)
    _, o = lax.scan(step, s0, (to_chunks(q, REC_DK), to_chunks(k, REC_DK),
                               to_chunks(v, REC_DV), to_chunks(lf, REC_DK)))
    o = o.transpose(1, 0, 3, 2, 4).reshape(B, S, REC_HEADS, REC_DV)
    o = _rms(o, out_g.reshape(REC_HEADS, REC_DV))
    gate = jax.nn.silu(g_raw.astype(f32)).reshape(B, S, REC_HEADS, REC_DV)
    return (o * gate).reshape(B, S, REC_W).astype(q_raw.dtype)


def _dsa_attention(q, k, v, iq, ik, iw, rel_bias):
    B, S = q.shape[0], q.shape[1]
    nb = S // QBLK
    k_sel = min(TOPK_MAX, S // 4)
    key_chunk = jnp.arange(S) // CHUNK
    ikf = ik.astype(jnp.float32)

    def blocks(a):
        return a.reshape((B, nb, QBLK) + a.shape[2:]).swapaxes(0, 1)

    def one_block(args):
        bi, qb, iqb, iwb = args
        t = bi * QBLK + jnp.arange(QBLK)
        t_chunk = t // CHUNK
        sc = jnp.einsum('bqhd,bsd->bqhs', iqb.astype(jnp.float32), ikf) * IDX_DIM ** -0.5
        score = jnp.einsum('bqhs,bqh->bqs', jax.nn.relu(sc), iwb.astype(jnp.float32))
        adm = key_chunk[None, :] <= t_chunk[:, None]
        score = jnp.where(adm[None], score, -jnp.inf)
        _, idx = lax.top_k(score, k_sel)
        valid = (idx // CHUNK) <= t_chunk[None, :, None]
        kg = jax.vmap(lambda a, j: a[j])(k, idx)
        vg = jax.vmap(lambda a, j: a[j])(v, idx)
        logits = jnp.einsum('bqhd,bqkhd->bqhk', qb, kg).astype(jnp.float32) * ATT_DH ** -0.5
        bias = rel_bias[_t5_bucket(idx - t[None, :, None])]
        logits = logits + jnp.swapaxes(bias, 2, 3).astype(jnp.float32)
        logits = jnp.where(valid[:, :, None, :], logits, -jnp.inf)
        p = jax.nn.softmax(logits, axis=-1).astype(vg.dtype)
        return jnp.einsum('bqhk,bqkhd->bqhd', p, vg)

    out = lax.map(one_block, (jnp.arange(nb), blocks(q), blocks(iq), blocks(iw)))
    return out.swapaxes(0, 1).reshape(B, S, ATT_HEADS, ATT_DH)


def _hier_moe(h, w_rg, b_rg, w_re, b_re, w1, w3, w2):
    B, S, D = h.shape
    hf = h.reshape(B * S, D)
    g_logits = (hf @ w_rg + b_rg).astype(jnp.float32)
    p_group = jax.nn.softmax(g_logits, axis=-1)
    g_top = jnp.argmax(g_logits, axis=-1)
    gate_g = jnp.take_along_axis(p_group, g_top[:, None], axis=-1)
    e_logits = (hf @ w_re + b_re).astype(jnp.float32).reshape(-1, N_GROUPS, EXPERTS_PER_GROUP)
    e_in = jnp.take_along_axis(e_logits, g_top[:, None, None], axis=1)[:, 0]
    top_v, top_j = lax.top_k(e_in, TOP_K_EXPERTS)
    w_sel = jax.nn.softmax(top_v, axis=-1) * gate_g
    e_id = g_top[:, None] * EXPERTS_PER_GROUP + top_j
    combine = jnp.einsum('nk,nke->ne', w_sel,
                         jax.nn.one_hot(e_id, N_EXPERTS, dtype=jnp.float32)).astype(h.dtype)
    y = jnp.zeros_like(hf)
    for e in range(N_EXPERTS):
        he = jax.nn.silu(hf @ w1[e]) * (hf @ w3[e])
        y = y + combine[:, e:e + 1] * (he @ w2[e])
    return y.reshape(B, S, D)


def setup_inputs(seed: int = 0) -> dict:
    key = jax.random.key(seed)
    ks = jax.random.split(key, 24)
    f32 = jnp.float32

    def nrm(k, shape, s):
        return jax.random.normal(k, shape, f32) * s

    D = D_MODEL
    return {
        'x': nrm(ks[0], (BATCH, SEQ, D), 1.0),
        'c': nrm(ks[1], (BATCH, D), 1.0),
        'w_ada': nrm(ks[2], (DEPTH, D, 6 * D), 0.5 * D ** -0.5),
        'b_ada': nrm(ks[3], (DEPTH, 6 * D), 0.02),
        'norm1_g': 1.0 + nrm(ks[4], (DEPTH, D), 0.02),
        'norm2_g': 1.0 + nrm(ks[5], (DEPTH, D), 0.02),
        'w_in': nrm(ks[6], (DEPTH, D, D_IN), D ** -0.5),
        'lb_logits': nrm(ks[7], (DEPTH + 1, REC_HEADS * REC_DK), 0.5),
        'rec_out_g': 1.0 + nrm(ks[8], (DEPTH, REC_W), 0.02),
        'q_norm_g': 1.0 + nrm(ks[9], (DEPTH, ATT_DH), 0.02),
        'k_norm_g': 1.0 + nrm(ks[10], (DEPTH, ATT_DH), 0.02),
        'idx_k_norm_g': 1.0 + nrm(ks[11], (DEPTH, IDX_DIM), 0.02),
        'idx_k_norm_b': nrm(ks[12], (DEPTH, IDX_DIM), 0.02),
        'attn_out_g': 1.0 + nrm(ks[13], (DEPTH, ATT_W), 0.02),
        'rel_bias': nrm(ks[14], (NUM_BUCKETS, ATT_HEADS), 0.5),
        'w_out': nrm(ks[15], (DEPTH, MIX_W, D), MIX_W ** -0.5),
        'w_rg': nrm(ks[16], (DEPTH, D, N_GROUPS), D ** -0.5),
        'b_rg': nrm(ks[17], (DEPTH, N_GROUPS), 0.01),
        'w_re': nrm(ks[18], (DEPTH, D, N_EXPERTS), D ** -0.5),
        'b_re': nrm(ks[19], (DEPTH, N_EXPERTS), 0.01),
        'w1': nrm(ks[20], (DEPTH, N_EXPERTS, D, D_EXPERT), D ** -0.5),
        'w3': nrm(ks[21], (DEPTH, N_EXPERTS, D, D_EXPERT), D ** -0.5),
        'w2': nrm(ks[22], (DEPTH, N_EXPERTS, D_EXPERT, D), D_EXPERT ** -0.5),
    }


def reference(x, c, w_ada, b_ada, norm1_g, norm2_g, w_in, lb_logits, rec_out_g,
              q_norm_g, k_norm_g, idx_k_norm_g, idx_k_norm_b, attn_out_g, rel_bias,
              w_out, w_rg, b_rg, w_re, b_re, w1, w3, w2):
    B, S, D = x.shape
    lb_all = jnp.cumsum(jax.nn.softmax(lb_logits.astype(jnp.float32), axis=0), axis=0)
    for l in range(DEPTH):
        mod = jax.nn.silu(c) @ w_ada[l] + b_ada[l]
        sh1, sc1, g1, sh2, sc2, g2 = jnp.split(mod, 6, axis=-1)

        h = _rms(x, norm1_g[l]) * (1.0 + sc1[:, None, :]) + sh1[:, None, :]
        z = h @ w_in[l]
        rq, rf, ri, rg, aq, ak, av, iq, ik, iw = jnp.split(z, IN_OFFSETS, axis=-1)

        rec = _hgrn2(rq, rf, ri, rg, lb_all[l], rec_out_g[l])

        aq = _rms(aq.reshape(B, S, ATT_HEADS, ATT_DH), q_norm_g[l])
        ak = _rms(ak.reshape(B, S, ATT_HEADS, ATT_DH), k_norm_g[l])
        av = av.reshape(B, S, ATT_HEADS, ATT_DH)
        iq = iq.reshape(B, S, IDX_HEADS, IDX_DIM)
        ik = _layernorm(ik, idx_k_norm_g[l], idx_k_norm_b[l])
        iw = iw * IDX_HEADS ** -0.5
        att = _dsa_attention(aq, ak, av, iq, ik, iw, rel_bias)
        att = _rms(att, attn_out_g[l].reshape(ATT_HEADS, ATT_DH)).reshape(B, S, ATT_W)

        mix = jnp.concatenate([rec, att.astype(rec.dtype)], axis=-1) @ w_out[l]
        x = x + g1[:, None, :] * mix

        h2 = _rms(x, norm2_g[l]) * (1.0 + sc2[:, None, :]) + sh2[:, None, :]
        x = x + g2[:, None, :] * _hier_moe(h2, w_rg[l], b_rg[l], w_re[l], b_re[l],
                                           w1[l], w3[l], w2[l])
    return x
```

```python
import functools

import numpy as np
import jax
import jax.numpy as jnp
from jax import lax
from jax.experimental import pallas as pl
from jax.experimental.pallas import tpu as pltpu

F32 = jnp.float32
BF16 = jnp.bfloat16
I32 = jnp.int32

CHUNK = 64
QBLK = 128
EPS = 1e-6
REC_HEADS = 4
REC_DK = 128
REC_DV = 128
REC_W = REC_HEADS * REC_DV
ATT_HEADS = 8
ATT_DH = 64
ATT_W = ATT_HEADS * ATT_DH
IDX_HEADS = 8
IDX_DIM = 64
TOPK_MAX = 256
NUM_BUCKETS = 32
MAX_DISTANCE = 128
N_GROUPS = 4
EXPERTS_PER_GROUP = 4
N_EXPERTS = N_GROUPS * EXPERTS_PER_GROUP
D_EXPERT = 512

LANES = 128
V7X_VMEM_LIMIT = 56 * 1024 * 1024
INT_MIN = -(2 ** 31)
NEG = -1e30
N_PAIRS = ATT_HEADS // 2
ROUTE_OFF = N_GROUPS


def _dot(a, b):
    return jnp.dot(a, b, preferred_element_type=F32)


def _dot_nt(a, b):
    return lax.dot_general(a, b, (((1,), (1,)), ((), ())), preferred_element_type=F32)


def _dot_tn(a, b):
    return lax.dot_general(a, b, (((0,), (0,)), ((), ())), preferred_element_type=F32)


def _split_bf16(a):
    hi = a.astype(BF16)
    lo = (a - hi.astype(F32)).astype(BF16)
    return hi, lo


def _silu(a):
    return a * jax.nn.sigmoid(a)


def _const_spec(shape):
    nd = len(shape)
    return pl.BlockSpec(shape, lambda *_: (0,) * nd, pipeline_mode=pl.Buffered(1))


def _params(sem):
    return pltpu.CompilerParams(dimension_semantics=sem, vmem_limit_bytes=V7X_VMEM_LIMIT)


def _adaln_kernel(c_ref, w_ref, b_ref, o_ref):
    a_hi, a_lo = _split_bf16(_silu(c_ref[...]))
    w_hi, w_lo = _split_bf16(w_ref[...])
    o_ref[...] = _dot(a_hi, w_hi) + _dot(a_lo, w_hi) + _dot(a_hi, w_lo) + b_ref[...]


def _adaln(c, w, b):
    bsz, d = c.shape
    n = w.shape[1]
    rows = 16
    bn = 1024
    cp = jnp.zeros((rows, d), F32).at[:bsz].set(c)
    out = pl.pallas_call(
        _adaln_kernel,
        grid=(n // bn,),
        in_specs=[pl.BlockSpec((rows, d), lambda i: (0, 0)),
                  pl.BlockSpec((d, bn), lambda i: (0, i)),
                  pl.BlockSpec((1, bn), lambda i: (0, i))],
        out_specs=pl.BlockSpec((rows, bn), lambda i: (0, i)),
        out_shape=jax.ShapeDtypeStruct((rows, n), F32),
        compiler_params=_params(("parallel",)),
        name="adaln",
    )(cp, w, b.reshape(1, n))
    return out[:bsz]


def _inproj_kernel(x_ref, sc_ref, sh_ref, ng_ref, wrec_ref, watt_ref, widx_ref, lbl_ref,
                   qg_ref, kg_ref, ikg_ref, ikb_ref, pm_ref,
                   q_ref, f_ref, v_ref, g_ref, aq_ref, ak_ref, avt_ref, iq_ref, ik_ref, iwt_ref,
                   *, layer, gt):
    x = x_ref[...]
    tm = x.shape[0]
    ms = jnp.mean(x * x, axis=-1, keepdims=True)
    h = x * lax.rsqrt(ms + EPS) * ng_ref[...] * (1.0 + sc_ref[...]) + sh_ref[...]
    hb = h.astype(BF16)

    zr = _dot(hb, wrec_ref[...])
    q_ref[...] = _silu(zr[:, 0:REC_W]).astype(BF16)
    lbl = lbl_ref[...]
    e = jnp.exp(lbl - jnp.max(lbl, axis=0, keepdims=True))
    sm = e / jnp.sum(e, axis=0, keepdims=True)
    lb = jnp.sum(sm[0:layer + 1], axis=0, keepdims=True)
    f_ref[...] = lb + (1.0 - lb) * jax.nn.sigmoid(zr[:, REC_W:2 * REC_W])
    v_ref[...] = zr[:, 2 * REC_W:3 * REC_W].astype(BF16)
    g_ref[...] = _silu(zr[:, 3 * REC_W:4 * REC_W]).astype(BF16)

    za = _dot(hb, watt_ref[...])
    aq = za[:, 0:ATT_W]
    ak = za[:, ATT_W:2 * ATT_W]
    av = za[:, 2 * ATT_W:3 * ATT_W]
    pm = pm_ref[...]
    aqn = aq * lax.rsqrt(_dot((aq * aq).astype(BF16), pm) + EPS) * qg_ref[...]
    akn = ak * lax.rsqrt(_dot((ak * ak).astype(BF16), pm) + EPS) * kg_ref[...]
    for j in range(N_PAIRS):
        aq_ref[j] = aqn[:, j * LANES:(j + 1) * LANES].astype(BF16)
        ak_ref[j] = akn[:, j * LANES:(j + 1) * LANES].astype(BF16)
    for t in range(tm // gt):
        avt_ref[t] = av[t * gt:(t + 1) * gt, :].T.astype(BF16)

    zi = _dot(hb, widx_ref[...])
    for j in range(N_PAIRS):
        iq_ref[j] = zi[:, j * LANES:(j + 1) * LANES].astype(BF16)
    tail = zi[:, IDX_HEADS * IDX_DIM:IDX_HEADS * IDX_DIM + LANES]
    lane = lax.broadcasted_iota(I32, tail.shape, 1)
    is_k = lane < IDX_DIM
    mu = jnp.sum(jnp.where(is_k, tail, 0.0), axis=-1, keepdims=True) * (1.0 / IDX_DIM)
    dlt = jnp.where(is_k, tail - mu, 0.0)
    var = jnp.sum(dlt * dlt, axis=-1, keepdims=True) * (1.0 / IDX_DIM)
    ikn = dlt * lax.rsqrt(var + EPS) * ikg_ref[...] + ikb_ref[...]
    ik_ref[...] = jnp.where(is_k, ikn, pltpu.roll(ikn, IDX_DIM, axis=1)).astype(BF16)
    iwt_ref[...] = tail.T[IDX_DIM:IDX_DIM + IDX_HEADS, :] * (IDX_HEADS ** -0.5 * IDX_DIM ** -0.5)


def _inproj(x, sc1, sh1, norm_g, w_in, lb_logits, q_g, k_g, ik_g, ik_b, *, layer, gt, tm):
    bsz, s, d = x.shape
    n_rec = 4 * REC_W
    n_att = 3 * ATT_W
    n_idx = IDX_HEADS * IDX_DIM + LANES
    wb = w_in.astype(BF16)
    w_rec = wb[:, :n_rec]
    w_att = wb[:, n_rec:n_rec + n_att]
    w_idx = jnp.zeros((d, n_idx), BF16).at[:, :w_in.shape[1] - n_rec - n_att].set(wb[:, n_rec + n_att:])
    pm = jnp.asarray(np.kron(np.eye(ATT_HEADS), np.full((ATT_DH, ATT_DH), 1.0 / ATT_DH)), BF16)
    qg = jnp.tile(q_g, ATT_HEADS).reshape(1, ATT_W) * (ATT_DH ** -0.5)
    kg = jnp.tile(k_g, ATT_HEADS).reshape(1, ATT_W)
    ikg = jnp.zeros((1, LANES), F32).at[0, :IDX_DIM].set(ik_g)
    ikb = jnp.zeros((1, LANES), F32).at[0, :IDX_DIM].set(ik_b)
    nl = lb_logits.shape[0]

    row = lambda w: pl.BlockSpec((None, tm, w), lambda b, i: (b, i, 0))
    pair = pl.BlockSpec((None, N_PAIRS, tm, LANES), lambda b, i: (b, 0, i, 0))
    vec = pl.BlockSpec((None, 1, d), lambda b, i: (b, 0, 0))
    out_shapes = (
        jax.ShapeDtypeStruct((bsz, s, REC_W), BF16),
        jax.ShapeDtypeStruct((bsz, s, REC_W), F32),
        jax.ShapeDtypeStruct((bsz, s, REC_W), BF16),
        jax.ShapeDtypeStruct((bsz, s, REC_W), BF16),
        jax.ShapeDtypeStruct((bsz, N_PAIRS, s, LANES), BF16),
        jax.ShapeDtypeStruct((bsz, N_PAIRS, s, LANES), BF16),
        jax.ShapeDtypeStruct((bsz, s // gt, ATT_W, gt), BF16),
        jax.ShapeDtypeStruct((bsz, N_PAIRS, s, LANES), BF16),
        jax.ShapeDtypeStruct((bsz, s, LANES), BF16),
        jax.ShapeDtypeStruct((bsz, IDX_HEADS, s), F32),
    )
    out_specs = (
        row(REC_W), row(REC_W), row(REC_W), row(REC_W), pair, pair,
        pl.BlockSpec((None, tm // gt, ATT_W, gt), lambda b, i: (b, i, 0, 0)),
        pair, row(LANES),
        pl.BlockSpec((None, IDX_HEADS, tm), lambda b, i: (b, 0, i)),
    )
    return pl.pallas_call(
        functools.partial(_inproj_kernel, layer=layer, gt=gt),
        grid=(bsz, s // tm),
        in_specs=[row(d), vec, vec, _const_spec((1, d)),
                  _const_spec((d, n_rec)), _const_spec((d, n_att)), _const_spec((d, n_idx)),
                  _const_spec((nl, REC_W)), _const_spec((1, ATT_W)), _const_spec((1, ATT_W)),
                  _const_spec((1, LANES)), _const_spec((1, LANES)), _const_spec((ATT_W, ATT_W))],
        out_specs=out_specs,
        out_shape=out_shapes,
        compiler_params=_params(("parallel", "parallel")),
        name="inproj",
    )(x, sc1.reshape(bsz, 1, d), sh1.reshape(bsz, 1, d), norm_g.reshape(1, d),
      w_rec, w_att, w_idx, lb_logits, qg, kg, ikg, ikb, pm)


N_LEVELS = 6


def _hgrn_tables():
    c = CHUNK
    w = np.zeros((N_LEVELS + 2, c, c), np.float32)
    am = np.zeros((N_LEVELS + 1, c, c), np.float32)
    t = np.arange(c)
    for m in range(N_LEVELS):
        hs = 1 << m
        blk = t // (2 * hs)
        upper = (t // hs) % 2 == 1
        ref = blk * 2 * hs + hs - 1
        for i in range(c):
            if upper[i]:
                w[m, i, ref[i] + 1:i + 1] = 1.0
            else:
                w[m, i, i + 1:ref[i] + 1] = 1.0
        am[m] = (blk[:, None] == blk[None, :]) & upper[:, None] & ~upper[None, :]
    w[N_LEVELS] = np.tril(np.ones((c, c)))
    w[N_LEVELS + 1] = np.triu(np.ones((c, c)), 1)
    am[N_LEVELS] = np.eye(c)
    w = w.reshape((N_LEVELS + 2) * c, c)
    return np.concatenate([w, w], axis=1), am


def _hgrn_kernel(q_ref, f_ref, v_ref, g_ref, og_ref, ww_ref, am_ref, o_ref, st_ref, *, n_chunks):
    @pl.when(pl.program_id(1) == 0)
    def _():
        st_ref[...] = jnp.zeros_like(st_ref)

    ww = ww_ref[...]
    c = CHUNK
    tbit = lax.broadcasted_iota(I32, (c, REC_DK), 0)

    def chunk(ci, carry):
        r0 = pl.multiple_of(ci * c, c)
        for hd in range(REC_HEADS):
            cs = slice(hd * REC_DK, (hd + 1) * REC_DK)
            f = f_ref[pl.ds(r0, c), cs]
            q = q_ref[pl.ds(r0, c), cs].astype(F32)
            v = v_ref[pl.ds(r0, c), cs]
            k = 1.0 - f
            lf_hi, lf_lo = _split_bf16(jnp.log(f))
            ex = jnp.exp(_dot(ww, jnp.concatenate([lf_hi, lf_lo], axis=0)))
            a = am_ref[N_LEVELS] * _dot_nt(q.astype(BF16), k.astype(BF16))
            for m in range(N_LEVELS):
                upper = ((tbit >> m) & 1) == 1
                tm_ = (ex[m * c:(m + 1) * c] * jnp.where(upper, q, k)).astype(BF16)
                a = a + am_ref[m] * _dot_nt(tm_, tm_)
            qb = (q * ex[N_LEVELS * c:(N_LEVELS + 1) * c]).astype(BF16)
            kb = (k * ex[(N_LEVELS + 1) * c:(N_LEVELS + 2) * c]).astype(BF16)
            st = st_ref[hd]
            o = _dot(a.astype(BF16), v) + _dot_nt(qb, st.astype(BF16))
            d_end = ex[(N_LEVELS + 1) * c - 1:(N_LEVELS + 1) * c, :]
            st_ref[hd] = st * d_end + _dot_tn(v, kb)
            ms = jnp.mean(o * o, axis=-1, keepdims=True)
            y = o * lax.rsqrt(ms + EPS) * og_ref[:, cs] * g_ref[pl.ds(r0, c), cs].astype(F32)
            o_ref[pl.ds(r0, c), cs] = y.astype(BF16)
        return carry

    lax.fori_loop(0, n_chunks, chunk, 0)


def _hgrn(q, f, v, g, out_g, *, ts):
    bsz, s, _ = q.shape
    ww_np, am_np = _hgrn_tables()
    ww = jnp.asarray(ww_np, BF16)
    am = jnp.asarray(am_np, F32)
    row = pl.BlockSpec((None, ts, REC_W), lambda b, i: (b, i, 0))
    return pl.pallas_call(
        functools.partial(_hgrn_kernel, n_chunks=ts // CHUNK),
        grid=(bsz, s // ts),
        in_specs=[row, row, row, row, _const_spec((1, REC_W)),
                  _const_spec(ww.shape), _const_spec(am.shape)],
        out_specs=row,
        out_shape=jax.ShapeDtypeStruct((bsz, s, REC_W), BF16),
        scratch_shapes=[pltpu.VMEM((REC_HEADS, REC_DV, REC_DK), F32)],
        compiler_params=_params(("parallel", "arbitrary")),
        name="hgrn",
    )(q, f, v, g, out_g.reshape(1, REC_W), ww, am)


def _t5_bucket(rel):
    nb = NUM_BUCKETS // 2
    max_exact = nb // 2
    ret = jnp.where(rel > 0, nb, 0)
    n = jnp.abs(rel)
    nf = jnp.maximum(n, 1).astype(F32)
    large = max_exact + (jnp.log(nf / max_exact) / np.log(MAX_DISTANCE / max_exact)
                         * (nb - max_exact)).astype(I32)
    large = jnp.minimum(large, nb - 1)
    return ret + jnp.where(n < max_exact, n, large)


def _bias_slots(rel_bias):
    sl = jnp.arange(2 * QBLK, dtype=I32)[:, None]
    tl = jnp.arange(QBLK, dtype=I32)[None, :]
    near = rel_bias[_t5_bucket(sl - QBLK - tl)]
    far = rel_bias[_t5_bucket(jnp.full((1, 1), -2 * QBLK - 1, I32))]
    nb = jnp.transpose(near - far, (2, 0, 1)).reshape(ATT_HEADS, 2, QBLK, QBLK)
    z = jnp.zeros((ATT_HEADS, 1, QBLK, QBLK), F32)
    return jnp.concatenate([z, nb, z], axis=1)


def _attn_kernel(aq_ref, ak_ref, avt_ref, iq_ref, ik_ref, iwt_ref, bias_ref, og_ref, o_ref,
                 key_ref, mb_ref, thr_ref, acc_ref, m_ref, l_ref, ot_ref, *, gk, k_sel):
    bi = pl.program_id(1)
    spg = gk // QBLK
    n_g = bi // spg + 1
    lane = lax.broadcasted_iota(I32, (QBLK, LANES), 1)

    def block_diag(xq):
        zero = jnp.zeros_like(xq)
        return jnp.concatenate([jnp.where(lane < ATT_DH, xq, zero), jnp.where(lane >= ATT_DH, xq, zero)], axis=0)

    def rows(g):
        return pl.ds(pl.multiple_of(g * gk, gk), gk)

    iq_all = jnp.concatenate([block_diag(iq_ref[j]) for j in range(N_PAIRS)], axis=0)
    w = iwt_ref[...]

    def idx_body(g, carry):
        sc = _dot_nt(ik_ref[rows(g), :], iq_all)
        acc = jnp.zeros((gk, LANES), F32)
        for hd in range(IDX_HEADS):
            acc = acc + jnp.maximum(sc[:, hd * LANES:(hd + 1) * LANES], 0.0) * w[hd:hd + 1, :]
        bits = pltpu.bitcast(acc, I32)
        bits = jnp.where(bits == INT_MIN, 0, bits)
        key_ref[rows(g), :] = bits ^ ((bits >> 31) & 0x7FFFFFFF)
        return carry

    lax.fori_loop(0, n_g, idx_body, 0)

    last = rows(n_g - 1)
    spos = (n_g - 1) * gk + lax.broadcasted_iota(I32, (gk, LANES), 0)
    tpos = bi * QBLK + lax.broadcasted_iota(I32, (gk, LANES), 1)
    key_ref[last, :] = jnp.where((spos // CHUNK) <= (tpos // CHUNK), key_ref[last, :], INT_MIN)

    thr_ref[...] = jnp.full(thr_ref.shape, INT_MIN, I32)

    @pl.when((2 * bi + 2) * CHUNK > k_sel)
    def _():
        def count_ge(cand):
            def body(g, acc):
                ind = jnp.where(key_ref[rows(g), :] >= cand, 1, 0)
                return acc + jnp.sum(ind.reshape(gk // 8, 8, LANES), axis=0)
            acc = lax.fori_loop(0, n_g, body, jnp.zeros((8, LANES), I32))
            return jnp.sum(acc, axis=0, keepdims=True)

        def bit_body(i, thr):
            cand = thr + (jnp.int32(1) << (31 - i))
            return jnp.where(count_ge(cand) >= k_sel, cand, thr)

        thr = lax.fori_loop(0, 32, bit_body, jnp.full((1, LANES), INT_MIN, I32))
        thr_ref[...] = jnp.broadcast_to(thr, thr_ref.shape)

    thr = jnp.maximum(thr_ref[0:1, :], INT_MIN + 1)

    def mb_body(g, carry):
        mb_ref[rows(g), :] = jnp.where(key_ref[rows(g), :] >= thr, 0.0, NEG)
        return carry

    lax.fori_loop(0, n_g, mb_body, 0)

    def pair_body(j, carry):
        qbd = block_diag(aq_ref[j])
        m_ref[...] = jnp.full(m_ref.shape, NEG, F32)
        l_ref[...] = jnp.zeros_like(l_ref)
        acc_ref[...] = jnp.zeros_like(acc_ref)

        def tile(g, with_bias):
            s = _dot_nt(ak_ref[j, rows(g), :], qbd)
            mb = mb_ref[rows(g), :]
            s = s + jnp.concatenate([mb, mb], axis=1)
            if with_bias:
                parts = []
                for st in range(spg):
                    slot = jnp.clip(g * spg + st - bi + 2, 0, 3)
                    parts.append(jnp.concatenate([bias_ref[2 * j, slot], bias_ref[2 * j + 1, slot]], axis=1))
                s = s + jnp.concatenate(parts, axis=0)
            m_old = m_ref[...]
            m_new = jnp.maximum(m_old, jnp.max(s, axis=0, keepdims=True))
            alpha = jnp.exp(m_old - m_new)
            p = jnp.exp(s - m_new)
            l_ref[...] = l_ref[...] * alpha + jnp.sum(p, axis=0, keepdims=True)
            m_ref[...] = m_new
            vt = avt_ref[g, pl.ds(pl.multiple_of(j * LANES, LANES), LANES), :]
            acc_ref[...] = acc_ref[...] * alpha + _dot(vt, p.astype(BF16))

        def far_body(g, c):
            tile(g, False)
            return c

        def near_body(g, c):
            tile(g, True)
            return c

        n_far = jnp.maximum(n_g - 2, 0)
        lax.fori_loop(0, n_far, far_body, 0)
        lax.fori_loop(n_far, n_g, near_body, 0)

        acc = acc_ref[...]
        l = l_ref[...]
        o0 = acc[0:ATT_DH, 0:LANES] / l[:, 0:LANES]
        o1 = acc[ATT_DH:2 * ATT_DH, LANES:2 * LANES] / l[:, LANES:2 * LANES]
        ot_ref[pl.ds(pl.multiple_of(j * LANES, LANES), LANES), :] = jnp.concatenate([o0, o1], axis=0)
        return carry

    lax.fori_loop(0, N_PAIRS, pair_body, 0)

    ot = ot_ref[...]
    og = og_ref[...]
    outs = []
    for hd in range(ATT_HEADS):
        oh = ot[hd * ATT_DH:(hd + 1) * ATT_DH, :]
        ms = jnp.mean(oh * oh, axis=0, keepdims=True)
        outs.append(oh * lax.rsqrt(ms + EPS) * og[hd * ATT_DH:(hd + 1) * ATT_DH, :])
    o_ref[...] = jnp.concatenate(outs, axis=0).T.astype(BF16)


def _attention(aq, ak, avt, iq, ik, iwt, rel_bias, out_g, *, gk):
    bsz, _, s, _ = aq.shape
    k_sel = min(TOPK_MAX, s // 4)
    bias = _bias_slots(rel_bias)
    qpair = pl.BlockSpec((None, N_PAIRS, QBLK, LANES), lambda b, i: (b, 0, i, 0))
    full = lambda shp: pl.BlockSpec((None,) + shp, lambda b, i: (b,) + (0,) * len(shp),
                                    pipeline_mode=pl.Buffered(1))
    return pl.pallas_call(
        functools.partial(_attn_kernel, gk=gk, k_sel=k_sel),
        grid=(bsz, s // QBLK),
        in_specs=[qpair, full((N_PAIRS, s, LANES)), full((s // gk, ATT_W, gk)), qpair,
                  full((s, LANES)), pl.BlockSpec((None, IDX_HEADS, QBLK), lambda b, i: (b, 0, i)),
                  _const_spec(bias.shape), _const_spec((ATT_W, 1))],
        out_specs=pl.BlockSpec((None, QBLK, ATT_W), lambda b, i: (b, i, 0)),
        out_shape=jax.ShapeDtypeStruct((bsz, s, ATT_W), BF16),
        scratch_shapes=[pltpu.VMEM((s, LANES), I32),
                        pltpu.VMEM((s, LANES), F32),
                        pltpu.VMEM((8, LANES), I32),
                        pltpu.VMEM((2 * ATT_DH, 2 * LANES), F32),
                        pltpu.VMEM((1, 2 * LANES), F32),
                        pltpu.VMEM((1, 2 * LANES), F32),
                        pltpu.VMEM((ATT_W, LANES), F32)],
        compiler_params=_params(("parallel", "arbitrary")),
        name="attn",
    )(aq, ak, avt, iq, ik, iwt, bias, out_g.reshape(ATT_W, 1))


def _outproj_kernel(rec_ref, att_ref, x_ref, g1_ref, sc_ref, sh_ref, ng_ref, wo_ref, wr_ref, br_ref,
                    x1_ref, h2_ref, comb_ref):
    mix = _dot(rec_ref[...], wo_ref[0:REC_W, :]) + _dot(att_ref[...], wo_ref[REC_W:REC_W + ATT_W, :])
    x1 = x_ref[...] + g1_ref[...] * mix
    x1_ref[...] = x1
    ms = jnp.mean(x1 * x1, axis=-1, keepdims=True)
    h2 = (x1 * lax.rsqrt(ms + EPS) * ng_ref[...] * (1.0 + sc_ref[...]) + sh_ref[...]).astype(BF16)
    h2_ref[...] = h2

    lg = _dot(h2, wr_ref[...]) + br_ref[...]
    lane = lax.broadcasted_iota(I32, lg.shape, 1)
    big = jnp.int32(2 * LANES)
    is_g = lane < N_GROUPS
    gl = jnp.where(is_g, lg, -jnp.inf)
    gmax = jnp.max(gl, axis=-1, keepdims=True)
    gate = 1.0 / jnp.sum(jnp.where(is_g, jnp.exp(lg - gmax), 0.0), axis=-1, keepdims=True)
    gtop = jnp.min(jnp.where(gl == gmax, lane, big), axis=-1, keepdims=True)
    e_lo = ROUTE_OFF + EXPERTS_PER_GROUP * gtop
    el = jnp.where((lane >= e_lo) & (lane < e_lo + EXPERTS_PER_GROUP), lg, -jnp.inf)
    v1 = jnp.max(el, axis=-1, keepdims=True)
    i1 = jnp.min(jnp.where(el == v1, lane, big), axis=-1, keepdims=True)
    el2 = jnp.where(lane == i1, -jnp.inf, el)
    v2 = jnp.max(el2, axis=-1, keepdims=True)
    i2 = jnp.min(jnp.where(el2 == v2, lane, big), axis=-1, keepdims=True)
    e2 = jnp.exp(v2 - v1)
    w1 = gate / (1.0 + e2)
    w2 = gate * e2 / (1.0 + e2)
    comb_ref[...] = jnp.where(lane == i1, w1, 0.0) + jnp.where(lane == i2, w2, 0.0)


def _outproj(rec, att, x, g1, sc2, sh2, norm_g, w_out, w_rg, b_rg, w_re, b_re, *, tm):
    bsz, s, d = x.shape
    wr = jnp.zeros((d, LANES), F32).at[:, :N_GROUPS].set(w_rg).at[:, ROUTE_OFF:ROUTE_OFF + N_EXPERTS].set(w_re)
    br = jnp.zeros((1, LANES), F32).at[0, :N_GROUPS].set(b_rg).at[0, ROUTE_OFF:ROUTE_OFF + N_EXPERTS].set(b_re)
    row = lambda w: pl.BlockSpec((None, tm, w), lambda b, i: (b, i, 0))
    vec = pl.BlockSpec((None, 1, d), lambda b, i: (b, 0, 0))
    return pl.pallas_call(
        _outproj_kernel,
        grid=(bsz, s // tm),
        in_specs=[row(REC_W), row(ATT_W), row(d), vec, vec, vec, _const_spec((1, d)),
                  _const_spec((REC_W + ATT_W, d)), _const_spec((d, LANES)), _const_spec((1, LANES))],
        out_specs=(row(d), row(d), row(LANES)),
        out_shape=(jax.ShapeDtypeStruct((bsz, s, d), F32),
                   jax.ShapeDtypeStruct((bsz, s, d), BF16),
                   jax.ShapeDtypeStruct((bsz, s, LANES), F32)),
        compiler_params=_params(("parallel", "parallel")),
        name="outproj",
    )(rec, att, x, g1.reshape(bsz, 1, d), sc2.reshape(bsz, 1, d), sh2.reshape(bsz, 1, d),
      norm_g.reshape(1, d), w_out.astype(BF16), wr.astype(BF16), br)


def _moe_kernel(h_ref, comb_ref, x1_ref, g2_ref, w1_ref, w3_ref, w2_ref, o_ref, y_ref):
    e = pl.program_id(2)

    @pl.when(e == 0)
    def _():
        y_ref[...] = jnp.zeros_like(y_ref)

    h = h_ref[...]
    comb = comb_ref[...]
    lane = lax.broadcasted_iota(I32, comb.shape, 1)
    col = jnp.sum(jnp.where(lane == e + ROUTE_OFF, comb, 0.0), axis=-1, keepdims=True)
    he = _silu(_dot(h, w1_ref[...])) * _dot(h, w3_ref[...]) * col
    y_ref[...] += _dot(he.astype(BF16), w2_ref[...])

    @pl.when(e == N_EXPERTS - 1)
    def _():
        o_ref[...] = x1_ref[...] + g2_ref[...] * y_ref[...]


def _moe(h2, comb, x1, g2, w1, w3, w2, *, tm):
    bsz, s, d = x1.shape
    row = lambda w: pl.BlockSpec((None, tm, w), lambda b, i, e: (b, i, 0))
    return pl.pallas_call(
        _moe_kernel,
        grid=(bsz, s // tm, N_EXPERTS),
        in_specs=[row(d), row(LANES), row(d), pl.BlockSpec((None, 1, d), lambda b, i, e: (b, 0, 0)),
                  pl.BlockSpec((None, d, D_EXPERT), lambda b, i, e: (e, 0, 0)),
                  pl.BlockSpec((None, d, D_EXPERT), lambda b, i, e: (e, 0, 0)),
                  pl.BlockSpec((None, D_EXPERT, d), lambda b, i, e: (e, 0, 0))],
        out_specs=row(d),
        out_shape=jax.ShapeDtypeStruct((bsz, s, d), F32),
        scratch_shapes=[pltpu.VMEM((tm, d), F32)],
        compiler_params=_params(("parallel", "parallel", "arbitrary")),
        name="moe",
    )(h2, comb, x1, g2.reshape(bsz, 1, d), w1.astype(BF16), w3.astype(BF16), w2.astype(BF16))


def _tile(s, pref):
    t = min(s, pref)
    assert s % t == 0
    return t


def kernel(x, c, w_ada, b_ada, norm1_g, norm2_g, w_in, lb_logits, rec_out_g, q_norm_g, k_norm_g,
           idx_k_norm_g, idx_k_norm_b, attn_out_g, rel_bias, w_out, w_rg, b_rg, w_re, b_re, w1, w3, w2):
    bsz, s, d = x.shape
    depth = w_ada.shape[0]
    gk = _tile(s, 256)
    for l in range(depth):
        mod = _adaln(c, w_ada[l], b_ada[l])
        sh1, sc1, g1, sh2, sc2, g2 = jnp.split(mod, 6, axis=-1)
        q, f, v, g, aq, ak, avt, iq, ik, iwt = _inproj(
            x, sc1, sh1, norm1_g[l], w_in[l], lb_logits, q_norm_g[l], k_norm_g[l],
            idx_k_norm_g[l], idx_k_norm_b[l], layer=l, gt=gk, tm=_tile(s, 512))
        rec = _hgrn(q, f, v, g, rec_out_g[l], ts=_tile(s, 256))
        att = _attention(aq, ak, avt, iq, ik, iwt, rel_bias, attn_out_g[l], gk=gk)
        x1, h2, comb = _outproj(rec, att, x, g1, sc2, sh2, norm2_g[l], w_out[l],
                                w_rg[l], b_rg[l], w_re[l], b_re[l], tm=_tile(s, 512))
        x = _moe(h2, comb, x1, g2, w1[l], w3[l], w2[l], tm=_tile(s, 1024))
    return x
```

```python
import functools

import numpy as np
import jax
import jax.numpy as jnp
from jax import lax
from jax.experimental import pallas as pl
from jax.experimental.pallas import tpu as pltpu

F32 = jnp.float32
BF16 = jnp.bfloat16
I32 = jnp.int32

CHUNK = 64
QBLK = 128
EPS = 1e-6
REC_HEADS = 4
REC_DK = 128
REC_DV = 128
REC_W = REC_HEADS * REC_DV
ATT_HEADS = 8
ATT_DH = 64
ATT_W = ATT_HEADS * ATT_DH
IDX_HEADS = 8
IDX_DIM = 64
TOPK_MAX = 256
NUM_BUCKETS = 32
MAX_DISTANCE = 128
N_GROUPS = 4
EXPERTS_PER_GROUP = 4
N_EXPERTS = N_GROUPS * EXPERTS_PER_GROUP
D_EXPERT = 512

LANES = 128
V7X_VMEM_LIMIT = 56 * 1024 * 1024
INT_MIN = -(2 ** 31)
NEG = -1e30
LOG2E = float(np.log2(np.e))
N_PAIRS = ATT_HEADS // 2
ROUTE_OFF = N_GROUPS


def _dot(a, b):
    return jnp.dot(a, b, preferred_element_type=F32)


def _dot_nt(a, b):
    return lax.dot_general(a, b, (((1,), (1,)), ((), ())), preferred_element_type=F32)


def _dot_tn(a, b):
    return lax.dot_general(a, b, (((0,), (0,)), ((), ())), preferred_element_type=F32)


def _split_bf16(a):
    hi = a.astype(BF16)
    lo = (a - hi.astype(F32)).astype(BF16)
    return hi, lo


def _silu(a):
    return a * jax.nn.sigmoid(a)


def _const_spec(shape):
    nd = len(shape)
    return pl.BlockSpec(shape, lambda *_: (0,) * nd, pipeline_mode=pl.Buffered(1))


def _params(sem):
    return pltpu.CompilerParams(dimension_semantics=sem, vmem_limit_bytes=V7X_VMEM_LIMIT)


def _adaln_kernel(c_ref, w_ref, b_ref, o_ref):
    a_hi, a_lo = _split_bf16(_silu(c_ref[...]))
    w_hi, w_lo = _split_bf16(w_ref[...])
    o_ref[...] = _dot(a_hi, w_hi) + _dot(a_lo, w_hi) + _dot(a_hi, w_lo) + b_ref[...]


def _adaln(c, w, b):
    bsz, d = c.shape
    n = w.shape[1]
    rows = 16
    bn = 1024
    cp = jnp.zeros((rows, d), F32).at[:bsz].set(c)
    out = pl.pallas_call(
        _adaln_kernel,
        grid=(n // bn,),
        in_specs=[pl.BlockSpec((rows, d), lambda i: (0, 0)),
                  pl.BlockSpec((d, bn), lambda i: (0, i)),
                  pl.BlockSpec((1, bn), lambda i: (0, i))],
        out_specs=pl.BlockSpec((rows, bn), lambda i: (0, i)),
        out_shape=jax.ShapeDtypeStruct((rows, n), F32),
        compiler_params=_params(("parallel",)),
        name="adaln",
    )(cp, w, b.reshape(1, n))
    return out[:bsz]


def _inproj_kernel(x_ref, sc_ref, sh_ref, ng_ref, wrec_ref, watt_ref, widx_ref, lbl_ref,
                   qg_ref, kg_ref, ikg_ref, ikb_ref, pm_ref,
                   q_ref, f_ref, v_ref, g_ref, aq_ref, ak_ref, avt_ref, iq_ref, ik_ref, iwt_ref,
                   *, layer, gt):
    x = x_ref[...]
    tm = x.shape[0]
    ms = jnp.mean(x * x, axis=-1, keepdims=True)
    h = x * lax.rsqrt(ms + EPS) * ng_ref[...] * (1.0 + sc_ref[...]) + sh_ref[...]
    hb = h.astype(BF16)

    zr = _dot(hb, wrec_ref[...])
    q_ref[...] = _silu(zr[:, 0:REC_W]).astype(BF16)
    lbl = lbl_ref[...]
    e = jnp.exp(lbl - jnp.max(lbl, axis=0, keepdims=True))
    sm = e / jnp.sum(e, axis=0, keepdims=True)
    lb = jnp.sum(sm[0:layer + 1], axis=0, keepdims=True)
    f_ref[...] = lb + (1.0 - lb) * jax.nn.sigmoid(zr[:, REC_W:2 * REC_W])
    v_ref[...] = zr[:, 2 * REC_W:3 * REC_W].astype(BF16)
    g_ref[...] = _silu(zr[:, 3 * REC_W:4 * REC_W]).astype(BF16)

    za = _dot(hb, watt_ref[...])
    aq = za[:, 0:ATT_W]
    ak = za[:, ATT_W:2 * ATT_W]
    av = za[:, 2 * ATT_W:3 * ATT_W]
    pm = pm_ref[...]
    aqn = aq * lax.rsqrt(_dot((aq * aq).astype(BF16), pm) + EPS) * qg_ref[...]
    akn = ak * lax.rsqrt(_dot((ak * ak).astype(BF16), pm) + EPS) * kg_ref[...]
    for j in range(N_PAIRS):
        aq_ref[j] = aqn[:, j * LANES:(j + 1) * LANES].astype(BF16)
        ak_ref[j] = akn[:, j * LANES:(j + 1) * LANES].astype(BF16)
    for t in range(tm // gt):
        avt_ref[t] = av[t * gt:(t + 1) * gt, :].T.astype(BF16)

    zi = _dot(hb, widx_ref[...])
    for j in range(N_PAIRS):
        iq_ref[j] = zi[:, j * LANES:(j + 1) * LANES].astype(BF16)
    tail = zi[:, IDX_HEADS * IDX_DIM:IDX_HEADS * IDX_DIM + LANES]
    lane = lax.broadcasted_iota(I32, tail.shape, 1)
    is_k = lane < IDX_DIM
    mu = jnp.sum(jnp.where(is_k, tail, 0.0), axis=-1, keepdims=True) * (1.0 / IDX_DIM)
    dlt = jnp.where(is_k, tail - mu, 0.0)
    var = jnp.sum(dlt * dlt, axis=-1, keepdims=True) * (1.0 / IDX_DIM)
    ikn = dlt * lax.rsqrt(var + EPS) * ikg_ref[...] + ikb_ref[...]
    ik_ref[...] = jnp.where(is_k, ikn, pltpu.roll(ikn, IDX_DIM, axis=1)).astype(BF16)
    iwt_ref[...] = tail.T[IDX_DIM:IDX_DIM + IDX_HEADS, :] * (IDX_HEADS ** -0.5 * IDX_DIM ** -0.5)


def _inproj(x, sc1, sh1, norm_g, w_in, lb_logits, q_g, k_g, ik_g, ik_b, *, layer, gt, tm):
    bsz, s, d = x.shape
    n_rec = 4 * REC_W
    n_att = 3 * ATT_W
    n_idx = IDX_HEADS * IDX_DIM + LANES
    wb = w_in.astype(BF16)
    w_rec = wb[:, :n_rec]
    w_att = wb[:, n_rec:n_rec + n_att]
    w_idx = jnp.zeros((d, n_idx), BF16).at[:, :w_in.shape[1] - n_rec - n_att].set(wb[:, n_rec + n_att:])
    pm = jnp.asarray(np.kron(np.eye(ATT_HEADS), np.full((ATT_DH, ATT_DH), 1.0 / ATT_DH)), BF16)
    qg = jnp.tile(q_g, ATT_HEADS).reshape(1, ATT_W) * (ATT_DH ** -0.5 * LOG2E)
    kg = jnp.tile(k_g, ATT_HEADS).reshape(1, ATT_W)
    ikg = jnp.zeros((1, LANES), F32).at[0, :IDX_DIM].set(ik_g)
    ikb = jnp.zeros((1, LANES), F32).at[0, :IDX_DIM].set(ik_b)
    nl = lb_logits.shape[0]

    row = lambda w: pl.BlockSpec((None, tm, w), lambda b, i: (b, i, 0))
    pair = pl.BlockSpec((None, N_PAIRS, tm, LANES), lambda b, i: (b, 0, i, 0))
    vec = pl.BlockSpec((None, 1, d), lambda b, i: (b, 0, 0))
    out_shapes = (
        jax.ShapeDtypeStruct((bsz, s, REC_W), BF16),
        jax.ShapeDtypeStruct((bsz, s, REC_W), F32),
        jax.ShapeDtypeStruct((bsz, s, REC_W), BF16),
        jax.ShapeDtypeStruct((bsz, s, REC_W), BF16),
        jax.ShapeDtypeStruct((bsz, N_PAIRS, s, LANES), BF16),
        jax.ShapeDtypeStruct((bsz, N_PAIRS, s, LANES), BF16),
        jax.ShapeDtypeStruct((bsz, s // gt, ATT_W, gt), BF16),
        jax.ShapeDtypeStruct((bsz, N_PAIRS, s, LANES), BF16),
        jax.ShapeDtypeStruct((bsz, s, LANES), BF16),
        jax.ShapeDtypeStruct((bsz, IDX_HEADS, s), F32),
    )
    out_specs = (
        row(REC_W), row(REC_W), row(REC_W), row(REC_W), pair, pair,
        pl.BlockSpec((None, tm // gt, ATT_W, gt), lambda b, i: (b, i, 0, 0)),
        pair, row(LANES),
        pl.BlockSpec((None, IDX_HEADS, tm), lambda b, i: (b, 0, i)),
    )
    return pl.pallas_call(
        functools.partial(_inproj_kernel, layer=layer, gt=gt),
        grid=(bsz, s // tm),
        in_specs=[row(d), vec, vec, _const_spec((1, d)),
                  _const_spec((d, n_rec)), _const_spec((d, n_att)), _const_spec((d, n_idx)),
                  _const_spec((nl, REC_W)), _const_spec((1, ATT_W)), _const_spec((1, ATT_W)),
                  _const_spec((1, LANES)), _const_spec((1, LANES)), _const_spec((ATT_W, ATT_W))],
        out_specs=out_specs,
        out_shape=out_shapes,
        compiler_params=_params(("parallel", "parallel")),
        name="inproj",
    )(x, sc1.reshape(bsz, 1, d), sh1.reshape(bsz, 1, d), norm_g.reshape(1, d),
      w_rec, w_att, w_idx, lb_logits, qg, kg, ikg, ikb, pm)


N_LEVELS = 6


def _hgrn_tables():
    c = CHUNK
    w = np.zeros((N_LEVELS + 2, c, c), np.float32)
    am = np.zeros((N_LEVELS + 1, c, c), np.float32)
    t = np.arange(c)
    for m in range(N_LEVELS):
        hs = 1 << m
        blk = t // (2 * hs)
        upper = (t // hs) % 2 == 1
        ref = blk * 2 * hs + hs - 1
        for i in range(c):
            if upper[i]:
                w[m, i, ref[i] + 1:i + 1] = 1.0
            else:
                w[m, i, i + 1:ref[i] + 1] = 1.0
        am[m] = (blk[:, None] == blk[None, :]) & upper[:, None] & ~upper[None, :]
    w[N_LEVELS] = np.tril(np.ones((c, c)))
    w[N_LEVELS + 1] = np.triu(np.ones((c, c)), 1)
    am[N_LEVELS] = np.eye(c)
    w = w.reshape((N_LEVELS + 2) * c, c)
    return np.concatenate([w, w], axis=1), am


def _hgrn_kernel(q_ref, f_ref, v_ref, g_ref, og_ref, ww_ref, am_ref, o_ref, st_ref, *, n_chunks):
    @pl.when(pl.program_id(1) == 0)
    def _():
        st_ref[...] = jnp.zeros_like(st_ref)

    ww = ww_ref[...]
    c = CHUNK
    tbit = lax.broadcasted_iota(I32, (c, REC_DK), 0)

    def chunk(ci, carry):
        r0 = pl.multiple_of(ci * c, c)
        for hd in range(REC_HEADS):
            cs = slice(hd * REC_DK, (hd + 1) * REC_DK)
            f = f_ref[pl.ds(r0, c), cs]
            q = q_ref[pl.ds(r0, c), cs].astype(F32)
            v = v_ref[pl.ds(r0, c), cs]
            k = 1.0 - f
            lf_hi, lf_lo = _split_bf16(jnp.log(f))
            ex = jnp.exp(_dot(ww, jnp.concatenate([lf_hi, lf_lo], axis=0)))
            a = am_ref[N_LEVELS] * _dot_nt(q.astype(BF16), k.astype(BF16))
            for m in range(N_LEVELS):
                upper = ((tbit >> m) & 1) == 1
                tm_ = (ex[m * c:(m + 1) * c] * jnp.where(upper, q, k)).astype(BF16)
                a = a + am_ref[m] * _dot_nt(tm_, tm_)
            qb = (q * ex[N_LEVELS * c:(N_LEVELS + 1) * c]).astype(BF16)
            kb = (k * ex[(N_LEVELS + 1) * c:(N_LEVELS + 2) * c]).astype(BF16)
            st = st_ref[hd]
            o = _dot(a.astype(BF16), v) + _dot_nt(qb, st.astype(BF16))
            d_end = ex[(N_LEVELS + 1) * c - 1:(N_LEVELS + 1) * c, :]
            st_ref[hd] = st * d_end + _dot_tn(v, kb)
            ms = jnp.mean(o * o, axis=-1, keepdims=True)
            y = o * lax.rsqrt(ms + EPS) * og_ref[:, cs] * g_ref[pl.ds(r0, c), cs].astype(F32)
            o_ref[pl.ds(r0, c), cs] = y.astype(BF16)
        return carry

    lax.fori_loop(0, n_chunks, chunk, 0)


def _hgrn(q, f, v, g, out_g, *, ts):
    bsz, s, _ = q.shape
    ww_np, am_np = _hgrn_tables()
    ww = jnp.asarray(ww_np, BF16)
    am = jnp.asarray(am_np, F32)
    row = pl.BlockSpec((None, ts, REC_W), lambda b, i: (b, i, 0))
    return pl.pallas_call(
        functools.partial(_hgrn_kernel, n_chunks=ts // CHUNK),
        grid=(bsz, s // ts),
        in_specs=[row, row, row, row, _const_spec((1, REC_W)),
                  _const_spec(ww.shape), _const_spec(am.shape)],
        out_specs=row,
        out_shape=jax.ShapeDtypeStruct((bsz, s, REC_W), BF16),
        scratch_shapes=[pltpu.VMEM((REC_HEADS, REC_DV, REC_DK), F32)],
        compiler_params=_params(("parallel", "arbitrary")),
        name="hgrn",
    )(q, f, v, g, out_g.reshape(1, REC_W), ww, am)


def _t5_bucket(rel):
    nb = NUM_BUCKETS // 2
    max_exact = nb // 2
    ret = jnp.where(rel > 0, nb, 0)
    n = jnp.abs(rel)
    nf = jnp.maximum(n, 1).astype(F32)
    large = max_exact + (jnp.log(nf / max_exact) / np.log(MAX_DISTANCE / max_exact)
                         * (nb - max_exact)).astype(I32)
    large = jnp.minimum(large, nb - 1)
    return ret + jnp.where(n < max_exact, n, large)


def _bias_slots(rel_bias):
    sl = jnp.arange(2 * QBLK, dtype=I32)[:, None]
    tl = jnp.arange(QBLK, dtype=I32)[None, :]
    def lookup(bucket):
        oh = (bucket[..., None] == jnp.arange(NUM_BUCKETS, dtype=I32)).astype(F32)
        return jnp.einsum("stb,bh->hst", oh, rel_bias, precision=lax.Precision.HIGHEST)

    near = lookup(_t5_bucket(sl - QBLK - tl))
    far = lookup(_t5_bucket(jnp.full((1, 1), -2 * QBLK - 1, I32)))
    nb = ((near - far) * LOG2E).reshape(ATT_HEADS, 2, QBLK, QBLK)
    z = jnp.zeros((ATT_HEADS, 1, QBLK, QBLK), F32)
    return jnp.concatenate([z, nb, z], axis=1)


def _attn_kernel(aq_ref, ak_ref, avt_ref, iq_ref, ik_ref, iwt_ref, bias_ref, og_ref, o_ref,
                 key_ref, mb_ref, thr_ref, acc_ref, m_ref, l_ref, al_ref, s_ref, mx_ref, p_ref, qbd_ref, ot_ref,
                 *, gs, gk, k_sel):
    bi = pl.program_id(1)
    n_s = bi // (gs // QBLK) + 1
    lane = lax.broadcasted_iota(I32, (QBLK, LANES), 1)

    def block_diag(xq):
        zero = jnp.zeros_like(xq)
        return jnp.concatenate([jnp.where(lane < ATT_DH, xq, zero), jnp.where(lane >= ATT_DH, xq, zero)], axis=0)

    def srows(g):
        return pl.ds(pl.multiple_of(g * gs, gs), gs)

    for j in range(N_PAIRS):
        qbd_ref[j] = block_diag(iq_ref[j])
    w = iwt_ref[...]

    def idx_body(g, carry):
        sc = _dot_nt(ik_ref[srows(g), :], qbd_ref[...].reshape(N_PAIRS * 2 * QBLK, LANES))
        acc = jnp.zeros((gs, LANES), F32)
        for hd in range(IDX_HEADS):
            acc = acc + jnp.maximum(sc[:, hd * LANES:(hd + 1) * LANES], 0.0) * w[hd:hd + 1, :]
        bits = pltpu.bitcast(acc, I32)
        bits = jnp.where(bits == INT_MIN, 0, bits)
        key_ref[srows(g), :] = bits ^ ((bits >> 31) & 0x7FFFFFFF)
        return carry

    lax.fori_loop(0, n_s, idx_body, 0)

    last = srows(n_s - 1)
    spos = (n_s - 1) * gs + lax.broadcasted_iota(I32, (gs, LANES), 0)
    tpos = bi * QBLK + lax.broadcasted_iota(I32, (gs, LANES), 1)
    key_ref[last, :] = jnp.where((spos // CHUNK) <= (tpos // CHUNK), key_ref[last, :], INT_MIN)

    thr_ref[...] = jnp.full(thr_ref.shape, INT_MIN, I32)

    @pl.when((2 * bi + 2) * CHUNK > k_sel)
    def _():
        def count_ge(cand):
            def body(g, acc):
                ind = jnp.where(key_ref[srows(g), :] >= cand, 1, 0)
                return acc + jnp.sum(ind.reshape(gs // 8, 8, LANES), axis=0)
            acc = lax.fori_loop(0, n_s, body, jnp.zeros((8, LANES), I32))
            return jnp.sum(acc, axis=0, keepdims=True)

        def bit_body(i, thr):
            cand = thr + (jnp.int32(1) << (31 - i))
            return jnp.where(count_ge(cand) >= k_sel, cand, thr)

        thr = lax.fori_loop(0, 32, bit_body, jnp.full((1, LANES), INT_MIN, I32))
        thr_ref[...] = jnp.broadcast_to(thr, thr_ref.shape)

    thr = jnp.maximum(thr_ref[0:1, :], INT_MIN + 1)

    def mb_body(g, carry):
        mb_ref[srows(g), :] = jnp.where(key_ref[srows(g), :] >= thr, 0.0, NEG)
        return carry

    lax.fori_loop(0, n_s, mb_body, 0)

    spg = gk // QBLK
    n_g = bi // spg + 1
    for j in range(N_PAIRS):
        qbd_ref[j] = block_diag(aq_ref[j])
    m_ref[...] = jnp.full(m_ref.shape, NEG, F32)
    l_ref[...] = jnp.zeros_like(l_ref)
    acc_ref[...] = jnp.zeros_like(acc_ref)
    al_ref[...] = jnp.ones_like(al_ref)
    p_ref[1] = jnp.zeros(p_ref.shape[1:], BF16)

    def rows(g):
        return pl.ds(pl.multiple_of(g * gk, gk), gk)

    def scores(t, slot):
        mb = mb_ref[rows(t), :]
        mb2 = jnp.concatenate([mb, mb], axis=1)
        for j in range(N_PAIRS):
            s = _dot_nt(ak_ref[j, rows(t), :], qbd_ref[j]) + mb2
            s_ref[slot, j] = s
            mx_ref[slot, j] = jnp.max(s, axis=0, keepdims=True)

    def pv(t, slot):
        return [_dot(avt_ref[t, j * LANES:(j + 1) * LANES, :], p_ref[slot, j]) for j in range(N_PAIRS)]

    def step(g, cur):
        prv = 1 - cur

        @pl.when(g >= n_g - 2)
        def _():
            for j in range(N_PAIRS):
                parts = []
                for st in range(spg):
                    slot = jnp.clip(g * spg + st - bi + 2, 0, 3)
                    parts.append(jnp.concatenate([bias_ref[2 * j, slot], bias_ref[2 * j + 1, slot]], axis=1))
                s = s_ref[cur, j] + jnp.concatenate(parts, axis=0)
                s_ref[cur, j] = s
                mx_ref[cur, j] = jnp.max(s, axis=0, keepdims=True)

        al_old = [al_ref[j] for j in range(N_PAIRS)]
        for j in range(N_PAIRS):
            m_old = m_ref[j]
            m_new = jnp.maximum(m_old, mx_ref[cur, j])
            p = jnp.exp2(s_ref[cur, j] - m_new)
            alpha = jnp.exp2(m_old - m_new)
            al_ref[j] = alpha
            l_ref[j] = l_ref[j] * alpha + jnp.sum(p, axis=0, keepdims=True)
            m_ref[j] = m_new
            p_ref[cur, j] = p.astype(BF16)
        o_prev = pv(jnp.maximum(g - 1, 0), prv)
        scores(jnp.minimum(g + 1, n_g - 1), prv)
        for j in range(N_PAIRS):
            acc_ref[j] = acc_ref[j] * al_old[j] + o_prev[j]

    scores(0, 0)

    def pipe_body(u, c):
        step(2 * u, 0)

        @pl.when(2 * u + 1 < n_g)
        def _():
            step(2 * u + 1, 1)
        return c

    lax.fori_loop(0, (n_g + 1) // 2, pipe_body, 0)
    o_last = pv(n_g - 1, (n_g - 1) % 2)

    for j in range(N_PAIRS):
        acc = acc_ref[j] * al_ref[j] + o_last[j]
        l = l_ref[j]
        o0 = acc[0:ATT_DH, 0:LANES] / l[:, 0:LANES]
        o1 = acc[ATT_DH:2 * ATT_DH, LANES:2 * LANES] / l[:, LANES:2 * LANES]
        ot_ref[j * LANES:(j + 1) * LANES, :] = jnp.concatenate([o0, o1], axis=0)

    ot = ot_ref[...]
    og = og_ref[...]
    outs = []
    for hd in range(ATT_HEADS):
        oh = ot[hd * ATT_DH:(hd + 1) * ATT_DH, :]
        ms = jnp.mean(oh * oh, axis=0, keepdims=True)
        outs.append(oh * lax.rsqrt(ms + EPS) * og[hd * ATT_DH:(hd + 1) * ATT_DH, :])
    o_ref[...] = jnp.concatenate(outs, axis=0).T.astype(BF16)


def _attention(aq, ak, avt, iq, ik, iwt, rel_bias, out_g, *, gk):
    bsz, _, s, _ = aq.shape
    k_sel = min(TOPK_MAX, s // 4)
    bias = _bias_slots(rel_bias)
    qpair = pl.BlockSpec((None, N_PAIRS, QBLK, LANES), lambda b, i: (b, 0, i, 0))
    full = lambda shp: pl.BlockSpec((None,) + shp, lambda b, i: (b,) + (0,) * len(shp),
                                    pipeline_mode=pl.Buffered(1))
    return pl.pallas_call(
        functools.partial(_attn_kernel, gs=_tile(s, 2 * gk), gk=gk, k_sel=k_sel),
        grid=(bsz, s // QBLK),
        in_specs=[qpair, full((N_PAIRS, s, LANES)), full((s // gk, ATT_W, gk)), qpair,
                  full((s, LANES)), pl.BlockSpec((None, IDX_HEADS, QBLK), lambda b, i: (b, 0, i)),
                  _const_spec(bias.shape), _const_spec((ATT_W, 1))],
        out_specs=pl.BlockSpec((None, QBLK, ATT_W), lambda b, i: (b, i, 0)),
        out_shape=jax.ShapeDtypeStruct((bsz, s, ATT_W), BF16),
        scratch_shapes=[pltpu.VMEM((s, LANES), I32),
                        pltpu.VMEM((s, LANES), F32),
                        pltpu.VMEM((8, LANES), I32),
                        pltpu.VMEM((N_PAIRS, 2 * ATT_DH, 2 * LANES), F32),
                        pltpu.VMEM((N_PAIRS, 1, 2 * LANES), F32),
                        pltpu.VMEM((N_PAIRS, 1, 2 * LANES), F32),
                        pltpu.VMEM((N_PAIRS, 1, 2 * LANES), F32),
                        pltpu.VMEM((2, N_PAIRS, gk, 2 * LANES), F32),
                        pltpu.VMEM((2, N_PAIRS, 1, 2 * LANES), F32),
                        pltpu.VMEM((2, N_PAIRS, gk, 2 * LANES), BF16),
                        pltpu.VMEM((N_PAIRS, 2 * QBLK, LANES), BF16),
                        pltpu.VMEM((ATT_W, LANES), F32)],
        compiler_params=_params(("parallel", "arbitrary")),
        name="attn",
    )(aq, ak, avt, iq, ik, iwt, bias, out_g.reshape(ATT_W, 1))


def _outproj_kernel(rec_ref, att_ref, x_ref, g1_ref, sc_ref, sh_ref, ng_ref, wo_ref, wr_ref, br_ref,
                    x1_ref, h2_ref, comb_ref):
    mix = _dot(rec_ref[...], wo_ref[0:REC_W, :]) + _dot(att_ref[...], wo_ref[REC_W:REC_W + ATT_W, :])
    x1 = x_ref[...] + g1_ref[...] * mix
    x1_ref[...] = x1
    ms = jnp.mean(x1 * x1, axis=-1, keepdims=True)
    h2 = (x1 * lax.rsqrt(ms + EPS) * ng_ref[...] * (1.0 + sc_ref[...]) + sh_ref[...]).astype(BF16)
    h2_ref[...] = h2

    lg = _dot(h2, wr_ref[...]) + br_ref[...]
    lane = lax.broadcasted_iota(I32, lg.shape, 1)
    big = jnp.int32(2 * LANES)
    is_g = lane < N_GROUPS
    gl = jnp.where(is_g, lg, -jnp.inf)
    gmax = jnp.max(gl, axis=-1, keepdims=True)
    gate = 1.0 / jnp.sum(jnp.where(is_g, jnp.exp(lg - gmax), 0.0), axis=-1, keepdims=True)
    gtop = jnp.min(jnp.where(gl == gmax, lane, big), axis=-1, keepdims=True)
    e_lo = ROUTE_OFF + EXPERTS_PER_GROUP * gtop
    el = jnp.where((lane >= e_lo) & (lane < e_lo + EXPERTS_PER_GROUP), lg, -jnp.inf)
    v1 = jnp.max(el, axis=-1, keepdims=True)
    i1 = jnp.min(jnp.where(el == v1, lane, big), axis=-1, keepdims=True)
    el2 = jnp.where(lane == i1, -jnp.inf, el)
    v2 = jnp.max(el2, axis=-1, keepdims=True)
    i2 = jnp.min(jnp.where(el2 == v2, lane, big), axis=-1, keepdims=True)
    e2 = jnp.exp(v2 - v1)
    w1 = gate / (1.0 + e2)
    w2 = gate * e2 / (1.0 + e2)
    comb_ref[...] = jnp.where(lane == i1, w1, 0.0) + jnp.where(lane == i2, w2, 0.0)


def _outproj(rec, att, x, g1, sc2, sh2, norm_g, w_out, w_rg, b_rg, w_re, b_re, *, tm):
    bsz, s, d = x.shape
    wr = jnp.zeros((d, LANES), F32).at[:, :N_GROUPS].set(w_rg).at[:, ROUTE_OFF:ROUTE_OFF + N_EXPERTS].set(w_re)
    br = jnp.zeros((1, LANES), F32).at[0, :N_GROUPS].set(b_rg).at[0, ROUTE_OFF:ROUTE_OFF + N_EXPERTS].set(b_re)
    row = lambda w: pl.BlockSpec((None, tm, w), lambda b, i: (b, i, 0))
    vec = pl.BlockSpec((None, 1, d), lambda b, i: (b, 0, 0))
    return pl.pallas_call(
        _outproj_kernel,
        grid=(bsz, s // tm),
        in_specs=[row(REC_W), row(ATT_W), row(d), vec, vec, vec, _const_spec((1, d)),
                  _const_spec((REC_W + ATT_W, d)), _const_spec((d, LANES)), _const_spec((1, LANES))],
        out_specs=(row(d), row(d), row(LANES)),
        out_shape=(jax.ShapeDtypeStruct((bsz, s, d), F32),
                   jax.ShapeDtypeStruct((bsz, s, d), BF16),
                   jax.ShapeDtypeStruct((bsz, s, LANES), F32)),
        compiler_params=_params(("parallel", "parallel")),
        name="outproj",
    )(rec, att, x, g1.reshape(bsz, 1, d), sc2.reshape(bsz, 1, d), sh2.reshape(bsz, 1, d),
      norm_g.reshape(1, d), w_out.astype(BF16), wr.astype(BF16), br)


def _moe_kernel(h_ref, comb_ref, x1_ref, g2_ref, w1_ref, w3_ref, w2_ref, o_ref, y_ref):
    e = pl.program_id(2)

    @pl.when(e == 0)
    def _():
        y_ref[...] = jnp.zeros_like(y_ref)

    h = h_ref[...]
    comb = comb_ref[...]
    lane = lax.broadcasted_iota(I32, comb.shape, 1)
    col = jnp.sum(jnp.where(lane == e + ROUTE_OFF, comb, 0.0), axis=-1, keepdims=True)
    he = _silu(_dot(h, w1_ref[...])) * _dot(h, w3_ref[...]) * col
    y_ref[...] += _dot(he.astype(BF16), w2_ref[...])

    @pl.when(e == N_EXPERTS - 1)
    def _():
        o_ref[...] = x1_ref[...] + g2_ref[...] * y_ref[...]


def _moe(h2, comb, x1, g2, w1, w3, w2, *, tm):
    bsz, s, d = x1.shape
    row = lambda w: pl.BlockSpec((None, tm, w), lambda b, i, e: (b, i, 0))
    return pl.pallas_call(
        _moe_kernel,
        grid=(bsz, s // tm, N_EXPERTS),
        in_specs=[row(d), row(LANES), row(d), pl.BlockSpec((None, 1, d), lambda b, i, e: (b, 0, 0)),
                  pl.BlockSpec((None, d, D_EXPERT), lambda b, i, e: (e, 0, 0)),
                  pl.BlockSpec((None, d, D_EXPERT), lambda b, i, e: (e, 0, 0)),
                  pl.BlockSpec((None, D_EXPERT, d), lambda b, i, e: (e, 0, 0))],
        out_specs=row(d),
        out_shape=jax.ShapeDtypeStruct((bsz, s, d), F32),
        scratch_shapes=[pltpu.VMEM((tm, d), F32)],
        compiler_params=_params(("parallel", "parallel", "arbitrary")),
        name="moe",
    )(h2, comb, x1, g2.reshape(bsz, 1, d), w1.astype(BF16), w3.astype(BF16), w2.astype(BF16))


def _tile(s, pref):
    t = min(s, pref)
    assert s % t == 0
    return t


def kernel(x, c, w_ada, b_ada, norm1_g, norm2_g, w_in, lb_logits, rec_out_g, q_norm_g, k_norm_g,
           idx_k_norm_g, idx_k_norm_b, attn_out_g, rel_bias, w_out, w_rg, b_rg, w_re, b_re, w1, w3, w2):
    bsz, s, d = x.shape
    depth = w_ada.shape[0]
    gk = _tile(s, 256)
    for l in range(depth):
        mod = _adaln(c, w_ada[l], b_ada[l])
        sh1, sc1, g1, sh2, sc2, g2 = jnp.split(mod, 6, axis=-1)
        q, f, v, g, aq, ak, avt, iq, ik, iwt = _inproj(
            x, sc1, sh1, norm1_g[l], w_in[l], lb_logits, q_norm_g[l], k_norm_g[l],
            idx_k_norm_g[l], idx_k_norm_b[l], layer=l, gt=gk, tm=_tile(s, 512))
        rec = _hgrn(q, f, v, g, rec_out_g[l], ts=_tile(s, 256))
        att = _attention(aq, ak, avt, iq, ik, iwt, rel_bias, attn_out_g[l], gk=gk)
        x1, h2, comb = _outproj(rec, att, x, g1, sc2, sh2, norm2_g[l], w_out[l],
                                w_rg[l], b_rg[l], w_re[l], b_re[l], tm=_tile(s, 512))
        x = _moe(h2, comb, x1, g2, w1[l], w3[l], w2[l], tm=_tile(s, 1024))
    return x
```

```python
import functools

import numpy as np
import jax
import jax.numpy as jnp
from jax import lax
from jax.experimental import pallas as pl
from jax.experimental.pallas import tpu as pltpu

F32 = jnp.float32
BF16 = jnp.bfloat16
I32 = jnp.int32

CHUNK = 64
QBLK = 128
EPS = 1e-6
REC_HEADS = 4
REC_DK = 128
REC_DV = 128
REC_W = REC_HEADS * REC_DV
ATT_HEADS = 8
ATT_DH = 64
ATT_W = ATT_HEADS * ATT_DH
IDX_HEADS = 8
IDX_DIM = 64
TOPK_MAX = 256
NUM_BUCKETS = 32
MAX_DISTANCE = 128
N_GROUPS = 4
EXPERTS_PER_GROUP = 4
N_EXPERTS = N_GROUPS * EXPERTS_PER_GROUP
D_EXPERT = 512

LANES = 128
V7X_VMEM_LIMIT = 56 * 1024 * 1024
F32_LOWEST = float(np.finfo(np.float32).min)
BISECT_CAP = 320
NEG = -1e30
LOG2E = float(np.log2(np.e))
N_PAIRS = ATT_HEADS // 2
ROUTE_OFF = N_GROUPS


def _dot(a, b):
    return jnp.dot(a, b, preferred_element_type=F32)


def _dot_nt(a, b):
    return lax.dot_general(a, b, (((1,), (1,)), ((), ())), preferred_element_type=F32)


def _dot_tn(a, b):
    return lax.dot_general(a, b, (((0,), (0,)), ((), ())), preferred_element_type=F32)


def _split_bf16(a):
    hi = a.astype(BF16)
    lo = (a - hi.astype(F32)).astype(BF16)
    return hi, lo


def _silu(a):
    return a * jax.nn.sigmoid(a)


def _const_spec(shape):
    nd = len(shape)
    return pl.BlockSpec(shape, lambda *_: (0,) * nd, pipeline_mode=pl.Buffered(1))


def _params(sem):
    return pltpu.CompilerParams(dimension_semantics=sem, vmem_limit_bytes=V7X_VMEM_LIMIT)


def _adaln_kernel(c_ref, w_ref, b_ref, o_ref):
    a_hi, a_lo = _split_bf16(_silu(c_ref[...]))
    w_hi, w_lo = _split_bf16(w_ref[...])
    o_ref[...] = _dot(a_hi, w_hi) + _dot(a_lo, w_hi) + _dot(a_hi, w_lo) + b_ref[...]


def _adaln(c, w, b):
    bsz, d = c.shape
    n = w.shape[1]
    rows = 16
    bn = 1024
    cp = jnp.zeros((rows, d), F32).at[:bsz].set(c)
    out = pl.pallas_call(
        _adaln_kernel,
        grid=(n // bn,),
        in_specs=[pl.BlockSpec((rows, d), lambda i: (0, 0)),
                  pl.BlockSpec((d, bn), lambda i: (0, i)),
                  pl.BlockSpec((1, bn), lambda i: (0, i))],
        out_specs=pl.BlockSpec((rows, bn), lambda i: (0, i)),
        out_shape=jax.ShapeDtypeStruct((rows, n), F32),
        compiler_params=_params(("parallel",)),
        name="adaln",
    )(cp, w, b.reshape(1, n))
    return out[:bsz]


def _inproj_kernel(x_ref, sc_ref, sh_ref, ng_ref, wrec_ref, watt_ref, widx_ref, lbl_ref,
                   qg_ref, kg_ref, ikg_ref, ikb_ref, pm_ref,
                   q_ref, f_ref, v_ref, g_ref, aq_ref, ak_ref, avt_ref, iq_ref, ik_ref, iwt_ref,
                   *, layer, gt):
    x = x_ref[...]
    tm = x.shape[0]
    ms = jnp.mean(x * x, axis=-1, keepdims=True)
    h = x * lax.rsqrt(ms + EPS) * ng_ref[...] * (1.0 + sc_ref[...]) + sh_ref[...]
    hb = h.astype(BF16)

    zr = _dot(hb, wrec_ref[...])
    q_ref[...] = _silu(zr[:, 0:REC_W]).astype(BF16)
    lbl = lbl_ref[...]
    e = jnp.exp(lbl - jnp.max(lbl, axis=0, keepdims=True))
    sm = e / jnp.sum(e, axis=0, keepdims=True)
    lb = jnp.sum(sm[0:layer + 1], axis=0, keepdims=True)
    f_ref[...] = lb + (1.0 - lb) * jax.nn.sigmoid(zr[:, REC_W:2 * REC_W])
    v_ref[...] = zr[:, 2 * REC_W:3 * REC_W].astype(BF16)
    g_ref[...] = _silu(zr[:, 3 * REC_W:4 * REC_W]).astype(BF16)

    za = _dot(hb, watt_ref[...])
    aq = za[:, 0:ATT_W]
    ak = za[:, ATT_W:2 * ATT_W]
    av = za[:, 2 * ATT_W:3 * ATT_W]
    pm = pm_ref[...]
    aqn = aq * lax.rsqrt(_dot((aq * aq).astype(BF16), pm) + EPS) * qg_ref[...]
    akn = ak * lax.rsqrt(_dot((ak * ak).astype(BF16), pm) + EPS) * kg_ref[...]
    for j in range(N_PAIRS):
        aq_ref[j] = aqn[:, j * LANES:(j + 1) * LANES].astype(BF16)
        ak_ref[j] = akn[:, j * LANES:(j + 1) * LANES].astype(BF16)
    for t in range(tm // gt):
        avt_ref[t] = av[t * gt:(t + 1) * gt, :].T.astype(BF16)

    zi = _dot(hb, widx_ref[...])
    for j in range(N_PAIRS):
        iq_ref[j] = zi[:, j * LANES:(j + 1) * LANES].astype(BF16)
    tail = zi[:, IDX_HEADS * IDX_DIM:IDX_HEADS * IDX_DIM + LANES]
    lane = lax.broadcasted_iota(I32, tail.shape, 1)
    is_k = lane < IDX_DIM
    mu = jnp.sum(jnp.where(is_k, tail, 0.0), axis=-1, keepdims=True) * (1.0 / IDX_DIM)
    dlt = jnp.where(is_k, tail - mu, 0.0)
    var = jnp.sum(dlt * dlt, axis=-1, keepdims=True) * (1.0 / IDX_DIM)
    ikn = dlt * lax.rsqrt(var + EPS) * ikg_ref[...] + ikb_ref[...]
    ik_ref[...] = jnp.where(is_k, ikn, pltpu.roll(ikn, IDX_DIM, axis=1)).astype(BF16)
    iwt_ref[...] = tail.T[IDX_DIM:IDX_DIM + IDX_HEADS, :] * (IDX_HEADS ** -0.5 * IDX_DIM ** -0.5)


def _inproj(x, sc1, sh1, norm_g, w_in, lb_logits, q_g, k_g, ik_g, ik_b, *, layer, gt, tm):
    bsz, s, d = x.shape
    n_rec = 4 * REC_W
    n_att = 3 * ATT_W
    n_idx = IDX_HEADS * IDX_DIM + LANES
    wb = w_in.astype(BF16)
    w_rec = wb[:, :n_rec]
    w_att = wb[:, n_rec:n_rec + n_att]
    w_idx = jnp.zeros((d, n_idx), BF16).at[:, :w_in.shape[1] - n_rec - n_att].set(wb[:, n_rec + n_att:])
    pm = jnp.asarray(np.kron(np.eye(ATT_HEADS), np.full((ATT_DH, ATT_DH), 1.0 / ATT_DH)), BF16)
    qg = jnp.tile(q_g, ATT_HEADS).reshape(1, ATT_W) * (ATT_DH ** -0.5 * LOG2E)
    kg = jnp.tile(k_g, ATT_HEADS).reshape(1, ATT_W)
    ikg = jnp.zeros((1, LANES), F32).at[0, :IDX_DIM].set(ik_g)
    ikb = jnp.zeros((1, LANES), F32).at[0, :IDX_DIM].set(ik_b)
    nl = lb_logits.shape[0]

    row = lambda w: pl.BlockSpec((None, tm, w), lambda b, i: (b, i, 0))
    pair = pl.BlockSpec((None, N_PAIRS, tm, LANES), lambda b, i: (b, 0, i, 0))
    vec = pl.BlockSpec((None, 1, d), lambda b, i: (b, 0, 0))
    out_shapes = (
        jax.ShapeDtypeStruct((bsz, s, REC_W), BF16),
        jax.ShapeDtypeStruct((bsz, s, REC_W), F32),
        jax.ShapeDtypeStruct((bsz, s, REC_W), BF16),
        jax.ShapeDtypeStruct((bsz, s, REC_W), BF16),
        jax.ShapeDtypeStruct((bsz, N_PAIRS, s, LANES), BF16),
        jax.ShapeDtypeStruct((bsz, N_PAIRS, s, LANES), BF16),
        jax.ShapeDtypeStruct((bsz, s // gt, ATT_W, gt), BF16),
        jax.ShapeDtypeStruct((bsz, N_PAIRS, s, LANES), BF16),
        jax.ShapeDtypeStruct((bsz, s, LANES), BF16),
        jax.ShapeDtypeStruct((bsz, IDX_HEADS, s), F32),
    )
    out_specs = (
        row(REC_W), row(REC_W), row(REC_W), row(REC_W), pair, pair,
        pl.BlockSpec((None, tm // gt, ATT_W, gt), lambda b, i: (b, i, 0, 0)),
        pair, row(LANES),
        pl.BlockSpec((None, IDX_HEADS, tm), lambda b, i: (b, 0, i)),
    )
    return pl.pallas_call(
        functools.partial(_inproj_kernel, layer=layer, gt=gt),
        grid=(bsz, s // tm),
        in_specs=[row(d), vec, vec, _const_spec((1, d)),
                  _const_spec((d, n_rec)), _const_spec((d, n_att)), _const_spec((d, n_idx)),
                  _const_spec((nl, REC_W)), _const_spec((1, ATT_W)), _const_spec((1, ATT_W)),
                  _const_spec((1, LANES)), _const_spec((1, LANES)), _const_spec((ATT_W, ATT_W))],
        out_specs=out_specs,
        out_shape=out_shapes,
        compiler_params=_params(("parallel", "parallel")),
        name="inproj",
    )(x, sc1.reshape(bsz, 1, d), sh1.reshape(bsz, 1, d), norm_g.reshape(1, d),
      w_rec, w_att, w_idx, lb_logits, qg, kg, ikg, ikb, pm)


N_LEVELS = 6


def _hgrn_tables():
    c = CHUNK
    w = np.zeros((N_LEVELS + 2, c, c), np.float32)
    am = np.zeros((N_LEVELS + 1, c, c), np.float32)
    t = np.arange(c)
    for m in range(N_LEVELS):
        hs = 1 << m
        blk = t // (2 * hs)
        upper = (t // hs) % 2 == 1
        ref = blk * 2 * hs + hs - 1
        for i in range(c):
            if upper[i]:
                w[m, i, ref[i] + 1:i + 1] = 1.0
            else:
                w[m, i, i + 1:ref[i] + 1] = 1.0
        am[m] = (blk[:, None] == blk[None, :]) & upper[:, None] & ~upper[None, :]
    w[N_LEVELS] = np.tril(np.ones((c, c)))
    w[N_LEVELS + 1] = np.triu(np.ones((c, c)), 1)
    am[N_LEVELS] = np.eye(c)
    w = w.reshape((N_LEVELS + 2) * c, c)
    return np.concatenate([w, w], axis=1), am


HGRN_UNROLL = 2


def _hgrn_kernel(q_ref, f_ref, v_ref, g_ref, og_ref, ww_ref, am_ref, o_ref, st_ref, ex_ref, *, n_chunks):
    @pl.when(pl.program_id(1) == 0)
    def _():
        st_ref[...] = jnp.zeros_like(st_ref)

    ww = ww_ref[...]
    c = CHUNK
    tbit = lax.broadcasted_iota(I32, (c, REC_DK), 0)
    items = [(cc, hd) for cc in range(HGRN_UNROLL) for hd in range(REC_HEADS)]

    def chunks(ci, carry):
        def blk(ref, it):
            r0 = pl.multiple_of((ci * HGRN_UNROLL + it[0]) * c, c)
            return ref.at[pl.ds(r0, c), it[1] * REC_DK:(it[1] + 1) * REC_DK]

        def ex(i, part):
            return ex_ref[i, part * c:(part + 1) * c, :]

        f = [blk(f_ref, it)[...] for it in items]
        q = [blk(q_ref, it)[...].astype(F32) for it in items]
        v = [blk(v_ref, it)[...] for it in items]
        k = [1.0 - fi for fi in f]
        for i, fi in enumerate(f):
            lf_hi, lf_lo = _split_bf16(jnp.log(fi))
            ex_ref[i] = jnp.exp(_dot(ww, jnp.concatenate([lf_hi, lf_lo], axis=0)))
        a = [am_ref[N_LEVELS] * _dot_nt(q[i].astype(BF16), k[i].astype(BF16)) for i in range(len(items))]
        for m in range(N_LEVELS):
            upper = ((tbit >> m) & 1) == 1
            for i in range(len(items)):
                tm_ = (ex(i, m) * jnp.where(upper, q[i], k[i])).astype(BF16)
                a[i] = a[i] + am_ref[m] * _dot_nt(tm_, tm_)
        qb = [(q[i] * ex(i, N_LEVELS)).astype(BF16) for i in range(len(items))]
        kb = [(k[i] * ex(i, N_LEVELS + 1)).astype(BF16) for i in range(len(items))]
        intra = [_dot(a[i].astype(BF16), v[i]) for i in range(len(items))]
        ut = [_dot_tn(v[i], kb[i]) for i in range(len(items))]
        for hd in range(REC_HEADS):
            st = st_ref[hd]
            for cc in range(HGRN_UNROLL):
                i = cc * REC_HEADS + hd
                o = intra[i] + _dot_nt(qb[i], st.astype(BF16))
                st = st * ex_ref[i, (N_LEVELS + 1) * c - 1:(N_LEVELS + 1) * c, :] + ut[i]
                ms = jnp.mean(o * o, axis=-1, keepdims=True)
                gate = blk(g_ref, items[i])[...].astype(F32)
                y = o * lax.rsqrt(ms + EPS) * og_ref[:, hd * REC_DV:(hd + 1) * REC_DV] * gate
                blk(o_ref, items[i])[...] = y.astype(BF16)
            st_ref[hd] = st
        return carry

    lax.fori_loop(0, n_chunks // HGRN_UNROLL, chunks, 0)


def _hgrn(q, f, v, g, out_g, *, ts):
    bsz, s, _ = q.shape
    ww_np, am_np = _hgrn_tables()
    ww = jnp.asarray(ww_np, BF16)
    am = jnp.asarray(am_np, F32)
    row = pl.BlockSpec((None, ts, REC_W), lambda b, i: (b, i, 0))
    return pl.pallas_call(
        functools.partial(_hgrn_kernel, n_chunks=ts // CHUNK),
        grid=(bsz, s // ts),
        in_specs=[row, row, row, row, _const_spec((1, REC_W)),
                  _const_spec(ww.shape), _const_spec(am.shape)],
        out_specs=row,
        out_shape=jax.ShapeDtypeStruct((bsz, s, REC_W), BF16),
        scratch_shapes=[pltpu.VMEM((REC_HEADS, REC_DV, REC_DK), F32),
                        pltpu.VMEM((HGRN_UNROLL * REC_HEADS, (N_LEVELS + 2) * CHUNK, REC_DK), F32)],
        compiler_params=_params(("parallel", "arbitrary")),
        name="hgrn",
    )(q, f, v, g, out_g.reshape(1, REC_W), ww, am)


def _t5_bucket(rel):
    nb = NUM_BUCKETS // 2
    max_exact = nb // 2
    ret = jnp.where(rel > 0, nb, 0)
    n = jnp.abs(rel)
    nf = jnp.maximum(n, 1).astype(F32)
    large = max_exact + (jnp.log(nf / max_exact) / np.log(MAX_DISTANCE / max_exact)
                         * (nb - max_exact)).astype(I32)
    large = jnp.minimum(large, nb - 1)
    return ret + jnp.where(n < max_exact, n, large)


def _bias_slots(rel_bias):
    n_rel = 3 * QBLK
    rel = jnp.arange(n_rel, dtype=I32) - 2 * QBLK
    tab = rel_bias[_t5_bucket(rel)].T
    far = rel_bias[_t5_bucket(jnp.full((1,), -2 * QBLK - 1, I32))].T
    rev = tab[:, ::-1]
    skew = jnp.tile(rev, (1, 2 * QBLK))[:, :2 * QBLK * (n_rel - 1)].reshape(ATT_HEADS, 2 * QBLK, n_rel - 1)
    near = skew[:, :, 2 * QBLK - 1:]
    nb = ((near - far[:, :, None]) * LOG2E).reshape(ATT_HEADS, 2, QBLK, QBLK)
    z = jnp.zeros((ATT_HEADS, 1, QBLK, QBLK), F32)
    return jnp.concatenate([z, nb, z], axis=1)


def _attn_kernel(aq_ref, ak_ref, avt_ref, iq_ref, ik_ref, iwt_ref, bias_ref, og_ref, o_ref,
                 key_ref, mb_ref, thr_ref, acc_ref, m_ref, l_ref, al_ref, s_ref, mx_ref, p_ref, qbd_ref, ot_ref, tie_ref,
                 *, gs, gk, k_sel):
    bi = pl.program_id(1)
    n_s = bi // (gs // QBLK) + 1
    lane = lax.broadcasted_iota(I32, (QBLK, LANES), 1)

    def block_diag(xq):
        zero = jnp.zeros_like(xq)
        return jnp.concatenate([jnp.where(lane < ATT_DH, xq, zero), jnp.where(lane >= ATT_DH, xq, zero)], axis=0)

    def srows(g):
        return pl.ds(pl.multiple_of(g * gs, gs), gs)

    for j in range(N_PAIRS):
        qbd_ref[j] = block_diag(iq_ref[j])
    w = iwt_ref[...]

    def group_scores(g):
        sc = _dot_nt(ik_ref[srows(g), :], qbd_ref[...].reshape(N_PAIRS * 2 * QBLK, LANES))
        acc = jnp.zeros((gs, LANES), F32)
        for hd in range(IDX_HEADS):
            acc = acc + jnp.maximum(sc[:, hd * LANES:(hd + 1) * LANES], 0.0) * w[hd:hd + 1, :]
        return acc

    def stats(sc, c):
        s3 = sc.reshape(gs // 8, 8, LANES)
        return (jnp.minimum(c[0], jnp.min(jnp.where(s3 == -jnp.inf, jnp.inf, s3), axis=0)),
                jnp.maximum(c[1], jnp.max(s3, axis=0)),
                c[2] + jnp.sum(jnp.where(s3 >= 0.0, 1, 0), axis=0),
                c[3] + jnp.sum(jnp.where(s3 > 0.0, 1, 0), axis=0))

    def idx_body(g, carry):
        sc = group_scores(g)
        key_ref[srows(g), :] = sc
        return stats(sc, carry)

    z8 = jnp.zeros((8, LANES), I32)
    st8 = lax.fori_loop(0, n_s - 1, idx_body,
                        (jnp.full((8, LANES), jnp.inf, F32), jnp.full((8, LANES), -jnp.inf, F32), z8, z8))

    last = srows(n_s - 1)
    spos = (n_s - 1) * gs + lax.broadcasted_iota(I32, (gs, LANES), 0)
    tpos = bi * QBLK + lax.broadcasted_iota(I32, (gs, LANES), 1)
    sc_last = jnp.where((spos // CHUNK) <= (tpos // CHUNK), group_scores(n_s - 1), -jnp.inf)
    key_ref[last, :] = sc_last
    mn8, mx8, ge8, gt8 = stats(sc_last, st8)

    thr_ref[...] = jnp.full(thr_ref.shape, F32_LOWEST, F32)
    tie_ref[0] = 0

    @pl.when((2 * bi + 2) * CHUNK > k_sel)
    def _():
        def groups(fn, init):
            return lax.fori_loop(0, n_s, lambda g, c: fn(key_ref[srows(g), :].reshape(gs // 8, 8, LANES), c), init)

        def count_ge(cand):
            acc = groups(lambda sc, c: c + jnp.sum(jnp.where(sc >= cand, 1, 0), axis=0), jnp.zeros((8, LANES), I32))
            return jnp.sum(acc, axis=0, keepdims=True)

        c0_ge = jnp.sum(ge8, axis=0, keepdims=True)
        c0_gt = jnp.sum(gt8, axis=0, keepdims=True)
        mn = jnp.min(mn8, axis=0, keepdims=True)
        mx = jnp.max(mx8, axis=0, keepdims=True)
        above = mx + (jnp.abs(mx) * 2.0 ** -20 + 1e-30)
        n_adm = (2 * bi + 1 + lax.broadcasted_iota(I32, (1, LANES), 1) // CHUNK) * CHUNK
        pos = c0_gt >= k_sel
        non_neg = c0_ge >= k_sel
        lo = jnp.where(non_neg, 0.0, mn)
        cnt_lo = jnp.where(non_neg, c0_ge, n_adm)
        hi = jnp.where(pos, above, 0.0)

        def midpoint(lo, hi):
            return 0.5 * lo + 0.5 * hi

        def unsettled(lo, hi, cnt_lo):
            mid = midpoint(lo, hi)
            return jnp.where(cnt_lo > k_sel, jnp.where(mid > lo, jnp.where(mid < hi, 1.0, 0.0), 0.0), 0.0)

        def cond(st):
            return jnp.logical_and(st[3] > 0.5, st[4] < BISECT_CAP)

        def halve(lo, hi, cnt_lo):
            act = unsettled(lo, hi, cnt_lo) > 0.5
            mid = midpoint(lo, hi)
            c = count_ge(mid)
            up = jnp.logical_and(act, c >= k_sel)
            dn = jnp.logical_and(act, c < k_sel)
            return jnp.where(up, mid, lo), jnp.where(dn, mid, hi), jnp.where(up, c, cnt_lo)

        def body(st):
            lo, hi, cnt_lo = halve(*halve(*st[:3]))
            return lo, hi, cnt_lo, jnp.max(unsettled(lo, hi, cnt_lo)), st[4] + 2

        st = lax.while_loop(cond, body, (lo, hi, cnt_lo, jnp.max(unsettled(lo, hi, cnt_lo)), jnp.int32(0)))
        thr_ref[...] = jnp.broadcast_to(st[0], thr_ref.shape)
        tie_ref[0] = (jnp.max(jnp.where(st[2] > k_sel, 1.0, 0.0)) > 0.5).astype(I32)

    thr = thr_ref[0:1, :]

    @pl.when(tie_ref[0] == 0)
    def _():
        def mb_body(g, carry):
            mb_ref[srows(g), :] = jnp.where(key_ref[srows(g), :] >= thr, 0.0, NEG)
            return carry

        lax.fori_loop(0, n_s, mb_body, 0)

    @pl.when(tie_ref[0] != 0)
    def _():
        def count_gt(g, acc):
            sc = key_ref[srows(g), :].reshape(gs // 8, 8, LANES)
            return acc + jnp.sum(jnp.where(sc > thr, 1, 0), axis=0)

        n_gt = jnp.sum(lax.fori_loop(0, n_s, count_gt, jnp.zeros((8, LANES), I32)), axis=0, keepdims=True)
        room = (k_sel - n_gt).astype(F32)
        r = lax.broadcasted_iota(I32, (gs, gs), 0)
        c = lax.broadcasted_iota(I32, (gs, gs), 1)
        tri = jnp.where(c <= r, 1.0, 0.0).astype(BF16)

        def mb_body(g, seen):
            sc = key_ref[srows(g), :]
            eq = jnp.where(sc == thr, 1.0, 0.0)
            rank = _dot(tri, eq.astype(BF16)) + seen
            keep = jnp.where(sc > thr, 1.0, jnp.where(rank <= room, eq, 0.0))
            mb_ref[srows(g), :] = jnp.where(keep > 0.5, 0.0, NEG)
            return seen + jnp.sum(eq, axis=0, keepdims=True)

        lax.fori_loop(0, n_s, mb_body, jnp.zeros((1, LANES), F32))

    spg = gk // QBLK
    n_g = bi // spg + 1
    for j in range(N_PAIRS):
        qbd_ref[j] = block_diag(aq_ref[j])
    m_ref[...] = jnp.full(m_ref.shape, NEG, F32)
    l_ref[...] = jnp.zeros_like(l_ref)
    acc_ref[...] = jnp.zeros_like(acc_ref)
    al_ref[...] = jnp.ones_like(al_ref)
    p_ref[1] = jnp.zeros(p_ref.shape[1:], BF16)

    def rows(g):
        return pl.ds(pl.multiple_of(g * gk, gk), gk)

    def scores(t, slot):
        mb = mb_ref[rows(t), :]
        mb2 = jnp.concatenate([mb, mb], axis=1)
        for j in range(N_PAIRS):
            s = _dot_nt(ak_ref[j, rows(t), :], qbd_ref[j]) + mb2
            s_ref[slot, j] = s
            mx_ref[slot, j] = jnp.max(s, axis=0, keepdims=True)

    def pv(t, slot):
        return [_dot(avt_ref[t, j * LANES:(j + 1) * LANES, :], p_ref[slot, j]) for j in range(N_PAIRS)]

    def step(g, cur):
        prv = 1 - cur

        @pl.when(g >= n_g - 2)
        def _():
            for j in range(N_PAIRS):
                parts = []
                for st in range(spg):
                    slot = jnp.clip(g * spg + st - bi + 2, 0, 3)
                    parts.append(jnp.concatenate([bias_ref[2 * j, slot], bias_ref[2 * j + 1, slot]], axis=1))
                s = s_ref[cur, j] + jnp.concatenate(parts, axis=0)
                s_ref[cur, j] = s
                mx_ref[cur, j] = jnp.max(s, axis=0, keepdims=True)

        al_old = [al_ref[j] for j in range(N_PAIRS)]
        for j in range(N_PAIRS):
            m_old = m_ref[j]
            m_new = jnp.maximum(m_old, mx_ref[cur, j])
            p = jnp.exp2(s_ref[cur, j] - m_new)
            alpha = jnp.exp2(m_old - m_new)
            al_ref[j] = alpha
            l_ref[j] = l_ref[j] * alpha + jnp.sum(p, axis=0, keepdims=True)
            m_ref[j] = m_new
            p_ref[cur, j] = p.astype(BF16)
        o_prev = pv(jnp.maximum(g - 1, 0), prv)
        scores(jnp.minimum(g + 1, n_g - 1), prv)
        for j in range(N_PAIRS):
            acc_ref[j] = acc_ref[j] * al_old[j] + o_prev[j]

    scores(0, 0)

    def pipe_body(u, c):
        step(2 * u, 0)

        @pl.when(2 * u + 1 < n_g)
        def _():
            step(2 * u + 1, 1)
        return c

    lax.fori_loop(0, (n_g + 1) // 2, pipe_body, 0)
    o_last = pv(n_g - 1, (n_g - 1) % 2)

    for j in range(N_PAIRS):
        acc = acc_ref[j] * al_ref[j] + o_last[j]
        l = l_ref[j]
        o0 = acc[0:ATT_DH, 0:LANES] / l[:, 0:LANES]
        o1 = acc[ATT_DH:2 * ATT_DH, LANES:2 * LANES] / l[:, LANES:2 * LANES]
        ot_ref[j * LANES:(j + 1) * LANES, :] = jnp.concatenate([o0, o1], axis=0)

    ot = ot_ref[...]
    og = og_ref[...]
    outs = []
    for hd in range(ATT_HEADS):
        oh = ot[hd * ATT_DH:(hd + 1) * ATT_DH, :]
        ms = jnp.mean(oh * oh, axis=0, keepdims=True)
        outs.append(oh * lax.rsqrt(ms + EPS) * og[hd * ATT_DH:(hd + 1) * ATT_DH, :])
    o_ref[...] = jnp.concatenate(outs, axis=0).T.astype(BF16)


def _attention(aq, ak, avt, iq, ik, iwt, rel_bias, out_g, *, gk):
    bsz, _, s, _ = aq.shape
    k_sel = min(TOPK_MAX, s // 4)
    bias = _bias_slots(rel_bias)
    qpair = pl.BlockSpec((None, N_PAIRS, QBLK, LANES), lambda b, i: (b, 0, i, 0))
    full = lambda shp: pl.BlockSpec((None,) + shp, lambda b, i: (b,) + (0,) * len(shp),
                                    pipeline_mode=pl.Buffered(1))
    return pl.pallas_call(
        functools.partial(_attn_kernel, gs=_tile(s, 2 * gk), gk=gk, k_sel=k_sel),
        grid=(bsz, s // QBLK),
        in_specs=[qpair, full((N_PAIRS, s, LANES)), full((s // gk, ATT_W, gk)), qpair,
                  full((s, LANES)), pl.BlockSpec((None, IDX_HEADS, QBLK), lambda b, i: (b, 0, i)),
                  _const_spec(bias.shape), _const_spec((ATT_W, 1))],
        out_specs=pl.BlockSpec((None, QBLK, ATT_W), lambda b, i: (b, i, 0)),
        out_shape=jax.ShapeDtypeStruct((bsz, s, ATT_W), BF16),
        scratch_shapes=[pltpu.VMEM((s, LANES), F32),
                        pltpu.VMEM((s, LANES), F32),
                        pltpu.VMEM((8, LANES), F32),
                        pltpu.VMEM((N_PAIRS, 2 * ATT_DH, 2 * LANES), F32),
                        pltpu.VMEM((N_PAIRS, 1, 2 * LANES), F32),
                        pltpu.VMEM((N_PAIRS, 1, 2 * LANES), F32),
                        pltpu.VMEM((N_PAIRS, 1, 2 * LANES), F32),
                        pltpu.VMEM((2, N_PAIRS, gk, 2 * LANES), F32),
                        pltpu.VMEM((2, N_PAIRS, 1, 2 * LANES), F32),
                        pltpu.VMEM((2, N_PAIRS, gk, 2 * LANES), BF16),
                        pltpu.VMEM((N_PAIRS, 2 * QBLK, LANES), BF16),
                        pltpu.VMEM((ATT_W, LANES), F32),
                        pltpu.SMEM((1,), I32)],
        compiler_params=_params(("parallel", "arbitrary")),
        name="attn",
    )(aq, ak, avt, iq, ik, iwt, bias, out_g.reshape(ATT_W, 1))


def _outproj_kernel(rec_ref, att_ref, x_ref, g1_ref, sc_ref, sh_ref, ng_ref, wo_ref, wr_ref, br_ref,
                    x1_ref, h2_ref, comb_ref):
    mix = _dot(rec_ref[...], wo_ref[0:REC_W, :]) + _dot(att_ref[...], wo_ref[REC_W:REC_W + ATT_W, :])
    x1 = x_ref[...] + g1_ref[...] * mix
    x1_ref[...] = x1
    ms = jnp.mean(x1 * x1, axis=-1, keepdims=True)
    h2 = (x1 * lax.rsqrt(ms + EPS) * ng_ref[...] * (1.0 + sc_ref[...]) + sh_ref[...]).astype(BF16)
    h2_ref[...] = h2

    lg = _dot(h2, wr_ref[...]) + br_ref[...]
    lane = lax.broadcasted_iota(I32, lg.shape, 1)
    big = jnp.int32(2 * LANES)
    is_g = lane < N_GROUPS
    gl = jnp.where(is_g, lg, -jnp.inf)
    gmax = jnp.max(gl, axis=-1, keepdims=True)
    gate = 1.0 / jnp.sum(jnp.where(is_g, jnp.exp(lg - gmax), 0.0), axis=-1, keepdims=True)
    gtop = jnp.min(jnp.where(gl == gmax, lane, big), axis=-1, keepdims=True)
    e_lo = ROUTE_OFF + EXPERTS_PER_GROUP * gtop
    el = jnp.where((lane >= e_lo) & (lane < e_lo + EXPERTS_PER_GROUP), lg, -jnp.inf)
    v1 = jnp.max(el, axis=-1, keepdims=True)
    i1 = jnp.min(jnp.where(el == v1, lane, big), axis=-1, keepdims=True)
    el2 = jnp.where(lane == i1, -jnp.inf, el)
    v2 = jnp.max(el2, axis=-1, keepdims=True)
    i2 = jnp.min(jnp.where(el2 == v2, lane, big), axis=-1, keepdims=True)
    e2 = jnp.exp(v2 - v1)
    w1 = gate / (1.0 + e2)
    w2 = gate * e2 / (1.0 + e2)
    comb = jnp.where(lane == i1, w1, 0.0) + jnp.where(lane == i2, w2, 0.0)
    comb_ref[...] = jnp.where(lane == 0, gtop.astype(F32), comb)


def _outproj(rec, att, x, g1, sc2, sh2, norm_g, w_out, w_rg, b_rg, w_re, b_re, *, tm):
    bsz, s, d = x.shape
    wr = jnp.zeros((d, LANES), F32).at[:, :N_GROUPS].set(w_rg).at[:, ROUTE_OFF:ROUTE_OFF + N_EXPERTS].set(w_re)
    br = jnp.zeros((1, LANES), F32).at[0, :N_GROUPS].set(b_rg).at[0, ROUTE_OFF:ROUTE_OFF + N_EXPERTS].set(b_re)
    row = lambda w: pl.BlockSpec((None, tm, w), lambda b, i: (b, i, 0))
    vec = pl.BlockSpec((None, 1, d), lambda b, i: (b, 0, 0))
    return pl.pallas_call(
        _outproj_kernel,
        grid=(bsz, s // tm),
        in_specs=[row(REC_W), row(ATT_W), row(d), vec, vec, vec, _const_spec((1, d)),
                  _const_spec((REC_W + ATT_W, d)), _const_spec((d, LANES)), _const_spec((1, LANES))],
        out_specs=(row(d), row(d), row(LANES)),
        out_shape=(jax.ShapeDtypeStruct((bsz, s, d), F32),
                   jax.ShapeDtypeStruct((bsz, s, d), BF16),
                   jax.ShapeDtypeStruct((bsz, s, LANES), F32)),
        compiler_params=_params(("parallel", "parallel")),
        name="outproj",
    )(rec, att, x, g1.reshape(bsz, 1, d), sc2.reshape(bsz, 1, d), sh2.reshape(bsz, 1, d),
      norm_g.reshape(1, d), w_out.astype(BF16), wr.astype(BF16), br)


MOE_ALIGN = 16
MOE_LANE_SHIFT = 32
MOE_CHUNK = 320
MOE_RB = 256


def _moe_kernel(h_ref, comb_ref, x1_ref, g2_ref, w1_ref, w3_ref, w2_ref, o_ref,
                p_ref, pt_ref, hs_ref, cs_ref, ys_ref, seg_ref, *, mc):
    e = pl.program_id(2)
    tm = h_ref.shape[0]
    npad = hs_ref.shape[0]

    @pl.when(e == 0)
    def _():
        comb = comb_ref[...]
        lane = lax.broadcasted_iota(I32, (tm, LANES), 1)
        gid = comb[:, 0:1].astype(I32)
        oh = jnp.where(lane == gid, 1.0, 0.0)
        ohb = oh.astype(BF16)
        pre = []
        for rb in range(tm // MOE_RB):
            r = rb * MOE_RB + lax.broadcasted_iota(I32, (MOE_RB, tm), 0)
            c = lax.broadcasted_iota(I32, (MOE_RB, tm), 1)
            pre.append(_dot(jnp.where(c < r, 1.0, 0.0).astype(BF16), ohb))
        prefix = jnp.concatenate(pre, axis=0)
        cnt = jnp.sum(oh, axis=0, keepdims=True).astype(I32)
        cnt_al = ((cnt + (MOE_ALIGN - 1)) // MOE_ALIGN) * MOE_ALIGN
        base = jnp.sum(jnp.where(lane < gid, cnt_al.astype(F32), 0.0), axis=-1, keepdims=True)
        rank = jnp.sum(prefix * oh, axis=-1, keepdims=True)
        pos = (base + rank).astype(I32)
        pos_row = jnp.broadcast_to(pos.astype(F32), (tm, LANES)).T[0:1, :].astype(I32)
        for rb in range(tm // MOE_RB):
            sl = slice(rb * MOE_RB, (rb + 1) * MOE_RB)
            coln = lax.broadcasted_iota(I32, (MOE_RB, npad), 1)
            pt_ref[sl, :] = jnp.where(coln == pos[sl], 1.0, 0.0).astype(BF16)
        for rb in range(npad // LANES):
            sl = slice(rb * LANES, (rb + 1) * LANES)
            rown = rb * LANES + lax.broadcasted_iota(I32, (LANES, tm), 0)
            p_ref[sl, :] = jnp.where(rown == pos_row, 1.0, 0.0).astype(BF16)
        c_hi = comb.astype(BF16)
        c_mid, c_lo = _split_bf16(comb - c_hi.astype(F32))
        packed = (c_hi.astype(F32) + pltpu.roll(c_mid.astype(F32), MOE_LANE_SHIFT, axis=1)
                  + pltpu.roll(c_lo.astype(F32), 2 * MOE_LANE_SHIFT, axis=1)).astype(BF16)
        srt = _dot(p_ref[...], jnp.concatenate([h_ref[...], packed], axis=1))
        d = h_ref.shape[1]
        hs_ref[...] = srt[:, :d].astype(BF16)
        cp = srt[:, d:]
        cs_ref[...] = (cp + pltpu.roll(cp, LANES - MOE_LANE_SHIFT, axis=1)
                       + pltpu.roll(cp, LANES - 2 * MOE_LANE_SHIFT, axis=1))
        ys_ref[...] = jnp.zeros_like(ys_ref)
        start = jnp.int32(0)
        for g in range(N_GROUPS):
            seg_ref[g] = start
            seg_ref[N_GROUPS + g] = cnt[0, g]
            start = start + cnt_al[0, g]

    grp = e // EXPERTS_PER_GROUP
    start = seg_ref[grp]
    n_rows = seg_ref[N_GROUPS + grp]

    def chunk(ci, carry):
        rs = pl.ds(pl.multiple_of(start + ci * mc, MOE_ALIGN), mc)
        hb = hs_ref[rs, :]
        cw = cs_ref[rs, :]
        lane = lax.broadcasted_iota(I32, cw.shape, 1)
        col = jnp.sum(jnp.where(lane == e + ROUTE_OFF, cw, 0.0), axis=-1, keepdims=True)
        he = _silu(_dot(hb, w1_ref[...])) * _dot(hb, w3_ref[...]) * col
        ys_ref[rs, :] += _dot(he.astype(BF16), w2_ref[...])
        return carry

    lax.fori_loop(0, (n_rows + mc - 1) // mc, chunk, 0)

    @pl.when(e == N_EXPERTS - 1)
    def _():
        o_ref[...] = x1_ref[...] + g2_ref[...] * _dot(pt_ref[...], ys_ref[...].astype(BF16))


def _moe(h2, comb, x1, g2, w1, w3, w2, *, tm):
    bsz, s, d = x1.shape
    mc = MOE_CHUNK
    npad = -(-(tm + N_GROUPS * MOE_ALIGN + mc) // LANES) * LANES
    row = lambda w, **kw: pl.BlockSpec((None, tm, w), lambda b, i, e: (b, i, 0), **kw)
    once = dict(pipeline_mode=pl.Buffered(1))
    return pl.pallas_call(
        functools.partial(_moe_kernel, mc=mc),
        grid=(bsz, s // tm, N_EXPERTS),
        in_specs=[row(d, **once), row(LANES), row(d, **once),
                  pl.BlockSpec((None, 1, d), lambda b, i, e: (b, 0, 0)),
                  pl.BlockSpec((None, d, D_EXPERT), lambda b, i, e: (e, 0, 0)),
                  pl.BlockSpec((None, d, D_EXPERT), lambda b, i, e: (e, 0, 0)),
                  pl.BlockSpec((None, D_EXPERT, d), lambda b, i, e: (e, 0, 0))],
        out_specs=row(d),
        out_shape=jax.ShapeDtypeStruct((bsz, s, d), F32),
        scratch_shapes=[pltpu.VMEM((npad, tm), BF16),
                        pltpu.VMEM((tm, npad), BF16),
                        pltpu.VMEM((npad, d), BF16),
                        pltpu.VMEM((npad, LANES), F32),
                        pltpu.VMEM((npad, d), F32),
                        pltpu.SMEM((2 * N_GROUPS,), I32)],
        compiler_params=_params(("parallel", "parallel", "arbitrary")),
        name="moe",
    )(h2, comb, x1, g2.reshape(bsz, 1, d), w1.astype(BF16), w3.astype(BF16), w2.astype(BF16))


def _tile(s, pref):
    t = min(s, pref)
    assert s % t == 0
    return t


def kernel(x, c, w_ada, b_ada, norm1_g, norm2_g, w_in, lb_logits, rec_out_g, q_norm_g, k_norm_g,
           idx_k_norm_g, idx_k_norm_b, attn_out_g, rel_bias, w_out, w_rg, b_rg, w_re, b_re, w1, w3, w2):
    bsz, s, d = x.shape
    depth = w_ada.shape[0]
    gk = _tile(s, 256)
    for l in range(depth):
        mod = _adaln(c, w_ada[l], b_ada[l])
        sh1, sc1, g1, sh2, sc2, g2 = jnp.split(mod, 6, axis=-1)
        q, f, v, g, aq, ak, avt, iq, ik, iwt = _inproj(
            x, sc1, sh1, norm1_g[l], w_in[l], lb_logits, q_norm_g[l], k_norm_g[l],
            idx_k_norm_g[l], idx_k_norm_b[l], layer=l, gt=gk, tm=_tile(s, 512))
        rec = _hgrn(q, f, v, g, rec_out_g[l], ts=_tile(s, 256))
        att = _attention(aq, ak, avt, iq, ik, iwt, rel_bias, attn_out_g[l], gk=gk)
        x1, h2, comb = _outproj(rec, att, x, g1, sc2, sh2, norm2_g[l], w_out[l],
                                w_rg[l], b_rg[l], w_re[l], b_re[l], tm=_tile(s, 512))
        x = _moe(h2, comb, x1, g2, w1[l], w3[l], w2[l], tm=_tile(s, 1024))
    return x
```

```python
import functools

import numpy as np
import jax
import jax.numpy as jnp
from jax import lax
from jax.experimental import pallas as pl
from jax.experimental.pallas import tpu as pltpu

F32 = jnp.float32
BF16 = jnp.bfloat16
I32 = jnp.int32

CHUNK = 64
QBLK = 128
EPS = 1e-6
REC_HEADS = 4
REC_DK = 128
REC_DV = 128
REC_W = REC_HEADS * REC_DV
ATT_HEADS = 8
ATT_DH = 64
ATT_W = ATT_HEADS * ATT_DH
IDX_HEADS = 8
IDX_DIM = 64
TOPK_MAX = 256
NUM_BUCKETS = 32
MAX_DISTANCE = 128
N_GROUPS = 4
EXPERTS_PER_GROUP = 4
N_EXPERTS = N_GROUPS * EXPERTS_PER_GROUP
D_EXPERT = 512

LANES = 128
V7X_VMEM_LIMIT = 56 * 1024 * 1024
F32_LOWEST = float(np.finfo(np.float32).min)
BISECT_CAP = 320
NEG = -1e30
LOG2E = float(np.log2(np.e))
N_PAIRS = ATT_HEADS // 2
ROUTE_OFF = N_GROUPS


def _dot(a, b):
    return jnp.dot(a, b, preferred_element_type=F32)


def _dot_nt(a, b):
    return lax.dot_general(a, b, (((1,), (1,)), ((), ())), preferred_element_type=F32)


def _dot_tn(a, b):
    return lax.dot_general(a, b, (((0,), (0,)), ((), ())), preferred_element_type=F32)


def _split_bf16(a):
    hi = a.astype(BF16)
    lo = (a - hi.astype(F32)).astype(BF16)
    return hi, lo


def _silu(a):
    return a * jax.nn.sigmoid(a)


def _const_spec(shape):
    nd = len(shape)
    return pl.BlockSpec(shape, lambda *_: (0,) * nd, pipeline_mode=pl.Buffered(1))


def _params(sem):
    return pltpu.CompilerParams(dimension_semantics=sem, vmem_limit_bytes=V7X_VMEM_LIMIT)


def _adaln_kernel(c_ref, w_ref, b_ref, o_ref):
    a_hi, a_lo = _split_bf16(_silu(c_ref[...]))
    w_hi, w_lo = _split_bf16(w_ref[...])
    o_ref[...] = _dot(a_hi, w_hi) + _dot(a_lo, w_hi) + _dot(a_hi, w_lo) + b_ref[...]


def _adaln(c, w, b):
    bsz, d = c.shape
    n = w.shape[1]
    rows = 16
    bn = 1024
    cp = jnp.zeros((rows, d), F32).at[:bsz].set(c)
    out = pl.pallas_call(
        _adaln_kernel,
        grid=(n // bn,),
        in_specs=[pl.BlockSpec((rows, d), lambda i: (0, 0)),
                  pl.BlockSpec((d, bn), lambda i: (0, i)),
                  pl.BlockSpec((1, bn), lambda i: (0, i))],
        out_specs=pl.BlockSpec((rows, bn), lambda i: (0, i)),
        out_shape=jax.ShapeDtypeStruct((rows, n), F32),
        compiler_params=_params(("parallel",)),
        name="adaln",
    )(cp, w, b.reshape(1, n))
    return out[:bsz]


def _inproj_kernel(x_ref, sc_ref, sh_ref, ng_ref, wrec_ref, watt_ref, widx_ref, lbl_ref,
                   qg_ref, kg_ref, ikg_ref, ikb_ref, pm_ref,
                   q_ref, f_ref, v_ref, g_ref, aq_ref, ak_ref, avt_ref, iq_ref, ik_ref, iwt_ref,
                   *, layer, gt):
    x = x_ref[...]
    tm = x.shape[0]
    ms = jnp.mean(x * x, axis=-1, keepdims=True)
    h = x * lax.rsqrt(ms + EPS) * ng_ref[...] * (1.0 + sc_ref[...]) + sh_ref[...]
    hb = h.astype(BF16)

    zr = _dot(hb, wrec_ref[...])
    q_ref[...] = _silu(zr[:, 0:REC_W]).astype(BF16)
    lbl = lbl_ref[...]
    e = jnp.exp(lbl - jnp.max(lbl, axis=0, keepdims=True))
    sm = e / jnp.sum(e, axis=0, keepdims=True)
    lb = jnp.sum(sm[0:layer + 1], axis=0, keepdims=True)
    f_ref[...] = lb + (1.0 - lb) * jax.nn.sigmoid(zr[:, REC_W:2 * REC_W])
    v_ref[...] = zr[:, 2 * REC_W:3 * REC_W].astype(BF16)
    g_ref[...] = _silu(zr[:, 3 * REC_W:4 * REC_W]).astype(BF16)

    za = _dot(hb, watt_ref[...])
    aq = za[:, 0:ATT_W]
    ak = za[:, ATT_W:2 * ATT_W]
    av = za[:, 2 * ATT_W:3 * ATT_W]
    pm = pm_ref[...]
    aqn = aq * lax.rsqrt(_dot((aq * aq).astype(BF16), pm) + EPS) * qg_ref[...]
    akn = ak * lax.rsqrt(_dot((ak * ak).astype(BF16), pm) + EPS) * kg_ref[...]
    for j in range(N_PAIRS):
        aq_ref[j] = aqn[:, j * LANES:(j + 1) * LANES].astype(BF16)
        ak_ref[j] = akn[:, j * LANES:(j + 1) * LANES].astype(BF16)
    for t in range(tm // gt):
        avt_ref[t] = av[t * gt:(t + 1) * gt, :].T.astype(BF16)

    zi = _dot(hb, widx_ref[...])
    for j in range(N_PAIRS):
        iq_ref[j] = zi[:, j * LANES:(j + 1) * LANES].astype(BF16)
    tail = zi[:, IDX_HEADS * IDX_DIM:IDX_HEADS * IDX_DIM + LANES]
    lane = lax.broadcasted_iota(I32, tail.shape, 1)
    is_k = lane < IDX_DIM
    mu = jnp.sum(jnp.where(is_k, tail, 0.0), axis=-1, keepdims=True) * (1.0 / IDX_DIM)
    dlt = jnp.where(is_k, tail - mu, 0.0)
    var = jnp.sum(dlt * dlt, axis=-1, keepdims=True) * (1.0 / IDX_DIM)
    ikn = dlt * lax.rsqrt(var + EPS) * ikg_ref[...] + ikb_ref[...]
    ik_ref[...] = jnp.where(is_k, ikn, pltpu.roll(ikn, IDX_DIM, axis=1)).astype(BF16)
    iwt_ref[...] = tail.T[IDX_DIM:IDX_DIM + IDX_HEADS, :] * (IDX_HEADS ** -0.5 * IDX_DIM ** -0.5)


def _inproj(x, sc1, sh1, norm_g, w_in, lb_logits, q_g, k_g, ik_g, ik_b, *, layer, gt, tm):
    bsz, s, d = x.shape
    n_rec = 4 * REC_W
    n_att = 3 * ATT_W
    n_idx = IDX_HEADS * IDX_DIM + LANES
    wb = w_in.astype(BF16)
    w_rec = wb[:, :n_rec]
    w_att = wb[:, n_rec:n_rec + n_att]
    w_idx = jnp.zeros((d, n_idx), BF16).at[:, :w_in.shape[1] - n_rec - n_att].set(wb[:, n_rec + n_att:])
    pm = jnp.asarray(np.kron(np.eye(ATT_HEADS), np.full((ATT_DH, ATT_DH), 1.0 / ATT_DH)), BF16)
    qg = jnp.tile(q_g, ATT_HEADS).reshape(1, ATT_W) * (ATT_DH ** -0.5 * LOG2E)
    kg = jnp.tile(k_g, ATT_HEADS).reshape(1, ATT_W)
    ikg = jnp.zeros((1, LANES), F32).at[0, :IDX_DIM].set(ik_g)
    ikb = jnp.zeros((1, LANES), F32).at[0, :IDX_DIM].set(ik_b)
    nl = lb_logits.shape[0]

    row = lambda w: pl.BlockSpec((None, tm, w), lambda b, i: (b, i, 0))
    pair = pl.BlockSpec((None, N_PAIRS, tm, LANES), lambda b, i: (b, 0, i, 0))
    vec = pl.BlockSpec((None, 1, d), lambda b, i: (b, 0, 0))
    out_shapes = (
        jax.ShapeDtypeStruct((bsz, s, REC_W), BF16),
        jax.ShapeDtypeStruct((bsz, s, REC_W), F32),
        jax.ShapeDtypeStruct((bsz, s, REC_W), BF16),
        jax.ShapeDtypeStruct((bsz, s, REC_W), BF16),
        jax.ShapeDtypeStruct((bsz, N_PAIRS, s, LANES), BF16),
        jax.ShapeDtypeStruct((bsz, N_PAIRS, s, LANES), BF16),
        jax.ShapeDtypeStruct((bsz, s // gt, ATT_W, gt), BF16),
        jax.ShapeDtypeStruct((bsz, N_PAIRS, s, LANES), BF16),
        jax.ShapeDtypeStruct((bsz, s, LANES), BF16),
        jax.ShapeDtypeStruct((bsz, IDX_HEADS, s), F32),
    )
    out_specs = (
        row(REC_W), row(REC_W), row(REC_W), row(REC_W), pair, pair,
        pl.BlockSpec((None, tm // gt, ATT_W, gt), lambda b, i: (b, i, 0, 0)),
        pair, row(LANES),
        pl.BlockSpec((None, IDX_HEADS, tm), lambda b, i: (b, 0, i)),
    )
    return pl.pallas_call(
        functools.partial(_inproj_kernel, layer=layer, gt=gt),
        grid=(bsz, s // tm),
        in_specs=[row(d), vec, vec, _const_spec((1, d)),
                  _const_spec((d, n_rec)), _const_spec((d, n_att)), _const_spec((d, n_idx)),
                  _const_spec((nl, REC_W)), _const_spec((1, ATT_W)), _const_spec((1, ATT_W)),
                  _const_spec((1, LANES)), _const_spec((1, LANES)), _const_spec((ATT_W, ATT_W))],
        out_specs=out_specs,
        out_shape=out_shapes,
        compiler_params=_params(("parallel", "parallel")),
        name="inproj",
    )(x, sc1.reshape(bsz, 1, d), sh1.reshape(bsz, 1, d), norm_g.reshape(1, d),
      w_rec, w_att, w_idx, lb_logits, qg, kg, ikg, ikb, pm)


N_LEVELS = 6


def _hgrn_tables():
    c = CHUNK
    w = np.zeros((N_LEVELS + 2, c, c), np.float32)
    am = np.zeros((N_LEVELS + 1, c, c), np.float32)
    t = np.arange(c)
    for m in range(N_LEVELS):
        hs = 1 << m
        blk = t // (2 * hs)
        upper = (t // hs) % 2 == 1
        ref = blk * 2 * hs + hs - 1
        for i in range(c):
            if upper[i]:
                w[m, i, ref[i] + 1:i + 1] = 1.0
            else:
                w[m, i, i + 1:ref[i] + 1] = 1.0
        am[m] = (blk[:, None] == blk[None, :]) & upper[:, None] & ~upper[None, :]
    w[N_LEVELS] = np.tril(np.ones((c, c)))
    w[N_LEVELS + 1] = np.triu(np.ones((c, c)), 1)
    am[N_LEVELS] = np.eye(c)
    w = w.reshape((N_LEVELS + 2) * c, c)
    return np.concatenate([w, w], axis=1), am


HGRN_UNROLL = 2


def _hgrn_kernel(q_ref, f_ref, v_ref, g_ref, og_ref, ww_ref, am_ref, o_ref, st_ref, ex_ref, *, n_chunks):
    @pl.when(pl.program_id(1) == 0)
    def _():
        st_ref[...] = jnp.zeros_like(st_ref)

    ww = ww_ref[...]
    c = CHUNK
    tbit = lax.broadcasted_iota(I32, (c, REC_DK), 0)
    items = [(cc, hd) for cc in range(HGRN_UNROLL) for hd in range(REC_HEADS)]

    def chunks(ci, carry):
        def blk(ref, it):
            r0 = pl.multiple_of((ci * HGRN_UNROLL + it[0]) * c, c)
            return ref.at[pl.ds(r0, c), it[1] * REC_DK:(it[1] + 1) * REC_DK]

        def ex(i, part):
            return ex_ref[i, part * c:(part + 1) * c, :]

        f = [blk(f_ref, it)[...] for it in items]
        q = [blk(q_ref, it)[...].astype(F32) for it in items]
        v = [blk(v_ref, it)[...] for it in items]
        k = [1.0 - fi for fi in f]
        for i, fi in enumerate(f):
            lf_hi, lf_lo = _split_bf16(jnp.log(fi))
            ex_ref[i] = jnp.exp(_dot(ww, jnp.concatenate([lf_hi, lf_lo], axis=0)))
        a = [am_ref[N_LEVELS] * _dot_nt(q[i].astype(BF16), k[i].astype(BF16)) for i in range(len(items))]
        for m in range(N_LEVELS):
            upper = ((tbit >> m) & 1) == 1
            for i in range(len(items)):
                tm_ = (ex(i, m) * jnp.where(upper, q[i], k[i])).astype(BF16)
                a[i] = a[i] + am_ref[m] * _dot_nt(tm_, tm_)
        qb = [(q[i] * ex(i, N_LEVELS)).astype(BF16) for i in range(len(items))]
        kb = [(k[i] * ex(i, N_LEVELS + 1)).astype(BF16) for i in range(len(items))]
        intra = [_dot(a[i].astype(BF16), v[i]) for i in range(len(items))]
        ut = [_dot_tn(v[i], kb[i]) for i in range(len(items))]
        for hd in range(REC_HEADS):
            st = st_ref[hd]
            for cc in range(HGRN_UNROLL):
                i = cc * REC_HEADS + hd
                o = intra[i] + _dot_nt(qb[i], st.astype(BF16))
                st = st * ex_ref[i, (N_LEVELS + 1) * c - 1:(N_LEVELS + 1) * c, :] + ut[i]
                ms = jnp.mean(o * o, axis=-1, keepdims=True)
                gate = blk(g_ref, items[i])[...].astype(F32)
                y = o * lax.rsqrt(ms + EPS) * og_ref[:, hd * REC_DV:(hd + 1) * REC_DV] * gate
                blk(o_ref, items[i])[...] = y.astype(BF16)
            st_ref[hd] = st
        return carry

    lax.fori_loop(0, n_chunks // HGRN_UNROLL, chunks, 0)


def _hgrn(q, f, v, g, out_g, *, ts):
    bsz, s, _ = q.shape
    ww_np, am_np = _hgrn_tables()
    ww = jnp.asarray(ww_np, BF16)
    am = jnp.asarray(am_np, F32)
    row = pl.BlockSpec((None, ts, REC_W), lambda b, i: (b, i, 0))
    return pl.pallas_call(
        functools.partial(_hgrn_kernel, n_chunks=ts // CHUNK),
        grid=(bsz, s // ts),
        in_specs=[row, row, row, row, _const_spec((1, REC_W)),
                  _const_spec(ww.shape), _const_spec(am.shape)],
        out_specs=row,
        out_shape=jax.ShapeDtypeStruct((bsz, s, REC_W), BF16),
        scratch_shapes=[pltpu.VMEM((REC_HEADS, REC_DV, REC_DK), F32),
                        pltpu.VMEM((HGRN_UNROLL * REC_HEADS, (N_LEVELS + 2) * CHUNK, REC_DK), F32)],
        compiler_params=_params(("parallel", "arbitrary")),
        name="hgrn",
    )(q, f, v, g, out_g.reshape(1, REC_W), ww, am)


def _t5_bucket(rel):
    nb = NUM_BUCKETS // 2
    max_exact = nb // 2
    ret = jnp.where(rel > 0, nb, 0)
    n = jnp.abs(rel)
    nf = jnp.maximum(n, 1).astype(F32)
    large = max_exact + (jnp.log(nf / max_exact) / np.log(MAX_DISTANCE / max_exact)
                         * (nb - max_exact)).astype(I32)
    large = jnp.minimum(large, nb - 1)
    return ret + jnp.where(n < max_exact, n, large)


def _bias_slots(rel_bias):
    n_rel = 3 * QBLK
    rel = jnp.arange(n_rel, dtype=I32) - 2 * QBLK
    tab = rel_bias[_t5_bucket(rel)].T
    far = rel_bias[_t5_bucket(jnp.full((1,), -2 * QBLK - 1, I32))].T
    rev = tab[:, ::-1]
    skew = jnp.tile(rev, (1, 2 * QBLK))[:, :2 * QBLK * (n_rel - 1)].reshape(ATT_HEADS, 2 * QBLK, n_rel - 1)
    near = skew[:, :, 2 * QBLK - 1:]
    nb = ((near - far[:, :, None]) * LOG2E).reshape(ATT_HEADS, 2, QBLK, QBLK)
    z = jnp.zeros((ATT_HEADS, 1, QBLK, QBLK), F32)
    return jnp.concatenate([z, nb, z], axis=1)


def _attn_kernel(aq_ref, ak_ref, avt_ref, iq_ref, ik_ref, iwt_ref, bias_ref, og_ref, o_ref,
                 key_ref, mb_ref, thr_ref, acc_ref, m_ref, l_ref, al_ref, s_ref, mx_ref, p_ref, qbd_ref, ot_ref, tie_ref,
                 *, gs, gk, k_sel):
    bi = pl.program_id(1)
    n_s = bi // (gs // QBLK) + 1
    lane = lax.broadcasted_iota(I32, (QBLK, LANES), 1)

    def block_diag(xq):
        zero = jnp.zeros_like(xq)
        return jnp.concatenate([jnp.where(lane < ATT_DH, xq, zero), jnp.where(lane >= ATT_DH, xq, zero)], axis=0)

    def srows(g):
        return pl.ds(pl.multiple_of(g * gs, gs), gs)

    for j in range(N_PAIRS):
        qbd_ref[j] = block_diag(iq_ref[j])
    w = iwt_ref[...]

    def group_scores(g):
        sc = _dot_nt(ik_ref[srows(g), :], qbd_ref[...].reshape(N_PAIRS * 2 * QBLK, LANES))
        acc = jnp.zeros((gs, LANES), F32)
        for hd in range(IDX_HEADS):
            acc = acc + jnp.maximum(sc[:, hd * LANES:(hd + 1) * LANES], 0.0) * w[hd:hd + 1, :]
        return acc

    def stats(sc, c):
        s3 = sc.reshape(gs // 8, 8, LANES)
        return (jnp.minimum(c[0], jnp.min(jnp.where(s3 == -jnp.inf, jnp.inf, s3), axis=0)),
                jnp.maximum(c[1], jnp.max(s3, axis=0)),
                c[2] + jnp.sum(jnp.where(s3 >= 0.0, 1, 0), axis=0),
                c[3] + jnp.sum(jnp.where(s3 > 0.0, 1, 0), axis=0))

    def idx_body(g, carry):
        sc = group_scores(g)
        key_ref[srows(g), :] = sc
        return stats(sc, carry)

    def idx_pair(u, carry):
        return idx_body(2 * u + 1, idx_body(2 * u, carry))

    z8 = jnp.zeros((8, LANES), I32)
    st8 = lax.fori_loop(0, (n_s - 1) // 2, idx_pair,
                        (jnp.full((8, LANES), jnp.inf, F32), jnp.full((8, LANES), -jnp.inf, F32), z8, z8))
    st8 = lax.cond((n_s - 1) % 2 == 1, lambda c: idx_body(n_s - 2, c), lambda c: c, st8)

    last = srows(n_s - 1)
    spos = (n_s - 1) * gs + lax.broadcasted_iota(I32, (gs, LANES), 0)
    tpos = bi * QBLK + lax.broadcasted_iota(I32, (gs, LANES), 1)
    sc_last = jnp.where((spos // CHUNK) <= (tpos // CHUNK), group_scores(n_s - 1), -jnp.inf)
    key_ref[last, :] = sc_last
    mn8, mx8, ge8, gt8 = stats(sc_last, st8)

    thr_ref[...] = jnp.full(thr_ref.shape, F32_LOWEST, F32)
    tie_ref[0] = 0

    @pl.when((2 * bi + 2) * CHUNK > k_sel)
    def _():
        def groups(fn, init):
            return lax.fori_loop(0, n_s, lambda g, c: fn(key_ref[srows(g), :].reshape(gs // 8, 8, LANES), c), init)

        def count_ge(cand):
            acc = groups(lambda sc, c: c + jnp.sum(jnp.where(sc >= cand, 1, 0), axis=0), jnp.zeros((8, LANES), I32))
            return jnp.sum(acc, axis=0, keepdims=True)

        c0_ge = jnp.sum(ge8, axis=0, keepdims=True)
        c0_gt = jnp.sum(gt8, axis=0, keepdims=True)
        mn = jnp.min(mn8, axis=0, keepdims=True)
        mx = jnp.max(mx8, axis=0, keepdims=True)
        above = mx + (jnp.abs(mx) * 2.0 ** -20 + 1e-30)
        n_adm = (2 * bi + 1 + lax.broadcasted_iota(I32, (1, LANES), 1) // CHUNK) * CHUNK
        pos = c0_gt >= k_sel
        non_neg = c0_ge >= k_sel
        lo = jnp.where(non_neg, 0.0, mn)
        cnt_lo = jnp.where(non_neg, c0_ge, n_adm)
        hi = jnp.where(pos, above, 0.0)

        def midpoint(lo, hi):
            return 0.5 * lo + 0.5 * hi

        def unsettled(lo, hi, cnt_lo):
            mid = midpoint(lo, hi)
            return jnp.where(cnt_lo > k_sel, jnp.where(mid > lo, jnp.where(mid < hi, 1.0, 0.0), 0.0), 0.0)

        def cond(st):
            return jnp.logical_and(st[3] > 0.5, st[4] < BISECT_CAP)

        def halve(lo, hi, cnt_lo):
            act = unsettled(lo, hi, cnt_lo) > 0.5
            mid = midpoint(lo, hi)
            c = count_ge(mid)
            up = jnp.logical_and(act, c >= k_sel)
            dn = jnp.logical_and(act, c < k_sel)
            return jnp.where(up, mid, lo), jnp.where(dn, mid, hi), jnp.where(up, c, cnt_lo)

        def body(st):
            lo, hi, cnt_lo = halve(*halve(*st[:3]))
            return lo, hi, cnt_lo, jnp.max(unsettled(lo, hi, cnt_lo)), st[4] + 2

        st = lax.while_loop(cond, body, (lo, hi, cnt_lo, jnp.max(unsettled(lo, hi, cnt_lo)), jnp.int32(0)))
        thr_ref[...] = jnp.broadcast_to(st[0], thr_ref.shape)
        tie_ref[0] = (jnp.max(jnp.where(st[2] > k_sel, 1.0, 0.0)) > 0.5).astype(I32)

    thr = thr_ref[0:1, :]

    @pl.when(tie_ref[0] == 0)
    def _():
        def mb_body(g, carry):
            mb_ref[srows(g), :] = jnp.where(key_ref[srows(g), :] >= thr, 0.0, NEG)
            return carry

        lax.fori_loop(0, n_s, mb_body, 0)

    @pl.when(tie_ref[0] != 0)
    def _():
        def count_gt(g, acc):
            sc = key_ref[srows(g), :].reshape(gs // 8, 8, LANES)
            return acc + jnp.sum(jnp.where(sc > thr, 1, 0), axis=0)

        n_gt = jnp.sum(lax.fori_loop(0, n_s, count_gt, jnp.zeros((8, LANES), I32)), axis=0, keepdims=True)
        room = (k_sel - n_gt).astype(F32)
        r = lax.broadcasted_iota(I32, (QBLK, QBLK), 0)
        c = lax.broadcasted_iota(I32, (QBLK, QBLK), 1)
        tri = jnp.where(c <= r, 1.0, 0.0).astype(BF16)

        def mb_body(g, seen):
            for t in range(gs // QBLK):
                rs = pl.ds(pl.multiple_of(g * gs + t * QBLK, QBLK), QBLK)
                sc = key_ref[rs, :]
                eq = jnp.where(sc == thr, 1.0, 0.0)
                rank = _dot(tri, eq.astype(BF16)) + seen
                keep = jnp.where(sc > thr, 1.0, jnp.where(rank <= room, eq, 0.0))
                mb_ref[rs, :] = jnp.where(keep > 0.5, 0.0, NEG)
                seen = seen + jnp.sum(eq, axis=0, keepdims=True)
            return seen

        lax.fori_loop(0, n_s, mb_body, jnp.zeros((1, LANES), F32))

    spg = gk // QBLK
    n_g = bi // spg + 1
    for j in range(N_PAIRS):
        qbd_ref[j] = block_diag(aq_ref[j])
    m_ref[...] = jnp.full(m_ref.shape, NEG, F32)
    l_ref[...] = jnp.zeros_like(l_ref)
    acc_ref[...] = jnp.zeros_like(acc_ref)
    al_ref[...] = jnp.ones_like(al_ref)
    p_ref[1] = jnp.zeros(p_ref.shape[1:], BF16)

    def rows(g):
        return pl.ds(pl.multiple_of(g * gk, gk), gk)

    def scores(t, slot):
        mb = mb_ref[rows(t), :]
        mb2 = jnp.concatenate([mb, mb], axis=1)
        for j in range(N_PAIRS):
            s = _dot_nt(ak_ref[j, rows(t), :], qbd_ref[j]) + mb2
            s_ref[slot, j] = s
            mx_ref[slot, j] = jnp.max(s, axis=0, keepdims=True)

    def pv(t, slot):
        return [_dot(avt_ref[t, j * LANES:(j + 1) * LANES, :], p_ref[slot, j]) for j in range(N_PAIRS)]

    def step(g, cur):
        prv = 1 - cur

        @pl.when(g >= n_g - 2)
        def _():
            for j in range(N_PAIRS):
                parts = []
                for st in range(spg):
                    slot = jnp.clip(g * spg + st - bi + 2, 0, 3)
                    parts.append(jnp.concatenate([bias_ref[2 * j, slot], bias_ref[2 * j + 1, slot]], axis=1))
                s = s_ref[cur, j] + jnp.concatenate(parts, axis=0)
                s_ref[cur, j] = s
                mx_ref[cur, j] = jnp.max(s, axis=0, keepdims=True)

        al_old = [al_ref[j] for j in range(N_PAIRS)]
        for j in range(N_PAIRS):
            m_old = m_ref[j]
            m_new = jnp.maximum(m_old, mx_ref[cur, j])
            p = jnp.exp2(s_ref[cur, j] - m_new)
            alpha = jnp.exp2(m_old - m_new)
            al_ref[j] = alpha
            l_ref[j] = l_ref[j] * alpha + jnp.sum(p, axis=0, keepdims=True)
            m_ref[j] = m_new
            p_ref[cur, j] = p.astype(BF16)
        o_prev = pv(jnp.maximum(g - 1, 0), prv)
        scores(jnp.minimum(g + 1, n_g - 1), prv)
        for j in range(N_PAIRS):
            acc_ref[j] = acc_ref[j] * al_old[j] + o_prev[j]

    scores(0, 0)

    def pipe_body(u, c):
        step(2 * u, 0)

        @pl.when(2 * u + 1 < n_g)
        def _():
            step(2 * u + 1, 1)
        return c

    lax.fori_loop(0, (n_g + 1) // 2, pipe_body, 0)
    o_last = pv(n_g - 1, (n_g - 1) % 2)

    for j in range(N_PAIRS):
        acc = acc_ref[j] * al_ref[j] + o_last[j]
        l = l_ref[j]
        o0 = acc[0:ATT_DH, 0:LANES] / l[:, 0:LANES]
        o1 = acc[ATT_DH:2 * ATT_DH, LANES:2 * LANES] / l[:, LANES:2 * LANES]
        ot_ref[j * LANES:(j + 1) * LANES, :] = jnp.concatenate([o0, o1], axis=0)

    ot = ot_ref[...]
    og = og_ref[...]
    outs = []
    for hd in range(ATT_HEADS):
        oh = ot[hd * ATT_DH:(hd + 1) * ATT_DH, :]
        ms = jnp.mean(oh * oh, axis=0, keepdims=True)
        outs.append(oh * lax.rsqrt(ms + EPS) * og[hd * ATT_DH:(hd + 1) * ATT_DH, :])
    o_ref[...] = jnp.concatenate(outs, axis=0).T.astype(BF16)


def _attention(aq, ak, avt, iq, ik, iwt, rel_bias, out_g, *, gk):
    bsz, _, s, _ = aq.shape
    k_sel = min(TOPK_MAX, s // 4)
    bias = _bias_slots(rel_bias)
    qpair = pl.BlockSpec((None, N_PAIRS, QBLK, LANES), lambda b, i: (b, 0, i, 0))
    full = lambda shp: pl.BlockSpec((None,) + shp, lambda b, i: (b,) + (0,) * len(shp),
                                    pipeline_mode=pl.Buffered(1))
    return pl.pallas_call(
        functools.partial(_attn_kernel, gs=_tile(s, 512), gk=gk, k_sel=k_sel),
        grid=(bsz, s // QBLK),
        in_specs=[qpair, full((N_PAIRS, s, LANES)), full((s // gk, ATT_W, gk)), qpair,
                  full((s, LANES)), pl.BlockSpec((None, IDX_HEADS, QBLK), lambda b, i: (b, 0, i)),
                  _const_spec(bias.shape), _const_spec((ATT_W, 1))],
        out_specs=pl.BlockSpec((None, QBLK, ATT_W), lambda b, i: (b, i, 0)),
        out_shape=jax.ShapeDtypeStruct((bsz, s, ATT_W), BF16),
        scratch_shapes=[pltpu.VMEM((s, LANES), F32),
                        pltpu.VMEM((s, LANES), F32),
                        pltpu.VMEM((8, LANES), F32),
                        pltpu.VMEM((N_PAIRS, 2 * ATT_DH, 2 * LANES), F32),
                        pltpu.VMEM((N_PAIRS, 1, 2 * LANES), F32),
                        pltpu.VMEM((N_PAIRS, 1, 2 * LANES), F32),
                        pltpu.VMEM((N_PAIRS, 1, 2 * LANES), F32),
                        pltpu.VMEM((2, N_PAIRS, gk, 2 * LANES), F32),
                        pltpu.VMEM((2, N_PAIRS, 1, 2 * LANES), F32),
                        pltpu.VMEM((2, N_PAIRS, gk, 2 * LANES), BF16),
                        pltpu.VMEM((N_PAIRS, 2 * QBLK, LANES), BF16),
                        pltpu.VMEM((ATT_W, LANES), F32),
                        pltpu.SMEM((1,), I32)],
        compiler_params=_params(("parallel", "arbitrary")),
        name="attn",
    )(aq, ak, avt, iq, ik, iwt, bias, out_g.reshape(ATT_W, 1))


def _outproj_kernel(rec_ref, att_ref, x_ref, g1_ref, sc_ref, sh_ref, ng_ref, wo_ref, wr_ref, br_ref,
                    x1_ref, h2_ref, comb_ref):
    mix = _dot(rec_ref[...], wo_ref[0:REC_W, :]) + _dot(att_ref[...], wo_ref[REC_W:REC_W + ATT_W, :])
    x1 = x_ref[...] + g1_ref[...] * mix
    x1_ref[...] = x1
    ms = jnp.mean(x1 * x1, axis=-1, keepdims=True)
    h2 = (x1 * lax.rsqrt(ms + EPS) * ng_ref[...] * (1.0 + sc_ref[...]) + sh_ref[...]).astype(BF16)
    h2_ref[...] = h2

    lg = _dot(h2, wr_ref[...]) + br_ref[...]
    lane = lax.broadcasted_iota(I32, lg.shape, 1)
    big = jnp.int32(2 * LANES)
    is_g = lane < N_GROUPS
    gl = jnp.where(is_g, lg, -jnp.inf)
    gmax = jnp.max(gl, axis=-1, keepdims=True)
    gate = 1.0 / jnp.sum(jnp.where(is_g, jnp.exp(lg - gmax), 0.0), axis=-1, keepdims=True)
    gtop = jnp.min(jnp.where(gl == gmax, lane, big), axis=-1, keepdims=True)
    e_lo = ROUTE_OFF + EXPERTS_PER_GROUP * gtop
    el = jnp.where((lane >= e_lo) & (lane < e_lo + EXPERTS_PER_GROUP), lg, -jnp.inf)
    v1 = jnp.max(el, axis=-1, keepdims=True)
    i1 = jnp.min(jnp.where(el == v1, lane, big), axis=-1, keepdims=True)
    el2 = jnp.where(lane == i1, -jnp.inf, el)
    v2 = jnp.max(el2, axis=-1, keepdims=True)
    i2 = jnp.min(jnp.where(el2 == v2, lane, big), axis=-1, keepdims=True)
    e2 = jnp.exp(v2 - v1)
    w1 = gate / (1.0 + e2)
    w2 = gate * e2 / (1.0 + e2)
    comb = jnp.where(lane == i1, w1, 0.0) + jnp.where(lane == i2, w2, 0.0)
    comb_ref[...] = jnp.where(lane == 0, gtop.astype(F32), comb)


def _outproj(rec, att, x, g1, sc2, sh2, norm_g, w_out, w_rg, b_rg, w_re, b_re, *, tm):
    bsz, s, d = x.shape
    wr = jnp.zeros((d, LANES), F32).at[:, :N_GROUPS].set(w_rg).at[:, ROUTE_OFF:ROUTE_OFF + N_EXPERTS].set(w_re)
    br = jnp.zeros((1, LANES), F32).at[0, :N_GROUPS].set(b_rg).at[0, ROUTE_OFF:ROUTE_OFF + N_EXPERTS].set(b_re)
    row = lambda w: pl.BlockSpec((None, tm, w), lambda b, i: (b, i, 0))
    vec = pl.BlockSpec((None, 1, d), lambda b, i: (b, 0, 0))
    return pl.pallas_call(
        _outproj_kernel,
        grid=(bsz, s // tm),
        in_specs=[row(REC_W), row(ATT_W), row(d), vec, vec, vec, _const_spec((1, d)),
                  _const_spec((REC_W + ATT_W, d)), _const_spec((d, LANES)), _const_spec((1, LANES))],
        out_specs=(row(d), row(d), row(LANES)),
        out_shape=(jax.ShapeDtypeStruct((bsz, s, d), F32),
                   jax.ShapeDtypeStruct((bsz, s, d), BF16),
                   jax.ShapeDtypeStruct((bsz, s, LANES), F32)),
        compiler_params=_params(("parallel", "parallel")),
        name="outproj",
    )(rec, att, x, g1.reshape(bsz, 1, d), sc2.reshape(bsz, 1, d), sh2.reshape(bsz, 1, d),
      norm_g.reshape(1, d), w_out.astype(BF16), wr.astype(BF16), br)


MOE_ALIGN = 16
MOE_LANE_SHIFT = 32
MOE_CHUNK = 288
MOE_RB = 256


def _moe_kernel(h_ref, comb_ref, x1_ref, g2_ref, w1_ref, w3_ref, w2_ref, o_ref,
                p_ref, pt_ref, hs_ref, cs_ref, ys_ref, seg_ref, *, mc):
    e = pl.program_id(2)
    tm = h_ref.shape[0]
    npad = hs_ref.shape[0]

    @pl.when(e == 0)
    def _():
        comb = comb_ref[...]
        lane = lax.broadcasted_iota(I32, (tm, LANES), 1)
        gid = comb[:, 0:1].astype(I32)
        oh = jnp.where(lane == gid, 1.0, 0.0)
        ohb = oh.astype(BF16)
        pre = []
        for rb in range(tm // MOE_RB):
            r = rb * MOE_RB + lax.broadcasted_iota(I32, (MOE_RB, tm), 0)
            c = lax.broadcasted_iota(I32, (MOE_RB, tm), 1)
            pre.append(_dot(jnp.where(c < r, 1.0, 0.0).astype(BF16), ohb))
        prefix = jnp.concatenate(pre, axis=0)
        cnt = jnp.sum(oh, axis=0, keepdims=True).astype(I32)
        cnt_al = ((cnt + (MOE_ALIGN - 1)) // MOE_ALIGN) * MOE_ALIGN
        base = jnp.sum(jnp.where(lane < gid, cnt_al.astype(F32), 0.0), axis=-1, keepdims=True)
        rank = jnp.sum(prefix * oh, axis=-1, keepdims=True)
        pos = (base + rank).astype(I32)
        pos_row = jnp.broadcast_to(pos.astype(F32), (tm, LANES)).T[0:1, :].astype(I32)
        for rb in range(tm // MOE_RB):
            sl = slice(rb * MOE_RB, (rb + 1) * MOE_RB)
            coln = lax.broadcasted_iota(I32, (MOE_RB, npad), 1)
            pt_ref[sl, :] = jnp.where(coln == pos[sl], 1.0, 0.0).astype(BF16)
        for rb in range(npad // LANES):
            sl = slice(rb * LANES, (rb + 1) * LANES)
            rown = rb * LANES + lax.broadcasted_iota(I32, (LANES, tm), 0)
            p_ref[sl, :] = jnp.where(rown == pos_row, 1.0, 0.0).astype(BF16)
        c_hi = comb.astype(BF16)
        c_mid, c_lo = _split_bf16(comb - c_hi.astype(F32))
        packed = (c_hi.astype(F32) + pltpu.roll(c_mid.astype(F32), MOE_LANE_SHIFT, axis=1)
                  + pltpu.roll(c_lo.astype(F32), 2 * MOE_LANE_SHIFT, axis=1)).astype(BF16)
        srt = _dot(p_ref[...], jnp.concatenate([h_ref[...], packed], axis=1))
        d = h_ref.shape[1]
        hs_ref[...] = srt[:, :d].astype(BF16)
        cp = srt[:, d:]
        cs_ref[...] = (cp + pltpu.roll(cp, LANES - MOE_LANE_SHIFT, axis=1)
                       + pltpu.roll(cp, LANES - 2 * MOE_LANE_SHIFT, axis=1))
        ys_ref[...] = jnp.zeros_like(ys_ref)
        start = jnp.int32(0)
        for g in range(N_GROUPS):
            seg_ref[g] = start
            seg_ref[N_GROUPS + g] = cnt[0, g]
            start = start + cnt_al[0, g]

    grp = e // EXPERTS_PER_GROUP
    start = seg_ref[grp]
    n_rows = seg_ref[N_GROUPS + grp]

    def chunk(ci, carry):
        rs = pl.ds(pl.multiple_of(start + ci * mc, MOE_ALIGN), mc)
        hb = hs_ref[rs, :]
        cw = cs_ref[rs, :]
        lane = lax.broadcasted_iota(I32, cw.shape, 1)
        col = jnp.sum(jnp.where(lane == e + ROUTE_OFF, cw, 0.0), axis=-1, keepdims=True)
        he = _silu(_dot(hb, w1_ref[...])) * _dot(hb, w3_ref[...]) * col
        ys_ref[rs, :] += _dot(he.astype(BF16), w2_ref[...])
        return carry

    lax.fori_loop(0, (n_rows + mc - 1) // mc, chunk, 0)

    @pl.when(e == N_EXPERTS - 1)
    def _():
        o_ref[...] = x1_ref[...] + g2_ref[...] * _dot(pt_ref[...], ys_ref[...].astype(BF16))


def _moe(h2, comb, x1, g2, w1, w3, w2, *, tm):
    bsz, s, d = x1.shape
    mc = MOE_CHUNK
    npad = -(-(tm + N_GROUPS * MOE_ALIGN + mc) // LANES) * LANES
    row = lambda w, **kw: pl.BlockSpec((None, tm, w), lambda b, i, e: (b, i, 0), **kw)
    once = dict(pipeline_mode=pl.Buffered(1))
    return pl.pallas_call(
        functools.partial(_moe_kernel, mc=mc),
        grid=(bsz, s // tm, N_EXPERTS),
        in_specs=[row(d, **once), row(LANES), row(d, **once),
                  pl.BlockSpec((None, 1, d), lambda b, i, e: (b, 0, 0)),
                  pl.BlockSpec((None, d, D_EXPERT), lambda b, i, e: (e, 0, 0)),
                  pl.BlockSpec((None, d, D_EXPERT), lambda b, i, e: (e, 0, 0)),
                  pl.BlockSpec((None, D_EXPERT, d), lambda b, i, e: (e, 0, 0))],
        out_specs=row(d),
        out_shape=jax.ShapeDtypeStruct((bsz, s, d), F32),
        scratch_shapes=[pltpu.VMEM((npad, tm), BF16),
                        pltpu.VMEM((tm, npad), BF16),
                        pltpu.VMEM((npad, d), BF16),
                        pltpu.VMEM((npad, LANES), F32),
                        pltpu.VMEM((npad, d), F32),
                        pltpu.SMEM((2 * N_GROUPS,), I32)],
        compiler_params=_params(("parallel", "parallel", "arbitrary")),
        name="moe",
    )(h2, comb, x1, g2.reshape(bsz, 1, d), w1.astype(BF16), w3.astype(BF16), w2.astype(BF16))


def _tile(s, pref):
    t = min(s, pref)
    assert s % t == 0
    return t


def kernel(x, c, w_ada, b_ada, norm1_g, norm2_g, w_in, lb_logits, rec_out_g, q_norm_g, k_norm_g,
           idx_k_norm_g, idx_k_norm_b, attn_out_g, rel_bias, w_out, w_rg, b_rg, w_re, b_re, w1, w3, w2):
    bsz, s, d = x.shape
    depth = w_ada.shape[0]
    gk = _tile(s, 512)
    for l in range(depth):
        mod = _adaln(c, w_ada[l], b_ada[l])
        sh1, sc1, g1, sh2, sc2, g2 = jnp.split(mod, 6, axis=-1)
        q, f, v, g, aq, ak, avt, iq, ik, iwt = _inproj(
            x, sc1, sh1, norm1_g[l], w_in[l], lb_logits, q_norm_g[l], k_norm_g[l],
            idx_k_norm_g[l], idx_k_norm_b[l], layer=l, gt=gk, tm=_tile(s, 512))
        rec = _hgrn(q, f, v, g, rec_out_g[l], ts=_tile(s, 256))
        att = _attention(aq, ak, avt, iq, ik, iwt, rel_bias, attn_out_g[l], gk=gk)
        x1, h2, comb = _outproj(rec, att, x, g1, sc2, sh2, norm2_g[l], w_out[l],
                                w_rg[l], b_rg[l], w_re[l], b_re[l], tm=_tile(s, 512))
        x = _moe(h2, comb, x1, g2, w1[l], w3[l], w2[l], tm=_tile(s, 1024))
    return x
```

```python
import functools

import numpy as np
import jax
import jax.numpy as jnp
from jax import lax
from jax.experimental import pallas as pl
from jax.experimental.pallas import tpu as pltpu

F32 = jnp.float32
BF16 = jnp.bfloat16
I32 = jnp.int32

CHUNK = 64
QBLK = 128
EPS = 1e-6
REC_HEADS = 4
REC_DK = 128
REC_DV = 128
REC_W = REC_HEADS * REC_DV
ATT_HEADS = 8
ATT_DH = 64
ATT_W = ATT_HEADS * ATT_DH
IDX_HEADS = 8
IDX_DIM = 64
TOPK_MAX = 256
NUM_BUCKETS = 32
MAX_DISTANCE = 128
N_GROUPS = 4
EXPERTS_PER_GROUP = 4
N_EXPERTS = N_GROUPS * EXPERTS_PER_GROUP
D_EXPERT = 512

LANES = 128
V7X_VMEM_LIMIT = 56 * 1024 * 1024
F32_LOWEST = float(np.finfo(np.float32).min)
COARSE_STEPS = 12
BISECT_CAP = 320
NEG = -1e30
LOG2E = float(np.log2(np.e))
N_PAIRS = ATT_HEADS // 2
ROUTE_OFF = N_GROUPS


def _dot(a, b):
    return jnp.dot(a, b, preferred_element_type=F32)


def _dot_nt(a, b):
    return lax.dot_general(a, b, (((1,), (1,)), ((), ())), preferred_element_type=F32)


def _dot_tn(a, b):
    return lax.dot_general(a, b, (((0,), (0,)), ((), ())), preferred_element_type=F32)


def _split_bf16(a):
    hi = a.astype(BF16)
    lo = (a - hi.astype(F32)).astype(BF16)
    return hi, lo


def _silu(a):
    return a * jax.nn.sigmoid(a)


def _const_spec(shape):
    nd = len(shape)
    return pl.BlockSpec(shape, lambda *_: (0,) * nd, pipeline_mode=pl.Buffered(1))


def _params(sem):
    return pltpu.CompilerParams(dimension_semantics=sem, vmem_limit_bytes=V7X_VMEM_LIMIT)


def _adaln_kernel(c_ref, w_ref, b_ref, o_ref):
    a_hi, a_lo = _split_bf16(_silu(c_ref[...]))
    w_hi, w_lo = _split_bf16(w_ref[...])
    o_ref[...] = _dot(a_hi, w_hi) + _dot(a_lo, w_hi) + _dot(a_hi, w_lo) + b_ref[...]


def _adaln(c, w, b):
    bsz, d = c.shape
    n = w.shape[1]
    rows = 16
    bn = 1024
    cp = jnp.zeros((rows, d), F32).at[:bsz].set(c)
    out = pl.pallas_call(
        _adaln_kernel,
        grid=(n // bn,),
        in_specs=[pl.BlockSpec((rows, d), lambda i: (0, 0)),
                  pl.BlockSpec((d, bn), lambda i: (0, i)),
                  pl.BlockSpec((1, bn), lambda i: (0, i))],
        out_specs=pl.BlockSpec((rows, bn), lambda i: (0, i)),
        out_shape=jax.ShapeDtypeStruct((rows, n), F32),
        compiler_params=_params(("parallel",)),
        name="adaln",
    )(cp, w, b.reshape(1, n))
    return out[:bsz]


def _inproj_kernel(x_ref, sc_ref, sh_ref, ng_ref, wrec_ref, watt_ref, widx_ref, lbl_ref,
                   qg_ref, kg_ref, ikg_ref, ikb_ref, pm_ref,
                   q_ref, f_ref, v_ref, g_ref, aq_ref, ak_ref, avt_ref, iq_ref, ik_ref, iwt_ref,
                   *, layer, gt):
    x = x_ref[...]
    tm = x.shape[0]
    ms = jnp.mean(x * x, axis=-1, keepdims=True)
    h = x * lax.rsqrt(ms + EPS) * ng_ref[...] * (1.0 + sc_ref[...]) + sh_ref[...]
    hb = h.astype(BF16)

    zr = _dot(hb, wrec_ref[...])
    q_ref[...] = _silu(zr[:, 0:REC_W]).astype(BF16)
    lbl = lbl_ref[...]
    e = jnp.exp(lbl - jnp.max(lbl, axis=0, keepdims=True))
    sm = e / jnp.sum(e, axis=0, keepdims=True)
    lb = jnp.sum(sm[0:layer + 1], axis=0, keepdims=True)
    f_ref[...] = lb + (1.0 - lb) * jax.nn.sigmoid(zr[:, REC_W:2 * REC_W])
    v_ref[...] = zr[:, 2 * REC_W:3 * REC_W].astype(BF16)
    g_ref[...] = _silu(zr[:, 3 * REC_W:4 * REC_W]).astype(BF16)

    za = _dot(hb, watt_ref[...])
    aq = za[:, 0:ATT_W]
    ak = za[:, ATT_W:2 * ATT_W]
    av = za[:, 2 * ATT_W:3 * ATT_W]
    pm = pm_ref[...]
    aqn = aq * lax.rsqrt(_dot((aq * aq).astype(BF16), pm) + EPS) * qg_ref[...]
    akn = ak * lax.rsqrt(_dot((ak * ak).astype(BF16), pm) + EPS) * kg_ref[...]
    for j in range(N_PAIRS):
        aq_ref[j] = aqn[:, j * LANES:(j + 1) * LANES].astype(BF16)
        ak_ref[j] = akn[:, j * LANES:(j + 1) * LANES].astype(BF16)
    for t in range(tm // gt):
        avt_ref[t] = av[t * gt:(t + 1) * gt, :].T.astype(BF16)

    zi = _dot(hb, widx_ref[...])
    for j in range(N_PAIRS):
        iq_ref[j] = zi[:, j * LANES:(j + 1) * LANES].astype(BF16)
    tail = zi[:, IDX_HEADS * IDX_DIM:IDX_HEADS * IDX_DIM + LANES]
    lane = lax.broadcasted_iota(I32, tail.shape, 1)
    is_k = lane < IDX_DIM
    mu = jnp.sum(jnp.where(is_k, tail, 0.0), axis=-1, keepdims=True) * (1.0 / IDX_DIM)
    dlt = jnp.where(is_k, tail - mu, 0.0)
    var = jnp.sum(dlt * dlt, axis=-1, keepdims=True) * (1.0 / IDX_DIM)
    ikn = dlt * lax.rsqrt(var + EPS) * ikg_ref[...] + ikb_ref[...]
    ik_ref[...] = jnp.where(is_k, ikn, pltpu.roll(ikn, IDX_DIM, axis=1)).astype(BF16)
    iwt_ref[...] = tail.T[IDX_DIM:IDX_DIM + IDX_HEADS, :] * (IDX_HEADS ** -0.5 * IDX_DIM ** -0.5)


def _inproj(x, sc1, sh1, norm_g, w_in, lb_logits, q_g, k_g, ik_g, ik_b, *, layer, gt, tm):
    bsz, s, d = x.shape
    n_rec = 4 * REC_W
    n_att = 3 * ATT_W
    n_idx = IDX_HEADS * IDX_DIM + LANES
    wb = w_in.astype(BF16)
    w_rec = wb[:, :n_rec]
    w_att = wb[:, n_rec:n_rec + n_att]
    w_idx = jnp.zeros((d, n_idx), BF16).at[:, :w_in.shape[1] - n_rec - n_att].set(wb[:, n_rec + n_att:])
    pm = jnp.asarray(np.kron(np.eye(ATT_HEADS), np.full((ATT_DH, ATT_DH), 1.0 / ATT_DH)), BF16)
    qg = jnp.tile(q_g, ATT_HEADS).reshape(1, ATT_W) * (ATT_DH ** -0.5 * LOG2E)
    kg = jnp.tile(k_g, ATT_HEADS).reshape(1, ATT_W)
    ikg = jnp.zeros((1, LANES), F32).at[0, :IDX_DIM].set(ik_g)
    ikb = jnp.zeros((1, LANES), F32).at[0, :IDX_DIM].set(ik_b)
    nl = lb_logits.shape[0]

    row = lambda w: pl.BlockSpec((None, tm, w), lambda b, i: (b, i, 0))
    pair = pl.BlockSpec((None, N_PAIRS, tm, LANES), lambda b, i: (b, 0, i, 0))
    vec = pl.BlockSpec((None, 1, d), lambda b, i: (b, 0, 0))
    out_shapes = (
        jax.ShapeDtypeStruct((bsz, s, REC_W), BF16),
        jax.ShapeDtypeStruct((bsz, s, REC_W), F32),
        jax.ShapeDtypeStruct((bsz, s, REC_W), BF16),
        jax.ShapeDtypeStruct((bsz, s, REC_W), BF16),
        jax.ShapeDtypeStruct((bsz, N_PAIRS, s, LANES), BF16),
        jax.ShapeDtypeStruct((bsz, N_PAIRS, s, LANES), BF16),
        jax.ShapeDtypeStruct((bsz, s // gt, ATT_W, gt), BF16),
        jax.ShapeDtypeStruct((bsz, N_PAIRS, s, LANES), BF16),
        jax.ShapeDtypeStruct((bsz, s, LANES), BF16),
        jax.ShapeDtypeStruct((bsz, IDX_HEADS, s), F32),
    )
    out_specs = (
        row(REC_W), row(REC_W), row(REC_W), row(REC_W), pair, pair,
        pl.BlockSpec((None, tm // gt, ATT_W, gt), lambda b, i: (b, i, 0, 0)),
        pair, row(LANES),
        pl.BlockSpec((None, IDX_HEADS, tm), lambda b, i: (b, 0, i)),
    )
    return pl.pallas_call(
        functools.partial(_inproj_kernel, layer=layer, gt=gt),
        grid=(bsz, s // tm),
        in_specs=[row(d), vec, vec, _const_spec((1, d)),
                  _const_spec((d, n_rec)), _const_spec((d, n_att)), _const_spec((d, n_idx)),
                  _const_spec((nl, REC_W)), _const_spec((1, ATT_W)), _const_spec((1, ATT_W)),
                  _const_spec((1, LANES)), _const_spec((1, LANES)), _const_spec((ATT_W, ATT_W))],
        out_specs=out_specs,
        out_shape=out_shapes,
        compiler_params=_params(("parallel", "parallel")),
        name="inproj",
    )(x, sc1.reshape(bsz, 1, d), sh1.reshape(bsz, 1, d), norm_g.reshape(1, d),
      w_rec, w_att, w_idx, lb_logits, qg, kg, ikg, ikb, pm)


N_LEVELS = 6


def _hgrn_tables():
    c = CHUNK
    w = np.zeros((N_LEVELS + 2, c, c), np.float32)
    am = np.zeros((N_LEVELS + 1, c, c), np.float32)
    t = np.arange(c)
    for m in range(N_LEVELS):
        hs = 1 << m
        blk = t // (2 * hs)
        upper = (t // hs) % 2 == 1
        ref = blk * 2 * hs + hs - 1
        for i in range(c):
            if upper[i]:
                w[m, i, ref[i] + 1:i + 1] = 1.0
            else:
                w[m, i, i + 1:ref[i] + 1] = 1.0
        am[m] = (blk[:, None] == blk[None, :]) & upper[:, None] & ~upper[None, :]
    w[N_LEVELS] = np.tril(np.ones((c, c)))
    w[N_LEVELS + 1] = np.triu(np.ones((c, c)), 1)
    am[N_LEVELS] = np.eye(c)
    w = w.reshape((N_LEVELS + 2) * c, c)
    return np.concatenate([w, w], axis=1), am


HGRN_UNROLL = 4


def _hgrn_kernel(q_ref, f_ref, v_ref, g_ref, og_ref, ww_ref, am_ref, o_ref, st_ref, ex_ref, *, n_chunks):
    @pl.when(pl.program_id(1) == 0)
    def _():
        st_ref[...] = jnp.zeros_like(st_ref)

    ww = ww_ref[...]
    c = CHUNK
    tbit = lax.broadcasted_iota(I32, (c, REC_DK), 0)
    items = [(cc, hd) for cc in range(HGRN_UNROLL) for hd in range(REC_HEADS)]

    def chunks(ci, carry):
        def blk(ref, it):
            r0 = pl.multiple_of((ci * HGRN_UNROLL + it[0]) * c, c)
            return ref.at[pl.ds(r0, c), it[1] * REC_DK:(it[1] + 1) * REC_DK]

        def ex(i, part):
            return ex_ref[i, part * c:(part + 1) * c, :]

        f = [blk(f_ref, it)[...] for it in items]
        q = [blk(q_ref, it)[...].astype(F32) for it in items]
        v = [blk(v_ref, it)[...] for it in items]
        k = [1.0 - fi for fi in f]
        for i, fi in enumerate(f):
            lf_hi, lf_lo = _split_bf16(jnp.log(fi))
            ex_ref[i] = jnp.exp(_dot(ww, jnp.concatenate([lf_hi, lf_lo], axis=0)))
        a = [am_ref[N_LEVELS] * _dot_nt(q[i].astype(BF16), k[i].astype(BF16)) for i in range(len(items))]
        for m in range(N_LEVELS):
            upper = ((tbit >> m) & 1) == 1
            for i in range(len(items)):
                tm_ = (ex(i, m) * jnp.where(upper, q[i], k[i])).astype(BF16)
                a[i] = a[i] + am_ref[m] * _dot_nt(tm_, tm_)
        qb = [(q[i] * ex(i, N_LEVELS)).astype(BF16) for i in range(len(items))]
        kb = [(k[i] * ex(i, N_LEVELS + 1)).astype(BF16) for i in range(len(items))]
        intra = [_dot(a[i].astype(BF16), v[i]) for i in range(len(items))]
        ut = [_dot_tn(v[i], kb[i]) for i in range(len(items))]
        for hd in range(REC_HEADS):
            st = st_ref[hd]
            for cc in range(HGRN_UNROLL):
                i = cc * REC_HEADS + hd
                o = intra[i] + _dot_nt(qb[i], st.astype(BF16))
                st = st * ex_ref[i, (N_LEVELS + 1) * c - 1:(N_LEVELS + 1) * c, :] + ut[i]
                ms = jnp.mean(o * o, axis=-1, keepdims=True)
                gate = blk(g_ref, items[i])[...].astype(F32)
                y = o * lax.rsqrt(ms + EPS) * og_ref[:, hd * REC_DV:(hd + 1) * REC_DV] * gate
                blk(o_ref, items[i])[...] = y.astype(BF16)
            st_ref[hd] = st
        return carry

    lax.fori_loop(0, n_chunks // HGRN_UNROLL, chunks, 0)


def _hgrn(q, f, v, g, out_g, *, ts):
    bsz, s, _ = q.shape
    ww_np, am_np = _hgrn_tables()
    ww = jnp.asarray(ww_np, BF16)
    am = jnp.asarray(am_np, F32)
    row = pl.BlockSpec((None, ts, REC_W), lambda b, i: (b, i, 0))
    return pl.pallas_call(
        functools.partial(_hgrn_kernel, n_chunks=ts // CHUNK),
        grid=(bsz, s // ts),
        in_specs=[row, row, row, row, _const_spec((1, REC_W)),
                  _const_spec(ww.shape), _const_spec(am.shape)],
        out_specs=row,
        out_shape=jax.ShapeDtypeStruct((bsz, s, REC_W), BF16),
        scratch_shapes=[pltpu.VMEM((REC_HEADS, REC_DV, REC_DK), F32),
                        pltpu.VMEM((HGRN_UNROLL * REC_HEADS, (N_LEVELS + 2) * CHUNK, REC_DK), F32)],
        compiler_params=_params(("parallel", "arbitrary")),
        name="hgrn",
    )(q, f, v, g, out_g.reshape(1, REC_W), ww, am)


def _t5_bucket(rel):
    nb = NUM_BUCKETS // 2
    max_exact = nb // 2
    ret = jnp.where(rel > 0, nb, 0)
    n = jnp.abs(rel)
    nf = jnp.maximum(n, 1).astype(F32)
    large = max_exact + (jnp.log(nf / max_exact) / np.log(MAX_DISTANCE / max_exact)
                         * (nb - max_exact)).astype(I32)
    large = jnp.minimum(large, nb - 1)
    return ret + jnp.where(n < max_exact, n, large)


def _bias_slots(rel_bias):
    n_rel = 3 * QBLK
    rel = jnp.arange(n_rel, dtype=I32) - 2 * QBLK
    tab = rel_bias[_t5_bucket(rel)].T
    far = rel_bias[_t5_bucket(jnp.full((1,), -2 * QBLK - 1, I32))].T
    rev = tab[:, ::-1]
    skew = jnp.tile(rev, (1, 2 * QBLK))[:, :2 * QBLK * (n_rel - 1)].reshape(ATT_HEADS, 2 * QBLK, n_rel - 1)
    near = skew[:, :, 2 * QBLK - 1:]
    nb = ((near - far[:, :, None]) * LOG2E).reshape(ATT_HEADS, 2, QBLK, QBLK)
    z = jnp.zeros((ATT_HEADS, 1, QBLK, QBLK), F32)
    return jnp.concatenate([z, nb, z], axis=1)


def _attn_kernel(aq_ref, ak_ref, avt_ref, iq_ref, ik_ref, iwt_ref, bias_ref, og_ref, o_ref,
                 key_ref, mb_ref, thr_ref, acc_ref, m_ref, l_ref, al_ref, s_ref, mx_ref, p_ref, qbd_ref, ot_ref, kb_ref, tie_ref,
                 *, gs, gk, k_sel):
    bi = pl.program_id(1)
    n_s = bi // (gs // QBLK) + 1
    lane = lax.broadcasted_iota(I32, (QBLK, LANES), 1)

    def block_diag(xq):
        zero = jnp.zeros_like(xq)
        return jnp.concatenate([jnp.where(lane < ATT_DH, xq, zero), jnp.where(lane >= ATT_DH, xq, zero)], axis=0)

    def srows(g):
        return pl.ds(pl.multiple_of(g * gs, gs), gs)

    for j in range(N_PAIRS):
        qbd_ref[j] = block_diag(iq_ref[j])
    w = iwt_ref[...]

    def group_scores(g):
        sc = _dot_nt(ik_ref[srows(g), :], qbd_ref[...].reshape(N_PAIRS * 2 * QBLK, LANES))
        acc = jnp.zeros((gs, LANES), F32)
        for hd in range(IDX_HEADS):
            acc = acc + jnp.maximum(sc[:, hd * LANES:(hd + 1) * LANES], 0.0) * w[hd:hd + 1, :]
        return acc

    def stats(sc, c):
        s3 = sc.reshape(gs // 8, 8, LANES)
        return (jnp.minimum(c[0], jnp.min(jnp.where(s3 == -jnp.inf, jnp.inf, s3), axis=0)),
                jnp.maximum(c[1], jnp.max(s3, axis=0)),
                c[2] + jnp.sum(jnp.where(s3 >= 0.0, 1, 0), axis=0),
                c[3] + jnp.sum(jnp.where(s3 > 0.0, 1, 0), axis=0))

    def idx_body(g, carry):
        sc = group_scores(g)
        key_ref[srows(g), :] = sc
        kb_ref[srows(g), :] = sc.astype(BF16)
        return stats(sc, carry)

    def idx_pair(u, carry):
        return idx_body(2 * u + 1, idx_body(2 * u, carry))

    z8 = jnp.zeros((8, LANES), I32)
    st8 = lax.fori_loop(0, (n_s - 1) // 2, idx_pair,
                        (jnp.full((8, LANES), jnp.inf, F32), jnp.full((8, LANES), -jnp.inf, F32), z8, z8))
    st8 = lax.cond((n_s - 1) % 2 == 1, lambda c: idx_body(n_s - 2, c), lambda c: c, st8)

    last = srows(n_s - 1)
    spos = (n_s - 1) * gs + lax.broadcasted_iota(I32, (gs, LANES), 0)
    tpos = bi * QBLK + lax.broadcasted_iota(I32, (gs, LANES), 1)
    sc_last = jnp.where((spos // CHUNK) <= (tpos // CHUNK), group_scores(n_s - 1), -jnp.inf)
    key_ref[last, :] = sc_last
    kb_ref[last, :] = sc_last.astype(BF16)
    mn8, mx8, ge8, gt8 = stats(sc_last, st8)

    thr_ref[...] = jnp.full(thr_ref.shape, F32_LOWEST, F32)
    tie_ref[0] = 0

    @pl.when((2 * bi + 2) * CHUNK > k_sel)
    def _():
        def groups(fn, init):
            return lax.fori_loop(0, n_s, lambda g, c: fn(key_ref[srows(g), :].reshape(gs // 8, 8, LANES), c), init)

        def count_ge(cand):
            acc = groups(lambda sc, c: c + jnp.sum(jnp.where(sc >= cand, 1, 0), axis=0), jnp.zeros((8, LANES), I32))
            return jnp.sum(acc, axis=0, keepdims=True)

        c0_ge = jnp.sum(ge8, axis=0, keepdims=True)
        c0_gt = jnp.sum(gt8, axis=0, keepdims=True)
        mn = jnp.min(mn8, axis=0, keepdims=True)
        mx = jnp.max(mx8, axis=0, keepdims=True)
        above = mx + (jnp.abs(mx) * 2.0 ** -20 + 1e-30)
        n_adm = (2 * bi + 1 + lax.broadcasted_iota(I32, (1, LANES), 1) // CHUNK) * CHUNK
        pos = c0_gt >= k_sel
        non_neg = c0_ge >= k_sel
        lo = jnp.where(non_neg, 0.0, mn)
        cnt_lo = jnp.where(non_neg, c0_ge, n_adm)
        hi = jnp.where(pos, above, 0.0)

        def midpoint(lo, hi):
            return 0.5 * lo + 0.5 * hi

        def unsettled(lo, hi, cnt_lo):
            mid = midpoint(lo, hi)
            return jnp.where(cnt_lo > k_sel, jnp.where(mid > lo, jnp.where(mid < hi, 1.0, 0.0), 0.0), 0.0)

        def count_rounded(cand):
            def one(g):
                kb = kb_ref[srows(g), :].reshape(gs // 16, 16, LANES)
                ind = jnp.where(kb >= cand, jnp.ones((), BF16), jnp.zeros((), BF16))
                parts = [ind[i] for i in range(gs // 16)]
                while len(parts) > 1:
                    parts = [a + b for a, b in zip(parts[0::2], parts[1::2])]
                return parts[0].astype(F32)

            def two(u, acc):
                second = jnp.where(2 * u + 1 < n_s, 1.0, 0.0)
                return acc + one(2 * u) + one(jnp.minimum(2 * u + 1, n_s - 1)) * second

            return jnp.sum(lax.fori_loop(0, (n_s + 1) // 2, two, jnp.zeros((16, LANES), F32)), axis=0, keepdims=True)

        def coarse(_, st):
            lo_b, hi_b, hi_ok = st
            cand = midpoint(lo_b, hi_b).astype(BF16)
            up = count_rounded(cand) >= k_sel
            c = cand.astype(F32)
            return jnp.where(up, c, lo_b), jnp.where(up, hi_b, c), jnp.where(up, hi_ok, 1.0)

        open0 = unsettled(lo, hi, cnt_lo) > 0.5
        mx_b = mx.astype(BF16).astype(F32)
        lo_b, hi_b, hi_ok = lax.fori_loop(
            0, COARSE_STEPS, coarse,
            (lo.astype(BF16).astype(F32), jnp.where(pos, mx_b + (jnp.abs(mx_b) * 2.0 ** -6 + 1e-30), 0.0),
             jnp.zeros((1, LANES), F32)))
        lo_c = lo_b - (jnp.abs(lo_b) * 2.0 ** -7 + 1e-37)
        raise_lo = jnp.logical_and(open0, lo_c > lo)
        lower_hi = jnp.logical_and(open0, jnp.logical_and(hi_ok > 0.5, hi_b < hi))
        lo = jnp.where(raise_lo, lo_c, lo)
        cnt_lo = jnp.where(raise_lo, n_adm, cnt_lo)
        hi = jnp.where(lower_hi, hi_b, hi)

        def cond(st):
            return jnp.logical_and(st[3] > 0.5, st[4] < BISECT_CAP)

        def halve(lo, hi, cnt_lo):
            act = unsettled(lo, hi, cnt_lo) > 0.5
            mid = midpoint(lo, hi)
            c = count_ge(mid)
            up = jnp.logical_and(act, c >= k_sel)
            dn = jnp.logical_and(act, c < k_sel)
            return jnp.where(up, mid, lo), jnp.where(dn, mid, hi), jnp.where(up, c, cnt_lo)

        def body(st):
            lo, hi, cnt_lo = halve(*halve(*st[:3]))
            return lo, hi, cnt_lo, jnp.max(unsettled(lo, hi, cnt_lo)), st[4] + 2

        st = lax.while_loop(cond, body, (lo, hi, cnt_lo, jnp.max(unsettled(lo, hi, cnt_lo)), jnp.int32(0)))
        thr_ref[...] = jnp.broadcast_to(st[0], thr_ref.shape)
        tie_ref[0] = (jnp.max(jnp.where(st[2] > k_sel, 1.0, 0.0)) > 0.5).astype(I32)

    thr = thr_ref[0:1, :]

    @pl.when(tie_ref[0] == 0)
    def _():
        def mb_body(g, carry):
            mb_ref[srows(g), :] = jnp.where(key_ref[srows(g), :] >= thr, 0.0, NEG)
            return carry

        lax.fori_loop(0, n_s, mb_body, 0)

    @pl.when(tie_ref[0] != 0)
    def _():
        def count_gt(g, acc):
            sc = key_ref[srows(g), :].reshape(gs // 8, 8, LANES)
            return acc + jnp.sum(jnp.where(sc > thr, 1, 0), axis=0)

        n_gt = jnp.sum(lax.fori_loop(0, n_s, count_gt, jnp.zeros((8, LANES), I32)), axis=0, keepdims=True)
        room = (k_sel - n_gt).astype(F32)
        r = lax.broadcasted_iota(I32, (QBLK, QBLK), 0)
        c = lax.broadcasted_iota(I32, (QBLK, QBLK), 1)
        tri = jnp.where(c <= r, 1.0, 0.0).astype(BF16)

        def mb_body(g, seen):
            for t in range(gs // QBLK):
                rs = pl.ds(pl.multiple_of(g * gs + t * QBLK, QBLK), QBLK)
                sc = key_ref[rs, :]
                eq = jnp.where(sc == thr, 1.0, 0.0)
                rank = _dot(tri, eq.astype(BF16)) + seen
                keep = jnp.where(sc > thr, 1.0, jnp.where(rank <= room, eq, 0.0))
                mb_ref[rs, :] = jnp.where(keep > 0.5, 0.0, NEG)
                seen = seen + jnp.sum(eq, axis=0, keepdims=True)
            return seen

        lax.fori_loop(0, n_s, mb_body, jnp.zeros((1, LANES), F32))

    spg = gk // QBLK
    n_g = bi // spg + 1
    for j in range(N_PAIRS):
        qbd_ref[j] = block_diag(aq_ref[j])
    m_ref[...] = jnp.full(m_ref.shape, NEG, F32)
    l_ref[...] = jnp.zeros_like(l_ref)
    acc_ref[...] = jnp.zeros_like(acc_ref)
    al_ref[...] = jnp.ones_like(al_ref)
    p_ref[1] = jnp.zeros(p_ref.shape[1:], BF16)

    def rows(g):
        return pl.ds(pl.multiple_of(g * gk, gk), gk)

    def scores(t, slot):
        mb = mb_ref[rows(t), :]
        mb2 = jnp.concatenate([mb, mb], axis=1)
        for j in range(N_PAIRS):
            s = _dot_nt(ak_ref[j, rows(t), :], qbd_ref[j]) + mb2
            s_ref[slot, j] = s
            mx_ref[slot, j] = jnp.max(s, axis=0, keepdims=True)

    def pv(t, slot):
        return [_dot(avt_ref[t, j * LANES:(j + 1) * LANES, :], p_ref[slot, j]) for j in range(N_PAIRS)]

    def step(g, cur):
        prv = 1 - cur

        @pl.when(g >= n_g - 2)
        def _():
            for j in range(N_PAIRS):
                parts = []
                for st in range(spg):
                    slot = jnp.clip(g * spg + st - bi + 2, 0, 3)
                    parts.append(jnp.concatenate([bias_ref[2 * j, slot], bias_ref[2 * j + 1, slot]], axis=1))
                s = s_ref[cur, j] + jnp.concatenate(parts, axis=0)
                s_ref[cur, j] = s
                mx_ref[cur, j] = jnp.max(s, axis=0, keepdims=True)

        al_old = [al_ref[j] for j in range(N_PAIRS)]
        for j in range(N_PAIRS):
            m_old = m_ref[j]
            m_new = jnp.maximum(m_old, mx_ref[cur, j])
            p = jnp.exp2(s_ref[cur, j] - m_new)
            alpha = jnp.exp2(m_old - m_new)
            al_ref[j] = alpha
            l_ref[j] = l_ref[j] * alpha + jnp.sum(p, axis=0, keepdims=True)
            m_ref[j] = m_new
            p_ref[cur, j] = p.astype(BF16)
        o_prev = pv(jnp.maximum(g - 1, 0), prv)
        scores(jnp.minimum(g + 1, n_g - 1), prv)
        for j in range(N_PAIRS):
            acc_ref[j] = acc_ref[j] * al_old[j] + o_prev[j]

    scores(0, 0)

    def pipe_body(u, c):
        step(2 * u, 0)

        @pl.when(2 * u + 1 < n_g)
        def _():
            step(2 * u + 1, 1)
        return c

    lax.fori_loop(0, (n_g + 1) // 2, pipe_body, 0)
    o_last = pv(n_g - 1, (n_g - 1) % 2)

    for j in range(N_PAIRS):
        acc = acc_ref[j] * al_ref[j] + o_last[j]
        l = l_ref[j]
        o0 = acc[0:ATT_DH, 0:LANES] / l[:, 0:LANES]
        o1 = acc[ATT_DH:2 * ATT_DH, LANES:2 * LANES] / l[:, LANES:2 * LANES]
        ot_ref[j * LANES:(j + 1) * LANES, :] = jnp.concatenate([o0, o1], axis=0)

    ot = ot_ref[...]
    og = og_ref[...]
    outs = []
    for hd in range(ATT_HEADS):
        oh = ot[hd * ATT_DH:(hd + 1) * ATT_DH, :]
        ms = jnp.mean(oh * oh, axis=0, keepdims=True)
        outs.append(oh * lax.rsqrt(ms + EPS) * og[hd * ATT_DH:(hd + 1) * ATT_DH, :])
    o_ref[...] = jnp.concatenate(outs, axis=0).T.astype(BF16)


def _attention(aq, ak, avt, iq, ik, iwt, rel_bias, out_g, *, gk):
    bsz, _, s, _ = aq.shape
    k_sel = min(TOPK_MAX, s // 4)
    bias = _bias_slots(rel_bias)
    qpair = pl.BlockSpec((None, N_PAIRS, QBLK, LANES), lambda b, i: (b, 0, i, 0))
    full = lambda shp: pl.BlockSpec((None,) + shp, lambda b, i: (b,) + (0,) * len(shp),
                                    pipeline_mode=pl.Buffered(1))
    return pl.pallas_call(
        functools.partial(_attn_kernel, gs=_tile(s, 512), gk=gk, k_sel=k_sel),
        grid=(bsz, s // QBLK),
        in_specs=[qpair, full((N_PAIRS, s, LANES)), full((s // gk, ATT_W, gk)), qpair,
                  full((s, LANES)), pl.BlockSpec((None, IDX_HEADS, QBLK), lambda b, i: (b, 0, i)),
                  _const_spec(bias.shape), _const_spec((ATT_W, 1))],
        out_specs=pl.BlockSpec((None, QBLK, ATT_W), lambda b, i: (b, i, 0)),
        out_shape=jax.ShapeDtypeStruct((bsz, s, ATT_W), BF16),
        scratch_shapes=[pltpu.VMEM((s, LANES), F32),
                        pltpu.VMEM((s, LANES), F32),
                        pltpu.VMEM((8, LANES), F32),
                        pltpu.VMEM((N_PAIRS, 2 * ATT_DH, 2 * LANES), F32),
                        pltpu.VMEM((N_PAIRS, 1, 2 * LANES), F32),
                        pltpu.VMEM((N_PAIRS, 1, 2 * LANES), F32),
                        pltpu.VMEM((N_PAIRS, 1, 2 * LANES), F32),
                        pltpu.VMEM((2, N_PAIRS, gk, 2 * LANES), F32),
                        pltpu.VMEM((2, N_PAIRS, 1, 2 * LANES), F32),
                        pltpu.VMEM((2, N_PAIRS, gk, 2 * LANES), BF16),
                        pltpu.VMEM((N_PAIRS, 2 * QBLK, LANES), BF16),
                        pltpu.VMEM((ATT_W, LANES), F32),
                        pltpu.VMEM((s, LANES), BF16),
                        pltpu.SMEM((1,), I32)],
        compiler_params=_params(("parallel", "arbitrary")),
        name="attn",
    )(aq, ak, avt, iq, ik, iwt, bias, out_g.reshape(ATT_W, 1))


def _outproj_kernel(rec_ref, att_ref, x_ref, g1_ref, sc_ref, sh_ref, ng_ref, wo_ref, wr_ref, br_ref,
                    x1_ref, h2_ref, comb_ref):
    mix = _dot(rec_ref[...], wo_ref[0:REC_W, :]) + _dot(att_ref[...], wo_ref[REC_W:REC_W + ATT_W, :])
    x1 = x_ref[...] + g1_ref[...] * mix
    x1_ref[...] = x1
    ms = jnp.mean(x1 * x1, axis=-1, keepdims=True)
    h2 = (x1 * lax.rsqrt(ms + EPS) * ng_ref[...] * (1.0 + sc_ref[...]) + sh_ref[...]).astype(BF16)
    h2_ref[...] = h2

    lg = _dot(h2, wr_ref[...]) + br_ref[...]
    lane = lax.broadcasted_iota(I32, lg.shape, 1)
    big = jnp.int32(2 * LANES)
    is_g = lane < N_GROUPS
    gl = jnp.where(is_g, lg, -jnp.inf)
    gmax = jnp.max(gl, axis=-1, keepdims=True)
    gate = 1.0 / jnp.sum(jnp.where(is_g, jnp.exp(lg - gmax), 0.0), axis=-1, keepdims=True)
    gtop = jnp.min(jnp.where(gl == gmax, lane, big), axis=-1, keepdims=True)
    e_lo = ROUTE_OFF + EXPERTS_PER_GROUP * gtop
    el = jnp.where((lane >= e_lo) & (lane < e_lo + EXPERTS_PER_GROUP), lg, -jnp.inf)
    v1 = jnp.max(el, axis=-1, keepdims=True)
    i1 = jnp.min(jnp.where(el == v1, lane, big), axis=-1, keepdims=True)
    el2 = jnp.where(lane == i1, -jnp.inf, el)
    v2 = jnp.max(el2, axis=-1, keepdims=True)
    i2 = jnp.min(jnp.where(el2 == v2, lane, big), axis=-1, keepdims=True)
    e2 = jnp.exp(v2 - v1)
    w1 = gate / (1.0 + e2)
    w2 = gate * e2 / (1.0 + e2)
    comb = jnp.where(lane == i1, w1, 0.0) + jnp.where(lane == i2, w2, 0.0)
    comb_ref[...] = jnp.where(lane == 0, gtop.astype(F32), comb)


def _outproj(rec, att, x, g1, sc2, sh2, norm_g, w_out, w_rg, b_rg, w_re, b_re, *, tm):
    bsz, s, d = x.shape
    wr = jnp.zeros((d, LANES), F32).at[:, :N_GROUPS].set(w_rg).at[:, ROUTE_OFF:ROUTE_OFF + N_EXPERTS].set(w_re)
    br = jnp.zeros((1, LANES), F32).at[0, :N_GROUPS].set(b_rg).at[0, ROUTE_OFF:ROUTE_OFF + N_EXPERTS].set(b_re)
    row = lambda w: pl.BlockSpec((None, tm, w), lambda b, i: (b, i, 0))
    vec = pl.BlockSpec((None, 1, d), lambda b, i: (b, 0, 0))
    return pl.pallas_call(
        _outproj_kernel,
        grid=(bsz, s // tm),
        in_specs=[row(REC_W), row(ATT_W), row(d), vec, vec, vec, _const_spec((1, d)),
                  _const_spec((REC_W + ATT_W, d)), _const_spec((d, LANES)), _const_spec((1, LANES))],
        out_specs=(row(d), row(d), row(LANES)),
        out_shape=(jax.ShapeDtypeStruct((bsz, s, d), F32),
                   jax.ShapeDtypeStruct((bsz, s, d), BF16),
                   jax.ShapeDtypeStruct((bsz, s, LANES), F32)),
        compiler_params=_params(("parallel", "parallel")),
        name="outproj",
    )(rec, att, x, g1.reshape(bsz, 1, d), sc2.reshape(bsz, 1, d), sh2.reshape(bsz, 1, d),
      norm_g.reshape(1, d), w_out.astype(BF16), wr.astype(BF16), br)


MOE_ALIGN = 16
MOE_LANE_SHIFT = 32
MOE_CHUNK = 288
MOE_RB = 256


def _moe_kernel(h_ref, comb_ref, x1_ref, g2_ref, w1_ref, w3_ref, w2_ref, o_ref,
                p_ref, pt_ref, hs_ref, cs_ref, ys_ref, seg_ref, *, mc):
    e = pl.program_id(2)
    tm = h_ref.shape[0]
    npad = hs_ref.shape[0]

    @pl.when(e == 0)
    def _():
        comb = comb_ref[...]
        lane = lax.broadcasted_iota(I32, (tm, LANES), 1)
        gid = comb[:, 0:1].astype(I32)
        oh = jnp.where(lane == gid, 1.0, 0.0)
        ohb = oh.astype(BF16)
        pre = []
        for rb in range(tm // MOE_RB):
            r = rb * MOE_RB + lax.broadcasted_iota(I32, (MOE_RB, tm), 0)
            c = lax.broadcasted_iota(I32, (MOE_RB, tm), 1)
            pre.append(_dot(jnp.where(c < r, 1.0, 0.0).astype(BF16), ohb))
        prefix = jnp.concatenate(pre, axis=0)
        cnt = jnp.sum(oh, axis=0, keepdims=True).astype(I32)
        cnt_al = ((cnt + (MOE_ALIGN - 1)) // MOE_ALIGN) * MOE_ALIGN
        base = jnp.sum(jnp.where(lane < gid, cnt_al.astype(F32), 0.0), axis=-1, keepdims=True)
        rank = jnp.sum(prefix * oh, axis=-1, keepdims=True)
        pos = (base + rank).astype(I32)
        pos_row = jnp.broadcast_to(pos.astype(F32), (tm, LANES)).T[0:1, :].astype(I32)
        for rb in range(tm // MOE_RB):
            sl = slice(rb * MOE_RB, (rb + 1) * MOE_RB)
            coln = lax.broadcasted_iota(I32, (MOE_RB, npad), 1)
            pt_ref[sl, :] = jnp.where(coln == pos[sl], 1.0, 0.0).astype(BF16)
        for rb in range(npad // LANES):
            sl = slice(rb * LANES, (rb + 1) * LANES)
            rown = rb * LANES + lax.broadcasted_iota(I32, (LANES, tm), 0)
            p_ref[sl, :] = jnp.where(rown == pos_row, 1.0, 0.0).astype(BF16)
        c_hi = comb.astype(BF16)
        c_mid, c_lo = _split_bf16(comb - c_hi.astype(F32))
        packed = (c_hi.astype(F32) + pltpu.roll(c_mid.astype(F32), MOE_LANE_SHIFT, axis=1)
                  + pltpu.roll(c_lo.astype(F32), 2 * MOE_LANE_SHIFT, axis=1)).astype(BF16)
        srt = _dot(p_ref[...], jnp.concatenate([h_ref[...], packed], axis=1))
        d = h_ref.shape[1]
        hs_ref[...] = srt[:, :d].astype(BF16)
        cp = srt[:, d:]
        cs_ref[...] = (cp + pltpu.roll(cp, LANES - MOE_LANE_SHIFT, axis=1)
                       + pltpu.roll(cp, LANES - 2 * MOE_LANE_SHIFT, axis=1))
        ys_ref[...] = jnp.zeros_like(ys_ref)
        start = jnp.int32(0)
        for g in range(N_GROUPS):
            seg_ref[g] = start
            seg_ref[N_GROUPS + g] = cnt[0, g]
            start = start + cnt_al[0, g]

    grp = e // EXPERTS_PER_GROUP
    start = seg_ref[grp]
    n_rows = seg_ref[N_GROUPS + grp]

    def chunk(ci, carry):
        rs = pl.ds(pl.multiple_of(start + ci * mc, MOE_ALIGN), mc)
        hb = hs_ref[rs, :]
        cw = cs_ref[rs, :]
        lane = lax.broadcasted_iota(I32, cw.shape, 1)
        col = jnp.sum(jnp.where(lane == e + ROUTE_OFF, cw, 0.0), axis=-1, keepdims=True)
        he = _silu(_dot(hb, w1_ref[...])) * _dot(hb, w3_ref[...]) * col
        ys_ref[rs, :] += _dot(he.astype(BF16), w2_ref[...])
        return carry

    lax.fori_loop(0, (n_rows + mc - 1) // mc, chunk, 0)

    @pl.when(e == N_EXPERTS - 1)
    def _():
        o_ref[...] = x1_ref[...] + g2_ref[...] * _dot(pt_ref[...], ys_ref[...].astype(BF16))


def _moe(h2, comb, x1, g2, w1, w3, w2, *, tm):
    bsz, s, d = x1.shape
    mc = MOE_CHUNK
    npad = -(-(tm + N_GROUPS * MOE_ALIGN + mc) // LANES) * LANES
    row = lambda w, **kw: pl.BlockSpec((None, tm, w), lambda b, i, e: (b, i, 0), **kw)
    once = dict(pipeline_mode=pl.Buffered(1))
    return pl.pallas_call(
        functools.partial(_moe_kernel, mc=mc),
        grid=(bsz, s // tm, N_EXPERTS),
        in_specs=[row(d, **once), row(LANES), row(d, **once),
                  pl.BlockSpec((None, 1, d), lambda b, i, e: (b, 0, 0)),
                  pl.BlockSpec((None, d, D_EXPERT), lambda b, i, e: (e, 0, 0)),
                  pl.BlockSpec((None, d, D_EXPERT), lambda b, i, e: (e, 0, 0)),
                  pl.BlockSpec((None, D_EXPERT, d), lambda b, i, e: (e, 0, 0))],
        out_specs=row(d),
        out_shape=jax.ShapeDtypeStruct((bsz, s, d), F32),
        scratch_shapes=[pltpu.VMEM((npad, tm), BF16),
                        pltpu.VMEM((tm, npad), BF16),
                        pltpu.VMEM((npad, d), BF16),
                        pltpu.VMEM((npad, LANES), F32),
                        pltpu.VMEM((npad, d), F32),
                        pltpu.SMEM((2 * N_GROUPS,), I32)],
        compiler_params=_params(("parallel", "parallel", "arbitrary")),
        name="moe",
    )(h2, comb, x1, g2.reshape(bsz, 1, d), w1.astype(BF16), w3.astype(BF16), w2.astype(BF16))


def _tile(s, pref):
    t = min(s, pref)
    assert s % t == 0
    return t


def kernel(x, c, w_ada, b_ada, norm1_g, norm2_g, w_in, lb_logits, rec_out_g, q_norm_g, k_norm_g,
           idx_k_norm_g, idx_k_norm_b, attn_out_g, rel_bias, w_out, w_rg, b_rg, w_re, b_re, w1, w3, w2):
    bsz, s, d = x.shape
    depth = w_ada.shape[0]
    gk = _tile(s, 512)
    for l in range(depth):
        mod = _adaln(c, w_ada[l], b_ada[l])
        sh1, sc1, g1, sh2, sc2, g2 = jnp.split(mod, 6, axis=-1)
        q, f, v, g, aq, ak, avt, iq, ik, iwt = _inproj(
            x, sc1, sh1, norm1_g[l], w_in[l], lb_logits, q_norm_g[l], k_norm_g[l],
            idx_k_norm_g[l], idx_k_norm_b[l], layer=l, gt=gk, tm=_tile(s, 512))
        rec = _hgrn(q, f, v, g, rec_out_g[l], ts=_tile(s, 256))
        att = _attention(aq, ak, avt, iq, ik, iwt, rel_bias, attn_out_g[l], gk=gk)
        x1, h2, comb = _outproj(rec, att, x, g1, sc2, sh2, norm2_g[l], w_out[l],
                                w_rg[l], b_rg[l], w_re[l], b_re[l], tm=_tile(s, 512))
        x = _moe(h2, comb, x1, g2, w1[l], w3[l], w2[l], tm=_tile(s, 1024))
    return x
```

```python
import functools

import numpy as np
import jax
import jax.numpy as jnp
from jax import lax
from jax.experimental import pallas as pl
from jax.experimental.pallas import tpu as pltpu

F32 = jnp.float32
BF16 = jnp.bfloat16
I32 = jnp.int32

CHUNK = 64
QBLK = 128
EPS = 1e-6
REC_HEADS = 4
REC_DK = 128
REC_DV = 128
REC_W = REC_HEADS * REC_DV
ATT_HEADS = 8
ATT_DH = 64
ATT_W = ATT_HEADS * ATT_DH
IDX_HEADS = 8
IDX_DIM = 64
TOPK_MAX = 256
NUM_BUCKETS = 32
MAX_DISTANCE = 128
N_GROUPS = 4
EXPERTS_PER_GROUP = 4
N_EXPERTS = N_GROUPS * EXPERTS_PER_GROUP
D_EXPERT = 512

LANES = 128
V7X_VMEM_LIMIT = 56 * 1024 * 1024
F32_LOWEST = float(np.finfo(np.float32).min)
BISECT_CAP = 320
NEG = -1e30
LOG2E = float(np.log2(np.e))
N_PAIRS = ATT_HEADS // 2
PV_ROWS = 2 * ATT_DH + 16
ROUTE_OFF = N_GROUPS


def _dot(a, b):
    return jnp.dot(a, b, preferred_element_type=F32)


def _dot_nt(a, b):
    return lax.dot_general(a, b, (((1,), (1,)), ((), ())), preferred_element_type=F32)


def _dot_tn(a, b):
    return lax.dot_general(a, b, (((0,), (0,)), ((), ())), preferred_element_type=F32)


def _split_bf16(a):
    hi = a.astype(BF16)
    lo = (a - hi.astype(F32)).astype(BF16)
    return hi, lo


def _silu(a):
    return a * jax.nn.sigmoid(a)


def _const_spec(shape):
    nd = len(shape)
    return pl.BlockSpec(shape, lambda *_: (0,) * nd, pipeline_mode=pl.Buffered(1))


def _params(sem):
    return pltpu.CompilerParams(dimension_semantics=sem, vmem_limit_bytes=V7X_VMEM_LIMIT)


def _adaln_kernel(c_ref, w_ref, b_ref, o_ref):
    a_hi, a_lo = _split_bf16(_silu(c_ref[...]))
    w_hi, w_lo = _split_bf16(w_ref[...])
    o_ref[...] = _dot(a_hi, w_hi) + _dot(a_lo, w_hi) + _dot(a_hi, w_lo) + b_ref[...]


def _adaln(c, w, b):
    bsz, d = c.shape
    n = w.shape[1]
    rows = 16
    bn = 1024
    cp = jnp.zeros((rows, d), F32).at[:bsz].set(c)
    out = pl.pallas_call(
        _adaln_kernel,
        grid=(n // bn,),
        in_specs=[pl.BlockSpec((rows, d), lambda i: (0, 0)),
                  pl.BlockSpec((d, bn), lambda i: (0, i)),
                  pl.BlockSpec((1, bn), lambda i: (0, i))],
        out_specs=pl.BlockSpec((rows, bn), lambda i: (0, i)),
        out_shape=jax.ShapeDtypeStruct((rows, n), F32),
        compiler_params=_params(("parallel",)),
        name="adaln",
    )(cp, w, b.reshape(1, n))
    return out[:bsz]


def _inproj_kernel(x_ref, sc_ref, sh_ref, ng_ref, wrec_ref, watt_ref, widx_ref, lbl_ref,
                   qg_ref, kg_ref, ikg_ref, ikb_ref, pm_ref,
                   q_ref, f_ref, v_ref, g_ref, aq_ref, ak_ref, avt_ref, iq_ref, ik_ref, iwt_ref,
                   *, layer, gt):
    x = x_ref[...]
    tm = x.shape[0]
    ms = jnp.mean(x * x, axis=-1, keepdims=True)
    h = x * lax.rsqrt(ms + EPS) * ng_ref[...] * (1.0 + sc_ref[...]) + sh_ref[...]
    hb = h.astype(BF16)

    zr = _dot(hb, wrec_ref[...])
    q_ref[...] = _silu(zr[:, 0:REC_W]).astype(BF16)
    lbl = lbl_ref[...]
    e = jnp.exp(lbl - jnp.max(lbl, axis=0, keepdims=True))
    sm = e / jnp.sum(e, axis=0, keepdims=True)
    lb = jnp.sum(sm[0:layer + 1], axis=0, keepdims=True)
    f_ref[...] = lb + (1.0 - lb) * jax.nn.sigmoid(zr[:, REC_W:2 * REC_W])
    v_ref[...] = zr[:, 2 * REC_W:3 * REC_W].astype(BF16)
    g_ref[...] = _silu(zr[:, 3 * REC_W:4 * REC_W]).astype(BF16)

    za = _dot(hb, watt_ref[...])
    aq = za[:, 0:ATT_W]
    ak = za[:, ATT_W:2 * ATT_W]
    av = za[:, 2 * ATT_W:3 * ATT_W]
    pm = pm_ref[...]
    aqn = aq * lax.rsqrt(_dot((aq * aq).astype(BF16), pm) + EPS) * qg_ref[...]
    akn = ak * lax.rsqrt(_dot((ak * ak).astype(BF16), pm) + EPS) * kg_ref[...]
    for j in range(N_PAIRS):
        aq_ref[j] = aqn[:, j * LANES:(j + 1) * LANES].astype(BF16)
        ak_ref[j] = akn[:, j * LANES:(j + 1) * LANES].astype(BF16)
    for t in range(tm // gt):
        avt = av[t * gt:(t + 1) * gt, :].T.astype(BF16)
        for j in range(N_PAIRS):
            avt_ref[t, j * PV_ROWS:j * PV_ROWS + LANES, :] = avt[j * LANES:(j + 1) * LANES, :]
            avt_ref[t, j * PV_ROWS + LANES:(j + 1) * PV_ROWS, :] = jnp.ones((PV_ROWS - LANES, gt), BF16)

    zi = _dot(hb, widx_ref[...])
    for j in range(N_PAIRS):
        iq_ref[j] = zi[:, j * LANES:(j + 1) * LANES].astype(BF16)
    tail = zi[:, IDX_HEADS * IDX_DIM:IDX_HEADS * IDX_DIM + LANES]
    lane = lax.broadcasted_iota(I32, tail.shape, 1)
    is_k = lane < IDX_DIM
    mu = jnp.sum(jnp.where(is_k, tail, 0.0), axis=-1, keepdims=True) * (1.0 / IDX_DIM)
    dlt = jnp.where(is_k, tail - mu, 0.0)
    var = jnp.sum(dlt * dlt, axis=-1, keepdims=True) * (1.0 / IDX_DIM)
    ikn = dlt * lax.rsqrt(var + EPS) * ikg_ref[...] + ikb_ref[...]
    ik_ref[...] = jnp.where(is_k, ikn, pltpu.roll(ikn, IDX_DIM, axis=1)).astype(BF16)
    iwt_ref[...] = tail.T[IDX_DIM:IDX_DIM + IDX_HEADS, :] * (IDX_HEADS ** -0.5 * IDX_DIM ** -0.5)


def _inproj(x, sc1, sh1, norm_g, w_in, lb_logits, q_g, k_g, ik_g, ik_b, *, layer, gt, tm):
    bsz, s, d = x.shape
    n_rec = 4 * REC_W
    n_att = 3 * ATT_W
    n_idx = IDX_HEADS * IDX_DIM + LANES
    wb = w_in.astype(BF16)
    w_rec = wb[:, :n_rec]
    w_att = wb[:, n_rec:n_rec + n_att]
    w_idx = jnp.zeros((d, n_idx), BF16).at[:, :w_in.shape[1] - n_rec - n_att].set(wb[:, n_rec + n_att:])
    pm = jnp.asarray(np.kron(np.eye(ATT_HEADS), np.full((ATT_DH, ATT_DH), 1.0 / ATT_DH)), BF16)
    qg = jnp.tile(q_g, ATT_HEADS).reshape(1, ATT_W) * (ATT_DH ** -0.5 * LOG2E)
    kg = jnp.tile(k_g, ATT_HEADS).reshape(1, ATT_W)
    ikg = jnp.zeros((1, LANES), F32).at[0, :IDX_DIM].set(ik_g)
    ikb = jnp.zeros((1, LANES), F32).at[0, :IDX_DIM].set(ik_b)
    nl = lb_logits.shape[0]

    row = lambda w: pl.BlockSpec((None, tm, w), lambda b, i: (b, i, 0))
    pair = pl.BlockSpec((None, N_PAIRS, tm, LANES), lambda b, i: (b, 0, i, 0))
    vec = pl.BlockSpec((None, 1, d), lambda b, i: (b, 0, 0))
    out_shapes = (
        jax.ShapeDtypeStruct((bsz, s, REC_W), BF16),
        jax.ShapeDtypeStruct((bsz, s, REC_W), F32),
        jax.ShapeDtypeStruct((bsz, s, REC_W), BF16),
        jax.ShapeDtypeStruct((bsz, s, REC_W), BF16),
        jax.ShapeDtypeStruct((bsz, N_PAIRS, s, LANES), BF16),
        jax.ShapeDtypeStruct((bsz, N_PAIRS, s, LANES), BF16),
        jax.ShapeDtypeStruct((bsz, s // gt, N_PAIRS * PV_ROWS, gt), BF16),
        jax.ShapeDtypeStruct((bsz, N_PAIRS, s, LANES), BF16),
        jax.ShapeDtypeStruct((bsz, s, LANES), BF16),
        jax.ShapeDtypeStruct((bsz, IDX_HEADS, s), F32),
    )
    out_specs = (
        row(REC_W), row(REC_W), row(REC_W), row(REC_W), pair, pair,
        pl.BlockSpec((None, tm // gt, N_PAIRS * PV_ROWS, gt), lambda b, i: (b, i, 0, 0)),
        pair, row(LANES),
        pl.BlockSpec((None, IDX_HEADS, tm), lambda b, i: (b, 0, i)),
    )
    return pl.pallas_call(
        functools.partial(_inproj_kernel, layer=layer, gt=gt),
        grid=(bsz, s // tm),
        in_specs=[row(d), vec, vec, _const_spec((1, d)),
                  _const_spec((d, n_rec)), _const_spec((d, n_att)), _const_spec((d, n_idx)),
                  _const_spec((nl, REC_W)), _const_spec((1, ATT_W)), _const_spec((1, ATT_W)),
                  _const_spec((1, LANES)), _const_spec((1, LANES)), _const_spec((ATT_W, ATT_W))],
        out_specs=out_specs,
        out_shape=out_shapes,
        compiler_params=_params(("parallel", "parallel")),
        name="inproj",
    )(x, sc1.reshape(bsz, 1, d), sh1.reshape(bsz, 1, d), norm_g.reshape(1, d),
      w_rec, w_att, w_idx, lb_logits, qg, kg, ikg, ikb, pm)


N_LEVELS = 6


def _hgrn_tables():
    c = CHUNK
    w = np.zeros((N_LEVELS + 2, c, c), np.float32)
    am = np.zeros((N_LEVELS + 1, c, c), np.float32)
    t = np.arange(c)
    for m in range(N_LEVELS):
        hs = 1 << m
        blk = t // (2 * hs)
        upper = (t // hs) % 2 == 1
        ref = blk * 2 * hs + hs - 1
        for i in range(c):
            if upper[i]:
                w[m, i, ref[i] + 1:i + 1] = 1.0
            else:
                w[m, i, i + 1:ref[i] + 1] = 1.0
        am[m] = (blk[:, None] == blk[None, :]) & upper[:, None] & ~upper[None, :]
    w[N_LEVELS] = np.tril(np.ones((c, c)))
    w[N_LEVELS + 1] = np.triu(np.ones((c, c)), 1)
    am[N_LEVELS] = np.eye(c)
    w = w.reshape((N_LEVELS + 2) * c, c)
    return np.concatenate([w, w], axis=1), am


HGRN_UNROLL = 4


def _hgrn_kernel(q_ref, f_ref, v_ref, g_ref, og_ref, ww_ref, am_ref, o_ref, st_ref, ex_ref, *, n_chunks):
    @pl.when(pl.program_id(1) == 0)
    def _():
        st_ref[...] = jnp.zeros_like(st_ref)

    ww = ww_ref[...]
    c = CHUNK
    tbit = lax.broadcasted_iota(I32, (c, REC_DK), 0)
    items = [(cc, hd) for cc in range(HGRN_UNROLL) for hd in range(REC_HEADS)]

    def chunks(ci, carry):
        def blk(ref, it):
            r0 = pl.multiple_of((ci * HGRN_UNROLL + it[0]) * c, c)
            return ref.at[pl.ds(r0, c), it[1] * REC_DK:(it[1] + 1) * REC_DK]

        def ex(i, part, last_row=False):
            r0 = (part + 1) * c - 1 if last_row else part * c
            return ex_ref[items[i][0], r0:(part + 1) * c, items[i][1] * REC_DK:(items[i][1] + 1) * REC_DK]

        f = [blk(f_ref, it)[...] for it in items]
        q = [blk(q_ref, it)[...].astype(F32) for it in items]
        v = [blk(v_ref, it)[...] for it in items]
        k = [1.0 - fi for fi in f]
        for cc in range(HGRN_UNROLL):
            cols = [jnp.concatenate(_split_bf16(jnp.log(f[cc * REC_HEADS + hd])), axis=0) for hd in range(REC_HEADS)]
            ex_ref[cc] = jnp.exp(_dot(ww, jnp.concatenate(cols, axis=1)))
        a = [am_ref[N_LEVELS] * _dot_nt(q[i].astype(BF16), k[i].astype(BF16)) for i in range(len(items))]
        for m in range(N_LEVELS):
            upper = ((tbit >> m) & 1) == 1
            for i in range(len(items)):
                tm_ = (ex(i, m) * jnp.where(upper, q[i], k[i])).astype(BF16)
                a[i] = a[i] + am_ref[m] * _dot_nt(tm_, tm_)
        qb = [(q[i] * ex(i, N_LEVELS)).astype(BF16) for i in range(len(items))]
        kb = [(k[i] * ex(i, N_LEVELS + 1)).astype(BF16) for i in range(len(items))]
        intra = [_dot(a[i].astype(BF16), v[i]) for i in range(len(items))]
        ut = [_dot_tn(v[i], kb[i]) for i in range(len(items))]
        for hd in range(REC_HEADS):
            st = st_ref[hd]
            for cc in range(HGRN_UNROLL):
                i = cc * REC_HEADS + hd
                o = intra[i] + _dot_nt(qb[i], st.astype(BF16))
                st = st * ex(i, N_LEVELS, last_row=True) + ut[i]
                ms = jnp.mean(o * o, axis=-1, keepdims=True)
                gate = blk(g_ref, items[i])[...].astype(F32)
                y = o * lax.rsqrt(ms + EPS) * og_ref[:, hd * REC_DV:(hd + 1) * REC_DV] * gate
                blk(o_ref, items[i])[...] = y.astype(BF16)
            st_ref[hd] = st
        return carry

    lax.fori_loop(0, n_chunks // HGRN_UNROLL, chunks, 0)


def _hgrn(q, f, v, g, out_g, *, ts):
    bsz, s, _ = q.shape
    ww_np, am_np = _hgrn_tables()
    ww = jnp.asarray(ww_np, BF16)
    am = jnp.asarray(am_np, F32)
    row = pl.BlockSpec((None, ts, REC_W), lambda b, i: (b, i, 0))
    return pl.pallas_call(
        functools.partial(_hgrn_kernel, n_chunks=ts // CHUNK),
        grid=(bsz, s // ts),
        in_specs=[row, row, row, row, _const_spec((1, REC_W)),
                  _const_spec(ww.shape), _const_spec(am.shape)],
        out_specs=row,
        out_shape=jax.ShapeDtypeStruct((bsz, s, REC_W), BF16),
        scratch_shapes=[pltpu.VMEM((REC_HEADS, REC_DV, REC_DK), F32),
                        pltpu.VMEM((HGRN_UNROLL, (N_LEVELS + 2) * CHUNK, REC_HEADS * REC_DK), F32)],
        compiler_params=_params(("parallel", "arbitrary")),
        name="hgrn",
    )(q, f, v, g, out_g.reshape(1, REC_W), ww, am)


def _t5_bucket(rel):
    nb = NUM_BUCKETS // 2
    max_exact = nb // 2
    ret = jnp.where(rel > 0, nb, 0)
    n = jnp.abs(rel)
    nf = jnp.maximum(n, 1).astype(F32)
    large = max_exact + (jnp.log(nf / max_exact) / np.log(MAX_DISTANCE / max_exact)
                         * (nb - max_exact)).astype(I32)
    large = jnp.minimum(large, nb - 1)
    return ret + jnp.where(n < max_exact, n, large)


def _bias_slots(rel_bias):
    n_rel = 3 * QBLK
    rel = jnp.arange(n_rel, dtype=I32) - 2 * QBLK
    tab = rel_bias[_t5_bucket(rel)].T
    far = rel_bias[_t5_bucket(jnp.full((1,), -2 * QBLK - 1, I32))].T
    rev = tab[:, ::-1]
    skew = jnp.tile(rev, (1, 2 * QBLK))[:, :2 * QBLK * (n_rel - 1)].reshape(ATT_HEADS, 2 * QBLK, n_rel - 1)
    near = skew[:, :, 2 * QBLK - 1:]
    nb = ((near - far[:, :, None]) * LOG2E).reshape(ATT_HEADS, 2, QBLK, QBLK)
    z = jnp.zeros((ATT_HEADS, 1, QBLK, QBLK), F32)
    return jnp.concatenate([z, nb, z], axis=1)


def _attn_kernel(aq_ref, ak_ref, avt_ref, iq_ref, ik_ref, iwt_ref, bias_ref, og_ref, o_ref,
                 key_ref, mb_ref, thr_ref, acc_ref, m_ref, al_ref, s_ref, mx_ref, p_ref, qbd_ref, ot_ref, tie_ref,
                 *, gs, gk, k_sel):
    bi = pl.program_id(1)
    n_s = bi // (gs // QBLK) + 1
    lane = lax.broadcasted_iota(I32, (QBLK, LANES), 1)

    def block_diag(xq):
        zero = jnp.zeros_like(xq)
        return jnp.concatenate([jnp.where(lane < ATT_DH, xq, zero), jnp.where(lane >= ATT_DH, xq, zero)], axis=0)

    def srows(g):
        return pl.ds(pl.multiple_of(g * gs, gs), gs)

    for j in range(N_PAIRS):
        qbd_ref[j] = block_diag(iq_ref[j])
    w = iwt_ref[...]

    def group_scores(g):
        sc = _dot_nt(ik_ref[srows(g), :], qbd_ref[...].reshape(N_PAIRS * 2 * QBLK, LANES))
        acc = jnp.zeros((gs, LANES), F32)
        for hd in range(IDX_HEADS):
            acc = acc + jnp.maximum(sc[:, hd * LANES:(hd + 1) * LANES], 0.0) * w[hd:hd + 1, :]
        return acc

    def stats(sc, c):
        s3 = sc.reshape(gs // 8, 8, LANES)
        return (jnp.minimum(c[0], jnp.min(jnp.where(s3 == -jnp.inf, jnp.inf, s3), axis=0)),
                jnp.maximum(c[1], jnp.max(s3, axis=0)),
                c[2] + jnp.sum(jnp.where(s3 >= 0.0, 1, 0), axis=0),
                c[3] + jnp.sum(jnp.where(s3 > 0.0, 1, 0), axis=0))

    def idx_body(g, carry):
        sc = group_scores(g)
        key_ref[srows(g), :] = sc
        return stats(sc, carry)

    def idx_pair(u, carry):
        return idx_body(2 * u + 1, idx_body(2 * u, carry))

    z8 = jnp.zeros((8, LANES), I32)
    st8 = lax.fori_loop(0, (n_s - 1) // 2, idx_pair,
                        (jnp.full((8, LANES), jnp.inf, F32), jnp.full((8, LANES), -jnp.inf, F32), z8, z8))
    st8 = lax.cond((n_s - 1) % 2 == 1, lambda c: idx_body(n_s - 2, c), lambda c: c, st8)

    last = srows(n_s - 1)
    spos = (n_s - 1) * gs + lax.broadcasted_iota(I32, (gs, LANES), 0)
    tpos = bi * QBLK + lax.broadcasted_iota(I32, (gs, LANES), 1)
    sc_last = jnp.where((spos // CHUNK) <= (tpos // CHUNK), group_scores(n_s - 1), -jnp.inf)
    key_ref[last, :] = sc_last
    mn8, mx8, ge8, gt8 = stats(sc_last, st8)

    thr_ref[...] = jnp.full(thr_ref.shape, F32_LOWEST, F32)
    tie_ref[0] = 0

    @pl.when((2 * bi + 2) * CHUNK > k_sel)
    def _():
        def groups(fn, init):
            return lax.fori_loop(0, n_s, lambda g, c: fn(key_ref[srows(g), :].reshape(gs // 8, 8, LANES), c), init)

        def count_ge(cand):
            acc = groups(lambda sc, c: c + jnp.sum(jnp.where(sc >= cand, 1, 0), axis=0), jnp.zeros((8, LANES), I32))
            return jnp.sum(acc, axis=0, keepdims=True)

        c0_ge = jnp.sum(ge8, axis=0, keepdims=True)
        c0_gt = jnp.sum(gt8, axis=0, keepdims=True)
        mn = jnp.min(mn8, axis=0, keepdims=True)
        mx = jnp.max(mx8, axis=0, keepdims=True)
        above = mx + (jnp.abs(mx) * 2.0 ** -20 + 1e-30)
        n_adm = (2 * bi + 1 + lax.broadcasted_iota(I32, (1, LANES), 1) // CHUNK) * CHUNK
        pos = c0_gt >= k_sel
        non_neg = c0_ge >= k_sel
        lo = jnp.where(non_neg, 0.0, mn)
        cnt_lo = jnp.where(non_neg, c0_ge, n_adm)
        hi = jnp.where(pos, above, 0.0)

        def midpoint(lo, hi):
            return 0.5 * lo + 0.5 * hi

        def unsettled(lo, hi, cnt_lo):
            mid = midpoint(lo, hi)
            return jnp.where(cnt_lo > k_sel, jnp.where(mid > lo, jnp.where(mid < hi, 1.0, 0.0), 0.0), 0.0)

        def cond(st):
            return jnp.logical_and(st[3] > 0.5, st[4] < BISECT_CAP)

        def halve(lo, hi, cnt_lo):
            act = unsettled(lo, hi, cnt_lo) > 0.5
            mid = midpoint(lo, hi)
            c = count_ge(mid)
            up = jnp.logical_and(act, c >= k_sel)
            dn = jnp.logical_and(act, c < k_sel)
            return jnp.where(up, mid, lo), jnp.where(dn, mid, hi), jnp.where(up, c, cnt_lo)

        def body(st):
            lo, hi, cnt_lo = halve(*halve(*st[:3]))
            return lo, hi, cnt_lo, jnp.max(unsettled(lo, hi, cnt_lo)), st[4] + 2

        st = lax.while_loop(cond, body, (lo, hi, cnt_lo, jnp.max(unsettled(lo, hi, cnt_lo)), jnp.int32(0)))
        thr_ref[...] = jnp.broadcast_to(st[0], thr_ref.shape)
        tie_ref[0] = (jnp.max(jnp.where(st[2] > k_sel, 1.0, 0.0)) > 0.5).astype(I32)

    thr = thr_ref[0:1, :]

    @pl.when(tie_ref[0] == 0)
    def _():
        def mb_body(g, carry):
            mb_ref[srows(g), :] = jnp.where(key_ref[srows(g), :] >= thr, 0.0, NEG)
            return carry

        lax.fori_loop(0, n_s, mb_body, 0)

    @pl.when(tie_ref[0] != 0)
    def _():
        def count_gt(g, acc):
            sc = key_ref[srows(g), :].reshape(gs // 8, 8, LANES)
            return acc + jnp.sum(jnp.where(sc > thr, 1, 0), axis=0)

        n_gt = jnp.sum(lax.fori_loop(0, n_s, count_gt, jnp.zeros((8, LANES), I32)), axis=0, keepdims=True)
        room = (k_sel - n_gt).astype(F32)
        r = lax.broadcasted_iota(I32, (QBLK, QBLK), 0)
        c = lax.broadcasted_iota(I32, (QBLK, QBLK), 1)
        tri = jnp.where(c <= r, 1.0, 0.0).astype(BF16)

        def mb_body(g, seen):
            for t in range(gs // QBLK):
                rs = pl.ds(pl.multiple_of(g * gs + t * QBLK, QBLK), QBLK)
                sc = key_ref[rs, :]
                eq = jnp.where(sc == thr, 1.0, 0.0)
                rank = _dot(tri, eq.astype(BF16)) + seen
                keep = jnp.where(sc > thr, 1.0, jnp.where(rank <= room, eq, 0.0))
                mb_ref[rs, :] = jnp.where(keep > 0.5, 0.0, NEG)
                seen = seen + jnp.sum(eq, axis=0, keepdims=True)
            return seen

        lax.fori_loop(0, n_s, mb_body, jnp.zeros((1, LANES), F32))

    spg = gk // QBLK
    n_g = bi // spg + 1
    for j in range(N_PAIRS):
        qbd_ref[j] = block_diag(aq_ref[j])
    m_ref[...] = jnp.full(m_ref.shape, NEG, F32)
    acc_ref[...] = jnp.zeros_like(acc_ref)
    al_ref[...] = jnp.ones_like(al_ref)
    p_ref[1] = jnp.zeros(p_ref.shape[1:], BF16)

    def rows(g):
        return pl.ds(pl.multiple_of(g * gk, gk), gk)

    def scores(t, slot):
        mb = mb_ref[rows(t), :]
        mb2 = jnp.concatenate([mb, mb], axis=1)
        for j in range(N_PAIRS):
            s = _dot_nt(ak_ref[j, rows(t), :], qbd_ref[j]) + mb2
            s_ref[slot, j] = s
            mx_ref[slot, j] = jnp.max(s, axis=0, keepdims=True)

    def pv(t, slot):
        return [_dot(avt_ref[t, j * PV_ROWS:(j + 1) * PV_ROWS, :], p_ref[slot, j]) for j in range(N_PAIRS)]

    def step(g, cur):
        prv = 1 - cur

        @pl.when(g >= n_g - 2)
        def _():
            for j in range(N_PAIRS):
                parts = []
                for st in range(spg):
                    slot = jnp.clip(g * spg + st - bi + 2, 0, 3)
                    parts.append(jnp.concatenate([bias_ref[2 * j, slot], bias_ref[2 * j + 1, slot]], axis=1))
                s = s_ref[cur, j] + jnp.concatenate(parts, axis=0)
                s_ref[cur, j] = s
                mx_ref[cur, j] = jnp.max(s, axis=0, keepdims=True)

        al_old = [al_ref[j] for j in range(N_PAIRS)]
        for j in range(N_PAIRS):
            m_old = m_ref[j]
            m_new = jnp.maximum(m_old, mx_ref[cur, j])
            p = jnp.exp2(s_ref[cur, j] - m_new)
            alpha = jnp.exp2(m_old - m_new)
            al_ref[j] = alpha
            m_ref[j] = m_new
            p_ref[cur, j] = p.astype(BF16)
        o_prev = pv(jnp.maximum(g - 1, 0), prv)
        scores(jnp.minimum(g + 1, n_g - 1), prv)
        for j in range(N_PAIRS):
            acc_ref[j] = acc_ref[j] * al_old[j] + o_prev[j]

    scores(0, 0)

    def pipe_body(u, c):
        step(2 * u, 0)

        @pl.when(2 * u + 1 < n_g)
        def _():
            step(2 * u + 1, 1)
        return c

    lax.fori_loop(0, (n_g + 1) // 2, pipe_body, 0)
    o_last = pv(n_g - 1, (n_g - 1) % 2)

    for j in range(N_PAIRS):
        acc = acc_ref[j] * al_ref[j] + o_last[j]
        l = acc[2 * ATT_DH:2 * ATT_DH + 1, :]
        o0 = acc[0:ATT_DH, 0:LANES] / l[:, 0:LANES]
        o1 = acc[ATT_DH:2 * ATT_DH, LANES:2 * LANES] / l[:, LANES:2 * LANES]
        ot_ref[j * LANES:(j + 1) * LANES, :] = jnp.concatenate([o0, o1], axis=0)

    ot = ot_ref[...]
    og = og_ref[...]
    outs = []
    for hd in range(ATT_HEADS):
        oh = ot[hd * ATT_DH:(hd + 1) * ATT_DH, :]
        ms = jnp.mean(oh * oh, axis=0, keepdims=True)
        outs.append(oh * lax.rsqrt(ms + EPS) * og[hd * ATT_DH:(hd + 1) * ATT_DH, :])
    o_ref[...] = jnp.concatenate(outs, axis=0).T.astype(BF16)


def _attention(aq, ak, avt, iq, ik, iwt, rel_bias, out_g, *, gk):
    bsz, _, s, _ = aq.shape
    k_sel = min(TOPK_MAX, s // 4)
    bias = _bias_slots(rel_bias)
    qpair = pl.BlockSpec((None, N_PAIRS, QBLK, LANES), lambda b, i: (b, 0, i, 0))
    full = lambda shp: pl.BlockSpec((None,) + shp, lambda b, i: (b,) + (0,) * len(shp),
                                    pipeline_mode=pl.Buffered(1))
    return pl.pallas_call(
        functools.partial(_attn_kernel, gs=_tile(s, 512), gk=gk, k_sel=k_sel),
        grid=(bsz, s // QBLK),
        in_specs=[qpair, full((N_PAIRS, s, LANES)), full((s // gk, N_PAIRS * PV_ROWS, gk)), qpair,
                  full((s, LANES)), pl.BlockSpec((None, IDX_HEADS, QBLK), lambda b, i: (b, 0, i)),
                  _const_spec(bias.shape), _const_spec((ATT_W, 1))],
        out_specs=pl.BlockSpec((None, QBLK, ATT_W), lambda b, i: (b, i, 0)),
        out_shape=jax.ShapeDtypeStruct((bsz, s, ATT_W), BF16),
        scratch_shapes=[pltpu.VMEM((s, LANES), F32),
                        pltpu.VMEM((s, LANES), F32),
                        pltpu.VMEM((8, LANES), F32),
                        pltpu.VMEM((N_PAIRS, PV_ROWS, 2 * LANES), F32),
                        pltpu.VMEM((N_PAIRS, 1, 2 * LANES), F32),
                        pltpu.VMEM((N_PAIRS, 1, 2 * LANES), F32),
                        pltpu.VMEM((2, N_PAIRS, gk, 2 * LANES), F32),
                        pltpu.VMEM((2, N_PAIRS, 1, 2 * LANES), F32),
                        pltpu.VMEM((2, N_PAIRS, gk, 2 * LANES), BF16),
                        pltpu.VMEM((N_PAIRS, 2 * QBLK, LANES), BF16),
                        pltpu.VMEM((ATT_W, LANES), F32),
                        pltpu.SMEM((1,), I32)],
        compiler_params=_params(("parallel", "arbitrary")),
        name="attn",
    )(aq, ak, avt, iq, ik, iwt, bias, out_g.reshape(ATT_W, 1))


def _outproj_kernel(rec_ref, att_ref, x_ref, g1_ref, sc_ref, sh_ref, ng_ref, wo_ref, wr_ref, br_ref,
                    x1_ref, h2_ref, comb_ref):
    mix = _dot(rec_ref[...], wo_ref[0:REC_W, :]) + _dot(att_ref[...], wo_ref[REC_W:REC_W + ATT_W, :])
    x1 = x_ref[...] + g1_ref[...] * mix
    x1_ref[...] = x1
    ms = jnp.mean(x1 * x1, axis=-1, keepdims=True)
    h2 = (x1 * lax.rsqrt(ms + EPS) * ng_ref[...] * (1.0 + sc_ref[...]) + sh_ref[...]).astype(BF16)
    h2_ref[...] = h2

    lg = _dot(h2, wr_ref[...]) + br_ref[...]
    lane = lax.broadcasted_iota(I32, lg.shape, 1)
    big = jnp.int32(2 * LANES)
    is_g = lane < N_GROUPS
    gl = jnp.where(is_g, lg, -jnp.inf)
    gmax = jnp.max(gl, axis=-1, keepdims=True)
    gate = 1.0 / jnp.sum(jnp.where(is_g, jnp.exp(lg - gmax), 0.0), axis=-1, keepdims=True)
    gtop = jnp.min(jnp.where(gl == gmax, lane, big), axis=-1, keepdims=True)
    e_lo = ROUTE_OFF + EXPERTS_PER_GROUP * gtop
    el = jnp.where((lane >= e_lo) & (lane < e_lo + EXPERTS_PER_GROUP), lg, -jnp.inf)
    v1 = jnp.max(el, axis=-1, keepdims=True)
    i1 = jnp.min(jnp.where(el == v1, lane, big), axis=-1, keepdims=True)
    el2 = jnp.where(lane == i1, -jnp.inf, el)
    v2 = jnp.max(el2, axis=-1, keepdims=True)
    i2 = jnp.min(jnp.where(el2 == v2, lane, big), axis=-1, keepdims=True)
    e2 = jnp.exp(v2 - v1)
    w1 = gate / (1.0 + e2)
    w2 = gate * e2 / (1.0 + e2)
    comb = jnp.where(lane == i1, w1, 0.0) + jnp.where(lane == i2, w2, 0.0)
    comb_ref[...] = jnp.where(lane == 0, gtop.astype(F32), comb)


def _outproj(rec, att, x, g1, sc2, sh2, norm_g, w_out, w_rg, b_rg, w_re, b_re, *, tm):
    bsz, s, d = x.shape
    wr = jnp.zeros((d, LANES), F32).at[:, :N_GROUPS].set(w_rg).at[:, ROUTE_OFF:ROUTE_OFF + N_EXPERTS].set(w_re)
    br = jnp.zeros((1, LANES), F32).at[0, :N_GROUPS].set(b_rg).at[0, ROUTE_OFF:ROUTE_OFF + N_EXPERTS].set(b_re)
    row = lambda w: pl.BlockSpec((None, tm, w), lambda b, i: (b, i, 0))
    vec = pl.BlockSpec((None, 1, d), lambda b, i: (b, 0, 0))
    return pl.pallas_call(
        _outproj_kernel,
        grid=(bsz, s // tm),
        in_specs=[row(REC_W), row(ATT_W), row(d), vec, vec, vec, _const_spec((1, d)),
                  _const_spec((REC_W + ATT_W, d)), _const_spec((d, LANES)), _const_spec((1, LANES))],
        out_specs=(row(d), row(d), row(LANES)),
        out_shape=(jax.ShapeDtypeStruct((bsz, s, d), F32),
                   jax.ShapeDtypeStruct((bsz, s, d), BF16),
                   jax.ShapeDtypeStruct((bsz, s, LANES), F32)),
        compiler_params=_params(("parallel", "parallel")),
        name="outproj",
    )(rec, att, x, g1.reshape(bsz, 1, d), sc2.reshape(bsz, 1, d), sh2.reshape(bsz, 1, d),
      norm_g.reshape(1, d), w_out.astype(BF16), wr.astype(BF16), br)


MOE_ALIGN = 16
MOE_LANE_SHIFT = 32
MOE_CHUNK = 288
MOE_RB = 256


def _moe_kernel(h_ref, comb_ref, x1_ref, g2_ref, w1_ref, w3_ref, w2_ref, o_ref,
                p_ref, pt_ref, hs_ref, cs_ref, ys_ref, seg_ref, *, mc):
    e = pl.program_id(2)
    tm = h_ref.shape[0]
    npad = hs_ref.shape[0]

    @pl.when(e == 0)
    def _():
        comb = comb_ref[...]
        lane = lax.broadcasted_iota(I32, (tm, LANES), 1)
        gid = comb[:, 0:1].astype(I32)
        oh = jnp.where(lane == gid, 1.0, 0.0)
        ohb = oh.astype(BF16)
        pre = []
        for rb in range(tm // MOE_RB):
            r = rb * MOE_RB + lax.broadcasted_iota(I32, (MOE_RB, tm), 0)
            c = lax.broadcasted_iota(I32, (MOE_RB, tm), 1)
            pre.append(_dot(jnp.where(c < r, 1.0, 0.0).astype(BF16), ohb))
        prefix = jnp.concatenate(pre, axis=0)
        cnt = jnp.sum(oh, axis=0, keepdims=True).astype(I32)
        cnt_al = ((cnt + (MOE_ALIGN - 1)) // MOE_ALIGN) * MOE_ALIGN
        base = jnp.sum(jnp.where(lane < gid, cnt_al.astype(F32), 0.0), axis=-1, keepdims=True)
        rank = jnp.sum(prefix * oh, axis=-1, keepdims=True)
        pos = (base + rank).astype(I32)
        pos_row = jnp.broadcast_to(pos.astype(F32), (tm, LANES)).T[0:1, :].astype(I32)
        for rb in range(tm // MOE_RB):
            sl = slice(rb * MOE_RB, (rb + 1) * MOE_RB)
            coln = lax.broadcasted_iota(I32, (MOE_RB, npad), 1)
            pt_ref[sl, :] = jnp.where(coln == pos[sl], 1.0, 0.0).astype(BF16)
        for rb in range(npad // LANES):
            sl = slice(rb * LANES, (rb + 1) * LANES)
            rown = rb * LANES + lax.broadcasted_iota(I32, (LANES, tm), 0)
            p_ref[sl, :] = jnp.where(rown == pos_row, 1.0, 0.0).astype(BF16)
        c_hi = comb.astype(BF16)
        c_mid, c_lo = _split_bf16(comb - c_hi.astype(F32))
        packed = (c_hi.astype(F32) + pltpu.roll(c_mid.astype(F32), MOE_LANE_SHIFT, axis=1)
                  + pltpu.roll(c_lo.astype(F32), 2 * MOE_LANE_SHIFT, axis=1)).astype(BF16)
        srt = _dot(p_ref[...], jnp.concatenate([h_ref[...], packed], axis=1))
        d = h_ref.shape[1]
        hs_ref[...] = srt[:, :d].astype(BF16)
        cp = srt[:, d:]
        cs_ref[...] = (cp + pltpu.roll(cp, LANES - MOE_LANE_SHIFT, axis=1)
                       + pltpu.roll(cp, LANES - 2 * MOE_LANE_SHIFT, axis=1))
        ys_ref[...] = jnp.zeros_like(ys_ref)
        start = jnp.int32(0)
        for g in range(N_GROUPS):
            seg_ref[g] = start
            seg_ref[N_GROUPS + g] = cnt[0, g]
            start = start + cnt_al[0, g]

    grp = e // EXPERTS_PER_GROUP
    start = seg_ref[grp]
    n_rows = seg_ref[N_GROUPS + grp]

    def chunk(ci, carry):
        rs = pl.ds(pl.multiple_of(start + ci * mc, MOE_ALIGN), mc)
        hb = hs_ref[rs, :]
        cw = cs_ref[rs, :]
        lane = lax.broadcasted_iota(I32, cw.shape, 1)
        col = jnp.sum(jnp.where(lane == e + ROUTE_OFF, cw, 0.0), axis=-1, keepdims=True)
        he = _silu(_dot(hb, w1_ref[...])) * _dot(hb, w3_ref[...]) * col
        ys_ref[rs, :] += _dot(he.astype(BF16), w2_ref[...])
        return carry

    lax.fori_loop(0, (n_rows + mc - 1) // mc, chunk, 0)

    @pl.when(e == N_EXPERTS - 1)
    def _():
        o_ref[...] = x1_ref[...] + g2_ref[...] * _dot(pt_ref[...], ys_ref[...].astype(BF16))


def _moe(h2, comb, x1, g2, w1, w3, w2, *, tm):
    bsz, s, d = x1.shape
    mc = MOE_CHUNK
    npad = -(-(tm + N_GROUPS * MOE_ALIGN + mc) // LANES) * LANES
    row = lambda w, **kw: pl.BlockSpec((None, tm, w), lambda b, i, e: (b, i, 0), **kw)
    once = dict(pipeline_mode=pl.Buffered(1))
    return pl.pallas_call(
        functools.partial(_moe_kernel, mc=mc),
        grid=(bsz, s // tm, N_EXPERTS),
        in_specs=[row(d, **once), row(LANES), row(d, **once),
                  pl.BlockSpec((None, 1, d), lambda b, i, e: (b, 0, 0)),
                  pl.BlockSpec((None, d, D_EXPERT), lambda b, i, e: (e, 0, 0)),
                  pl.BlockSpec((None, d, D_EXPERT), lambda b, i, e: (e, 0, 0)),
                  pl.BlockSpec((None, D_EXPERT, d), lambda b, i, e: (e, 0, 0))],
        out_specs=row(d),
        out_shape=jax.ShapeDtypeStruct((bsz, s, d), F32),
        scratch_shapes=[pltpu.VMEM((npad, tm), BF16),
                        pltpu.VMEM((tm, npad), BF16),
                        pltpu.VMEM((npad, d), BF16),
                        pltpu.VMEM((npad, LANES), F32),
                        pltpu.VMEM((npad, d), F32),
                        pltpu.SMEM((2 * N_GROUPS,), I32)],
        compiler_params=_params(("parallel", "parallel", "arbitrary")),
        name="moe",
    )(h2, comb, x1, g2.reshape(bsz, 1, d), w1.astype(BF16), w3.astype(BF16), w2.astype(BF16))


def _tile(s, pref):
    t = min(s, pref)
    assert s % t == 0
    return t


def kernel(x, c, w_ada, b_ada, norm1_g, norm2_g, w_in, lb_logits, rec_out_g, q_norm_g, k_norm_g,
           idx_k_norm_g, idx_k_norm_b, attn_out_g, rel_bias, w_out, w_rg, b_rg, w_re, b_re, w1, w3, w2):
    bsz, s, d = x.shape
    depth = w_ada.shape[0]
    gk = _tile(s, 512)
    for l in range(depth):
        mod = _adaln(c, w_ada[l], b_ada[l])
        sh1, sc1, g1, sh2, sc2, g2 = jnp.split(mod, 6, axis=-1)
        q, f, v, g, aq, ak, avt, iq, ik, iwt = _inproj(
            x, sc1, sh1, norm1_g[l], w_in[l], lb_logits, q_norm_g[l], k_norm_g[l],
            idx_k_norm_g[l], idx_k_norm_b[l], layer=l, gt=gk, tm=_tile(s, 512))
        rec = _hgrn(q, f, v, g, rec_out_g[l], ts=_tile(s, 256))
        att = _attention(aq, ak, avt, iq, ik, iwt, rel_bias, attn_out_g[l], gk=gk)
        x1, h2, comb = _outproj(rec, att, x, g1, sc2, sh2, norm2_g[l], w_out[l],
                                w_rg[l], b_rg[l], w_re[l], b_re[l], tm=_tile(s, 512))
        x = _moe(h2, comb, x1, g2, w1[l], w3[l], w2[l], tm=_tile(s, 1024))
    return x
```

```python
import functools

import numpy as np
import jax
import jax.numpy as jnp
from jax import lax
from jax.experimental import pallas as pl
from jax.experimental.pallas import tpu as pltpu

F32 = jnp.float32
BF16 = jnp.bfloat16
I32 = jnp.int32

CHUNK = 64
QBLK = 128
EPS = 1e-6
REC_HEADS = 4
REC_DK = 128
REC_DV = 128
REC_W = REC_HEADS * REC_DV
ATT_HEADS = 8
ATT_DH = 64
ATT_W = ATT_HEADS * ATT_DH
IDX_HEADS = 8
IDX_DIM = 64
TOPK_MAX = 256
NUM_BUCKETS = 32
MAX_DISTANCE = 128
N_GROUPS = 4
EXPERTS_PER_GROUP = 4
N_EXPERTS = N_GROUPS * EXPERTS_PER_GROUP
D_EXPERT = 512

LANES = 128
V7X_VMEM_LIMIT = 56 * 1024 * 1024
F32_LOWEST = float(np.finfo(np.float32).min)
BISECT_CAP = 320
NEG = -1e30
LOG2E = float(np.log2(np.e))
N_PAIRS = ATT_HEADS // 2
ROUTE_OFF = N_GROUPS


def _dot(a, b):
    return jnp.dot(a, b, preferred_element_type=F32)


def _dot_nt(a, b):
    return lax.dot_general(a, b, (((1,), (1,)), ((), ())), preferred_element_type=F32)


def _dot_tn(a, b):
    return lax.dot_general(a, b, (((0,), (0,)), ((), ())), preferred_element_type=F32)


def _split_bf16(a):
    hi = a.astype(BF16)
    lo = (a - hi.astype(F32)).astype(BF16)
    return hi, lo


def _silu(a):
    return a * jax.nn.sigmoid(a)


def _const_spec(shape):
    nd = len(shape)
    return pl.BlockSpec(shape, lambda *_: (0,) * nd, pipeline_mode=pl.Buffered(1))


def _params(sem):
    return pltpu.CompilerParams(dimension_semantics=sem, vmem_limit_bytes=V7X_VMEM_LIMIT)


def _adaln_kernel(c_ref, w_ref, b_ref, o_ref):
    a_hi, a_lo = _split_bf16(_silu(c_ref[...]))
    w_hi, w_lo = _split_bf16(w_ref[...])
    o_ref[...] = _dot(a_hi, w_hi) + _dot(a_lo, w_hi) + _dot(a_hi, w_lo) + b_ref[...]


def _adaln(c, w, b):
    bsz, d = c.shape
    n = w.shape[1]
    rows = 16
    bn = 1024
    cp = jnp.zeros((rows, d), F32).at[:bsz].set(c)
    out = pl.pallas_call(
        _adaln_kernel,
        grid=(n // bn,),
        in_specs=[pl.BlockSpec((rows, d), lambda i: (0, 0)),
                  pl.BlockSpec((d, bn), lambda i: (0, i)),
                  pl.BlockSpec((1, bn), lambda i: (0, i))],
        out_specs=pl.BlockSpec((rows, bn), lambda i: (0, i)),
        out_shape=jax.ShapeDtypeStruct((rows, n), F32),
        compiler_params=_params(("parallel",)),
        name="adaln",
    )(cp, w, b.reshape(1, n))
    return out[:bsz]


def _inproj_kernel(x_ref, sc_ref, sh_ref, ng_ref, wrec_ref, watt_ref, widx_ref, lbl_ref,
                   qg_ref, kg_ref, ikg_ref, ikb_ref, pm_ref,
                   q_ref, f_ref, v_ref, g_ref, aq_ref, ak_ref, avt_ref, iq_ref, ik_ref, iwt_ref,
                   *, layer, gt):
    x = x_ref[...]
    tm = x.shape[0]
    ms = jnp.mean(x * x, axis=-1, keepdims=True)
    h = x * lax.rsqrt(ms + EPS) * ng_ref[...] * (1.0 + sc_ref[...]) + sh_ref[...]
    hb = h.astype(BF16)

    zr = _dot(hb, wrec_ref[...])
    q_ref[...] = _silu(zr[:, 0:REC_W]).astype(BF16)
    lbl = lbl_ref[...]
    e = jnp.exp(lbl - jnp.max(lbl, axis=0, keepdims=True))
    sm = e / jnp.sum(e, axis=0, keepdims=True)
    lb = jnp.sum(sm[0:layer + 1], axis=0, keepdims=True)
    f_ref[...] = lb + (1.0 - lb) * jax.nn.sigmoid(zr[:, REC_W:2 * REC_W])
    v_ref[...] = zr[:, 2 * REC_W:3 * REC_W].astype(BF16)
    g_ref[...] = _silu(zr[:, 3 * REC_W:4 * REC_W]).astype(BF16)

    za = _dot(hb, watt_ref[...])
    aq = za[:, 0:ATT_W]
    ak = za[:, ATT_W:2 * ATT_W]
    av = za[:, 2 * ATT_W:3 * ATT_W]
    pm = pm_ref[...]
    aqn = aq * lax.rsqrt(_dot((aq * aq).astype(BF16), pm) + EPS) * qg_ref[...]
    akn = ak * lax.rsqrt(_dot((ak * ak).astype(BF16), pm) + EPS) * kg_ref[...]
    for j in range(N_PAIRS):
        aq_ref[j] = aqn[:, j * LANES:(j + 1) * LANES].astype(BF16)
        ak_ref[j] = akn[:, j * LANES:(j + 1) * LANES].astype(BF16)
    for t in range(tm // gt):
        avt_ref[t] = av[t * gt:(t + 1) * gt, :].T.astype(BF16)

    zi = _dot(hb, widx_ref[...])
    for j in range(N_PAIRS):
        iq_ref[j] = zi[:, j * LANES:(j + 1) * LANES].astype(BF16)
    tail = zi[:, IDX_HEADS * IDX_DIM:IDX_HEADS * IDX_DIM + LANES]
    lane = lax.broadcasted_iota(I32, tail.shape, 1)
    is_k = lane < IDX_DIM
    mu = jnp.sum(jnp.where(is_k, tail, 0.0), axis=-1, keepdims=True) * (1.0 / IDX_DIM)
    dlt = jnp.where(is_k, tail - mu, 0.0)
    var = jnp.sum(dlt * dlt, axis=-1, keepdims=True) * (1.0 / IDX_DIM)
    ikn = dlt * lax.rsqrt(var + EPS) * ikg_ref[...] + ikb_ref[...]
    ik_ref[...] = jnp.where(is_k, ikn, pltpu.roll(ikn, IDX_DIM, axis=1)).astype(BF16)
    iwt_ref[...] = tail.T[IDX_DIM:IDX_DIM + IDX_HEADS, :] * (IDX_HEADS ** -0.5 * IDX_DIM ** -0.5)


def _inproj(x, sc1, sh1, norm_g, w_in, lb_logits, q_g, k_g, ik_g, ik_b, *, layer, gt, tm):
    bsz, s, d = x.shape
    n_rec = 4 * REC_W
    n_att = 3 * ATT_W
    n_idx = IDX_HEADS * IDX_DIM + LANES
    wb = w_in.astype(BF16)
    w_rec = wb[:, :n_rec]
    w_att = wb[:, n_rec:n_rec + n_att]
    w_idx = jnp.zeros((d, n_idx), BF16).at[:, :w_in.shape[1] - n_rec - n_att].set(wb[:, n_rec + n_att:])
    pm = jnp.asarray(np.kron(np.eye(ATT_HEADS), np.full((ATT_DH, ATT_DH), 1.0 / ATT_DH)), BF16)
    qg = jnp.tile(q_g, ATT_HEADS).reshape(1, ATT_W) * (ATT_DH ** -0.5 * LOG2E)
    kg = jnp.tile(k_g, ATT_HEADS).reshape(1, ATT_W)
    ikg = jnp.zeros((1, LANES), F32).at[0, :IDX_DIM].set(ik_g)
    ikb = jnp.zeros((1, LANES), F32).at[0, :IDX_DIM].set(ik_b)
    nl = lb_logits.shape[0]

    row = lambda w: pl.BlockSpec((None, tm, w), lambda b, i: (b, i, 0))
    pair = pl.BlockSpec((None, N_PAIRS, tm, LANES), lambda b, i: (b, 0, i, 0))
    vec = pl.BlockSpec((None, 1, d), lambda b, i: (b, 0, 0))
    out_shapes = (
        jax.ShapeDtypeStruct((bsz, s, REC_W), BF16),
        jax.ShapeDtypeStruct((bsz, s, REC_W), F32),
        jax.ShapeDtypeStruct((bsz, s, REC_W), BF16),
        jax.ShapeDtypeStruct((bsz, s, REC_W), BF16),
        jax.ShapeDtypeStruct((bsz, N_PAIRS, s, LANES), BF16),
        jax.ShapeDtypeStruct((bsz, N_PAIRS, s, LANES), BF16),
        jax.ShapeDtypeStruct((bsz, s // gt, ATT_W, gt), BF16),
        jax.ShapeDtypeStruct((bsz, N_PAIRS, s, LANES), BF16),
        jax.ShapeDtypeStruct((bsz, s, LANES), BF16),
        jax.ShapeDtypeStruct((bsz, IDX_HEADS, s), F32),
    )
    out_specs = (
        row(REC_W), row(REC_W), row(REC_W), row(REC_W), pair, pair,
        pl.BlockSpec((None, tm // gt, ATT_W, gt), lambda b, i: (b, i, 0, 0)),
        pair, row(LANES),
        pl.BlockSpec((None, IDX_HEADS, tm), lambda b, i: (b, 0, i)),
    )
    return pl.pallas_call(
        functools.partial(_inproj_kernel, layer=layer, gt=gt),
        grid=(bsz, s // tm),
        in_specs=[row(d), vec, vec, _const_spec((1, d)),
                  _const_spec((d, n_rec)), _const_spec((d, n_att)), _const_spec((d, n_idx)),
                  _const_spec((nl, REC_W)), _const_spec((1, ATT_W)), _const_spec((1, ATT_W)),
                  _const_spec((1, LANES)), _const_spec((1, LANES)), _const_spec((ATT_W, ATT_W))],
        out_specs=out_specs,
        out_shape=out_shapes,
        compiler_params=_params(("parallel", "parallel")),
        name="inproj",
    )(x, sc1.reshape(bsz, 1, d), sh1.reshape(bsz, 1, d), norm_g.reshape(1, d),
      w_rec, w_att, w_idx, lb_logits, qg, kg, ikg, ikb, pm)


N_LEVELS = 6


def _hgrn_tables():
    c = CHUNK
    w = np.zeros((N_LEVELS + 2, c, c), np.float32)
    am = np.zeros((N_LEVELS + 1, c, c), np.float32)
    t = np.arange(c)
    for m in range(N_LEVELS):
        hs = 1 << m
        blk = t // (2 * hs)
        upper = (t // hs) % 2 == 1
        ref = blk * 2 * hs + hs - 1
        for i in range(c):
            if upper[i]:
                w[m, i, ref[i] + 1:i + 1] = 1.0
            else:
                w[m, i, i + 1:ref[i] + 1] = 1.0
        am[m] = (blk[:, None] == blk[None, :]) & upper[:, None] & ~upper[None, :]
    w[N_LEVELS] = np.tril(np.ones((c, c)))
    w[N_LEVELS + 1] = np.triu(np.ones((c, c)), 1)
    am[N_LEVELS] = np.eye(c)
    w = w.reshape((N_LEVELS + 2) * c, c)
    return np.concatenate([w, w], axis=1), am


HGRN_UNROLL = 4


def _hgrn_kernel(q_ref, f_ref, v_ref, g_ref, og_ref, ww_ref, am_ref, o_ref, st_ref, ex_ref, *, n_chunks):
    @pl.when(pl.program_id(1) == 0)
    def _():
        st_ref[...] = jnp.zeros_like(st_ref)

    ww = ww_ref[...]
    c = CHUNK
    tbit = lax.broadcasted_iota(I32, (c, REC_DK), 0)
    items = [(cc, hd) for cc in range(HGRN_UNROLL) for hd in range(REC_HEADS)]

    def chunks(ci, carry):
        def blk(ref, it):
            r0 = pl.multiple_of((ci * HGRN_UNROLL + it[0]) * c, c)
            return ref.at[pl.ds(r0, c), it[1] * REC_DK:(it[1] + 1) * REC_DK]

        def ex(i, part, last_row=False):
            r0 = (part + 1) * c - 1 if last_row else part * c
            return ex_ref[items[i][0], r0:(part + 1) * c, items[i][1] * REC_DK:(items[i][1] + 1) * REC_DK]

        f = [blk(f_ref, it)[...] for it in items]
        q = [blk(q_ref, it)[...].astype(F32) for it in items]
        v = [blk(v_ref, it)[...] for it in items]
        k = [1.0 - fi for fi in f]
        for cc in range(HGRN_UNROLL):
            cols = [jnp.concatenate(_split_bf16(jnp.log(f[cc * REC_HEADS + hd])), axis=0) for hd in range(REC_HEADS)]
            ex_ref[cc] = jnp.exp(_dot(ww, jnp.concatenate(cols, axis=1)))
        a = [am_ref[N_LEVELS] * _dot_nt(q[i].astype(BF16), k[i].astype(BF16)) for i in range(len(items))]
        for m in range(N_LEVELS):
            upper = ((tbit >> m) & 1) == 1
            for i in range(len(items)):
                tm_ = (ex(i, m) * jnp.where(upper, q[i], k[i])).astype(BF16)
                a[i] = a[i] + am_ref[m] * _dot_nt(tm_, tm_)
        qb = [(q[i] * ex(i, N_LEVELS)).astype(BF16) for i in range(len(items))]
        kb = [(k[i] * ex(i, N_LEVELS + 1)).astype(BF16) for i in range(len(items))]
        intra = [_dot(a[i].astype(BF16), v[i]) for i in range(len(items))]
        ut = [_dot_tn(v[i], kb[i]) for i in range(len(items))]
        for hd in range(REC_HEADS):
            st = st_ref[hd]
            for cc in range(HGRN_UNROLL):
                i = cc * REC_HEADS + hd
                o = intra[i] + _dot_nt(qb[i], st.astype(BF16))
                st = st * ex(i, N_LEVELS, last_row=True) + ut[i]
                ms = jnp.mean(o * o, axis=-1, keepdims=True)
                gate = blk(g_ref, items[i])[...].astype(F32)
                y = o * lax.rsqrt(ms + EPS) * og_ref[:, hd * REC_DV:(hd + 1) * REC_DV] * gate
                blk(o_ref, items[i])[...] = y.astype(BF16)
            st_ref[hd] = st
        return carry

    lax.fori_loop(0, n_chunks // HGRN_UNROLL, chunks, 0)


def _hgrn(q, f, v, g, out_g, *, ts):
    bsz, s, _ = q.shape
    ww_np, am_np = _hgrn_tables()
    ww = jnp.asarray(ww_np, BF16)
    am = jnp.asarray(am_np, F32)
    row = pl.BlockSpec((None, ts, REC_W), lambda b, i: (b, i, 0))
    return pl.pallas_call(
        functools.partial(_hgrn_kernel, n_chunks=ts // CHUNK),
        grid=(bsz, s // ts),
        in_specs=[row, row, row, row, _const_spec((1, REC_W)),
                  _const_spec(ww.shape), _const_spec(am.shape)],
        out_specs=row,
        out_shape=jax.ShapeDtypeStruct((bsz, s, REC_W), BF16),
        scratch_shapes=[pltpu.VMEM((REC_HEADS, REC_DV, REC_DK), F32),
                        pltpu.VMEM((HGRN_UNROLL, (N_LEVELS + 2) * CHUNK, REC_HEADS * REC_DK), F32)],
        compiler_params=_params(("parallel", "arbitrary")),
        name="hgrn",
    )(q, f, v, g, out_g.reshape(1, REC_W), ww, am)


def _t5_bucket(rel):
    nb = NUM_BUCKETS // 2
    max_exact = nb // 2
    ret = jnp.where(rel > 0, nb, 0)
    n = jnp.abs(rel)
    nf = jnp.maximum(n, 1).astype(F32)
    large = max_exact + (jnp.log(nf / max_exact) / np.log(MAX_DISTANCE / max_exact)
                         * (nb - max_exact)).astype(I32)
    large = jnp.minimum(large, nb - 1)
    return ret + jnp.where(n < max_exact, n, large)


def _bias_slots(rel_bias):
    n_rel = 3 * QBLK
    rel = jnp.arange(n_rel, dtype=I32) - 2 * QBLK
    tab = rel_bias[_t5_bucket(rel)].T
    far = rel_bias[_t5_bucket(jnp.full((1,), -2 * QBLK - 1, I32))].T
    rev = tab[:, ::-1]
    skew = jnp.tile(rev, (1, 2 * QBLK))[:, :2 * QBLK * (n_rel - 1)].reshape(ATT_HEADS, 2 * QBLK, n_rel - 1)
    near = skew[:, :, 2 * QBLK - 1:]
    nb = ((near - far[:, :, None]) * LOG2E).reshape(ATT_HEADS, 2, QBLK, QBLK)
    z = jnp.zeros((ATT_HEADS, 1, QBLK, QBLK), F32)
    return jnp.concatenate([z, nb, z], axis=1)


def _attn_kernel(aq_ref, ak_ref, avt_ref, iq_ref, ik_ref, iwt_ref, bias_ref, og_ref, o_ref,
                 key_ref, mb_ref, thr_ref, acc_ref, m_ref, l_ref, al_ref, s_ref, mx_ref, p_ref, qbd_ref, ot_ref, tie_ref,
                 *, gs, gk, k_sel):
    bi = pl.program_id(1)
    n_s = bi // (gs // QBLK) + 1
    lane = lax.broadcasted_iota(I32, (QBLK, LANES), 1)

    def block_diag(xq):
        zero = jnp.zeros_like(xq)
        return jnp.concatenate([jnp.where(lane < ATT_DH, xq, zero), jnp.where(lane >= ATT_DH, xq, zero)], axis=0)

    def srows(g):
        return pl.ds(pl.multiple_of(g * gs, gs), gs)

    for j in range(N_PAIRS):
        qbd_ref[j] = block_diag(iq_ref[j])
    w = iwt_ref[...]

    def group_scores(g):
        sc = _dot_nt(ik_ref[srows(g), :], qbd_ref[...].reshape(N_PAIRS * 2 * QBLK, LANES))
        acc = jnp.zeros((gs, LANES), F32)
        for hd in range(IDX_HEADS):
            acc = acc + jnp.maximum(sc[:, hd * LANES:(hd + 1) * LANES], 0.0) * w[hd:hd + 1, :]
        return acc

    def stats(sc, c):
        s3 = sc.reshape(gs // 8, 8, LANES)
        return (jnp.minimum(c[0], jnp.min(jnp.where(s3 == -jnp.inf, jnp.inf, s3), axis=0)),
                jnp.maximum(c[1], jnp.max(s3, axis=0)),
                c[2] + jnp.sum(jnp.where(s3 >= 0.0, 1, 0), axis=0),
                c[3] + jnp.sum(jnp.where(s3 > 0.0, 1, 0), axis=0))

    def idx_body(g, carry):
        sc = group_scores(g)
        key_ref[srows(g), :] = sc
        return stats(sc, carry)

    def idx_pair(u, carry):
        return idx_body(2 * u + 1, idx_body(2 * u, carry))

    z8 = jnp.zeros((8, LANES), I32)
    st8 = lax.fori_loop(0, (n_s - 1) // 2, idx_pair,
                        (jnp.full((8, LANES), jnp.inf, F32), jnp.full((8, LANES), -jnp.inf, F32), z8, z8))
    st8 = lax.cond((n_s - 1) % 2 == 1, lambda c: idx_body(n_s - 2, c), lambda c: c, st8)

    last = srows(n_s - 1)
    spos = (n_s - 1) * gs + lax.broadcasted_iota(I32, (gs, LANES), 0)
    tpos = bi * QBLK + lax.broadcasted_iota(I32, (gs, LANES), 1)
    sc_last = jnp.where((spos // CHUNK) <= (tpos // CHUNK), group_scores(n_s - 1), -jnp.inf)
    key_ref[last, :] = sc_last
    mn8, mx8, ge8, gt8 = stats(sc_last, st8)

    thr_ref[...] = jnp.full(thr_ref.shape, F32_LOWEST, F32)
    tie_ref[0] = 0

    @pl.when((2 * bi + 2) * CHUNK > k_sel)
    def _():
        def count_ge(cand):
            def one(g):
                sc = key_ref[srows(g), :].reshape(gs // 8, 8, LANES)
                return jnp.sum(jnp.where(sc >= cand, 1, 0), axis=0)

            def two(u, acc):
                second = jnp.where(2 * u + 1 < n_s, 1, 0)
                return acc + one(2 * u) + one(jnp.minimum(2 * u + 1, n_s - 1)) * second

            acc = lax.fori_loop(0, (n_s + 1) // 2, two, jnp.zeros((8, LANES), I32))
            return jnp.sum(acc, axis=0, keepdims=True)

        c0_ge = jnp.sum(ge8, axis=0, keepdims=True)
        c0_gt = jnp.sum(gt8, axis=0, keepdims=True)
        mn = jnp.min(mn8, axis=0, keepdims=True)
        mx = jnp.max(mx8, axis=0, keepdims=True)
        above = mx + (jnp.abs(mx) * 2.0 ** -20 + 1e-30)
        n_adm = (2 * bi + 1 + lax.broadcasted_iota(I32, (1, LANES), 1) // CHUNK) * CHUNK
        pos = c0_gt >= k_sel
        non_neg = c0_ge >= k_sel
        lo = jnp.where(non_neg, 0.0, mn)
        cnt_lo = jnp.where(non_neg, c0_ge, n_adm)
        hi = jnp.where(pos, above, 0.0)

        def midpoint(lo, hi):
            return 0.5 * lo + 0.5 * hi

        def unsettled(lo, hi, cnt_lo):
            mid = midpoint(lo, hi)
            return jnp.where(cnt_lo > k_sel, jnp.where(mid > lo, jnp.where(mid < hi, 1.0, 0.0), 0.0), 0.0)

        def cond(st):
            return jnp.logical_and(st[3] > 0.5, st[4] < BISECT_CAP)

        def halve(lo, hi, cnt_lo):
            act = unsettled(lo, hi, cnt_lo) > 0.5
            mid = midpoint(lo, hi)
            c = count_ge(mid)
            up = jnp.logical_and(act, c >= k_sel)
            dn = jnp.logical_and(act, c < k_sel)
            return jnp.where(up, mid, lo), jnp.where(dn, mid, hi), jnp.where(up, c, cnt_lo)

        def body(st):
            lo, hi, cnt_lo = halve(*halve(*halve(*st[:3])))
            return lo, hi, cnt_lo, jnp.max(unsettled(lo, hi, cnt_lo)), st[4] + 3

        st = lax.while_loop(cond, body, (lo, hi, cnt_lo, jnp.max(unsettled(lo, hi, cnt_lo)), jnp.int32(0)))
        thr_ref[...] = jnp.broadcast_to(st[0], thr_ref.shape)
        tie_ref[0] = (jnp.max(jnp.where(st[2] > k_sel, 1.0, 0.0)) > 0.5).astype(I32)

    thr = thr_ref[0:1, :]

    @pl.when(tie_ref[0] == 0)
    def _():
        def mb_body(g, carry):
            mb_ref[srows(g), :] = jnp.where(key_ref[srows(g), :] >= thr, 0.0, NEG)
            return carry

        lax.fori_loop(0, n_s, mb_body, 0)

    @pl.when(tie_ref[0] != 0)
    def _():
        def count_gt(g, acc):
            sc = key_ref[srows(g), :].reshape(gs // 8, 8, LANES)
            return acc + jnp.sum(jnp.where(sc > thr, 1, 0), axis=0)

        n_gt = jnp.sum(lax.fori_loop(0, n_s, count_gt, jnp.zeros((8, LANES), I32)), axis=0, keepdims=True)
        room = (k_sel - n_gt).astype(F32)
        r = lax.broadcasted_iota(I32, (QBLK, QBLK), 0)
        c = lax.broadcasted_iota(I32, (QBLK, QBLK), 1)
        tri = jnp.where(c <= r, 1.0, 0.0).astype(BF16)

        def mb_body(g, seen):
            for t in range(gs // QBLK):
                rs = pl.ds(pl.multiple_of(g * gs + t * QBLK, QBLK), QBLK)
                sc = key_ref[rs, :]
                eq = jnp.where(sc == thr, 1.0, 0.0)
                rank = _dot(tri, eq.astype(BF16)) + seen
                keep = jnp.where(sc > thr, 1.0, jnp.where(rank <= room, eq, 0.0))
                mb_ref[rs, :] = jnp.where(keep > 0.5, 0.0, NEG)
                seen = seen + jnp.sum(eq, axis=0, keepdims=True)
            return seen

        lax.fori_loop(0, n_s, mb_body, jnp.zeros((1, LANES), F32))

    spg = gk // QBLK
    n_g = bi // spg + 1
    for j in range(N_PAIRS):
        qbd_ref[j] = block_diag(aq_ref[j])
    m_ref[...] = jnp.full(m_ref.shape, NEG, F32)
    l_ref[...] = jnp.zeros_like(l_ref)
    acc_ref[...] = jnp.zeros_like(acc_ref)
    al_ref[...] = jnp.ones_like(al_ref)
    p_ref[1] = jnp.zeros(p_ref.shape[1:], BF16)

    def rows(g):
        return pl.ds(pl.multiple_of(g * gk, gk), gk)

    def scores(t, slot):
        mb = mb_ref[rows(t), :]
        mb2 = jnp.concatenate([mb, mb], axis=1)
        for j in range(N_PAIRS):
            s = _dot_nt(ak_ref[j, rows(t), :], qbd_ref[j]) + mb2
            s_ref[slot, j] = s
            mx_ref[slot, j] = jnp.max(s, axis=0, keepdims=True)

    def pv(t, slot):
        return [_dot(avt_ref[t, j * LANES:(j + 1) * LANES, :], p_ref[slot, j]) for j in range(N_PAIRS)]

    def step(g, cur):
        prv = 1 - cur

        @pl.when(g >= n_g - 2)
        def _():
            for j in range(N_PAIRS):
                parts = []
                for st in range(spg):
                    slot = jnp.clip(g * spg + st - bi + 2, 0, 3)
                    parts.append(jnp.concatenate([bias_ref[2 * j, slot], bias_ref[2 * j + 1, slot]], axis=1))
                s = s_ref[cur, j] + jnp.concatenate(parts, axis=0)
                s_ref[cur, j] = s
                mx_ref[cur, j] = jnp.max(s, axis=0, keepdims=True)

        al_old = [al_ref[j] for j in range(N_PAIRS)]
        for j in range(N_PAIRS):
            m_old = m_ref[j]
            m_new = jnp.maximum(m_old, mx_ref[cur, j])
            p = jnp.exp2(s_ref[cur, j] - m_new)
            alpha = jnp.exp2(m_old - m_new)
            al_ref[j] = alpha
            l_ref[j] = l_ref[j] * alpha + jnp.sum(p, axis=0, keepdims=True)
            m_ref[j] = m_new
            p_ref[cur, j] = p.astype(BF16)
        o_prev = pv(jnp.maximum(g - 1, 0), prv)
        scores(jnp.minimum(g + 1, n_g - 1), prv)
        for j in range(N_PAIRS):
            acc_ref[j] = acc_ref[j] * al_old[j] + o_prev[j]

    scores(0, 0)

    def pipe_body(u, c):
        step(2 * u, 0)

        @pl.when(2 * u + 1 < n_g)
        def _():
            step(2 * u + 1, 1)
        return c

    lax.fori_loop(0, (n_g + 1) // 2, pipe_body, 0)
    o_last = pv(n_g - 1, (n_g - 1) % 2)

    for j in range(N_PAIRS):
        acc = acc_ref[j] * al_ref[j] + o_last[j]
        l = l_ref[j]
        o0 = acc[0:ATT_DH, 0:LANES] / l[:, 0:LANES]
        o1 = acc[ATT_DH:2 * ATT_DH, LANES:2 * LANES] / l[:, LANES:2 * LANES]
        ot_ref[j * LANES:(j + 1) * LANES, :] = jnp.concatenate([o0, o1], axis=0)

    ot = ot_ref[...]
    og = og_ref[...]
    outs = []
    for hd in range(ATT_HEADS):
        oh = ot[hd * ATT_DH:(hd + 1) * ATT_DH, :]
        ms = jnp.mean(oh * oh, axis=0, keepdims=True)
        outs.append(oh * lax.rsqrt(ms + EPS) * og[hd * ATT_DH:(hd + 1) * ATT_DH, :])
    o_ref[...] = jnp.concatenate(outs, axis=0).T.astype(BF16)


def _attention(aq, ak, avt, iq, ik, iwt, rel_bias, out_g, *, gk):
    bsz, _, s, _ = aq.shape
    k_sel = min(TOPK_MAX, s // 4)
    bias = _bias_slots(rel_bias)
    qpair = pl.BlockSpec((None, N_PAIRS, QBLK, LANES), lambda b, i: (b, 0, i, 0))
    full = lambda shp: pl.BlockSpec((None,) + shp, lambda b, i: (b,) + (0,) * len(shp),
                                    pipeline_mode=pl.Buffered(1))
    return pl.pallas_call(
        functools.partial(_attn_kernel, gs=_tile(s, 512), gk=gk, k_sel=k_sel),
        grid=(bsz, s // QBLK),
        in_specs=[qpair, full((N_PAIRS, s, LANES)), full((s // gk, ATT_W, gk)), qpair,
                  full((s, LANES)), pl.BlockSpec((None, IDX_HEADS, QBLK), lambda b, i: (b, 0, i)),
                  _const_spec(bias.shape), _const_spec((ATT_W, 1))],
        out_specs=pl.BlockSpec((None, QBLK, ATT_W), lambda b, i: (b, i, 0)),
        out_shape=jax.ShapeDtypeStruct((bsz, s, ATT_W), BF16),
        scratch_shapes=[pltpu.VMEM((s, LANES), F32),
                        pltpu.VMEM((s, LANES), F32),
                        pltpu.VMEM((8, LANES), F32),
                        pltpu.VMEM((N_PAIRS, 2 * ATT_DH, 2 * LANES), F32),
                        pltpu.VMEM((N_PAIRS, 1, 2 * LANES), F32),
                        pltpu.VMEM((N_PAIRS, 1, 2 * LANES), F32),
                        pltpu.VMEM((N_PAIRS, 1, 2 * LANES), F32),
                        pltpu.VMEM((2, N_PAIRS, gk, 2 * LANES), F32),
                        pltpu.VMEM((2, N_PAIRS, 1, 2 * LANES), F32),
                        pltpu.VMEM((2, N_PAIRS, gk, 2 * LANES), BF16),
                        pltpu.VMEM((N_PAIRS, 2 * QBLK, LANES), BF16),
                        pltpu.VMEM((ATT_W, LANES), F32),
                        pltpu.SMEM((1,), I32)],
        compiler_params=_params(("parallel", "arbitrary")),
        name="attn",
    )(aq, ak, avt, iq, ik, iwt, bias, out_g.reshape(ATT_W, 1))


def _outproj_kernel(rec_ref, att_ref, x_ref, g1_ref, sc_ref, sh_ref, ng_ref, wo_ref, wr_ref, br_ref,
                    x1_ref, h2_ref, comb_ref):
    mix = _dot(rec_ref[...], wo_ref[0:REC_W, :]) + _dot(att_ref[...], wo_ref[REC_W:REC_W + ATT_W, :])
    x1 = x_ref[...] + g1_ref[...] * mix
    x1_ref[...] = x1
    ms = jnp.mean(x1 * x1, axis=-1, keepdims=True)
    h2 = (x1 * lax.rsqrt(ms + EPS) * ng_ref[...] * (1.0 + sc_ref[...]) + sh_ref[...]).astype(BF16)
    h2_ref[...] = h2

    lg = _dot(h2, wr_ref[...]) + br_ref[...]
    lane = lax.broadcasted_iota(I32, lg.shape, 1)
    big = jnp.int32(2 * LANES)
    is_g = lane < N_GROUPS
    gl = jnp.where(is_g, lg, -jnp.inf)
    gmax = jnp.max(gl, axis=-1, keepdims=True)
    gate = 1.0 / jnp.sum(jnp.where(is_g, jnp.exp(lg - gmax), 0.0), axis=-1, keepdims=True)
    gtop = jnp.min(jnp.where(gl == gmax, lane, big), axis=-1, keepdims=True)
    e_lo = ROUTE_OFF + EXPERTS_PER_GROUP * gtop
    el = jnp.where((lane >= e_lo) & (lane < e_lo + EXPERTS_PER_GROUP), lg, -jnp.inf)
    v1 = jnp.max(el, axis=-1, keepdims=True)
    i1 = jnp.min(jnp.where(el == v1, lane, big), axis=-1, keepdims=True)
    el2 = jnp.where(lane == i1, -jnp.inf, el)
    v2 = jnp.max(el2, axis=-1, keepdims=True)
    i2 = jnp.min(jnp.where(el2 == v2, lane, big), axis=-1, keepdims=True)
    e2 = jnp.exp(v2 - v1)
    w1 = gate / (1.0 + e2)
    w2 = gate * e2 / (1.0 + e2)
    comb = jnp.where(lane == i1, w1, 0.0) + jnp.where(lane == i2, w2, 0.0)
    comb_ref[...] = jnp.where(lane == 0, gtop.astype(F32), comb)


def _outproj(rec, att, x, g1, sc2, sh2, norm_g, w_out, w_rg, b_rg, w_re, b_re, *, tm):
    bsz, s, d = x.shape
    wr = jnp.zeros((d, LANES), F32).at[:, :N_GROUPS].set(w_rg).at[:, ROUTE_OFF:ROUTE_OFF + N_EXPERTS].set(w_re)
    br = jnp.zeros((1, LANES), F32).at[0, :N_GROUPS].set(b_rg).at[0, ROUTE_OFF:ROUTE_OFF + N_EXPERTS].set(b_re)
    row = lambda w: pl.BlockSpec((None, tm, w), lambda b, i: (b, i, 0))
    vec = pl.BlockSpec((None, 1, d), lambda b, i: (b, 0, 0))
    return pl.pallas_call(
        _outproj_kernel,
        grid=(bsz, s // tm),
        in_specs=[row(REC_W), row(ATT_W), row(d), vec, vec, vec, _const_spec((1, d)),
                  _const_spec((REC_W + ATT_W, d)), _const_spec((d, LANES)), _const_spec((1, LANES))],
        out_specs=(row(d), row(d), row(LANES)),
        out_shape=(jax.ShapeDtypeStruct((bsz, s, d), F32),
                   jax.ShapeDtypeStruct((bsz, s, d), BF16),
                   jax.ShapeDtypeStruct((bsz, s, LANES), F32)),
        compiler_params=_params(("parallel", "parallel")),
        name="outproj",
    )(rec, att, x, g1.reshape(bsz, 1, d), sc2.reshape(bsz, 1, d), sh2.reshape(bsz, 1, d),
      norm_g.reshape(1, d), w_out.astype(BF16), wr.astype(BF16), br)


MOE_ALIGN = 16
MOE_LANE_SHIFT = 32
MOE_CHUNK = 288
MOE_RB = 256


def _moe_kernel(h_ref, comb_ref, x1_ref, g2_ref, w1_ref, w3_ref, w2_ref, o_ref,
                p_ref, pt_ref, hs_ref, cs_ref, ys_ref, seg_ref, *, mc):
    e = pl.program_id(2)
    tm = h_ref.shape[0]
    npad = hs_ref.shape[0]

    @pl.when(e == 0)
    def _():
        comb = comb_ref[...]
        lane = lax.broadcasted_iota(I32, (tm, LANES), 1)
        gid = comb[:, 0:1].astype(I32)
        oh = jnp.where(lane == gid, 1.0, 0.0)
        ohb = oh.astype(BF16)
        pre = []
        for rb in range(tm // MOE_RB):
            r = rb * MOE_RB + lax.broadcasted_iota(I32, (MOE_RB, tm), 0)
            c = lax.broadcasted_iota(I32, (MOE_RB, tm), 1)
            pre.append(_dot(jnp.where(c < r, 1.0, 0.0).astype(BF16), ohb))
        prefix = jnp.concatenate(pre, axis=0)
        cnt = jnp.sum(oh, axis=0, keepdims=True).astype(I32)
        cnt_al = ((cnt + (MOE_ALIGN - 1)) // MOE_ALIGN) * MOE_ALIGN
        base = jnp.sum(jnp.where(lane < gid, cnt_al.astype(F32), 0.0), axis=-1, keepdims=True)
        rank = jnp.sum(prefix * oh, axis=-1, keepdims=True)
        pos = (base + rank).astype(I32)
        pos_row = jnp.broadcast_to(pos.astype(F32), (tm, LANES)).T[0:1, :].astype(I32)
        for rb in range(tm // MOE_RB):
            sl = slice(rb * MOE_RB, (rb + 1) * MOE_RB)
            coln = lax.broadcasted_iota(I32, (MOE_RB, npad), 1)
            pt_ref[sl, :] = jnp.where(coln == pos[sl], 1.0, 0.0).astype(BF16)
        for rb in range(npad // LANES):
            sl = slice(rb * LANES, (rb + 1) * LANES)
            rown = rb * LANES + lax.broadcasted_iota(I32, (LANES, tm), 0)
            p_ref[sl, :] = jnp.where(rown == pos_row, 1.0, 0.0).astype(BF16)
        c_hi = comb.astype(BF16)
        c_mid, c_lo = _split_bf16(comb - c_hi.astype(F32))
        packed = (c_hi.astype(F32) + pltpu.roll(c_mid.astype(F32), MOE_LANE_SHIFT, axis=1)
                  + pltpu.roll(c_lo.astype(F32), 2 * MOE_LANE_SHIFT, axis=1)).astype(BF16)
        srt = _dot(p_ref[...], jnp.concatenate([h_ref[...], packed], axis=1))
        d = h_ref.shape[1]
        hs_ref[...] = srt[:, :d].astype(BF16)
        cp = srt[:, d:]
        cs_ref[...] = (cp + pltpu.roll(cp, LANES - MOE_LANE_SHIFT, axis=1)
                       + pltpu.roll(cp, LANES - 2 * MOE_LANE_SHIFT, axis=1))
        ys_ref[...] = jnp.zeros_like(ys_ref)
        start = jnp.int32(0)
        for g in range(N_GROUPS):
            seg_ref[g] = start
            seg_ref[N_GROUPS + g] = cnt[0, g]
            start = start + cnt_al[0, g]

    grp = e // EXPERTS_PER_GROUP
    start = seg_ref[grp]
    n_rows = seg_ref[N_GROUPS + grp]

    def chunk(ci, carry):
        rs = pl.ds(pl.multiple_of(start + ci * mc, MOE_ALIGN), mc)
        hb = hs_ref[rs, :]
        cw = cs_ref[rs, :]
        lane = lax.broadcasted_iota(I32, cw.shape, 1)
        col = jnp.sum(jnp.where(lane == e + ROUTE_OFF, cw, 0.0), axis=-1, keepdims=True)
        he = _silu(_dot(hb, w1_ref[...])) * _dot(hb, w3_ref[...]) * col
        ys_ref[rs, :] += _dot(he.astype(BF16), w2_ref[...])
        return carry

    lax.fori_loop(0, (n_rows + mc - 1) // mc, chunk, 0)

    @pl.when(e == N_EXPERTS - 1)
    def _():
        o_ref[...] = x1_ref[...] + g2_ref[...] * _dot(pt_ref[...], ys_ref[...].astype(BF16))


def _moe(h2, comb, x1, g2, w1, w3, w2, *, tm):
    bsz, s, d = x1.shape
    mc = MOE_CHUNK
    npad = -(-(tm + N_GROUPS * MOE_ALIGN + mc) // LANES) * LANES
    row = lambda w, **kw: pl.BlockSpec((None, tm, w), lambda b, i, e: (b, i, 0), **kw)
    once = dict(pipeline_mode=pl.Buffered(1))
    return pl.pallas_call(
        functools.partial(_moe_kernel, mc=mc),
        grid=(bsz, s // tm, N_EXPERTS),
        in_specs=[row(d, **once), row(LANES), row(d, **once),
                  pl.BlockSpec((None, 1, d), lambda b, i, e: (b, 0, 0)),
                  pl.BlockSpec((None, d, D_EXPERT), lambda b, i, e: (e, 0, 0)),
                  pl.BlockSpec((None, d, D_EXPERT), lambda b, i, e: (e, 0, 0)),
                  pl.BlockSpec((None, D_EXPERT, d), lambda b, i, e: (e, 0, 0))],
        out_specs=row(d),
        out_shape=jax.ShapeDtypeStruct((bsz, s, d), F32),
        scratch_shapes=[pltpu.VMEM((npad, tm), BF16),
                        pltpu.VMEM((tm, npad), BF16),
                        pltpu.VMEM((npad, d), BF16),
                        pltpu.VMEM((npad, LANES), F32),
                        pltpu.VMEM((npad, d), F32),
                        pltpu.SMEM((2 * N_GROUPS,), I32)],
        compiler_params=_params(("parallel", "parallel", "arbitrary")),
        name="moe",
    )(h2, comb, x1, g2.reshape(bsz, 1, d), w1.astype(BF16), w3.astype(BF16), w2.astype(BF16))


def _tile(s, pref):
    t = min(s, pref)
    assert s % t == 0
    return t


def kernel(x, c, w_ada, b_ada, norm1_g, norm2_g, w_in, lb_logits, rec_out_g, q_norm_g, k_norm_g,
           idx_k_norm_g, idx_k_norm_b, attn_out_g, rel_bias, w_out, w_rg, b_rg, w_re, b_re, w1, w3, w2):
    bsz, s, d = x.shape
    depth = w_ada.shape[0]
    gk = _tile(s, 512)
    for l in range(depth):
        mod = _adaln(c, w_ada[l], b_ada[l])
        sh1, sc1, g1, sh2, sc2, g2 = jnp.split(mod, 6, axis=-1)
        q, f, v, g, aq, ak, avt, iq, ik, iwt = _inproj(
            x, sc1, sh1, norm1_g[l], w_in[l], lb_logits, q_norm_g[l], k_norm_g[l],
            idx_k_norm_g[l], idx_k_norm_b[l], layer=l, gt=gk, tm=_tile(s, 512))
        rec = _hgrn(q, f, v, g, rec_out_g[l], ts=_tile(s, 256))
        att = _attention(aq, ak, avt, iq, ik, iwt, rel_bias, attn_out_g[l], gk=gk)
        x1, h2, comb = _outproj(rec, att, x, g1, sc2, sh2, norm2_g[l], w_out[l],
                                w_rg[l], b_rg[l], w_re[l], b_re[l], tm=_tile(s, 512))
        x = _moe(h2, comb, x1, g2, w1[l], w3[l], w2[l], tm=_tile(s, 1024))
    return x
```

```python
import functools

import numpy as np
import jax
import jax.numpy as jnp
from jax import lax
from jax.experimental import pallas as pl
from jax.experimental.pallas import tpu as pltpu

F32 = jnp.float32
BF16 = jnp.bfloat16
I32 = jnp.int32

CHUNK = 64
QBLK = 128
EPS = 1e-6
REC_HEADS = 4
REC_DK = 128
REC_DV = 128
REC_W = REC_HEADS * REC_DV
ATT_HEADS = 8
ATT_DH = 64
ATT_W = ATT_HEADS * ATT_DH
IDX_HEADS = 8
IDX_DIM = 64
TOPK_MAX = 256
NUM_BUCKETS = 32
MAX_DISTANCE = 128
N_GROUPS = 4
EXPERTS_PER_GROUP = 4
N_EXPERTS = N_GROUPS * EXPERTS_PER_GROUP
D_EXPERT = 512

LANES = 128
V7X_VMEM_LIMIT = 56 * 1024 * 1024
F32_LOWEST = float(np.finfo(np.float32).min)
BISECT_FIRST = 15
BISECT_CAP = 320
NEG = -1e30
LOG2E = float(np.log2(np.e))
N_PAIRS = ATT_HEADS // 2
ROUTE_OFF = N_GROUPS


def _dot(a, b):
    return jnp.dot(a, b, preferred_element_type=F32)


def _dot_nt(a, b):
    return lax.dot_general(a, b, (((1,), (1,)), ((), ())), preferred_element_type=F32)


def _dot_tn(a, b):
    return lax.dot_general(a, b, (((0,), (0,)), ((), ())), preferred_element_type=F32)


def _split_bf16(a):
    hi = a.astype(BF16)
    lo = (a - hi.astype(F32)).astype(BF16)
    return hi, lo


def _silu(a):
    return a * jax.nn.sigmoid(a)


def _const_spec(shape):
    nd = len(shape)
    return pl.BlockSpec(shape, lambda *_: (0,) * nd, pipeline_mode=pl.Buffered(1))


def _params(sem):
    return pltpu.CompilerParams(dimension_semantics=sem, vmem_limit_bytes=V7X_VMEM_LIMIT)


def _adaln_kernel(c_ref, w_ref, b_ref, o_ref):
    a_hi, a_lo = _split_bf16(_silu(c_ref[...]))
    w_hi, w_lo = _split_bf16(w_ref[...])
    o_ref[...] = _dot(a_hi, w_hi) + _dot(a_lo, w_hi) + _dot(a_hi, w_lo) + b_ref[...]


def _adaln(c, w, b):
    bsz, d = c.shape
    n = w.shape[1]
    rows = 16
    bn = 1024
    cp = jnp.zeros((rows, d), F32).at[:bsz].set(c)
    out = pl.pallas_call(
        _adaln_kernel,
        grid=(n // bn,),
        in_specs=[pl.BlockSpec((rows, d), lambda i: (0, 0)),
                  pl.BlockSpec((d, bn), lambda i: (0, i)),
                  pl.BlockSpec((1, bn), lambda i: (0, i))],
        out_specs=pl.BlockSpec((rows, bn), lambda i: (0, i)),
        out_shape=jax.ShapeDtypeStruct((rows, n), F32),
        compiler_params=_params(("parallel",)),
        name="adaln",
    )(cp, w, b.reshape(1, n))
    return out[:bsz]


def _inproj_kernel(x_ref, sc_ref, sh_ref, ng_ref, wrec_ref, watt_ref, widx_ref, lbl_ref,
                   qg_ref, kg_ref, ikg_ref, ikb_ref, pm_ref,
                   q_ref, f_ref, v_ref, g_ref, aq_ref, ak_ref, avt_ref, iq_ref, ik_ref, iwt_ref,
                   *, layer, gt):
    x = x_ref[...]
    tm = x.shape[0]
    ms = jnp.mean(x * x, axis=-1, keepdims=True)
    h = x * lax.rsqrt(ms + EPS) * ng_ref[...] * (1.0 + sc_ref[...]) + sh_ref[...]
    hb = h.astype(BF16)

    zr = _dot(hb, wrec_ref[...])
    q_ref[...] = _silu(zr[:, 0:REC_W]).astype(BF16)
    lbl = lbl_ref[...]
    e = jnp.exp(lbl - jnp.max(lbl, axis=0, keepdims=True))
    sm = e / jnp.sum(e, axis=0, keepdims=True)
    lb = jnp.sum(sm[0:layer + 1], axis=0, keepdims=True)
    f_ref[...] = lb + (1.0 - lb) * jax.nn.sigmoid(zr[:, REC_W:2 * REC_W])
    v_ref[...] = zr[:, 2 * REC_W:3 * REC_W].astype(BF16)
    g_ref[...] = _silu(zr[:, 3 * REC_W:4 * REC_W]).astype(BF16)

    za = _dot(hb, watt_ref[...])
    aq = za[:, 0:ATT_W]
    ak = za[:, ATT_W:2 * ATT_W]
    av = za[:, 2 * ATT_W:3 * ATT_W]
    pm = pm_ref[...]
    aqn = aq * lax.rsqrt(_dot((aq * aq).astype(BF16), pm) + EPS) * qg_ref[...]
    akn = ak * lax.rsqrt(_dot((ak * ak).astype(BF16), pm) + EPS) * kg_ref[...]
    for j in range(N_PAIRS):
        aq_ref[j] = aqn[:, j * LANES:(j + 1) * LANES].astype(BF16)
        ak_ref[j] = akn[:, j * LANES:(j + 1) * LANES].astype(BF16)
    for t in range(tm // gt):
        avt_ref[t] = av[t * gt:(t + 1) * gt, :].T.astype(BF16)

    zi = _dot(hb, widx_ref[...])
    for j in range(N_PAIRS):
        iq_ref[j] = zi[:, j * LANES:(j + 1) * LANES].astype(BF16)
    tail = zi[:, IDX_HEADS * IDX_DIM:IDX_HEADS * IDX_DIM + LANES]
    lane = lax.broadcasted_iota(I32, tail.shape, 1)
    is_k = lane < IDX_DIM
    mu = jnp.sum(jnp.where(is_k, tail, 0.0), axis=-1, keepdims=True) * (1.0 / IDX_DIM)
    dlt = jnp.where(is_k, tail - mu, 0.0)
    var = jnp.sum(dlt * dlt, axis=-1, keepdims=True) * (1.0 / IDX_DIM)
    ikn = dlt * lax.rsqrt(var + EPS) * ikg_ref[...] + ikb_ref[...]
    ik_ref[...] = jnp.where(is_k, ikn, pltpu.roll(ikn, IDX_DIM, axis=1)).astype(BF16)
    iwt_ref[...] = tail.T[IDX_DIM:IDX_DIM + IDX_HEADS, :] * (IDX_HEADS ** -0.5 * IDX_DIM ** -0.5)


def _inproj(x, sc1, sh1, norm_g, w_in, lb_logits, q_g, k_g, ik_g, ik_b, *, layer, gt, tm):
    bsz, s, d = x.shape
    n_rec = 4 * REC_W
    n_att = 3 * ATT_W
    n_idx = IDX_HEADS * IDX_DIM + LANES
    wb = w_in.astype(BF16)
    w_rec = wb[:, :n_rec]
    w_att = wb[:, n_rec:n_rec + n_att]
    w_idx = jnp.zeros((d, n_idx), BF16).at[:, :w_in.shape[1] - n_rec - n_att].set(wb[:, n_rec + n_att:])
    pm = jnp.asarray(np.kron(np.eye(ATT_HEADS), np.full((ATT_DH, ATT_DH), 1.0 / ATT_DH)), BF16)
    qg = jnp.tile(q_g, ATT_HEADS).reshape(1, ATT_W) * (ATT_DH ** -0.5 * LOG2E)
    kg = jnp.tile(k_g, ATT_HEADS).reshape(1, ATT_W)
    ikg = jnp.zeros((1, LANES), F32).at[0, :IDX_DIM].set(ik_g)
    ikb = jnp.zeros((1, LANES), F32).at[0, :IDX_DIM].set(ik_b)
    nl = lb_logits.shape[0]

    row = lambda w: pl.BlockSpec((None, tm, w), lambda b, i: (b, i, 0))
    pair = pl.BlockSpec((None, N_PAIRS, tm, LANES), lambda b, i: (b, 0, i, 0))
    vec = pl.BlockSpec((None, 1, d), lambda b, i: (b, 0, 0))
    out_shapes = (
        jax.ShapeDtypeStruct((bsz, s, REC_W), BF16),
        jax.ShapeDtypeStruct((bsz, s, REC_W), F32),
        jax.ShapeDtypeStruct((bsz, s, REC_W), BF16),
        jax.ShapeDtypeStruct((bsz, s, REC_W), BF16),
        jax.ShapeDtypeStruct((bsz, N_PAIRS, s, LANES), BF16),
        jax.ShapeDtypeStruct((bsz, N_PAIRS, s, LANES), BF16),
        jax.ShapeDtypeStruct((bsz, s // gt, ATT_W, gt), BF16),
        jax.ShapeDtypeStruct((bsz, N_PAIRS, s, LANES), BF16),
        jax.ShapeDtypeStruct((bsz, s, LANES), BF16),
        jax.ShapeDtypeStruct((bsz, IDX_HEADS, s), F32),
    )
    out_specs = (
        row(REC_W), row(REC_W), row(REC_W), row(REC_W), pair, pair,
        pl.BlockSpec((None, tm // gt, ATT_W, gt), lambda b, i: (b, i, 0, 0)),
        pair, row(LANES),
        pl.BlockSpec((None, IDX_HEADS, tm), lambda b, i: (b, 0, i)),
    )
    return pl.pallas_call(
        functools.partial(_inproj_kernel, layer=layer, gt=gt),
        grid=(bsz, s // tm),
        in_specs=[row(d), vec, vec, _const_spec((1, d)),
                  _const_spec((d, n_rec)), _const_spec((d, n_att)), _const_spec((d, n_idx)),
                  _const_spec((nl, REC_W)), _const_spec((1, ATT_W)), _const_spec((1, ATT_W)),
                  _const_spec((1, LANES)), _const_spec((1, LANES)), _const_spec((ATT_W, ATT_W))],
        out_specs=out_specs,
        out_shape=out_shapes,
        compiler_params=_params(("parallel", "parallel")),
        name="inproj",
    )(x, sc1.reshape(bsz, 1, d), sh1.reshape(bsz, 1, d), norm_g.reshape(1, d),
      w_rec, w_att, w_idx, lb_logits, qg, kg, ikg, ikb, pm)


N_LEVELS = 6


def _hgrn_tables():
    c = CHUNK
    w = np.zeros((N_LEVELS + 2, c, c), np.float32)
    am = np.zeros((N_LEVELS + 1, c, c), np.float32)
    t = np.arange(c)
    for m in range(N_LEVELS):
        hs = 1 << m
        blk = t // (2 * hs)
        upper = (t // hs) % 2 == 1
        ref = blk * 2 * hs + hs - 1
        for i in range(c):
            if upper[i]:
                w[m, i, ref[i] + 1:i + 1] = 1.0
            else:
                w[m, i, i + 1:ref[i] + 1] = 1.0
        am[m] = (blk[:, None] == blk[None, :]) & upper[:, None] & ~upper[None, :]
    w[N_LEVELS] = np.tril(np.ones((c, c)))
    w[N_LEVELS + 1] = np.triu(np.ones((c, c)), 1)
    am[N_LEVELS] = np.eye(c)
    w = w.reshape((N_LEVELS + 2) * c, c)
    return np.concatenate([w, w], axis=1), am


HGRN_UNROLL = 4


def _hgrn_kernel(q_ref, f_ref, v_ref, g_ref, og_ref, ww_ref, am_ref, o_ref, st_ref, ex_ref, *, n_chunks):
    @pl.when(pl.program_id(1) == 0)
    def _():
        st_ref[...] = jnp.zeros_like(st_ref)

    ww = ww_ref[...]
    c = CHUNK
    tbit = lax.broadcasted_iota(I32, (c, REC_DK), 0)
    items = [(cc, hd) for cc in range(HGRN_UNROLL) for hd in range(REC_HEADS)]

    def chunks(ci, carry):
        def blk(ref, it):
            r0 = pl.multiple_of((ci * HGRN_UNROLL + it[0]) * c, c)
            return ref.at[pl.ds(r0, c), it[1] * REC_DK:(it[1] + 1) * REC_DK]

        def ex(i, part, last_row=False):
            r0 = (part + 1) * c - 1 if last_row else part * c
            return ex_ref[items[i][0], r0:(part + 1) * c, items[i][1] * REC_DK:(items[i][1] + 1) * REC_DK]

        f = [blk(f_ref, it)[...] for it in items]
        q = [blk(q_ref, it)[...].astype(F32) for it in items]
        v = [blk(v_ref, it)[...] for it in items]
        k = [1.0 - fi for fi in f]
        for cc in range(HGRN_UNROLL):
            cols = [jnp.concatenate(_split_bf16(jnp.log(f[cc * REC_HEADS + hd])), axis=0) for hd in range(REC_HEADS)]
            ex_ref[cc] = jnp.exp(_dot(ww, jnp.concatenate(cols, axis=1)))
        a = [am_ref[N_LEVELS] * _dot_nt(q[i].astype(BF16), k[i].astype(BF16)) for i in range(len(items))]
        for m in range(N_LEVELS):
            upper = ((tbit >> m) & 1) == 1
            for i in range(len(items)):
                tm_ = (ex(i, m) * jnp.where(upper, q[i], k[i])).astype(BF16)
                a[i] = a[i] + am_ref[m] * _dot_nt(tm_, tm_)
        qb = [(q[i] * ex(i, N_LEVELS)).astype(BF16) for i in range(len(items))]
        kb = [(k[i] * ex(i, N_LEVELS + 1)).astype(BF16) for i in range(len(items))]
        intra = [_dot(a[i].astype(BF16), v[i]) for i in range(len(items))]
        ut = [_dot_tn(v[i], kb[i]) for i in range(len(items))]
        for hd in range(REC_HEADS):
            st = st_ref[hd]
            for cc in range(HGRN_UNROLL):
                i = cc * REC_HEADS + hd
                o = intra[i] + _dot_nt(qb[i], st.astype(BF16))
                st = st * ex(i, N_LEVELS, last_row=True) + ut[i]
                ms = jnp.mean(o * o, axis=-1, keepdims=True)
                gate = blk(g_ref, items[i])[...].astype(F32)
                y = o * lax.rsqrt(ms + EPS) * og_ref[:, hd * REC_DV:(hd + 1) * REC_DV] * gate
                blk(o_ref, items[i])[...] = y.astype(BF16)
            st_ref[hd] = st
        return carry

    lax.fori_loop(0, n_chunks // HGRN_UNROLL, chunks, 0)


def _hgrn(q, f, v, g, out_g, *, ts):
    bsz, s, _ = q.shape
    ww_np, am_np = _hgrn_tables()
    ww = jnp.asarray(ww_np, BF16)
    am = jnp.asarray(am_np, F32)
    row = pl.BlockSpec((None, ts, REC_W), lambda b, i: (b, i, 0))
    return pl.pallas_call(
        functools.partial(_hgrn_kernel, n_chunks=ts // CHUNK),
        grid=(bsz, s // ts),
        in_specs=[row, row, row, row, _const_spec((1, REC_W)),
                  _const_spec(ww.shape), _const_spec(am.shape)],
        out_specs=row,
        out_shape=jax.ShapeDtypeStruct((bsz, s, REC_W), BF16),
        scratch_shapes=[pltpu.VMEM((REC_HEADS, REC_DV, REC_DK), F32),
                        pltpu.VMEM((HGRN_UNROLL, (N_LEVELS + 2) * CHUNK, REC_HEADS * REC_DK), F32)],
        compiler_params=_params(("parallel", "arbitrary")),
        name="hgrn",
    )(q, f, v, g, out_g.reshape(1, REC_W), ww, am)


def _t5_bucket(rel):
    nb = NUM_BUCKETS // 2
    max_exact = nb // 2
    ret = jnp.where(rel > 0, nb, 0)
    n = jnp.abs(rel)
    nf = jnp.maximum(n, 1).astype(F32)
    large = max_exact + (jnp.log(nf / max_exact) / np.log(MAX_DISTANCE / max_exact)
                         * (nb - max_exact)).astype(I32)
    large = jnp.minimum(large, nb - 1)
    return ret + jnp.where(n < max_exact, n, large)


def _bias_slots(rel_bias):
    n_rel = 3 * QBLK
    rel = jnp.arange(n_rel, dtype=I32) - 2 * QBLK
    tab = rel_bias[_t5_bucket(rel)].T
    far = rel_bias[_t5_bucket(jnp.full((1,), -2 * QBLK - 1, I32))].T
    rev = tab[:, ::-1]
    skew = jnp.tile(rev, (1, 2 * QBLK))[:, :2 * QBLK * (n_rel - 1)].reshape(ATT_HEADS, 2 * QBLK, n_rel - 1)
    near = skew[:, :, 2 * QBLK - 1:]
    nb = ((near - far[:, :, None]) * LOG2E).reshape(ATT_HEADS, 2, QBLK, QBLK)
    z = jnp.zeros((ATT_HEADS, 1, QBLK, QBLK), F32)
    return jnp.concatenate([z, nb, z], axis=1)


def _attn_kernel(aq_ref, ak_ref, avt_ref, iq_ref, ik_ref, iwt_ref, bias_ref, og_ref, o_ref,
                 key_ref, mb_ref, thr_ref, acc_ref, m_ref, l_ref, al_ref, s_ref, mx_ref, p_ref, qbd_ref, ot_ref, tie_ref,
                 *, gs, gk, k_sel):
    bi = pl.program_id(1)
    n_s = bi // (gs // QBLK) + 1
    lane = lax.broadcasted_iota(I32, (QBLK, LANES), 1)

    def block_diag(xq):
        zero = jnp.zeros_like(xq)
        return jnp.concatenate([jnp.where(lane < ATT_DH, xq, zero), jnp.where(lane >= ATT_DH, xq, zero)], axis=0)

    def srows(g):
        return pl.ds(pl.multiple_of(g * gs, gs), gs)

    for j in range(N_PAIRS):
        qbd_ref[j] = block_diag(iq_ref[j])
    w = iwt_ref[...]

    def group_scores(g):
        sc = _dot_nt(ik_ref[srows(g), :], qbd_ref[...].reshape(N_PAIRS * 2 * QBLK, LANES))
        acc = jnp.zeros((gs, LANES), F32)
        for hd in range(IDX_HEADS):
            acc = acc + jnp.maximum(sc[:, hd * LANES:(hd + 1) * LANES], 0.0) * w[hd:hd + 1, :]
        return acc

    def stats(sc, c):
        s3 = sc.reshape(gs // 8, 8, LANES)
        return (jnp.minimum(c[0], jnp.min(jnp.where(s3 == -jnp.inf, jnp.inf, s3), axis=0)),
                jnp.maximum(c[1], jnp.max(s3, axis=0)),
                c[2] + jnp.sum(jnp.where(s3 >= 0.0, 1, 0), axis=0),
                c[3] + jnp.sum(jnp.where(s3 > 0.0, 1, 0), axis=0))

    def idx_body(g, carry):
        sc = group_scores(g)
        key_ref[srows(g), :] = sc
        return stats(sc, carry)

    def idx_pair(u, carry):
        return idx_body(2 * u + 1, idx_body(2 * u, carry))

    z8 = jnp.zeros((8, LANES), I32)
    st8 = lax.fori_loop(0, (n_s - 1) // 2, idx_pair,
                        (jnp.full((8, LANES), jnp.inf, F32), jnp.full((8, LANES), -jnp.inf, F32), z8, z8))
    st8 = lax.cond((n_s - 1) % 2 == 1, lambda c: idx_body(n_s - 2, c), lambda c: c, st8)

    last = srows(n_s - 1)
    spos = (n_s - 1) * gs + lax.broadcasted_iota(I32, (gs, LANES), 0)
    tpos = bi * QBLK + lax.broadcasted_iota(I32, (gs, LANES), 1)
    sc_last = jnp.where((spos // CHUNK) <= (tpos // CHUNK), group_scores(n_s - 1), -jnp.inf)
    key_ref[last, :] = sc_last
    mn8, mx8, ge8, gt8 = stats(sc_last, st8)

    thr_ref[...] = jnp.full(thr_ref.shape, F32_LOWEST, F32)
    tie_ref[0] = 0

    @pl.when((2 * bi + 2) * CHUNK > k_sel)
    def _():
        def count_ge(cand):
            def one(g):
                sc = key_ref[srows(g), :].reshape(gs // 8, 8, LANES)
                return jnp.sum(jnp.where(sc >= cand, 1, 0), axis=0)

            def two(u, acc):
                second = jnp.where(2 * u + 1 < n_s, 1, 0)
                return acc + one(2 * u) + one(jnp.minimum(2 * u + 1, n_s - 1)) * second

            acc = lax.fori_loop(0, (n_s + 1) // 2, two, jnp.zeros((8, LANES), I32))
            return jnp.sum(acc, axis=0, keepdims=True)

        c0_ge = jnp.sum(ge8, axis=0, keepdims=True)
        c0_gt = jnp.sum(gt8, axis=0, keepdims=True)
        mn = jnp.min(mn8, axis=0, keepdims=True)
        mx = jnp.max(mx8, axis=0, keepdims=True)
        above = mx + (jnp.abs(mx) * 2.0 ** -20 + 1e-30)
        n_adm = (2 * bi + 1 + lax.broadcasted_iota(I32, (1, LANES), 1) // CHUNK) * CHUNK
        pos = c0_gt >= k_sel
        non_neg = c0_ge >= k_sel
        lo = jnp.where(non_neg, 0.0, mn)
        cnt_lo = jnp.where(non_neg, c0_ge, n_adm)
        hi = jnp.where(pos, above, 0.0)

        def midpoint(lo, hi):
            return 0.5 * lo + 0.5 * hi

        def unsettled(lo, hi, cnt_lo):
            mid = midpoint(lo, hi)
            return jnp.where(cnt_lo > k_sel, jnp.where(mid > lo, jnp.where(mid < hi, 1.0, 0.0), 0.0), 0.0)

        def cond(st):
            return jnp.logical_and(st[3] > 0.5, st[4] < BISECT_CAP)

        def halve(lo, hi, cnt_lo):
            act = unsettled(lo, hi, cnt_lo) > 0.5
            mid = midpoint(lo, hi)
            c = count_ge(mid)
            up = jnp.logical_and(act, c >= k_sel)
            dn = jnp.logical_and(act, c < k_sel)
            return jnp.where(up, mid, lo), jnp.where(dn, mid, hi), jnp.where(up, c, cnt_lo)

        def body(st):
            lo, hi, cnt_lo = halve(*halve(*halve(*st[:3])))
            return lo, hi, cnt_lo, jnp.max(unsettled(lo, hi, cnt_lo)), st[4] + 3

        lo, hi, cnt_lo = lax.fori_loop(0, BISECT_FIRST, lambda _, s3: halve(*s3), (lo, hi, cnt_lo))
        st = lax.while_loop(cond, body,
                            (lo, hi, cnt_lo, jnp.max(unsettled(lo, hi, cnt_lo)), jnp.int32(BISECT_FIRST)))
        thr_ref[...] = jnp.broadcast_to(st[0], thr_ref.shape)
        tie_ref[0] = (jnp.max(jnp.where(st[2] > k_sel, 1.0, 0.0)) > 0.5).astype(I32)

    thr = thr_ref[0:1, :]

    @pl.when(tie_ref[0] == 0)
    def _():
        def mb_body(g, carry):
            mb_ref[srows(g), :] = jnp.where(key_ref[srows(g), :] >= thr, 0.0, NEG)
            return carry

        lax.fori_loop(0, n_s, mb_body, 0)

    @pl.when(tie_ref[0] != 0)
    def _():
        def count_gt(g, acc):
            sc = key_ref[srows(g), :].reshape(gs // 8, 8, LANES)
            return acc + jnp.sum(jnp.where(sc > thr, 1, 0), axis=0)

        n_gt = jnp.sum(lax.fori_loop(0, n_s, count_gt, jnp.zeros((8, LANES), I32)), axis=0, keepdims=True)
        room = (k_sel - n_gt).astype(F32)
        r = lax.broadcasted_iota(I32, (QBLK, QBLK), 0)
        c = lax.broadcasted_iota(I32, (QBLK, QBLK), 1)
        tri = jnp.where(c <= r, 1.0, 0.0).astype(BF16)

        def mb_body(g, seen):
            for t in range(gs // QBLK):
                rs = pl.ds(pl.multiple_of(g * gs + t * QBLK, QBLK), QBLK)
                sc = key_ref[rs, :]
                eq = jnp.where(sc == thr, 1.0, 0.0)
                rank = _dot(tri, eq.astype(BF16)) + seen
                keep = jnp.where(sc > thr, 1.0, jnp.where(rank <= room, eq, 0.0))
                mb_ref[rs, :] = jnp.where(keep > 0.5, 0.0, NEG)
                seen = seen + jnp.sum(eq, axis=0, keepdims=True)
            return seen

        lax.fori_loop(0, n_s, mb_body, jnp.zeros((1, LANES), F32))

    spg = gk // QBLK
    n_g = bi // spg + 1
    for j in range(N_PAIRS):
        qbd_ref[j] = block_diag(aq_ref[j])
    m_ref[...] = jnp.full(m_ref.shape, NEG, F32)
    l_ref[...] = jnp.zeros_like(l_ref)
    acc_ref[...] = jnp.zeros_like(acc_ref)
    al_ref[...] = jnp.ones_like(al_ref)
    p_ref[1] = jnp.zeros(p_ref.shape[1:], BF16)

    def rows(g):
        return pl.ds(pl.multiple_of(g * gk, gk), gk)

    def scores(t, slot):
        mb = mb_ref[rows(t), :]
        mb2 = jnp.concatenate([mb, mb], axis=1)
        for j in range(N_PAIRS):
            s = _dot_nt(ak_ref[j, rows(t), :], qbd_ref[j]) + mb2
            s_ref[slot, j] = s
            mx_ref[slot, j] = jnp.max(s, axis=0, keepdims=True)

    def pv(t, slot):
        return [_dot(avt_ref[t, j * LANES:(j + 1) * LANES, :], p_ref[slot, j]) for j in range(N_PAIRS)]

    def step(g, cur):
        prv = 1 - cur

        @pl.when(g >= n_g - 2)
        def _():
            for j in range(N_PAIRS):
                parts = []
                for st in range(spg):
                    slot = jnp.clip(g * spg + st - bi + 2, 0, 3)
                    parts.append(jnp.concatenate([bias_ref[2 * j, slot], bias_ref[2 * j + 1, slot]], axis=1))
                s = s_ref[cur, j] + jnp.concatenate(parts, axis=0)
                s_ref[cur, j] = s
                mx_ref[cur, j] = jnp.max(s, axis=0, keepdims=True)

        al_old = [al_ref[j] for j in range(N_PAIRS)]
        for j in range(N_PAIRS):
            m_old = m_ref[j]
            m_new = jnp.maximum(m_old, mx_ref[cur, j])
            p = jnp.exp2(s_ref[cur, j] - m_new)
            alpha = jnp.exp2(m_old - m_new)
            al_ref[j] = alpha
            l_ref[j] = l_ref[j] * alpha + jnp.sum(p, axis=0, keepdims=True)
            m_ref[j] = m_new
            p_ref[cur, j] = p.astype(BF16)
        o_prev = pv(jnp.maximum(g - 1, 0), prv)
        scores(jnp.minimum(g + 1, n_g - 1), prv)
        for j in range(N_PAIRS):
            acc_ref[j] = acc_ref[j] * al_old[j] + o_prev[j]

    scores(0, 0)

    def pipe_body(u, c):
        step(2 * u, 0)

        @pl.when(2 * u + 1 < n_g)
        def _():
            step(2 * u + 1, 1)
        return c

    lax.fori_loop(0, (n_g + 1) // 2, pipe_body, 0)
    o_last = pv(n_g - 1, (n_g - 1) % 2)

    for j in range(N_PAIRS):
        acc = acc_ref[j] * al_ref[j] + o_last[j]
        l = l_ref[j]
        o0 = acc[0:ATT_DH, 0:LANES] / l[:, 0:LANES]
        o1 = acc[ATT_DH:2 * ATT_DH, LANES:2 * LANES] / l[:, LANES:2 * LANES]
        ot_ref[j * LANES:(j + 1) * LANES, :] = jnp.concatenate([o0, o1], axis=0)

    ot = ot_ref[...]
    og = og_ref[...]
    outs = []
    for hd in range(ATT_HEADS):
        oh = ot[hd * ATT_DH:(hd + 1) * ATT_DH, :]
        ms = jnp.mean(oh * oh, axis=0, keepdims=True)
        outs.append(oh * lax.rsqrt(ms + EPS) * og[hd * ATT_DH:(hd + 1) * ATT_DH, :])
    o_ref[...] = jnp.concatenate(outs, axis=0).T.astype(BF16)


def _attention(aq, ak, avt, iq, ik, iwt, rel_bias, out_g, *, gk):
    bsz, _, s, _ = aq.shape
    k_sel = min(TOPK_MAX, s // 4)
    bias = _bias_slots(rel_bias)
    qpair = pl.BlockSpec((None, N_PAIRS, QBLK, LANES), lambda b, i: (b, 0, i, 0))
    full = lambda shp: pl.BlockSpec((None,) + shp, lambda b, i: (b,) + (0,) * len(shp),
                                    pipeline_mode=pl.Buffered(1))
    return pl.pallas_call(
        functools.partial(_attn_kernel, gs=_tile(s, 512), gk=gk, k_sel=k_sel),
        grid=(bsz, s // QBLK),
        in_specs=[qpair, full((N_PAIRS, s, LANES)), full((s // gk, ATT_W, gk)), qpair,
                  full((s, LANES)), pl.BlockSpec((None, IDX_HEADS, QBLK), lambda b, i: (b, 0, i)),
                  _const_spec(bias.shape), _const_spec((ATT_W, 1))],
        out_specs=pl.BlockSpec((None, QBLK, ATT_W), lambda b, i: (b, i, 0)),
        out_shape=jax.ShapeDtypeStruct((bsz, s, ATT_W), BF16),
        scratch_shapes=[pltpu.VMEM((s, LANES), F32),
                        pltpu.VMEM((s, LANES), F32),
                        pltpu.VMEM((8, LANES), F32),
                        pltpu.VMEM((N_PAIRS, 2 * ATT_DH, 2 * LANES), F32),
                        pltpu.VMEM((N_PAIRS, 1, 2 * LANES), F32),
                        pltpu.VMEM((N_PAIRS, 1, 2 * LANES), F32),
                        pltpu.VMEM((N_PAIRS, 1, 2 * LANES), F32),
                        pltpu.VMEM((2, N_PAIRS, gk, 2 * LANES), F32),
                        pltpu.VMEM((2, N_PAIRS, 1, 2 * LANES), F32),
                        pltpu.VMEM((2, N_PAIRS, gk, 2 * LANES), BF16),
                        pltpu.VMEM((N_PAIRS, 2 * QBLK, LANES), BF16),
                        pltpu.VMEM((ATT_W, LANES), F32),
                        pltpu.SMEM((1,), I32)],
        compiler_params=_params(("parallel", "arbitrary")),
        name="attn",
    )(aq, ak, avt, iq, ik, iwt, bias, out_g.reshape(ATT_W, 1))


def _outproj_kernel(rec_ref, att_ref, x_ref, g1_ref, sc_ref, sh_ref, ng_ref, wo_ref, wr_ref, br_ref,
                    x1_ref, h2_ref, comb_ref):
    mix = _dot(rec_ref[...], wo_ref[0:REC_W, :]) + _dot(att_ref[...], wo_ref[REC_W:REC_W + ATT_W, :])
    x1 = x_ref[...] + g1_ref[...] * mix
    x1_ref[...] = x1
    ms = jnp.mean(x1 * x1, axis=-1, keepdims=True)
    h2 = (x1 * lax.rsqrt(ms + EPS) * ng_ref[...] * (1.0 + sc_ref[...]) + sh_ref[...]).astype(BF16)
    h2_ref[...] = h2

    lg = _dot(h2, wr_ref[...]) + br_ref[...]
    lane = lax.broadcasted_iota(I32, lg.shape, 1)
    big = jnp.int32(2 * LANES)
    is_g = lane < N_GROUPS
    gl = jnp.where(is_g, lg, -jnp.inf)
    gmax = jnp.max(gl, axis=-1, keepdims=True)
    gate = 1.0 / jnp.sum(jnp.where(is_g, jnp.exp(lg - gmax), 0.0), axis=-1, keepdims=True)
    gtop = jnp.min(jnp.where(gl == gmax, lane, big), axis=-1, keepdims=True)
    e_lo = ROUTE_OFF + EXPERTS_PER_GROUP * gtop
    el = jnp.where((lane >= e_lo) & (lane < e_lo + EXPERTS_PER_GROUP), lg, -jnp.inf)
    v1 = jnp.max(el, axis=-1, keepdims=True)
    i1 = jnp.min(jnp.where(el == v1, lane, big), axis=-1, keepdims=True)
    el2 = jnp.where(lane == i1, -jnp.inf, el)
    v2 = jnp.max(el2, axis=-1, keepdims=True)
    i2 = jnp.min(jnp.where(el2 == v2, lane, big), axis=-1, keepdims=True)
    e2 = jnp.exp(v2 - v1)
    w1 = gate / (1.0 + e2)
    w2 = gate * e2 / (1.0 + e2)
    comb = jnp.where(lane == i1, w1, 0.0) + jnp.where(lane == i2, w2, 0.0)
    comb_ref[...] = jnp.where(lane == 0, gtop.astype(F32), comb)


def _outproj(rec, att, x, g1, sc2, sh2, norm_g, w_out, w_rg, b_rg, w_re, b_re, *, tm):
    bsz, s, d = x.shape
    wr = jnp.zeros((d, LANES), F32).at[:, :N_GROUPS].set(w_rg).at[:, ROUTE_OFF:ROUTE_OFF + N_EXPERTS].set(w_re)
    br = jnp.zeros((1, LANES), F32).at[0, :N_GROUPS].set(b_rg).at[0, ROUTE_OFF:ROUTE_OFF + N_EXPERTS].set(b_re)
    row = lambda w: pl.BlockSpec((None, tm, w), lambda b, i: (b, i, 0))
    vec = pl.BlockSpec((None, 1, d), lambda b, i: (b, 0, 0))
    return pl.pallas_call(
        _outproj_kernel,
        grid=(bsz, s // tm),
        in_specs=[row(REC_W), row(ATT_W), row(d), vec, vec, vec, _const_spec((1, d)),
                  _const_spec((REC_W + ATT_W, d)), _const_spec((d, LANES)), _const_spec((1, LANES))],
        out_specs=(row(d), row(d), row(LANES)),
        out_shape=(jax.ShapeDtypeStruct((bsz, s, d), F32),
                   jax.ShapeDtypeStruct((bsz, s, d), BF16),
                   jax.ShapeDtypeStruct((bsz, s, LANES), F32)),
        compiler_params=_params(("parallel", "parallel")),
        name="outproj",
    )(rec, att, x, g1.reshape(bsz, 1, d), sc2.reshape(bsz, 1, d), sh2.reshape(bsz, 1, d),
      norm_g.reshape(1, d), w_out.astype(BF16), wr.astype(BF16), br)


MOE_ALIGN = 16
MOE_LANE_SHIFT = 32
MOE_CHUNK = 288
MOE_RB = 256


def _moe_kernel(h_ref, comb_ref, x1_ref, g2_ref, w1_ref, w3_ref, w2_ref, o_ref,
                p_ref, pt_ref, hs_ref, cs_ref, ys_ref, seg_ref, *, mc):
    e = pl.program_id(2)
    tm = h_ref.shape[0]
    npad = hs_ref.shape[0]

    @pl.when(e == 0)
    def _():
        comb = comb_ref[...]
        lane = lax.broadcasted_iota(I32, (tm, LANES), 1)
        gid = comb[:, 0:1].astype(I32)
        oh = jnp.where(lane == gid, 1.0, 0.0)
        ohb = oh.astype(BF16)
        pre = []
        for rb in range(tm // MOE_RB):
            r = rb * MOE_RB + lax.broadcasted_iota(I32, (MOE_RB, tm), 0)
            c = lax.broadcasted_iota(I32, (MOE_RB, tm), 1)
            pre.append(_dot(jnp.where(c < r, 1.0, 0.0).astype(BF16), ohb))
        prefix = jnp.concatenate(pre, axis=0)
        cnt = jnp.sum(oh, axis=0, keepdims=True).astype(I32)
        cnt_al = ((cnt + (MOE_ALIGN - 1)) // MOE_ALIGN) * MOE_ALIGN
        base = jnp.sum(jnp.where(lane < gid, cnt_al.astype(F32), 0.0), axis=-1, keepdims=True)
        rank = jnp.sum(prefix * oh, axis=-1, keepdims=True)
        pos = (base + rank).astype(I32)
        pos_row = jnp.broadcast_to(pos.astype(F32), (tm, LANES)).T[0:1, :].astype(I32)
        for rb in range(tm // MOE_RB):
            sl = slice(rb * MOE_RB, (rb + 1) * MOE_RB)
            coln = lax.broadcasted_iota(I32, (MOE_RB, npad), 1)
            pt_ref[sl, :] = jnp.where(coln == pos[sl], 1.0, 0.0).astype(BF16)
        for rb in range(npad // LANES):
            sl = slice(rb * LANES, (rb + 1) * LANES)
            rown = rb * LANES + lax.broadcasted_iota(I32, (LANES, tm), 0)
            p_ref[sl, :] = jnp.where(rown == pos_row, 1.0, 0.0).astype(BF16)
        c_hi = comb.astype(BF16)
        c_mid, c_lo = _split_bf16(comb - c_hi.astype(F32))
        packed = (c_hi.astype(F32) + pltpu.roll(c_mid.astype(F32), MOE_LANE_SHIFT, axis=1)
                  + pltpu.roll(c_lo.astype(F32), 2 * MOE_LANE_SHIFT, axis=1)).astype(BF16)
        srt = _dot(p_ref[...], jnp.concatenate([h_ref[...], packed], axis=1))
        d = h_ref.shape[1]
        hs_ref[...] = srt[:, :d].astype(BF16)
        cp = srt[:, d:]
        cs_ref[...] = (cp + pltpu.roll(cp, LANES - MOE_LANE_SHIFT, axis=1)
                       + pltpu.roll(cp, LANES - 2 * MOE_LANE_SHIFT, axis=1))
        ys_ref[...] = jnp.zeros_like(ys_ref)
        start = jnp.int32(0)
        for g in range(N_GROUPS):
            seg_ref[g] = start
            seg_ref[N_GROUPS + g] = cnt[0, g]
            start = start + cnt_al[0, g]

    grp = e // EXPERTS_PER_GROUP
    start = seg_ref[grp]
    n_rows = seg_ref[N_GROUPS + grp]

    def chunk(ci, carry):
        rs = pl.ds(pl.multiple_of(start + ci * mc, MOE_ALIGN), mc)
        hb = hs_ref[rs, :]
        cw = cs_ref[rs, :]
        lane = lax.broadcasted_iota(I32, cw.shape, 1)
        col = jnp.sum(jnp.where(lane == e + ROUTE_OFF, cw, 0.0), axis=-1, keepdims=True)
        he = _silu(_dot(hb, w1_ref[...])) * _dot(hb, w3_ref[...]) * col
        ys_ref[rs, :] += _dot(he.astype(BF16), w2_ref[...])
        return carry

    lax.fori_loop(0, (n_rows + mc - 1) // mc, chunk, 0)

    @pl.when(e == N_EXPERTS - 1)
    def _():
        o_ref[...] = x1_ref[...] + g2_ref[...] * _dot(pt_ref[...], ys_ref[...].astype(BF16))


def _moe(h2, comb, x1, g2, w1, w3, w2, *, tm):
    bsz, s, d = x1.shape
    mc = MOE_CHUNK
    npad = -(-(tm + N_GROUPS * MOE_ALIGN + mc) // LANES) * LANES
    row = lambda w, **kw: pl.BlockSpec((None, tm, w), lambda b, i, e: (b, i, 0), **kw)
    once = dict(pipeline_mode=pl.Buffered(1))
    return pl.pallas_call(
        functools.partial(_moe_kernel, mc=mc),
        grid=(bsz, s // tm, N_EXPERTS),
        in_specs=[row(d, **once), row(LANES), row(d, **once),
                  pl.BlockSpec((None, 1, d), lambda b, i, e: (b, 0, 0)),
                  pl.BlockSpec((None, d, D_EXPERT), lambda b, i, e: (e, 0, 0)),
                  pl.BlockSpec((None, d, D_EXPERT), lambda b, i, e: (e, 0, 0)),
                  pl.BlockSpec((None, D_EXPERT, d), lambda b, i, e: (e, 0, 0))],
        out_specs=row(d),
        out_shape=jax.ShapeDtypeStruct((bsz, s, d), F32),
        scratch_shapes=[pltpu.VMEM((npad, tm), BF16),
                        pltpu.VMEM((tm, npad), BF16),
                        pltpu.VMEM((npad, d), BF16),
                        pltpu.VMEM((npad, LANES), F32),
                        pltpu.VMEM((npad, d), F32),
                        pltpu.SMEM((2 * N_GROUPS,), I32)],
        compiler_params=_params(("parallel", "parallel", "arbitrary")),
        name="moe",
    )(h2, comb, x1, g2.reshape(bsz, 1, d), w1.astype(BF16), w3.astype(BF16), w2.astype(BF16))


def _tile(s, pref):
    t = min(s, pref)
    assert s % t == 0
    return t


def kernel(x, c, w_ada, b_ada, norm1_g, norm2_g, w_in, lb_logits, rec_out_g, q_norm_g, k_norm_g,
           idx_k_norm_g, idx_k_norm_b, attn_out_g, rel_bias, w_out, w_rg, b_rg, w_re, b_re, w1, w3, w2):
    bsz, s, d = x.shape
    depth = w_ada.shape[0]
    gk = _tile(s, 512)
    for l in range(depth):
        mod = _adaln(c, w_ada[l], b_ada[l])
        sh1, sc1, g1, sh2, sc2, g2 = jnp.split(mod, 6, axis=-1)
        q, f, v, g, aq, ak, avt, iq, ik, iwt = _inproj(
            x, sc1, sh1, norm1_g[l], w_in[l], lb_logits, q_norm_g[l], k_norm_g[l],
            idx_k_norm_g[l], idx_k_norm_b[l], layer=l, gt=gk, tm=_tile(s, 512))
        rec = _hgrn(q, f, v, g, rec_out_g[l], ts=_tile(s, 256))
        att = _attention(aq, ak, avt, iq, ik, iwt, rel_bias, attn_out_g[l], gk=gk)
        x1, h2, comb = _outproj(rec, att, x, g1, sc2, sh2, norm2_g[l], w_out[l],
                                w_rg[l], b_rg[l], w_re[l], b_re[l], tm=_tile(s, 512))
        x = _moe(h2, comb, x1, g2, w1[l], w3[l], w2[l], tm=_tile(s, 1024))
    return x
```

```python
import functools

import numpy as np
import jax
import jax.numpy as jnp
from jax import lax
from jax.experimental import pallas as pl
from jax.experimental.pallas import tpu as pltpu

F32 = jnp.float32
BF16 = jnp.bfloat16
I32 = jnp.int32

CHUNK = 64
QBLK = 128
EPS = 1e-6
REC_HEADS = 4
REC_DK = 128
REC_DV = 128
REC_W = REC_HEADS * REC_DV
ATT_HEADS = 8
ATT_DH = 64
ATT_W = ATT_HEADS * ATT_DH
IDX_HEADS = 8
IDX_DIM = 64
TOPK_MAX = 256
NUM_BUCKETS = 32
MAX_DISTANCE = 128
N_GROUPS = 4
EXPERTS_PER_GROUP = 4
N_EXPERTS = N_GROUPS * EXPERTS_PER_GROUP
D_EXPERT = 512

LANES = 128
V7X_VMEM_LIMIT = 56 * 1024 * 1024
F32_LOWEST = float(np.finfo(np.float32).min)
BISECT_FIRST = 17
BISECT_CAP = 320
NEG = -1e30
LOG2E = float(np.log2(np.e))
N_PAIRS = ATT_HEADS // 2
ROUTE_OFF = N_GROUPS
ROUTE_ROWS = 32


def _dot(a, b):
    return jnp.dot(a, b, preferred_element_type=F32)


def _dot_nt(a, b):
    return lax.dot_general(a, b, (((1,), (1,)), ((), ())), preferred_element_type=F32)


def _dot_tn(a, b):
    return lax.dot_general(a, b, (((0,), (0,)), ((), ())), preferred_element_type=F32)


def _split_bf16(a):
    hi = a.astype(BF16)
    lo = (a - hi.astype(F32)).astype(BF16)
    return hi, lo


def _silu(a):
    return a * jax.nn.sigmoid(a)


def _const_spec(shape):
    nd = len(shape)
    return pl.BlockSpec(shape, lambda *_: (0,) * nd, pipeline_mode=pl.Buffered(1))


def _params(sem):
    return pltpu.CompilerParams(dimension_semantics=sem, vmem_limit_bytes=V7X_VMEM_LIMIT)


def _adaln_kernel(c_ref, w_ref, b_ref, o_ref):
    a_hi, a_lo = _split_bf16(_silu(c_ref[...]))
    w_hi, w_lo = _split_bf16(w_ref[...])
    o_ref[...] = _dot(a_hi, w_hi) + _dot(a_lo, w_hi) + _dot(a_hi, w_lo) + b_ref[...]


def _adaln(c, w, b):
    bsz, d = c.shape
    n = w.shape[1]
    rows = 16
    bn = 1024
    cp = jnp.zeros((rows, d), F32).at[:bsz].set(c)
    out = pl.pallas_call(
        _adaln_kernel,
        grid=(n // bn,),
        in_specs=[pl.BlockSpec((rows, d), lambda i: (0, 0)),
                  pl.BlockSpec((d, bn), lambda i: (0, i)),
                  pl.BlockSpec((1, bn), lambda i: (0, i))],
        out_specs=pl.BlockSpec((rows, bn), lambda i: (0, i)),
        out_shape=jax.ShapeDtypeStruct((rows, n), F32),
        compiler_params=_params(("parallel",)),
        name="adaln",
    )(cp, w, b.reshape(1, n))
    return out[:bsz]


def _inproj_kernel(x_ref, sc_ref, sh_ref, ng_ref, wrec_ref, watt_ref, widx_ref, lbl_ref,
                   qg_ref, kg_ref, ikg_ref, ikb_ref, pm_ref,
                   q_ref, f_ref, v_ref, g_ref, aq_ref, ak_ref, avt_ref, iq_ref, ik_ref, iwt_ref,
                   *, layer, gt):
    x = x_ref[...]
    tm = x.shape[0]
    ms = jnp.mean(x * x, axis=-1, keepdims=True)
    h = x * lax.rsqrt(ms + EPS) * ng_ref[...] * (1.0 + sc_ref[...]) + sh_ref[...]
    hb = h.astype(BF16)

    zr = _dot(hb, wrec_ref[...])
    q_ref[...] = _silu(zr[:, 0:REC_W]).astype(BF16)
    lbl = lbl_ref[...]
    e = jnp.exp(lbl - jnp.max(lbl, axis=0, keepdims=True))
    sm = e / jnp.sum(e, axis=0, keepdims=True)
    lb = jnp.sum(sm[0:layer + 1], axis=0, keepdims=True)
    f_ref[...] = lb + (1.0 - lb) * jax.nn.sigmoid(zr[:, REC_W:2 * REC_W])
    v_ref[...] = zr[:, 2 * REC_W:3 * REC_W].astype(BF16)
    g_ref[...] = _silu(zr[:, 3 * REC_W:4 * REC_W]).astype(BF16)

    za = _dot(hb, watt_ref[...])
    aq = za[:, 0:ATT_W]
    ak = za[:, ATT_W:2 * ATT_W]
    av = za[:, 2 * ATT_W:3 * ATT_W]
    pm = pm_ref[...]
    aqn = aq * lax.rsqrt(_dot((aq * aq).astype(BF16), pm) + EPS) * qg_ref[...]
    akn = ak * lax.rsqrt(_dot((ak * ak).astype(BF16), pm) + EPS) * kg_ref[...]
    for j in range(N_PAIRS):
        aq_ref[j] = aqn[:, j * LANES:(j + 1) * LANES].astype(BF16)
        ak_ref[j] = akn[:, j * LANES:(j + 1) * LANES].astype(BF16)
    for t in range(tm // gt):
        avt_ref[t] = av[t * gt:(t + 1) * gt, :].T.astype(BF16)

    zi = _dot(hb, widx_ref[...])
    for j in range(N_PAIRS):
        iq_ref[j] = zi[:, j * LANES:(j + 1) * LANES].astype(BF16)
    tail = zi[:, IDX_HEADS * IDX_DIM:IDX_HEADS * IDX_DIM + LANES]
    lane = lax.broadcasted_iota(I32, tail.shape, 1)
    is_k = lane < IDX_DIM
    mu = jnp.sum(jnp.where(is_k, tail, 0.0), axis=-1, keepdims=True) * (1.0 / IDX_DIM)
    dlt = jnp.where(is_k, tail - mu, 0.0)
    var = jnp.sum(dlt * dlt, axis=-1, keepdims=True) * (1.0 / IDX_DIM)
    ikn = dlt * lax.rsqrt(var + EPS) * ikg_ref[...] + ikb_ref[...]
    ik_ref[...] = jnp.where(is_k, ikn, pltpu.roll(ikn, IDX_DIM, axis=1)).astype(BF16)
    iwt_ref[...] = tail.T[IDX_DIM:IDX_DIM + IDX_HEADS, :] * (IDX_HEADS ** -0.5 * IDX_DIM ** -0.5)


def _inproj(x, sc1, sh1, norm_g, w_in, lb_logits, q_g, k_g, ik_g, ik_b, *, layer, gt, tm):
    bsz, s, d = x.shape
    n_rec = 4 * REC_W
    n_att = 3 * ATT_W
    n_idx = IDX_HEADS * IDX_DIM + LANES
    wb = w_in.astype(BF16)
    w_rec = wb[:, :n_rec]
    w_att = wb[:, n_rec:n_rec + n_att]
    w_idx = jnp.zeros((d, n_idx), BF16).at[:, :w_in.shape[1] - n_rec - n_att].set(wb[:, n_rec + n_att:])
    pm = jnp.asarray(np.kron(np.eye(ATT_HEADS), np.full((ATT_DH, ATT_DH), 1.0 / ATT_DH)), BF16)
    qg = jnp.tile(q_g, ATT_HEADS).reshape(1, ATT_W) * (ATT_DH ** -0.5 * LOG2E)
    kg = jnp.tile(k_g, ATT_HEADS).reshape(1, ATT_W)
    ikg = jnp.zeros((1, LANES), F32).at[0, :IDX_DIM].set(ik_g)
    ikb = jnp.zeros((1, LANES), F32).at[0, :IDX_DIM].set(ik_b)
    nl = lb_logits.shape[0]

    row = lambda w: pl.BlockSpec((None, tm, w), lambda b, i: (b, i, 0))
    pair = pl.BlockSpec((None, N_PAIRS, tm, LANES), lambda b, i: (b, 0, i, 0))
    vec = pl.BlockSpec((None, 1, d), lambda b, i: (b, 0, 0))
    out_shapes = (
        jax.ShapeDtypeStruct((bsz, s, REC_W), BF16),
        jax.ShapeDtypeStruct((bsz, s, REC_W), F32),
        jax.ShapeDtypeStruct((bsz, s, REC_W), BF16),
        jax.ShapeDtypeStruct((bsz, s, REC_W), BF16),
        jax.ShapeDtypeStruct((bsz, N_PAIRS, s, LANES), BF16),
        jax.ShapeDtypeStruct((bsz, N_PAIRS, s, LANES), BF16),
        jax.ShapeDtypeStruct((bsz, s // gt, ATT_W, gt), BF16),
        jax.ShapeDtypeStruct((bsz, N_PAIRS, s, LANES), BF16),
        jax.ShapeDtypeStruct((bsz, s, LANES), BF16),
        jax.ShapeDtypeStruct((bsz, IDX_HEADS, s), F32),
    )
    out_specs = (
        row(REC_W), row(REC_W), row(REC_W), row(REC_W), pair, pair,
        pl.BlockSpec((None, tm // gt, ATT_W, gt), lambda b, i: (b, i, 0, 0)),
        pair, row(LANES),
        pl.BlockSpec((None, IDX_HEADS, tm), lambda b, i: (b, 0, i)),
    )
    return pl.pallas_call(
        functools.partial(_inproj_kernel, layer=layer, gt=gt),
        grid=(bsz, s // tm),
        in_specs=[row(d), vec, vec, _const_spec((1, d)),
                  _const_spec((d, n_rec)), _const_spec((d, n_att)), _const_spec((d, n_idx)),
                  _const_spec((nl, REC_W)), _const_spec((1, ATT_W)), _const_spec((1, ATT_W)),
                  _const_spec((1, LANES)), _const_spec((1, LANES)), _const_spec((ATT_W, ATT_W))],
        out_specs=out_specs,
        out_shape=out_shapes,
        compiler_params=_params(("parallel", "parallel")),
        name="inproj",
    )(x, sc1.reshape(bsz, 1, d), sh1.reshape(bsz, 1, d), norm_g.reshape(1, d),
      w_rec, w_att, w_idx, lb_logits, qg, kg, ikg, ikb, pm)


N_LEVELS = 6


def _hgrn_tables():
    c = CHUNK
    w = np.zeros((N_LEVELS + 2, c, c), np.float32)
    am = np.zeros((N_LEVELS + 1, c, c), np.float32)
    t = np.arange(c)
    for m in range(N_LEVELS):
        hs = 1 << m
        blk = t // (2 * hs)
        upper = (t // hs) % 2 == 1
        ref = blk * 2 * hs + hs - 1
        for i in range(c):
            if upper[i]:
                w[m, i, ref[i] + 1:i + 1] = 1.0
            else:
                w[m, i, i + 1:ref[i] + 1] = 1.0
        am[m] = (blk[:, None] == blk[None, :]) & upper[:, None] & ~upper[None, :]
    w[N_LEVELS] = np.tril(np.ones((c, c)))
    w[N_LEVELS + 1] = np.triu(np.ones((c, c)), 1)
    am[N_LEVELS] = np.eye(c)
    w = w.reshape((N_LEVELS + 2) * c, c)
    return np.concatenate([w, w], axis=1), am


HGRN_UNROLL = 4


def _hgrn_kernel(q_ref, f_ref, v_ref, g_ref, og_ref, ww_ref, am_ref, o_ref, st_ref, ex_ref, *, n_chunks):
    @pl.when(pl.program_id(1) == 0)
    def _():
        st_ref[...] = jnp.zeros_like(st_ref)

    ww = ww_ref[...]
    c = CHUNK
    tbit = lax.broadcasted_iota(I32, (c, REC_DK), 0)
    items = [(cc, hd) for cc in range(HGRN_UNROLL) for hd in range(REC_HEADS)]

    def chunks(ci, carry):
        def blk(ref, it):
            r0 = pl.multiple_of((ci * HGRN_UNROLL + it[0]) * c, c)
            return ref.at[pl.ds(r0, c), it[1] * REC_DK:(it[1] + 1) * REC_DK]

        def ex(i, part, last_row=False):
            r0 = (part + 1) * c - 1 if last_row else part * c
            return ex_ref[items[i][0], r0:(part + 1) * c, items[i][1] * REC_DK:(items[i][1] + 1) * REC_DK]

        f = [blk(f_ref, it)[...] for it in items]
        q = [blk(q_ref, it)[...].astype(F32) for it in items]
        v = [blk(v_ref, it)[...] for it in items]
        k = [1.0 - fi for fi in f]
        for cc in range(HGRN_UNROLL):
            cols = [jnp.concatenate(_split_bf16(jnp.log(f[cc * REC_HEADS + hd])), axis=0) for hd in range(REC_HEADS)]
            ex_ref[cc] = jnp.exp(_dot(ww, jnp.concatenate(cols, axis=1)))
        a = [am_ref[N_LEVELS] * _dot_nt(q[i].astype(BF16), k[i].astype(BF16)) for i in range(len(items))]
        for m in range(N_LEVELS):
            upper = ((tbit >> m) & 1) == 1
            for i in range(len(items)):
                tm_ = (ex(i, m) * jnp.where(upper, q[i], k[i])).astype(BF16)
                a[i] = a[i] + am_ref[m] * _dot_nt(tm_, tm_)
        qb = [(q[i] * ex(i, N_LEVELS)).astype(BF16) for i in range(len(items))]
        kb = [(k[i] * ex(i, N_LEVELS + 1)).astype(BF16) for i in range(len(items))]
        intra = [_dot(a[i].astype(BF16), v[i]) for i in range(len(items))]
        ut = [_dot_tn(v[i], kb[i]) for i in range(len(items))]
        for hd in range(REC_HEADS):
            st = st_ref[hd]
            for cc in range(HGRN_UNROLL):
                i = cc * REC_HEADS + hd
                o = intra[i] + _dot_nt(qb[i], st.astype(BF16))
                st = st * ex(i, N_LEVELS, last_row=True) + ut[i]
                ms = jnp.mean(o * o, axis=-1, keepdims=True)
                gate = blk(g_ref, items[i])[...].astype(F32)
                y = o * lax.rsqrt(ms + EPS) * og_ref[:, hd * REC_DV:(hd + 1) * REC_DV] * gate
                blk(o_ref, items[i])[...] = y.astype(BF16)
            st_ref[hd] = st
        return carry

    lax.fori_loop(0, n_chunks // HGRN_UNROLL, chunks, 0)


def _hgrn(q, f, v, g, out_g, *, ts):
    bsz, s, _ = q.shape
    ww_np, am_np = _hgrn_tables()
    ww = jnp.asarray(ww_np, BF16)
    am = jnp.asarray(am_np, F32)
    row = pl.BlockSpec((None, ts, REC_W), lambda b, i: (b, i, 0))
    return pl.pallas_call(
        functools.partial(_hgrn_kernel, n_chunks=ts // CHUNK),
        grid=(bsz, s // ts),
        in_specs=[row, row, row, row, _const_spec((1, REC_W)),
                  _const_spec(ww.shape), _const_spec(am.shape)],
        out_specs=row,
        out_shape=jax.ShapeDtypeStruct((bsz, s, REC_W), BF16),
        scratch_shapes=[pltpu.VMEM((REC_HEADS, REC_DV, REC_DK), F32),
                        pltpu.VMEM((HGRN_UNROLL, (N_LEVELS + 2) * CHUNK, REC_HEADS * REC_DK), F32)],
        compiler_params=_params(("parallel", "arbitrary")),
        name="hgrn",
    )(q, f, v, g, out_g.reshape(1, REC_W), ww, am)


def _t5_bucket(rel):
    nb = NUM_BUCKETS // 2
    max_exact = nb // 2
    ret = jnp.where(rel > 0, nb, 0)
    n = jnp.abs(rel)
    nf = jnp.maximum(n, 1).astype(F32)
    large = max_exact + (jnp.log(nf / max_exact) / np.log(MAX_DISTANCE / max_exact)
                         * (nb - max_exact)).astype(I32)
    large = jnp.minimum(large, nb - 1)
    return ret + jnp.where(n < max_exact, n, large)


def _bias_slots(rel_bias):
    n_rel = 3 * QBLK
    rel = jnp.arange(n_rel, dtype=I32) - 2 * QBLK
    tab = rel_bias[_t5_bucket(rel)].T
    far = rel_bias[_t5_bucket(jnp.full((1,), -2 * QBLK - 1, I32))].T
    rev = tab[:, ::-1]
    skew = jnp.tile(rev, (1, 2 * QBLK))[:, :2 * QBLK * (n_rel - 1)].reshape(ATT_HEADS, 2 * QBLK, n_rel - 1)
    near = skew[:, :, 2 * QBLK - 1:]
    nb = ((near - far[:, :, None]) * LOG2E).reshape(ATT_HEADS, 2, QBLK, QBLK)
    z = jnp.zeros((ATT_HEADS, 1, QBLK, QBLK), F32)
    return jnp.concatenate([z, nb, z], axis=1)


def _attn_kernel(aq_ref, ak_ref, avt_ref, iq_ref, ik_ref, iwt_ref, bias_ref, og_ref, o_ref,
                 key_ref, mb_ref, thr_ref, acc_ref, m_ref, l_ref, al_ref, s_ref, mx_ref, p_ref, qbd_ref, ot_ref, tie_ref,
                 *, gs, gk, k_sel):
    bi = pl.program_id(1)
    n_s = bi // (gs // QBLK) + 1
    lane = lax.broadcasted_iota(I32, (QBLK, LANES), 1)

    def block_diag(xq):
        zero = jnp.zeros_like(xq)
        return jnp.concatenate([jnp.where(lane < ATT_DH, xq, zero), jnp.where(lane >= ATT_DH, xq, zero)], axis=0)

    def srows(g):
        return pl.ds(pl.multiple_of(g * gs, gs), gs)

    for j in range(N_PAIRS):
        qbd_ref[j] = block_diag(iq_ref[j])
    w = iwt_ref[...]

    def group_scores(g):
        sc = _dot_nt(ik_ref[srows(g), :], qbd_ref[...].reshape(N_PAIRS * 2 * QBLK, LANES))
        acc = jnp.zeros((gs, LANES), F32)
        for hd in range(IDX_HEADS):
            acc = acc + jnp.maximum(sc[:, hd * LANES:(hd + 1) * LANES], 0.0) * w[hd:hd + 1, :]
        return acc

    def stats(sc, c):
        s3 = sc.reshape(gs // 8, 8, LANES)
        return (jnp.minimum(c[0], jnp.min(jnp.where(s3 == -jnp.inf, jnp.inf, s3), axis=0)),
                jnp.maximum(c[1], jnp.max(s3, axis=0)),
                c[2] + jnp.sum(jnp.where(s3 >= 0.0, 1, 0), axis=0),
                c[3] + jnp.sum(jnp.where(s3 > 0.0, 1, 0), axis=0))

    def idx_body(g, carry):
        sc = group_scores(g)
        key_ref[srows(g), :] = sc
        return stats(sc, carry)

    def idx_pair(u, carry):
        return idx_body(2 * u + 1, idx_body(2 * u, carry))

    z8 = jnp.zeros((8, LANES), I32)
    st8 = lax.fori_loop(0, (n_s - 1) // 2, idx_pair,
                        (jnp.full((8, LANES), jnp.inf, F32), jnp.full((8, LANES), -jnp.inf, F32), z8, z8))
    st8 = lax.cond((n_s - 1) % 2 == 1, lambda c: idx_body(n_s - 2, c), lambda c: c, st8)

    last = srows(n_s - 1)
    spos = (n_s - 1) * gs + lax.broadcasted_iota(I32, (gs, LANES), 0)
    tpos = bi * QBLK + lax.broadcasted_iota(I32, (gs, LANES), 1)
    sc_last = jnp.where((spos // CHUNK) <= (tpos // CHUNK), group_scores(n_s - 1), -jnp.inf)
    key_ref[last, :] = sc_last
    mn8, mx8, ge8, gt8 = stats(sc_last, st8)

    thr_ref[...] = jnp.full(thr_ref.shape, F32_LOWEST, F32)
    tie_ref[0] = 0

    @pl.when((2 * bi + 2) * CHUNK > k_sel)
    def _():
        def count_ge(cand):
            def one(g):
                sc = key_ref[srows(g), :].reshape(gs // 8, 8, LANES)
                return jnp.sum(jnp.where(sc >= cand, 1, 0), axis=0)

            def two(u, acc):
                second = jnp.where(2 * u + 1 < n_s, 1, 0)
                return acc + one(2 * u) + one(jnp.minimum(2 * u + 1, n_s - 1)) * second

            acc = lax.fori_loop(0, (n_s + 1) // 2, two, jnp.zeros((8, LANES), I32))
            return jnp.sum(acc, axis=0, keepdims=True)

        c0_ge = jnp.sum(ge8, axis=0, keepdims=True)
        c0_gt = jnp.sum(gt8, axis=0, keepdims=True)
        mn = jnp.min(mn8, axis=0, keepdims=True)
        mx = jnp.max(mx8, axis=0, keepdims=True)
        above = mx + (jnp.abs(mx) * 2.0 ** -20 + 1e-30)
        n_adm = (2 * bi + 1 + lax.broadcasted_iota(I32, (1, LANES), 1) // CHUNK) * CHUNK
        pos = c0_gt >= k_sel
        non_neg = c0_ge >= k_sel
        lo = jnp.where(non_neg, 0.0, mn)
        cnt_lo = jnp.where(non_neg, c0_ge, n_adm)
        hi = jnp.where(pos, above, 0.0)

        def midpoint(lo, hi):
            return 0.5 * lo + 0.5 * hi

        def unsettled(lo, hi, cnt_lo):
            mid = midpoint(lo, hi)
            return jnp.where(cnt_lo > k_sel, jnp.where(mid > lo, jnp.where(mid < hi, 1.0, 0.0), 0.0), 0.0)

        def cond(st):
            return jnp.logical_and(st[3] > 0.5, st[4] < BISECT_CAP)

        def halve(lo, hi, cnt_lo):
            act = unsettled(lo, hi, cnt_lo) > 0.5
            mid = midpoint(lo, hi)
            c = count_ge(mid)
            up = jnp.logical_and(act, c >= k_sel)
            dn = jnp.logical_and(act, c < k_sel)
            return jnp.where(up, mid, lo), jnp.where(dn, mid, hi), jnp.where(up, c, cnt_lo)

        def body(st):
            lo, hi, cnt_lo = halve(*halve(*st[:3]))
            return lo, hi, cnt_lo, jnp.max(unsettled(lo, hi, cnt_lo)), st[4] + 2

        lo, hi, cnt_lo = lax.fori_loop(0, BISECT_FIRST, lambda _, s3: halve(*s3), (lo, hi, cnt_lo))
        st = lax.while_loop(cond, body,
                            (lo, hi, cnt_lo, jnp.max(unsettled(lo, hi, cnt_lo)), jnp.int32(BISECT_FIRST)))
        thr_ref[...] = jnp.broadcast_to(st[0], thr_ref.shape)
        tie_ref[0] = (jnp.max(jnp.where(st[2] > k_sel, 1.0, 0.0)) > 0.5).astype(I32)

    thr = thr_ref[0:1, :]

    @pl.when(tie_ref[0] == 0)
    def _():
        def mb_body(g, carry):
            mb_ref[srows(g), :] = jnp.where(key_ref[srows(g), :] >= thr, 0.0, NEG)
            return carry

        lax.fori_loop(0, n_s, mb_body, 0)

    @pl.when(tie_ref[0] != 0)
    def _():
        def count_gt(g, acc):
            sc = key_ref[srows(g), :].reshape(gs // 8, 8, LANES)
            return acc + jnp.sum(jnp.where(sc > thr, 1, 0), axis=0)

        n_gt = jnp.sum(lax.fori_loop(0, n_s, count_gt, jnp.zeros((8, LANES), I32)), axis=0, keepdims=True)
        room = (k_sel - n_gt).astype(F32)
        r = lax.broadcasted_iota(I32, (QBLK, QBLK), 0)
        c = lax.broadcasted_iota(I32, (QBLK, QBLK), 1)
        tri = jnp.where(c <= r, 1.0, 0.0).astype(BF16)

        def mb_body(g, seen):
            for t in range(gs // QBLK):
                rs = pl.ds(pl.multiple_of(g * gs + t * QBLK, QBLK), QBLK)
                sc = key_ref[rs, :]
                eq = jnp.where(sc == thr, 1.0, 0.0)
                rank = _dot(tri, eq.astype(BF16)) + seen
                keep = jnp.where(sc > thr, 1.0, jnp.where(rank <= room, eq, 0.0))
                mb_ref[rs, :] = jnp.where(keep > 0.5, 0.0, NEG)
                seen = seen + jnp.sum(eq, axis=0, keepdims=True)
            return seen

        lax.fori_loop(0, n_s, mb_body, jnp.zeros((1, LANES), F32))

    spg = gk // QBLK
    n_g = bi // spg + 1
    for j in range(N_PAIRS):
        qbd_ref[j] = block_diag(aq_ref[j])
    m_ref[...] = jnp.full(m_ref.shape, NEG, F32)
    l_ref[...] = jnp.zeros_like(l_ref)
    acc_ref[...] = jnp.zeros_like(acc_ref)
    al_ref[...] = jnp.ones_like(al_ref)
    p_ref[1] = jnp.zeros(p_ref.shape[1:], BF16)

    def rows(g):
        return pl.ds(pl.multiple_of(g * gk, gk), gk)

    def scores(t, slot):
        mb = mb_ref[rows(t), :]
        mb2 = jnp.concatenate([mb, mb], axis=1)
        for j in range(N_PAIRS):
            s = _dot_nt(ak_ref[j, rows(t), :], qbd_ref[j]) + mb2
            s_ref[slot, j] = s
            mx_ref[slot, j] = jnp.max(s, axis=0, keepdims=True)

    def pv(t, slot):
        return [_dot(avt_ref[t, j * LANES:(j + 1) * LANES, :], p_ref[slot, j]) for j in range(N_PAIRS)]

    def step(g, cur):
        prv = 1 - cur

        @pl.when(g >= n_g - 2)
        def _():
            for j in range(N_PAIRS):
                parts = []
                for st in range(spg):
                    slot = jnp.clip(g * spg + st - bi + 2, 0, 3)
                    parts.append(jnp.concatenate([bias_ref[2 * j, slot], bias_ref[2 * j + 1, slot]], axis=1))
                s = s_ref[cur, j] + jnp.concatenate(parts, axis=0)
                s_ref[cur, j] = s
                mx_ref[cur, j] = jnp.max(s, axis=0, keepdims=True)

        al_old = [al_ref[j] for j in range(N_PAIRS)]
        for j in range(N_PAIRS):
            m_old = m_ref[j]
            m_new = jnp.maximum(m_old, mx_ref[cur, j])
            p = jnp.exp2(s_ref[cur, j] - m_new)
            alpha = jnp.exp2(m_old - m_new)
            al_ref[j] = alpha
            l_ref[j] = l_ref[j] * alpha + jnp.sum(p, axis=0, keepdims=True)
            m_ref[j] = m_new
            p_ref[cur, j] = p.astype(BF16)
        o_prev = pv(jnp.maximum(g - 1, 0), prv)
        scores(jnp.minimum(g + 1, n_g - 1), prv)
        for j in range(N_PAIRS):
            acc_ref[j] = acc_ref[j] * al_old[j] + o_prev[j]

    scores(0, 0)

    def pipe_body(u, c):
        step(2 * u, 0)

        @pl.when(2 * u + 1 < n_g)
        def _():
            step(2 * u + 1, 1)
        return c

    lax.fori_loop(0, (n_g + 1) // 2, pipe_body, 0)
    o_last = pv(n_g - 1, (n_g - 1) % 2)

    for j in range(N_PAIRS):
        acc = acc_ref[j] * al_ref[j] + o_last[j]
        l = l_ref[j]
        o0 = acc[0:ATT_DH, 0:LANES] / l[:, 0:LANES]
        o1 = acc[ATT_DH:2 * ATT_DH, LANES:2 * LANES] / l[:, LANES:2 * LANES]
        ot_ref[j * LANES:(j + 1) * LANES, :] = jnp.concatenate([o0, o1], axis=0)

    ot = ot_ref[...]
    og = og_ref[...]
    outs = []
    for hd in range(ATT_HEADS):
        oh = ot[hd * ATT_DH:(hd + 1) * ATT_DH, :]
        ms = jnp.mean(oh * oh, axis=0, keepdims=True)
        outs.append(oh * lax.rsqrt(ms + EPS) * og[hd * ATT_DH:(hd + 1) * ATT_DH, :])
    o_ref[...] = jnp.concatenate(outs, axis=0).T.astype(BF16)


def _attention(aq, ak, avt, iq, ik, iwt, rel_bias, out_g, *, gk):
    bsz, _, s, _ = aq.shape
    k_sel = min(TOPK_MAX, s // 4)
    bias = _bias_slots(rel_bias)
    qpair = pl.BlockSpec((None, N_PAIRS, QBLK, LANES), lambda b, i: (b, 0, i, 0))
    full = lambda shp: pl.BlockSpec((None,) + shp, lambda b, i: (b,) + (0,) * len(shp),
                                    pipeline_mode=pl.Buffered(1))
    return pl.pallas_call(
        functools.partial(_attn_kernel, gs=_tile(s, 512), gk=gk, k_sel=k_sel),
        grid=(bsz, s // QBLK),
        in_specs=[qpair, full((N_PAIRS, s, LANES)), full((s // gk, ATT_W, gk)), qpair,
                  full((s, LANES)), pl.BlockSpec((None, IDX_HEADS, QBLK), lambda b, i: (b, 0, i)),
                  _const_spec(bias.shape), _const_spec((ATT_W, 1))],
        out_specs=pl.BlockSpec((None, QBLK, ATT_W), lambda b, i: (b, i, 0)),
        out_shape=jax.ShapeDtypeStruct((bsz, s, ATT_W), BF16),
        scratch_shapes=[pltpu.VMEM((s, LANES), F32),
                        pltpu.VMEM((s, LANES), F32),
                        pltpu.VMEM((8, LANES), F32),
                        pltpu.VMEM((N_PAIRS, 2 * ATT_DH, 2 * LANES), F32),
                        pltpu.VMEM((N_PAIRS, 1, 2 * LANES), F32),
                        pltpu.VMEM((N_PAIRS, 1, 2 * LANES), F32),
                        pltpu.VMEM((N_PAIRS, 1, 2 * LANES), F32),
                        pltpu.VMEM((2, N_PAIRS, gk, 2 * LANES), F32),
                        pltpu.VMEM((2, N_PAIRS, 1, 2 * LANES), F32),
                        pltpu.VMEM((2, N_PAIRS, gk, 2 * LANES), BF16),
                        pltpu.VMEM((N_PAIRS, 2 * QBLK, LANES), BF16),
                        pltpu.VMEM((ATT_W, LANES), F32),
                        pltpu.SMEM((1,), I32)],
        compiler_params=_params(("parallel", "arbitrary")),
        name="attn",
    )(aq, ak, avt, iq, ik, iwt, bias, out_g.reshape(ATT_W, 1))


def _outproj_kernel(rec_ref, att_ref, x_ref, g1_ref, sc_ref, sh_ref, ng_ref, wo_ref, wr_ref, br_ref,
                    x1_ref, h2_ref, comb_ref):
    mix = _dot(rec_ref[...], wo_ref[0:REC_W, :]) + _dot(att_ref[...], wo_ref[REC_W:REC_W + ATT_W, :])
    x1 = x_ref[...] + g1_ref[...] * mix
    x1_ref[...] = x1
    ms = jnp.mean(x1 * x1, axis=-1, keepdims=True)
    h2 = (x1 * lax.rsqrt(ms + EPS) * ng_ref[...] * (1.0 + sc_ref[...]) + sh_ref[...]).astype(BF16)
    h2_ref[...] = h2

    lg = _dot_nt(wr_ref[...], h2) + br_ref[...]
    row = lax.broadcasted_iota(I32, lg.shape, 0)
    big = jnp.int32(2 * LANES)
    is_g = row < N_GROUPS
    gl = jnp.where(is_g, lg, -jnp.inf)
    gmax = jnp.max(gl, axis=0, keepdims=True)
    gate = 1.0 / jnp.sum(jnp.where(is_g, jnp.exp(lg - gmax), 0.0), axis=0, keepdims=True)
    gtop = jnp.min(jnp.where(gl == gmax, row, big), axis=0, keepdims=True)
    e_lo = ROUTE_OFF + EXPERTS_PER_GROUP * gtop
    el = jnp.where((row >= e_lo) & (row < e_lo + EXPERTS_PER_GROUP), lg, -jnp.inf)
    v1 = jnp.max(el, axis=0, keepdims=True)
    i1 = jnp.min(jnp.where(el == v1, row, big), axis=0, keepdims=True)
    el2 = jnp.where(row == i1, -jnp.inf, el)
    v2 = jnp.max(el2, axis=0, keepdims=True)
    i2 = jnp.min(jnp.where(el2 == v2, row, big), axis=0, keepdims=True)
    e2 = jnp.exp(v2 - v1)
    w1 = gate / (1.0 + e2)
    w2 = gate * e2 / (1.0 + e2)
    comb = jnp.where(row == i1, w1, 0.0) + jnp.where(row == i2, w2, 0.0)
    comb = jnp.where(row == 0, gtop.astype(F32), comb)
    pad = jnp.zeros((LANES - ROUTE_ROWS, comb.shape[1]), F32)
    comb_ref[...] = jnp.concatenate([comb, pad], axis=0).T


def _outproj(rec, att, x, g1, sc2, sh2, norm_g, w_out, w_rg, b_rg, w_re, b_re, *, tm):
    bsz, s, d = x.shape
    wr = jnp.zeros((ROUTE_ROWS, d), F32).at[:N_GROUPS].set(w_rg.T).at[ROUTE_OFF:ROUTE_OFF + N_EXPERTS].set(w_re.T)
    br = jnp.zeros((ROUTE_ROWS, 1), F32).at[:N_GROUPS, 0].set(b_rg).at[ROUTE_OFF:ROUTE_OFF + N_EXPERTS, 0].set(b_re)
    row = lambda w: pl.BlockSpec((None, tm, w), lambda b, i: (b, i, 0))
    vec = pl.BlockSpec((None, 1, d), lambda b, i: (b, 0, 0))
    return pl.pallas_call(
        _outproj_kernel,
        grid=(bsz, s // tm),
        in_specs=[row(REC_W), row(ATT_W), row(d), vec, vec, vec, _const_spec((1, d)),
                  _const_spec((REC_W + ATT_W, d)), _const_spec((ROUTE_ROWS, d)), _const_spec((ROUTE_ROWS, 1))],
        out_specs=(row(d), row(d), row(LANES)),
        out_shape=(jax.ShapeDtypeStruct((bsz, s, d), F32),
                   jax.ShapeDtypeStruct((bsz, s, d), BF16),
                   jax.ShapeDtypeStruct((bsz, s, LANES), F32)),
        compiler_params=_params(("parallel", "parallel")),
        name="outproj",
    )(rec, att, x, g1.reshape(bsz, 1, d), sc2.reshape(bsz, 1, d), sh2.reshape(bsz, 1, d),
      norm_g.reshape(1, d), w_out.astype(BF16), wr.astype(BF16), br)


MOE_ALIGN = 16
MOE_LANE_SHIFT = 32
MOE_CHUNK = 288
MOE_RB = 256


def _moe_kernel(h_ref, comb_ref, x1_ref, g2_ref, w1_ref, w3_ref, w2_ref, o_ref,
                p_ref, pt_ref, hs_ref, cs_ref, ys_ref, seg_ref, *, mc):
    e = pl.program_id(2)
    tm = h_ref.shape[0]
    npad = hs_ref.shape[0]

    @pl.when(e == 0)
    def _():
        comb = comb_ref[...]
        lane = lax.broadcasted_iota(I32, (tm, LANES), 1)
        gid = comb[:, 0:1].astype(I32)
        oh = jnp.where(lane == gid, 1.0, 0.0)
        ohb = oh.astype(BF16)
        pre = []
        for rb in range(tm // MOE_RB):
            r = rb * MOE_RB + lax.broadcasted_iota(I32, (MOE_RB, tm), 0)
            c = lax.broadcasted_iota(I32, (MOE_RB, tm), 1)
            pre.append(_dot(jnp.where(c < r, 1.0, 0.0).astype(BF16), ohb))
        prefix = jnp.concatenate(pre, axis=0)
        cnt = jnp.sum(oh, axis=0, keepdims=True).astype(I32)
        cnt_al = ((cnt + (MOE_ALIGN - 1)) // MOE_ALIGN) * MOE_ALIGN
        base = jnp.sum(jnp.where(lane < gid, cnt_al.astype(F32), 0.0), axis=-1, keepdims=True)
        rank = jnp.sum(prefix * oh, axis=-1, keepdims=True)
        pos = (base + rank).astype(I32)
        pos_row = jnp.broadcast_to(pos.astype(F32), (tm, LANES)).T[0:1, :].astype(I32)
        for rb in range(tm // MOE_RB):
            sl = slice(rb * MOE_RB, (rb + 1) * MOE_RB)
            coln = lax.broadcasted_iota(I32, (MOE_RB, npad), 1)
            pt_ref[sl, :] = jnp.where(coln == pos[sl], 1.0, 0.0).astype(BF16)
        for rb in range(npad // LANES):
            sl = slice(rb * LANES, (rb + 1) * LANES)
            rown = rb * LANES + lax.broadcasted_iota(I32, (LANES, tm), 0)
            p_ref[sl, :] = jnp.where(rown == pos_row, 1.0, 0.0).astype(BF16)
        c_hi = comb.astype(BF16)
        c_mid, c_lo = _split_bf16(comb - c_hi.astype(F32))
        packed = (c_hi.astype(F32) + pltpu.roll(c_mid.astype(F32), MOE_LANE_SHIFT, axis=1)
                  + pltpu.roll(c_lo.astype(F32), 2 * MOE_LANE_SHIFT, axis=1)).astype(BF16)
        srt = _dot(p_ref[...], jnp.concatenate([h_ref[...], packed], axis=1))
        d = h_ref.shape[1]
        hs_ref[...] = srt[:, :d].astype(BF16)
        cp = srt[:, d:]
        cs_ref[...] = (cp + pltpu.roll(cp, LANES - MOE_LANE_SHIFT, axis=1)
                       + pltpu.roll(cp, LANES - 2 * MOE_LANE_SHIFT, axis=1))
        ys_ref[...] = jnp.zeros_like(ys_ref)
        start = jnp.int32(0)
        for g in range(N_GROUPS):
            seg_ref[g] = start
            seg_ref[N_GROUPS + g] = cnt[0, g]
            start = start + cnt_al[0, g]

    grp = e // EXPERTS_PER_GROUP
    start = seg_ref[grp]
    n_rows = seg_ref[N_GROUPS + grp]

    def chunk(ci, carry):
        rs = pl.ds(pl.multiple_of(start + ci * mc, MOE_ALIGN), mc)
        hb = hs_ref[rs, :]
        cw = cs_ref[rs, :]
        lane = lax.broadcasted_iota(I32, cw.shape, 1)
        col = jnp.sum(jnp.where(lane == e + ROUTE_OFF, cw, 0.0), axis=-1, keepdims=True)
        he = _silu(_dot(hb, w1_ref[...])) * _dot(hb, w3_ref[...]) * col
        ys_ref[rs, :] += _dot(he.astype(BF16), w2_ref[...])
        return carry

    lax.fori_loop(0, (n_rows + mc - 1) // mc, chunk, 0)

    @pl.when(e == N_EXPERTS - 1)
    def _():
        o_ref[...] = x1_ref[...] + g2_ref[...] * _dot(pt_ref[...], ys_ref[...].astype(BF16))


def _moe(h2, comb, x1, g2, w1, w3, w2, *, tm):
    bsz, s, d = x1.shape
    mc = MOE_CHUNK
    npad = -(-(tm + N_GROUPS * MOE_ALIGN + mc) // LANES) * LANES
    row = lambda w, **kw: pl.BlockSpec((None, tm, w), lambda b, i, e: (b, i, 0), **kw)
    once = dict(pipeline_mode=pl.Buffered(1))
    return pl.pallas_call(
        functools.partial(_moe_kernel, mc=mc),
        grid=(bsz, s // tm, N_EXPERTS),
        in_specs=[row(d, **once), row(LANES), row(d, **once),
                  pl.BlockSpec((None, 1, d), lambda b, i, e: (b, 0, 0)),
                  pl.BlockSpec((None, d, D_EXPERT), lambda b, i, e: (e, 0, 0)),
                  pl.BlockSpec((None, d, D_EXPERT), lambda b, i, e: (e, 0, 0)),
                  pl.BlockSpec((None, D_EXPERT, d), lambda b, i, e: (e, 0, 0))],
        out_specs=row(d),
        out_shape=jax.ShapeDtypeStruct((bsz, s, d), F32),
        scratch_shapes=[pltpu.VMEM((npad, tm), BF16),
                        pltpu.VMEM((tm, npad), BF16),
                        pltpu.VMEM((npad, d), BF16),
                        pltpu.VMEM((npad, LANES), F32),
                        pltpu.VMEM((npad, d), F32),
                        pltpu.SMEM((2 * N_GROUPS,), I32)],
        compiler_params=_params(("parallel", "parallel", "arbitrary")),
        name="moe",
    )(h2, comb, x1, g2.reshape(bsz, 1, d), w1.astype(BF16), w3.astype(BF16), w2.astype(BF16))


def _tile(s, pref):
    t = min(s, pref)
    assert s % t == 0
    return t


def kernel(x, c, w_ada, b_ada, norm1_g, norm2_g, w_in, lb_logits, rec_out_g, q_norm_g, k_norm_g,
           idx_k_norm_g, idx_k_norm_b, attn_out_g, rel_bias, w_out, w_rg, b_rg, w_re, b_re, w1, w3, w2):
    bsz, s, d = x.shape
    depth = w_ada.shape[0]
    gk = _tile(s, 512)
    for l in range(depth):
        mod = _adaln(c, w_ada[l], b_ada[l])
        sh1, sc1, g1, sh2, sc2, g2 = jnp.split(mod, 6, axis=-1)
        q, f, v, g, aq, ak, avt, iq, ik, iwt = _inproj(
            x, sc1, sh1, norm1_g[l], w_in[l], lb_logits, q_norm_g[l], k_norm_g[l],
            idx_k_norm_g[l], idx_k_norm_b[l], layer=l, gt=gk, tm=_tile(s, 512))
        rec = _hgrn(q, f, v, g, rec_out_g[l], ts=_tile(s, 256))
        att = _attention(aq, ak, avt, iq, ik, iwt, rel_bias, attn_out_g[l], gk=gk)
        x1, h2, comb = _outproj(rec, att, x, g1, sc2, sh2, norm2_g[l], w_out[l],
                                w_rg[l], b_rg[l], w_re[l], b_re[l], tm=_tile(s, 512))
        x = _moe(h2, comb, x1, g2, w1[l], w3[l], w2[l], tm=_tile(s, 1024))
    return x
```

```python
import functools

import numpy as np
import jax
import jax.numpy as jnp
from jax import lax
from jax.experimental import pallas as pl
from jax.experimental.pallas import tpu as pltpu

F32 = jnp.float32
BF16 = jnp.bfloat16
I32 = jnp.int32

CHUNK = 64
QBLK = 128
EPS = 1e-6
REC_HEADS = 4
REC_DK = 128
REC_DV = 128
REC_W = REC_HEADS * REC_DV
ATT_HEADS = 8
ATT_DH = 64
ATT_W = ATT_HEADS * ATT_DH
IDX_HEADS = 8
IDX_DIM = 64
TOPK_MAX = 256
NUM_BUCKETS = 32
MAX_DISTANCE = 128
N_GROUPS = 4
EXPERTS_PER_GROUP = 4
N_EXPERTS = N_GROUPS * EXPERTS_PER_GROUP
D_EXPERT = 512

LANES = 128
V7X_VMEM_LIMIT = 56 * 1024 * 1024
F32_LOWEST = float(np.finfo(np.float32).min)
BISECT_FIRST = 17
BISECT_CAP = 320
NEG = -1e30
LOG2E = float(np.log2(np.e))
N_PAIRS = ATT_HEADS // 2
ROUTE_OFF = N_GROUPS
ROUTE_ROWS = 32


def _dot(a, b):
    return jnp.dot(a, b, preferred_element_type=F32)


def _dot_nt(a, b):
    return lax.dot_general(a, b, (((1,), (1,)), ((), ())), preferred_element_type=F32)


def _dot_tn(a, b):
    return lax.dot_general(a, b, (((0,), (0,)), ((), ())), preferred_element_type=F32)


def _split_bf16(a):
    hi = a.astype(BF16)
    lo = (a - hi.astype(F32)).astype(BF16)
    return hi, lo


def _silu(a):
    return a * jax.nn.sigmoid(a)


def _const_spec(shape):
    nd = len(shape)
    return pl.BlockSpec(shape, lambda *_: (0,) * nd, pipeline_mode=pl.Buffered(1))


def _params(sem):
    return pltpu.CompilerParams(dimension_semantics=sem, vmem_limit_bytes=V7X_VMEM_LIMIT)


def _adaln_kernel(c_ref, w_ref, b_ref, o_ref):
    a_hi, a_lo = _split_bf16(_silu(c_ref[...]))
    w_hi, w_lo = _split_bf16(w_ref[...])
    o_ref[...] = _dot(a_hi, w_hi) + _dot(a_lo, w_hi) + _dot(a_hi, w_lo) + b_ref[...]


def _adaln(c, w_all, b, layer):
    bsz, d = c.shape
    n = w_all.shape[2]
    rows = 16
    bn = 1024
    cp = jnp.zeros((rows, d), F32).at[:bsz].set(c)
    out = pl.pallas_call(
        _adaln_kernel,
        grid=(n // bn,),
        in_specs=[pl.BlockSpec((rows, d), lambda i: (0, 0)),
                  pl.BlockSpec((None, d, bn), lambda i: (layer, 0, i)),
                  pl.BlockSpec((1, bn), lambda i: (0, i))],
        out_specs=pl.BlockSpec((rows, bn), lambda i: (0, i)),
        out_shape=jax.ShapeDtypeStruct((rows, n), F32),
        compiler_params=_params(("parallel",)),
        name="adaln",
    )(cp, w_all, b.reshape(1, n))
    return out[:bsz]


def _inproj_kernel(x_ref, sc_ref, sh_ref, ng_ref, w_ref, lbl_ref,
                   qg_ref, kg_ref, ikg_ref, ikb_ref, pm_ref,
                   q_ref, f_ref, v_ref, g_ref, aq_ref, ak_ref, avt_ref, iq_ref, ik_ref, iwt_ref,
                   *, layer, gt):
    x = x_ref[...]
    tm = x.shape[0]
    ms = jnp.mean(x * x, axis=-1, keepdims=True)
    h = x * lax.rsqrt(ms + EPS) * ng_ref[...] * (1.0 + sc_ref[...]) + sh_ref[...]
    hb = h.astype(BF16)

    n_rec, n_att = 4 * REC_W, 3 * ATT_W
    zr = _dot(hb, w_ref[:, 0:n_rec])
    q_ref[...] = _silu(zr[:, 0:REC_W]).astype(BF16)
    lbl = lbl_ref[...]
    e = jnp.exp(lbl - jnp.max(lbl, axis=0, keepdims=True))
    sm = e / jnp.sum(e, axis=0, keepdims=True)
    lb = jnp.sum(sm[0:layer + 1], axis=0, keepdims=True)
    f_ref[...] = lb + (1.0 - lb) * jax.nn.sigmoid(zr[:, REC_W:2 * REC_W])
    v_ref[...] = zr[:, 2 * REC_W:3 * REC_W].astype(BF16)
    g_ref[...] = _silu(zr[:, 3 * REC_W:4 * REC_W]).astype(BF16)

    za = _dot(hb, w_ref[:, n_rec:n_rec + n_att])
    aq = za[:, 0:ATT_W]
    ak = za[:, ATT_W:2 * ATT_W]
    av = za[:, 2 * ATT_W:3 * ATT_W]
    pm = pm_ref[...]
    aqn = aq * lax.rsqrt(_dot((aq * aq).astype(BF16), pm) + EPS) * qg_ref[...]
    akn = ak * lax.rsqrt(_dot((ak * ak).astype(BF16), pm) + EPS) * kg_ref[...]
    for j in range(N_PAIRS):
        aq_ref[j] = aqn[:, j * LANES:(j + 1) * LANES].astype(BF16)
        ak_ref[j] = akn[:, j * LANES:(j + 1) * LANES].astype(BF16)
    for t in range(tm // gt):
        avt_ref[t] = av[t * gt:(t + 1) * gt, :].T.astype(BF16)

    zi = _dot(hb, w_ref[:, n_rec + n_att:])
    for j in range(N_PAIRS):
        iq_ref[j] = zi[:, j * LANES:(j + 1) * LANES].astype(BF16)
    tail = zi[:, IDX_HEADS * IDX_DIM:IDX_HEADS * IDX_DIM + LANES]
    lane = lax.broadcasted_iota(I32, tail.shape, 1)
    is_k = lane < IDX_DIM
    mu = jnp.sum(jnp.where(is_k, tail, 0.0), axis=-1, keepdims=True) * (1.0 / IDX_DIM)
    dlt = jnp.where(is_k, tail - mu, 0.0)
    var = jnp.sum(dlt * dlt, axis=-1, keepdims=True) * (1.0 / IDX_DIM)
    ikn = dlt * lax.rsqrt(var + EPS) * ikg_ref[...] + ikb_ref[...]
    ik_ref[...] = jnp.where(is_k, ikn, pltpu.roll(ikn, IDX_DIM, axis=1)).astype(BF16)
    iwt_ref[...] = tail.T[IDX_DIM:IDX_DIM + IDX_HEADS, :] * (IDX_HEADS ** -0.5 * IDX_DIM ** -0.5)


def _inproj(x, sc1, sh1, norm_g, w_in, lb_logits, q_g, k_g, ik_g, ik_b, *, layer, gt, tm):
    bsz, s, d = x.shape
    n_rec = 4 * REC_W
    n_att = 3 * ATT_W
    n_idx = IDX_HEADS * IDX_DIM + LANES
    n_all = n_rec + n_att + n_idx
    wb = jnp.zeros((d, n_all), BF16).at[:, :w_in.shape[1]].set(w_in.astype(BF16))
    pm = jnp.asarray(np.kron(np.eye(ATT_HEADS), np.full((ATT_DH, ATT_DH), 1.0 / ATT_DH)), BF16)
    qg = jnp.tile(q_g, ATT_HEADS).reshape(1, ATT_W) * (ATT_DH ** -0.5 * LOG2E)
    kg = jnp.tile(k_g, ATT_HEADS).reshape(1, ATT_W)
    ikg = jnp.zeros((1, LANES), F32).at[0, :IDX_DIM].set(ik_g)
    ikb = jnp.zeros((1, LANES), F32).at[0, :IDX_DIM].set(ik_b)
    nl = lb_logits.shape[0]

    row = lambda w: pl.BlockSpec((None, tm, w), lambda b, i: (b, i, 0))
    pair = pl.BlockSpec((None, N_PAIRS, tm, LANES), lambda b, i: (b, 0, i, 0))
    vec = pl.BlockSpec((None, 1, d), lambda b, i: (b, 0, 0))
    out_shapes = (
        jax.ShapeDtypeStruct((bsz, s, REC_W), BF16),
        jax.ShapeDtypeStruct((bsz, s, REC_W), F32),
        jax.ShapeDtypeStruct((bsz, s, REC_W), BF16),
        jax.ShapeDtypeStruct((bsz, s, REC_W), BF16),
        jax.ShapeDtypeStruct((bsz, N_PAIRS, s, LANES), BF16),
        jax.ShapeDtypeStruct((bsz, N_PAIRS, s, LANES), BF16),
        jax.ShapeDtypeStruct((bsz, s // gt, ATT_W, gt), BF16),
        jax.ShapeDtypeStruct((bsz, N_PAIRS, s, LANES), BF16),
        jax.ShapeDtypeStruct((bsz, s, LANES), BF16),
        jax.ShapeDtypeStruct((bsz, IDX_HEADS, s), F32),
    )
    out_specs = (
        row(REC_W), row(REC_W), row(REC_W), row(REC_W), pair, pair,
        pl.BlockSpec((None, tm // gt, ATT_W, gt), lambda b, i: (b, i, 0, 0)),
        pair, row(LANES),
        pl.BlockSpec((None, IDX_HEADS, tm), lambda b, i: (b, 0, i)),
    )
    return pl.pallas_call(
        functools.partial(_inproj_kernel, layer=layer, gt=gt),
        grid=(bsz, s // tm),
        in_specs=[row(d), vec, vec, _const_spec((1, d)),
                  _const_spec((d, n_all)),
                  _const_spec((nl, REC_W)), _const_spec((1, ATT_W)), _const_spec((1, ATT_W)),
                  _const_spec((1, LANES)), _const_spec((1, LANES)), _const_spec((ATT_W, ATT_W))],
        out_specs=out_specs,
        out_shape=out_shapes,
        compiler_params=_params(("parallel", "parallel")),
        name="inproj",
    )(x, sc1.reshape(bsz, 1, d), sh1.reshape(bsz, 1, d), norm_g.reshape(1, d),
      wb, lb_logits, qg, kg, ikg, ikb, pm)


N_LEVELS = 6


def _hgrn_tables():
    c = CHUNK
    w = np.zeros((N_LEVELS + 2, c, c), np.float32)
    am = np.zeros((N_LEVELS + 1, c, c), np.float32)
    t = np.arange(c)
    for m in range(N_LEVELS):
        hs = 1 << m
        blk = t // (2 * hs)
        upper = (t // hs) % 2 == 1
        ref = blk * 2 * hs + hs - 1
        for i in range(c):
            if upper[i]:
                w[m, i, ref[i] + 1:i + 1] = 1.0
            else:
                w[m, i, i + 1:ref[i] + 1] = 1.0
        am[m] = (blk[:, None] == blk[None, :]) & upper[:, None] & ~upper[None, :]
    w[N_LEVELS] = np.tril(np.ones((c, c)))
    w[N_LEVELS + 1] = np.triu(np.ones((c, c)), 1)
    am[N_LEVELS] = np.eye(c)
    w = w.reshape((N_LEVELS + 2) * c, c)
    return np.concatenate([w, w], axis=1), am


HGRN_UNROLL = 4


def _hgrn_kernel(q_ref, f_ref, v_ref, g_ref, og_ref, ww_ref, am_ref, o_ref, st_ref, ex_ref, *, n_chunks):
    @pl.when(pl.program_id(1) == 0)
    def _():
        st_ref[...] = jnp.zeros_like(st_ref)

    ww = ww_ref[...]
    c = CHUNK
    tbit = lax.broadcasted_iota(I32, (c, REC_DK), 0)
    items = [(cc, hd) for cc in range(HGRN_UNROLL) for hd in range(REC_HEADS)]

    def chunks(ci, carry):
        def blk(ref, it):
            r0 = pl.multiple_of((ci * HGRN_UNROLL + it[0]) * c, c)
            return ref.at[pl.ds(r0, c), it[1] * REC_DK:(it[1] + 1) * REC_DK]

        def ex(i, part, last_row=False):
            r0 = (part + 1) * c - 1 if last_row else part * c
            return ex_ref[items[i][0], r0:(part + 1) * c, items[i][1] * REC_DK:(items[i][1] + 1) * REC_DK]

        f = [blk(f_ref, it)[...] for it in items]
        q = [blk(q_ref, it)[...].astype(F32) for it in items]
        v = [blk(v_ref, it)[...] for it in items]
        k = [1.0 - fi for fi in f]
        for cc in range(HGRN_UNROLL):
            cols = [jnp.concatenate(_split_bf16(jnp.log(f[cc * REC_HEADS + hd])), axis=0) for hd in range(REC_HEADS)]
            ex_ref[cc] = jnp.exp(_dot(ww, jnp.concatenate(cols, axis=1)))
        a = [am_ref[N_LEVELS] * _dot_nt(q[i].astype(BF16), k[i].astype(BF16)) for i in range(len(items))]
        for m in range(N_LEVELS):
            upper = ((tbit >> m) & 1) == 1
            for i in range(len(items)):
                tm_ = (ex(i, m) * jnp.where(upper, q[i], k[i])).astype(BF16)
                a[i] = a[i] + am_ref[m] * _dot_nt(tm_, tm_)
        qb = [(q[i] * ex(i, N_LEVELS)).astype(BF16) for i in range(len(items))]
        kb = [(k[i] * ex(i, N_LEVELS + 1)).astype(BF16) for i in range(len(items))]
        intra = [_dot(a[i].astype(BF16), v[i]) for i in range(len(items))]
        ut = [_dot_tn(v[i], kb[i]) for i in range(len(items))]
        for hd in range(REC_HEADS):
            st = st_ref[hd]
            for cc in range(HGRN_UNROLL):
                i = cc * REC_HEADS + hd
                o = intra[i] + _dot_nt(qb[i], st.astype(BF16))
                st = st * ex(i, N_LEVELS, last_row=True) + ut[i]
                ms = jnp.mean(o * o, axis=-1, keepdims=True)
                gate = blk(g_ref, items[i])[...].astype(F32)
                y = o * lax.rsqrt(ms + EPS) * og_ref[:, hd * REC_DV:(hd + 1) * REC_DV] * gate
                blk(o_ref, items[i])[...] = y.astype(BF16)
            st_ref[hd] = st
        return carry

    lax.fori_loop(0, n_chunks // HGRN_UNROLL, chunks, 0)


def _hgrn(q, f, v, g, out_g, *, ts):
    bsz, s, _ = q.shape
    ww_np, am_np = _hgrn_tables()
    ww = jnp.asarray(ww_np, BF16)
    am = jnp.asarray(am_np, F32)
    row = pl.BlockSpec((None, ts, REC_W), lambda b, i: (b, i, 0))
    return pl.pallas_call(
        functools.partial(_hgrn_kernel, n_chunks=ts // CHUNK),
        grid=(bsz, s // ts),
        in_specs=[row, row, row, row, _const_spec((1, REC_W)),
                  _const_spec(ww.shape), _const_spec(am.shape)],
        out_specs=row,
        out_shape=jax.ShapeDtypeStruct((bsz, s, REC_W), BF16),
        scratch_shapes=[pltpu.VMEM((REC_HEADS, REC_DV, REC_DK), F32),
                        pltpu.VMEM((HGRN_UNROLL, (N_LEVELS + 2) * CHUNK, REC_HEADS * REC_DK), F32)],
        compiler_params=_params(("parallel", "arbitrary")),
        name="hgrn",
    )(q, f, v, g, out_g.reshape(1, REC_W), ww, am)


def _t5_bucket(rel):
    nb = NUM_BUCKETS // 2
    max_exact = nb // 2
    ret = jnp.where(rel > 0, nb, 0)
    n = jnp.abs(rel)
    nf = jnp.maximum(n, 1).astype(F32)
    large = max_exact + (jnp.log(nf / max_exact) / np.log(MAX_DISTANCE / max_exact)
                         * (nb - max_exact)).astype(I32)
    large = jnp.minimum(large, nb - 1)
    return ret + jnp.where(n < max_exact, n, large)


def _bias_slots(rel_bias):
    n_rel = 3 * QBLK
    rel = jnp.arange(n_rel, dtype=I32) - 2 * QBLK
    tab = rel_bias[_t5_bucket(rel)].T
    far = rel_bias[_t5_bucket(jnp.full((1,), -2 * QBLK - 1, I32))].T
    rev = tab[:, ::-1]
    skew = jnp.tile(rev, (1, 2 * QBLK))[:, :2 * QBLK * (n_rel - 1)].reshape(ATT_HEADS, 2 * QBLK, n_rel - 1)
    near = skew[:, :, 2 * QBLK - 1:]
    nb = ((near - far[:, :, None]) * LOG2E).reshape(ATT_HEADS, 2, QBLK, QBLK)
    z = jnp.zeros((ATT_HEADS, 1, QBLK, QBLK), F32)
    return jnp.concatenate([z, nb, z], axis=1)


def _attn_kernel(aq_ref, ak_ref, avt_ref, iq_ref, ik_ref, iwt_ref, bias_ref, og_ref, o_ref,
                 key_ref, mb_ref, thr_ref, acc_ref, m_ref, l_ref, al_ref, s_ref, mx_ref, p_ref, qbd_ref, ot_ref, tie_ref,
                 *, gs, gk, k_sel):
    bi = pl.program_id(1)
    n_s = bi // (gs // QBLK) + 1
    lane = lax.broadcasted_iota(I32, (QBLK, LANES), 1)

    def block_diag(xq):
        zero = jnp.zeros_like(xq)
        return jnp.concatenate([jnp.where(lane < ATT_DH, xq, zero), jnp.where(lane >= ATT_DH, xq, zero)], axis=0)

    def srows(g):
        return pl.ds(pl.multiple_of(g * gs, gs), gs)

    for j in range(N_PAIRS):
        qbd_ref[j] = block_diag(iq_ref[j])
    w = iwt_ref[...]

    def group_scores(g):
        sc = _dot_nt(ik_ref[srows(g), :], qbd_ref[...].reshape(N_PAIRS * 2 * QBLK, LANES))
        acc = jnp.zeros((gs, LANES), F32)
        for hd in range(IDX_HEADS):
            acc = acc + jnp.maximum(sc[:, hd * LANES:(hd + 1) * LANES], 0.0) * w[hd:hd + 1, :]
        return acc

    def stats(sc, c):
        s3 = sc.reshape(gs // 8, 8, LANES)
        return (jnp.minimum(c[0], jnp.min(jnp.where(s3 == -jnp.inf, jnp.inf, s3), axis=0)),
                jnp.maximum(c[1], jnp.max(s3, axis=0)),
                c[2] + jnp.sum(jnp.where(s3 >= 0.0, 1, 0), axis=0),
                c[3] + jnp.sum(jnp.where(s3 > 0.0, 1, 0), axis=0))

    def idx_body(g, carry):
        sc = group_scores(g)
        key_ref[srows(g), :] = sc
        return stats(sc, carry)

    def idx_pair(u, carry):
        return idx_body(2 * u + 1, idx_body(2 * u, carry))

    z8 = jnp.zeros((8, LANES), I32)
    st8 = lax.fori_loop(0, (n_s - 1) // 2, idx_pair,
                        (jnp.full((8, LANES), jnp.inf, F32), jnp.full((8, LANES), -jnp.inf, F32), z8, z8))
    st8 = lax.cond((n_s - 1) % 2 == 1, lambda c: idx_body(n_s - 2, c), lambda c: c, st8)

    last = srows(n_s - 1)
    spos = (n_s - 1) * gs + lax.broadcasted_iota(I32, (gs, LANES), 0)
    tpos = bi * QBLK + lax.broadcasted_iota(I32, (gs, LANES), 1)
    sc_last = jnp.where((spos // CHUNK) <= (tpos // CHUNK), group_scores(n_s - 1), -jnp.inf)
    key_ref[last, :] = sc_last
    mn8, mx8, ge8, gt8 = stats(sc_last, st8)

    thr_ref[...] = jnp.full(thr_ref.shape, F32_LOWEST, F32)
    tie_ref[0] = 0

    @pl.when((2 * bi + 2) * CHUNK > k_sel)
    def _():
        def count_ge(cand):
            def one(g):
                sc = key_ref[srows(g), :].reshape(gs // 8, 8, LANES)
                return jnp.sum(jnp.where(sc >= cand, 1, 0), axis=0)

            def two(u, acc):
                second = jnp.where(2 * u + 1 < n_s, 1, 0)
                return acc + one(2 * u) + one(jnp.minimum(2 * u + 1, n_s - 1)) * second

            acc = lax.fori_loop(0, (n_s + 1) // 2, two, jnp.zeros((8, LANES), I32))
            return jnp.sum(acc, axis=0, keepdims=True)

        c0_ge = jnp.sum(ge8, axis=0, keepdims=True)
        c0_gt = jnp.sum(gt8, axis=0, keepdims=True)
        mn = jnp.min(mn8, axis=0, keepdims=True)
        mx = jnp.max(mx8, axis=0, keepdims=True)
        above = mx + (jnp.abs(mx) * 2.0 ** -20 + 1e-30)
        n_adm = (2 * bi + 1 + lax.broadcasted_iota(I32, (1, LANES), 1) // CHUNK) * CHUNK
        pos = c0_gt >= k_sel
        non_neg = c0_ge >= k_sel
        lo = jnp.where(non_neg, 0.0, mn)
        cnt_lo = jnp.where(non_neg, c0_ge, n_adm)
        hi = jnp.where(pos, above, 0.0)

        def midpoint(lo, hi):
            return 0.5 * lo + 0.5 * hi

        def unsettled(lo, hi, cnt_lo):
            mid = midpoint(lo, hi)
            return jnp.where(cnt_lo > k_sel, jnp.where(mid > lo, jnp.where(mid < hi, 1.0, 0.0), 0.0), 0.0)

        def cond(st):
            return jnp.logical_and(st[3] > 0.5, st[4] < BISECT_CAP)

        def halve(lo, hi, cnt_lo):
            act = unsettled(lo, hi, cnt_lo) > 0.5
            mid = midpoint(lo, hi)
            c = count_ge(mid)
            up = jnp.logical_and(act, c >= k_sel)
            dn = jnp.logical_and(act, c < k_sel)
            return jnp.where(up, mid, lo), jnp.where(dn, mid, hi), jnp.where(up, c, cnt_lo)

        def body(st):
            lo, hi, cnt_lo = halve(*halve(*st[:3]))
            return lo, hi, cnt_lo, jnp.max(unsettled(lo, hi, cnt_lo)), st[4] + 2

        lo, hi, cnt_lo = lax.fori_loop(0, BISECT_FIRST, lambda _, s3: halve(*s3), (lo, hi, cnt_lo))
        st = lax.while_loop(cond, body,
                            (lo, hi, cnt_lo, jnp.max(unsettled(lo, hi, cnt_lo)), jnp.int32(BISECT_FIRST)))
        thr_ref[...] = jnp.broadcast_to(st[0], thr_ref.shape)
        tie_ref[0] = (jnp.max(jnp.where(st[2] > k_sel, 1.0, 0.0)) > 0.5).astype(I32)

    thr = thr_ref[0:1, :]

    @pl.when(tie_ref[0] == 0)
    def _():
        def mb_body(g, carry):
            mb_ref[srows(g), :] = jnp.where(key_ref[srows(g), :] >= thr, 0.0, NEG)
            return carry

        lax.fori_loop(0, n_s, mb_body, 0)

    @pl.when(tie_ref[0] != 0)
    def _():
        def count_gt(g, acc):
            sc = key_ref[srows(g), :].reshape(gs // 8, 8, LANES)
            return acc + jnp.sum(jnp.where(sc > thr, 1, 0), axis=0)

        n_gt = jnp.sum(lax.fori_loop(0, n_s, count_gt, jnp.zeros((8, LANES), I32)), axis=0, keepdims=True)
        room = (k_sel - n_gt).astype(F32)
        r = lax.broadcasted_iota(I32, (QBLK, QBLK), 0)
        c = lax.broadcasted_iota(I32, (QBLK, QBLK), 1)
        tri = jnp.where(c <= r, 1.0, 0.0).astype(BF16)

        def mb_body(g, seen):
            for t in range(gs // QBLK):
                rs = pl.ds(pl.multiple_of(g * gs + t * QBLK, QBLK), QBLK)
                sc = key_ref[rs, :]
                eq = jnp.where(sc == thr, 1.0, 0.0)
                rank = _dot(tri, eq.astype(BF16)) + seen
                keep = jnp.where(sc > thr, 1.0, jnp.where(rank <= room, eq, 0.0))
                mb_ref[rs, :] = jnp.where(keep > 0.5, 0.0, NEG)
                seen = seen + jnp.sum(eq, axis=0, keepdims=True)
            return seen

        lax.fori_loop(0, n_s, mb_body, jnp.zeros((1, LANES), F32))

    spg = gk // QBLK
    n_g = bi // spg + 1
    for j in range(N_PAIRS):
        qbd_ref[j] = block_diag(aq_ref[j])
    m_ref[...] = jnp.full(m_ref.shape, NEG, F32)
    l_ref[...] = jnp.zeros_like(l_ref)
    acc_ref[...] = jnp.zeros_like(acc_ref)
    al_ref[...] = jnp.ones_like(al_ref)
    p_ref[1] = jnp.zeros(p_ref.shape[1:], BF16)

    def rows(g):
        return pl.ds(pl.multiple_of(g * gk, gk), gk)

    def scores(t, slot):
        mb = mb_ref[rows(t), :]
        mb2 = jnp.concatenate([mb, mb], axis=1)
        for j in range(N_PAIRS):
            s = _dot_nt(ak_ref[j, rows(t), :], qbd_ref[j]) + mb2
            s_ref[slot, j] = s
            mx_ref[slot, j] = jnp.max(s, axis=0, keepdims=True)

    def pv(t, slot):
        return [_dot(avt_ref[t, j * LANES:(j + 1) * LANES, :], p_ref[slot, j]) for j in range(N_PAIRS)]

    def step(g, cur):
        prv = 1 - cur

        @pl.when(g >= n_g - 2)
        def _():
            for j in range(N_PAIRS):
                parts = []
                for st in range(spg):
                    slot = jnp.clip(g * spg + st - bi + 2, 0, 3)
                    parts.append(jnp.concatenate([bias_ref[2 * j, slot], bias_ref[2 * j + 1, slot]], axis=1))
                s = s_ref[cur, j] + jnp.concatenate(parts, axis=0)
                s_ref[cur, j] = s
                mx_ref[cur, j] = jnp.max(s, axis=0, keepdims=True)

        al_old = [al_ref[j] for j in range(N_PAIRS)]
        for j in range(N_PAIRS):
            m_old = m_ref[j]
            m_new = jnp.maximum(m_old, mx_ref[cur, j])
            p = jnp.exp2(s_ref[cur, j] - m_new)
            alpha = jnp.exp2(m_old - m_new)
            al_ref[j] = alpha
            l_ref[j] = l_ref[j] * alpha + jnp.sum(p, axis=0, keepdims=True)
            m_ref[j] = m_new
            p_ref[cur, j] = p.astype(BF16)
        o_prev = pv(jnp.maximum(g - 1, 0), prv)
        scores(jnp.minimum(g + 1, n_g - 1), prv)
        for j in range(N_PAIRS):
            acc_ref[j] = acc_ref[j] * al_old[j] + o_prev[j]

    scores(0, 0)

    def pipe_body(u, c):
        step(2 * u, 0)

        @pl.when(2 * u + 1 < n_g)
        def _():
            step(2 * u + 1, 1)
        return c

    lax.fori_loop(0, (n_g + 1) // 2, pipe_body, 0)
    o_last = pv(n_g - 1, (n_g - 1) % 2)

    for j in range(N_PAIRS):
        acc = acc_ref[j] * al_ref[j] + o_last[j]
        l = l_ref[j]
        o0 = acc[0:ATT_DH, 0:LANES] / l[:, 0:LANES]
        o1 = acc[ATT_DH:2 * ATT_DH, LANES:2 * LANES] / l[:, LANES:2 * LANES]
        ot_ref[j * LANES:(j + 1) * LANES, :] = jnp.concatenate([o0, o1], axis=0)

    ot = ot_ref[...]
    og = og_ref[...]
    outs = []
    for hd in range(ATT_HEADS):
        oh = ot[hd * ATT_DH:(hd + 1) * ATT_DH, :]
        ms = jnp.mean(oh * oh, axis=0, keepdims=True)
        outs.append(oh * lax.rsqrt(ms + EPS) * og[hd * ATT_DH:(hd + 1) * ATT_DH, :])
    o_ref[...] = jnp.concatenate(outs, axis=0).T.astype(BF16)


def _attention(aq, ak, avt, iq, ik, iwt, rel_bias, out_g, *, gk):
    bsz, _, s, _ = aq.shape
    k_sel = min(TOPK_MAX, s // 4)
    bias = _bias_slots(rel_bias)
    qpair = pl.BlockSpec((None, N_PAIRS, QBLK, LANES), lambda b, i: (b, 0, i, 0))
    full = lambda shp: pl.BlockSpec((None,) + shp, lambda b, i: (b,) + (0,) * len(shp),
                                    pipeline_mode=pl.Buffered(1))
    return pl.pallas_call(
        functools.partial(_attn_kernel, gs=_tile(s, 512), gk=gk, k_sel=k_sel),
        grid=(bsz, s // QBLK),
        in_specs=[qpair, full((N_PAIRS, s, LANES)), full((s // gk, ATT_W, gk)), qpair,
                  full((s, LANES)), pl.BlockSpec((None, IDX_HEADS, QBLK), lambda b, i: (b, 0, i)),
                  _const_spec(bias.shape), _const_spec((ATT_W, 1))],
        out_specs=pl.BlockSpec((None, QBLK, ATT_W), lambda b, i: (b, i, 0)),
        out_shape=jax.ShapeDtypeStruct((bsz, s, ATT_W), BF16),
        scratch_shapes=[pltpu.VMEM((s, LANES), F32),
                        pltpu.VMEM((s, LANES), F32),
                        pltpu.VMEM((8, LANES), F32),
                        pltpu.VMEM((N_PAIRS, 2 * ATT_DH, 2 * LANES), F32),
                        pltpu.VMEM((N_PAIRS, 1, 2 * LANES), F32),
                        pltpu.VMEM((N_PAIRS, 1, 2 * LANES), F32),
                        pltpu.VMEM((N_PAIRS, 1, 2 * LANES), F32),
                        pltpu.VMEM((2, N_PAIRS, gk, 2 * LANES), F32),
                        pltpu.VMEM((2, N_PAIRS, 1, 2 * LANES), F32),
                        pltpu.VMEM((2, N_PAIRS, gk, 2 * LANES), BF16),
                        pltpu.VMEM((N_PAIRS, 2 * QBLK, LANES), BF16),
                        pltpu.VMEM((ATT_W, LANES), F32),
                        pltpu.SMEM((1,), I32)],
        compiler_params=_params(("parallel", "arbitrary")),
        name="attn",
    )(aq, ak, avt, iq, ik, iwt, bias, out_g.reshape(ATT_W, 1))


def _outproj_kernel(rec_ref, att_ref, x_ref, g1_ref, sc_ref, sh_ref, ng_ref, wo_ref, wr_ref, br_ref,
                    x1_ref, h2_ref, comb_ref):
    mix = _dot(rec_ref[...], wo_ref[0:REC_W, :]) + _dot(att_ref[...], wo_ref[REC_W:REC_W + ATT_W, :])
    x1 = x_ref[...] + g1_ref[...] * mix
    x1_ref[...] = x1
    ms = jnp.mean(x1 * x1, axis=-1, keepdims=True)
    h2 = (x1 * lax.rsqrt(ms + EPS) * ng_ref[...] * (1.0 + sc_ref[...]) + sh_ref[...]).astype(BF16)
    h2_ref[...] = h2

    lg = _dot_nt(wr_ref[...], h2) + br_ref[...]
    row = lax.broadcasted_iota(I32, lg.shape, 0)
    big = jnp.int32(2 * LANES)
    is_g = row < N_GROUPS
    gl = jnp.where(is_g, lg, -jnp.inf)
    gmax = jnp.max(gl, axis=0, keepdims=True)
    gate = 1.0 / jnp.sum(jnp.where(is_g, jnp.exp(lg - gmax), 0.0), axis=0, keepdims=True)
    gtop = jnp.min(jnp.where(gl == gmax, row, big), axis=0, keepdims=True)
    e_lo = ROUTE_OFF + EXPERTS_PER_GROUP * gtop
    el = jnp.where((row >= e_lo) & (row < e_lo + EXPERTS_PER_GROUP), lg, -jnp.inf)
    v1 = jnp.max(el, axis=0, keepdims=True)
    i1 = jnp.min(jnp.where(el == v1, row, big), axis=0, keepdims=True)
    el2 = jnp.where(row == i1, -jnp.inf, el)
    v2 = jnp.max(el2, axis=0, keepdims=True)
    i2 = jnp.min(jnp.where(el2 == v2, row, big), axis=0, keepdims=True)
    e2 = jnp.exp(v2 - v1)
    w1 = gate / (1.0 + e2)
    w2 = gate * e2 / (1.0 + e2)
    comb = jnp.where(row == i1, w1, 0.0) + jnp.where(row == i2, w2, 0.0)
    comb = jnp.where(row == 0, gtop.astype(F32), comb)
    pad = jnp.zeros((LANES - ROUTE_ROWS, comb.shape[1]), F32)
    comb_ref[...] = jnp.concatenate([comb, pad], axis=0).T


def _outproj(rec, att, x, g1, sc2, sh2, norm_g, w_out, w_rg, b_rg, w_re, b_re, *, tm):
    bsz, s, d = x.shape
    wr = jnp.zeros((ROUTE_ROWS, d), F32).at[:N_GROUPS].set(w_rg.T).at[ROUTE_OFF:ROUTE_OFF + N_EXPERTS].set(w_re.T)
    br = jnp.zeros((ROUTE_ROWS, 1), F32).at[:N_GROUPS, 0].set(b_rg).at[ROUTE_OFF:ROUTE_OFF + N_EXPERTS, 0].set(b_re)
    row = lambda w: pl.BlockSpec((None, tm, w), lambda b, i: (b, i, 0))
    vec = pl.BlockSpec((None, 1, d), lambda b, i: (b, 0, 0))
    return pl.pallas_call(
        _outproj_kernel,
        grid=(bsz, s // tm),
        in_specs=[row(REC_W), row(ATT_W), row(d), vec, vec, vec, _const_spec((1, d)),
                  _const_spec((REC_W + ATT_W, d)), _const_spec((ROUTE_ROWS, d)), _const_spec((ROUTE_ROWS, 1))],
        out_specs=(row(d), row(d), row(LANES)),
        out_shape=(jax.ShapeDtypeStruct((bsz, s, d), F32),
                   jax.ShapeDtypeStruct((bsz, s, d), BF16),
                   jax.ShapeDtypeStruct((bsz, s, LANES), F32)),
        compiler_params=_params(("parallel", "parallel")),
        name="outproj",
    )(rec, att, x, g1.reshape(bsz, 1, d), sc2.reshape(bsz, 1, d), sh2.reshape(bsz, 1, d),
      norm_g.reshape(1, d), w_out.astype(BF16), wr.astype(BF16), br)


MOE_ALIGN = 16
MOE_LANE_SHIFT = 32
MOE_CHUNK = 288
MOE_RB = 256


def _moe_kernel(h_ref, comb_ref, x1_ref, g2_ref, w1_ref, w3_ref, w2_ref, o_ref,
                p_ref, pt_ref, hs_ref, cs_ref, ys_ref, seg_ref, *, mc):
    e = pl.program_id(2)
    tm = h_ref.shape[0]
    npad = hs_ref.shape[0]

    @pl.when(e == 0)
    def _():
        comb = comb_ref[...]
        lane = lax.broadcasted_iota(I32, (tm, LANES), 1)
        gid = comb[:, 0:1].astype(I32)
        oh = jnp.where(lane == gid, 1.0, 0.0)
        ohb = oh.astype(BF16)
        pre = []
        for rb in range(tm // MOE_RB):
            r = rb * MOE_RB + lax.broadcasted_iota(I32, (MOE_RB, tm), 0)
            c = lax.broadcasted_iota(I32, (MOE_RB, tm), 1)
            pre.append(_dot(jnp.where(c < r, 1.0, 0.0).astype(BF16), ohb))
        prefix = jnp.concatenate(pre, axis=0)
        cnt = jnp.sum(oh, axis=0, keepdims=True).astype(I32)
        cnt_al = ((cnt + (MOE_ALIGN - 1)) // MOE_ALIGN) * MOE_ALIGN
        base = jnp.sum(jnp.where(lane < gid, cnt_al.astype(F32), 0.0), axis=-1, keepdims=True)
        rank = jnp.sum(prefix * oh, axis=-1, keepdims=True)
        pos = (base + rank).astype(I32)
        pos_row = jnp.broadcast_to(pos.astype(F32), (tm, LANES)).T[0:1, :].astype(I32)
        for rb in range(tm // MOE_RB):
            sl = slice(rb * MOE_RB, (rb + 1) * MOE_RB)
            coln = lax.broadcasted_iota(I32, (MOE_RB, npad), 1)
            pt_ref[sl, :] = jnp.where(coln == pos[sl], 1.0, 0.0).astype(BF16)
        for rb in range(npad // LANES):
            sl = slice(rb * LANES, (rb + 1) * LANES)
            rown = rb * LANES + lax.broadcasted_iota(I32, (LANES, tm), 0)
            p_ref[sl, :] = jnp.where(rown == pos_row, 1.0, 0.0).astype(BF16)
        c_hi = comb.astype(BF16)
        c_mid, c_lo = _split_bf16(comb - c_hi.astype(F32))
        packed = (c_hi.astype(F32) + pltpu.roll(c_mid.astype(F32), MOE_LANE_SHIFT, axis=1)
                  + pltpu.roll(c_lo.astype(F32), 2 * MOE_LANE_SHIFT, axis=1)).astype(BF16)
        srt = _dot(p_ref[...], jnp.concatenate([h_ref[...], packed], axis=1))
        d = h_ref.shape[1]
        hs_ref[...] = srt[:, :d].astype(BF16)
        cp = srt[:, d:]
        cs_ref[...] = (cp + pltpu.roll(cp, LANES - MOE_LANE_SHIFT, axis=1)
                       + pltpu.roll(cp, LANES - 2 * MOE_LANE_SHIFT, axis=1))
        ys_ref[...] = jnp.zeros_like(ys_ref)
        start = jnp.int32(0)
        for g in range(N_GROUPS):
            seg_ref[g] = start
            seg_ref[N_GROUPS + g] = cnt[0, g]
            start = start + cnt_al[0, g]

    grp = e // EXPERTS_PER_GROUP
    start = seg_ref[grp]
    n_rows = seg_ref[N_GROUPS + grp]

    def chunk(ci, carry):
        rs = pl.ds(pl.multiple_of(start + ci * mc, MOE_ALIGN), mc)
        hb = hs_ref[rs, :]
        cw = cs_ref[rs, :]
        lane = lax.broadcasted_iota(I32, cw.shape, 1)
        col = jnp.sum(jnp.where(lane == e + ROUTE_OFF, cw, 0.0), axis=-1, keepdims=True)
        he = _silu(_dot(hb, w1_ref[...])) * _dot(hb, w3_ref[...]) * col
        ys_ref[rs, :] += _dot(he.astype(BF16), w2_ref[...])
        return carry

    lax.fori_loop(0, (n_rows + mc - 1) // mc, chunk, 0)

    @pl.when(e == N_EXPERTS - 1)
    def _():
        o_ref[...] = x1_ref[...] + g2_ref[...] * _dot(pt_ref[...], ys_ref[...].astype(BF16))


def _moe(h2, comb, x1, g2, w1, w3, w2, *, tm):
    bsz, s, d = x1.shape
    mc = MOE_CHUNK
    npad = -(-(tm + N_GROUPS * MOE_ALIGN + mc) // LANES) * LANES
    row = lambda w, **kw: pl.BlockSpec((None, tm, w), lambda b, i, e: (b, i, 0), **kw)
    once = dict(pipeline_mode=pl.Buffered(1))
    return pl.pallas_call(
        functools.partial(_moe_kernel, mc=mc),
        grid=(bsz, s // tm, N_EXPERTS),
        in_specs=[row(d, **once), row(LANES), row(d, **once),
                  pl.BlockSpec((None, 1, d), lambda b, i, e: (b, 0, 0)),
                  pl.BlockSpec((None, d, D_EXPERT), lambda b, i, e: (e, 0, 0)),
                  pl.BlockSpec((None, d, D_EXPERT), lambda b, i, e: (e, 0, 0)),
                  pl.BlockSpec((None, D_EXPERT, d), lambda b, i, e: (e, 0, 0))],
        out_specs=row(d),
        out_shape=jax.ShapeDtypeStruct((bsz, s, d), F32),
        scratch_shapes=[pltpu.VMEM((npad, tm), BF16),
                        pltpu.VMEM((tm, npad), BF16),
                        pltpu.VMEM((npad, d), BF16),
                        pltpu.VMEM((npad, LANES), F32),
                        pltpu.VMEM((npad, d), F32),
                        pltpu.SMEM((2 * N_GROUPS,), I32)],
        compiler_params=_params(("parallel", "parallel", "arbitrary")),
        name="moe",
    )(h2, comb, x1, g2.reshape(bsz, 1, d), w1.astype(BF16), w3.astype(BF16), w2.astype(BF16))


def _tile(s, pref):
    t = min(s, pref)
    assert s % t == 0
    return t


def kernel(x, c, w_ada, b_ada, norm1_g, norm2_g, w_in, lb_logits, rec_out_g, q_norm_g, k_norm_g,
           idx_k_norm_g, idx_k_norm_b, attn_out_g, rel_bias, w_out, w_rg, b_rg, w_re, b_re, w1, w3, w2):
    bsz, s, d = x.shape
    depth = w_ada.shape[0]
    gk = _tile(s, 512)
    for l in range(depth):
        mod = _adaln(c, w_ada, b_ada[l], l)
        sh1, sc1, g1, sh2, sc2, g2 = jnp.split(mod, 6, axis=-1)
        q, f, v, g, aq, ak, avt, iq, ik, iwt = _inproj(
            x, sc1, sh1, norm1_g[l], w_in[l], lb_logits, q_norm_g[l], k_norm_g[l],
            idx_k_norm_g[l], idx_k_norm_b[l], layer=l, gt=gk, tm=_tile(s, 512))
        rec = _hgrn(q, f, v, g, rec_out_g[l], ts=_tile(s, 256))
        att = _attention(aq, ak, avt, iq, ik, iwt, rel_bias, attn_out_g[l], gk=gk)
        x1, h2, comb = _outproj(rec, att, x, g1, sc2, sh2, norm2_g[l], w_out[l],
                                w_rg[l], b_rg[l], w_re[l], b_re[l], tm=_tile(s, 512))
        x = _moe(h2, comb, x1, g2, w1[l], w3[l], w2[l], tm=_tile(s, 1024))
    return x
```

```python
import functools

import numpy as np
import jax
import jax.numpy as jnp
from jax import lax
from jax.experimental import pallas as pl
from jax.experimental.pallas import tpu as pltpu

F32 = jnp.float32
BF16 = jnp.bfloat16
I32 = jnp.int32

CHUNK = 64
QBLK = 128
EPS = 1e-6
REC_HEADS = 4
REC_DK = 128
REC_DV = 128
REC_W = REC_HEADS * REC_DV
ATT_HEADS = 8
ATT_DH = 64
ATT_W = ATT_HEADS * ATT_DH
IDX_HEADS = 8
IDX_DIM = 64
TOPK_MAX = 256
NUM_BUCKETS = 32
MAX_DISTANCE = 128
N_GROUPS = 4
EXPERTS_PER_GROUP = 4
N_EXPERTS = N_GROUPS * EXPERTS_PER_GROUP
D_EXPERT = 512

LANES = 128
V7X_VMEM_LIMIT = 56 * 1024 * 1024
F32_LOWEST = float(np.finfo(np.float32).min)
BISECT_FIRST = 17
BISECT_CAP = 320
NEG = -1e30
LOG2E = float(np.log2(np.e))
N_PAIRS = ATT_HEADS // 2
ROUTE_OFF = N_GROUPS
ROUTE_ROWS = 32


def _dot(a, b):
    return jnp.dot(a, b, preferred_element_type=F32)


def _dot_nt(a, b):
    return lax.dot_general(a, b, (((1,), (1,)), ((), ())), preferred_element_type=F32)


def _dot_tn(a, b):
    return lax.dot_general(a, b, (((0,), (0,)), ((), ())), preferred_element_type=F32)


def _split_bf16(a):
    hi = a.astype(BF16)
    lo = (a - hi.astype(F32)).astype(BF16)
    return hi, lo


def _silu(a):
    return a * jax.nn.sigmoid(a)


def _const_spec(shape):
    nd = len(shape)
    return pl.BlockSpec(shape, lambda *_: (0,) * nd, pipeline_mode=pl.Buffered(1))


def _params(sem):
    return pltpu.CompilerParams(dimension_semantics=sem, vmem_limit_bytes=V7X_VMEM_LIMIT)


def _adaln_kernel(c_ref, w_ref, b_ref, o_ref):
    a_hi, a_lo = _split_bf16(_silu(c_ref[...]))
    w_hi, w_lo = _split_bf16(w_ref[...])
    o_ref[...] = _dot(a_hi, w_hi) + _dot(a_lo, w_hi) + _dot(a_hi, w_lo) + b_ref[...]


def _adaln(c, w_all, b, layer):
    bsz, d = c.shape
    n = w_all.shape[2]
    rows = 16
    bn = 1024
    cp = jnp.zeros((rows, d), F32).at[:bsz].set(c)
    out = pl.pallas_call(
        _adaln_kernel,
        grid=(n // bn,),
        in_specs=[pl.BlockSpec((rows, d), lambda i: (0, 0)),
                  pl.BlockSpec((None, d, bn), lambda i: (layer, 0, i)),
                  pl.BlockSpec((1, bn), lambda i: (0, i))],
        out_specs=pl.BlockSpec((rows, bn), lambda i: (0, i)),
        out_shape=jax.ShapeDtypeStruct((rows, n), F32),
        compiler_params=_params(("parallel",)),
        name="adaln",
    )(cp, w_all, b.reshape(1, n))
    return out[:bsz]


def _inproj_kernel(x_ref, sc_ref, sh_ref, ng_ref, wmain_ref, wtail_ref, lbl_ref,
                   qg_ref, kg_ref, ikg_ref, ikb_ref, pm_ref,
                   q_ref, f_ref, v_ref, g_ref, aq_ref, ak_ref, avt_ref, iq_ref, ik_ref, iwt_ref, w_ref,
                   *, layer, gt):
    @pl.when((pl.program_id(0) == 0) & (pl.program_id(1) == 0))
    def _():
        n_main = wmain_ref.shape[1]
        w_ref[:, 0:n_main] = wmain_ref[...].astype(BF16)
        w_ref[:, n_main:] = wtail_ref[...].astype(BF16)

    x = x_ref[...]
    tm = x.shape[0]
    ms = jnp.mean(x * x, axis=-1, keepdims=True)
    h = x * lax.rsqrt(ms + EPS) * ng_ref[...] * (1.0 + sc_ref[...]) + sh_ref[...]
    hb = h.astype(BF16)

    n_rec, n_att = 4 * REC_W, 3 * ATT_W
    zr = _dot(hb, w_ref[:, 0:n_rec])
    q_ref[...] = _silu(zr[:, 0:REC_W]).astype(BF16)
    lbl = lbl_ref[...]
    e = jnp.exp(lbl - jnp.max(lbl, axis=0, keepdims=True))
    sm = e / jnp.sum(e, axis=0, keepdims=True)
    lb = jnp.sum(sm[0:layer + 1], axis=0, keepdims=True)
    f_ref[...] = lb + (1.0 - lb) * jax.nn.sigmoid(zr[:, REC_W:2 * REC_W])
    v_ref[...] = zr[:, 2 * REC_W:3 * REC_W].astype(BF16)
    g_ref[...] = _silu(zr[:, 3 * REC_W:4 * REC_W]).astype(BF16)

    za = _dot(hb, w_ref[:, n_rec:n_rec + n_att])
    aq = za[:, 0:ATT_W]
    ak = za[:, ATT_W:2 * ATT_W]
    av = za[:, 2 * ATT_W:3 * ATT_W]
    pm = pm_ref[...]
    aqn = aq * lax.rsqrt(_dot((aq * aq).astype(BF16), pm) + EPS) * qg_ref[...]
    akn = ak * lax.rsqrt(_dot((ak * ak).astype(BF16), pm) + EPS) * kg_ref[...]
    for j in range(N_PAIRS):
        aq_ref[j] = aqn[:, j * LANES:(j + 1) * LANES].astype(BF16)
        ak_ref[j] = akn[:, j * LANES:(j + 1) * LANES].astype(BF16)
    for t in range(tm // gt):
        avt_ref[t] = av[t * gt:(t + 1) * gt, :].T.astype(BF16)

    zi = _dot(hb, w_ref[:, n_rec + n_att:])
    for j in range(N_PAIRS):
        iq_ref[j] = zi[:, j * LANES:(j + 1) * LANES].astype(BF16)
    tail = zi[:, IDX_HEADS * IDX_DIM:IDX_HEADS * IDX_DIM + LANES]
    lane = lax.broadcasted_iota(I32, tail.shape, 1)
    is_k = lane < IDX_DIM
    mu = jnp.sum(jnp.where(is_k, tail, 0.0), axis=-1, keepdims=True) * (1.0 / IDX_DIM)
    dlt = jnp.where(is_k, tail - mu, 0.0)
    var = jnp.sum(dlt * dlt, axis=-1, keepdims=True) * (1.0 / IDX_DIM)
    ikn = dlt * lax.rsqrt(var + EPS) * ikg_ref[...] + ikb_ref[...]
    ik_ref[...] = jnp.where(is_k, ikn, pltpu.roll(ikn, IDX_DIM, axis=1)).astype(BF16)
    iwt_ref[...] = tail.T[IDX_DIM:IDX_DIM + IDX_HEADS, :] * (IDX_HEADS ** -0.5 * IDX_DIM ** -0.5)


def _inproj(x, sc1, sh1, norm_g, w_in, lb_logits, q_g, k_g, ik_g, ik_b, *, layer, gt, tm):
    bsz, s, d = x.shape
    n_rec = 4 * REC_W
    n_att = 3 * ATT_W
    n_idx = IDX_HEADS * IDX_DIM + LANES
    n_all = n_rec + n_att + n_idx
    n_main = n_rec + n_att
    w_tail = jnp.zeros((d, n_idx), F32).at[:, :w_in.shape[2] - n_main].set(w_in[layer, :, n_main:])
    pm = jnp.asarray(np.kron(np.eye(ATT_HEADS), np.full((ATT_DH, ATT_DH), 1.0 / ATT_DH)), BF16)
    qg = jnp.tile(q_g, ATT_HEADS).reshape(1, ATT_W) * (ATT_DH ** -0.5 * LOG2E)
    kg = jnp.tile(k_g, ATT_HEADS).reshape(1, ATT_W)
    ikg = jnp.zeros((1, LANES), F32).at[0, :IDX_DIM].set(ik_g)
    ikb = jnp.zeros((1, LANES), F32).at[0, :IDX_DIM].set(ik_b)
    nl = lb_logits.shape[0]

    row = lambda w: pl.BlockSpec((None, tm, w), lambda b, i: (b, i, 0))
    pair = pl.BlockSpec((None, N_PAIRS, tm, LANES), lambda b, i: (b, 0, i, 0))
    vec = pl.BlockSpec((None, 1, d), lambda b, i: (b, 0, 0))
    out_shapes = (
        jax.ShapeDtypeStruct((bsz, s, REC_W), BF16),
        jax.ShapeDtypeStruct((bsz, s, REC_W), F32),
        jax.ShapeDtypeStruct((bsz, s, REC_W), BF16),
        jax.ShapeDtypeStruct((bsz, s, REC_W), BF16),
        jax.ShapeDtypeStruct((bsz, N_PAIRS, s, LANES), BF16),
        jax.ShapeDtypeStruct((bsz, N_PAIRS, s, LANES), BF16),
        jax.ShapeDtypeStruct((bsz, s // gt, ATT_W, gt), BF16),
        jax.ShapeDtypeStruct((bsz, N_PAIRS, s, LANES), BF16),
        jax.ShapeDtypeStruct((bsz, s, LANES), BF16),
        jax.ShapeDtypeStruct((bsz, IDX_HEADS, s), F32),
    )
    out_specs = (
        row(REC_W), row(REC_W), row(REC_W), row(REC_W), pair, pair,
        pl.BlockSpec((None, tm // gt, ATT_W, gt), lambda b, i: (b, i, 0, 0)),
        pair, row(LANES),
        pl.BlockSpec((None, IDX_HEADS, tm), lambda b, i: (b, 0, i)),
    )
    return pl.pallas_call(
        functools.partial(_inproj_kernel, layer=layer, gt=gt),
        grid=(bsz, s // tm),
        in_specs=[row(d), vec, vec, _const_spec((1, d)),
                  pl.BlockSpec((None, d, n_main), lambda b, i: (layer, 0, 0), pipeline_mode=pl.Buffered(1)),
                  _const_spec((d, n_idx)),
                  _const_spec((nl, REC_W)), _const_spec((1, ATT_W)), _const_spec((1, ATT_W)),
                  _const_spec((1, LANES)), _const_spec((1, LANES)), _const_spec((ATT_W, ATT_W))],
        out_specs=out_specs,
        out_shape=out_shapes,
        scratch_shapes=[pltpu.VMEM((d, n_all), BF16)],
        compiler_params=_params(("arbitrary", "arbitrary")),
        name="inproj",
    )(x, sc1.reshape(bsz, 1, d), sh1.reshape(bsz, 1, d), norm_g.reshape(1, d),
      w_in, w_tail, lb_logits, qg, kg, ikg, ikb, pm)


N_LEVELS = 6


def _hgrn_tables():
    c = CHUNK
    w = np.zeros((N_LEVELS + 2, c, c), np.float32)
    am = np.zeros((N_LEVELS + 1, c, c), np.float32)
    t = np.arange(c)
    for m in range(N_LEVELS):
        hs = 1 << m
        blk = t // (2 * hs)
        upper = (t // hs) % 2 == 1
        ref = blk * 2 * hs + hs - 1
        for i in range(c):
            if upper[i]:
                w[m, i, ref[i] + 1:i + 1] = 1.0
            else:
                w[m, i, i + 1:ref[i] + 1] = 1.0
        am[m] = (blk[:, None] == blk[None, :]) & upper[:, None] & ~upper[None, :]
    w[N_LEVELS] = np.tril(np.ones((c, c)))
    w[N_LEVELS + 1] = np.triu(np.ones((c, c)), 1)
    am[N_LEVELS] = np.eye(c)
    w = w.reshape((N_LEVELS + 2) * c, c)
    return np.concatenate([w, w], axis=1), am


HGRN_UNROLL = 4


def _hgrn_kernel(q_ref, f_ref, v_ref, g_ref, og_ref, ww_ref, am_ref, o_ref, st_ref, ex_ref, *, n_chunks):
    @pl.when(pl.program_id(1) == 0)
    def _():
        st_ref[...] = jnp.zeros_like(st_ref)

    ww = ww_ref[...]
    c = CHUNK
    tbit = lax.broadcasted_iota(I32, (c, REC_DK), 0)
    items = [(cc, hd) for cc in range(HGRN_UNROLL) for hd in range(REC_HEADS)]

    def chunks(ci, carry):
        def blk(ref, it):
            r0 = pl.multiple_of((ci * HGRN_UNROLL + it[0]) * c, c)
            return ref.at[pl.ds(r0, c), it[1] * REC_DK:(it[1] + 1) * REC_DK]

        def ex(i, part, last_row=False):
            r0 = (part + 1) * c - 1 if last_row else part * c
            return ex_ref[items[i][0], r0:(part + 1) * c, items[i][1] * REC_DK:(items[i][1] + 1) * REC_DK]

        f = [blk(f_ref, it)[...] for it in items]
        q = [blk(q_ref, it)[...].astype(F32) for it in items]
        v = [blk(v_ref, it)[...] for it in items]
        k = [1.0 - fi for fi in f]
        for cc in range(HGRN_UNROLL):
            cols = [jnp.concatenate(_split_bf16(jnp.log(f[cc * REC_HEADS + hd])), axis=0) for hd in range(REC_HEADS)]
            ex_ref[cc] = jnp.exp(_dot(ww, jnp.concatenate(cols, axis=1)))
        a = [am_ref[N_LEVELS] * _dot_nt(q[i].astype(BF16), k[i].astype(BF16)) for i in range(len(items))]
        for m in range(N_LEVELS):
            upper = ((tbit >> m) & 1) == 1
            for i in range(len(items)):
                tm_ = (ex(i, m) * jnp.where(upper, q[i], k[i])).astype(BF16)
                a[i] = a[i] + am_ref[m] * _dot_nt(tm_, tm_)
        qb = [(q[i] * ex(i, N_LEVELS)).astype(BF16) for i in range(len(items))]
        kb = [(k[i] * ex(i, N_LEVELS + 1)).astype(BF16) for i in range(len(items))]
        intra = [_dot(a[i].astype(BF16), v[i]) for i in range(len(items))]
        ut = [_dot_tn(v[i], kb[i]) for i in range(len(items))]
        for hd in range(REC_HEADS):
            st = st_ref[hd]
            for cc in range(HGRN_UNROLL):
                i = cc * REC_HEADS + hd
                o = intra[i] + _dot_nt(qb[i], st.astype(BF16))
                st = st * ex(i, N_LEVELS, last_row=True) + ut[i]
                ms = jnp.mean(o * o, axis=-1, keepdims=True)
                gate = blk(g_ref, items[i])[...].astype(F32)
                y = o * lax.rsqrt(ms + EPS) * og_ref[:, hd * REC_DV:(hd + 1) * REC_DV] * gate
                blk(o_ref, items[i])[...] = y.astype(BF16)
            st_ref[hd] = st
        return carry

    lax.fori_loop(0, n_chunks // HGRN_UNROLL, chunks, 0)


def _hgrn(q, f, v, g, out_g, *, ts):
    bsz, s, _ = q.shape
    ww_np, am_np = _hgrn_tables()
    ww = jnp.asarray(ww_np, BF16)
    am = jnp.asarray(am_np, F32)
    row = pl.BlockSpec((None, ts, REC_W), lambda b, i: (b, i, 0))
    return pl.pallas_call(
        functools.partial(_hgrn_kernel, n_chunks=ts // CHUNK),
        grid=(bsz, s // ts),
        in_specs=[row, row, row, row, _const_spec((1, REC_W)),
                  _const_spec(ww.shape), _const_spec(am.shape)],
        out_specs=row,
        out_shape=jax.ShapeDtypeStruct((bsz, s, REC_W), BF16),
        scratch_shapes=[pltpu.VMEM((REC_HEADS, REC_DV, REC_DK), F32),
                        pltpu.VMEM((HGRN_UNROLL, (N_LEVELS + 2) * CHUNK, REC_HEADS * REC_DK), F32)],
        compiler_params=_params(("parallel", "arbitrary")),
        name="hgrn",
    )(q, f, v, g, out_g.reshape(1, REC_W), ww, am)


def _t5_bucket(rel):
    nb = NUM_BUCKETS // 2
    max_exact = nb // 2
    ret = jnp.where(rel > 0, nb, 0)
    n = jnp.abs(rel)
    nf = jnp.maximum(n, 1).astype(F32)
    large = max_exact + (jnp.log(nf / max_exact) / np.log(MAX_DISTANCE / max_exact)
                         * (nb - max_exact)).astype(I32)
    large = jnp.minimum(large, nb - 1)
    return ret + jnp.where(n < max_exact, n, large)


def _bias_slots(rel_bias):
    n_rel = 3 * QBLK
    rel = jnp.arange(n_rel, dtype=I32) - 2 * QBLK
    tab = rel_bias[_t5_bucket(rel)].T
    far = rel_bias[_t5_bucket(jnp.full((1,), -2 * QBLK - 1, I32))].T
    rev = tab[:, ::-1]
    skew = jnp.tile(rev, (1, 2 * QBLK))[:, :2 * QBLK * (n_rel - 1)].reshape(ATT_HEADS, 2 * QBLK, n_rel - 1)
    near = skew[:, :, 2 * QBLK - 1:]
    nb = ((near - far[:, :, None]) * LOG2E).reshape(ATT_HEADS, 2, QBLK, QBLK)
    z = jnp.zeros((ATT_HEADS, 1, QBLK, QBLK), F32)
    return jnp.concatenate([z, nb, z], axis=1)


def _attn_kernel(aq_ref, ak_ref, avt_ref, iq_ref, ik_ref, iwt_ref, bias_ref, og_ref, o_ref,
                 key_ref, mb_ref, thr_ref, acc_ref, m_ref, l_ref, al_ref, s_ref, mx_ref, p_ref, qbd_ref, ot_ref, tie_ref,
                 *, gs, gk, k_sel):
    bi = pl.program_id(1)
    n_s = bi // (gs // QBLK) + 1
    lane = lax.broadcasted_iota(I32, (QBLK, LANES), 1)

    def block_diag(xq):
        zero = jnp.zeros_like(xq)
        return jnp.concatenate([jnp.where(lane < ATT_DH, xq, zero), jnp.where(lane >= ATT_DH, xq, zero)], axis=0)

    def srows(g):
        return pl.ds(pl.multiple_of(g * gs, gs), gs)

    for j in range(N_PAIRS):
        qbd_ref[j] = block_diag(iq_ref[j])
    w = iwt_ref[...]

    def group_scores(g):
        sc = _dot_nt(ik_ref[srows(g), :], qbd_ref[...].reshape(N_PAIRS * 2 * QBLK, LANES))
        acc = jnp.zeros((gs, LANES), F32)
        for hd in range(IDX_HEADS):
            acc = acc + jnp.maximum(sc[:, hd * LANES:(hd + 1) * LANES], 0.0) * w[hd:hd + 1, :]
        return acc

    def stats(sc, c):
        s3 = sc.reshape(gs // 8, 8, LANES)
        return (jnp.minimum(c[0], jnp.min(jnp.where(s3 == -jnp.inf, jnp.inf, s3), axis=0)),
                jnp.maximum(c[1], jnp.max(s3, axis=0)),
                c[2] + jnp.sum(jnp.where(s3 >= 0.0, 1, 0), axis=0),
                c[3] + jnp.sum(jnp.where(s3 > 0.0, 1, 0), axis=0))

    def idx_body(g, carry):
        sc = group_scores(g)
        key_ref[srows(g), :] = sc
        return stats(sc, carry)

    def idx_pair(u, carry):
        return idx_body(2 * u + 1, idx_body(2 * u, carry))

    z8 = jnp.zeros((8, LANES), I32)
    st8 = lax.fori_loop(0, (n_s - 1) // 2, idx_pair,
                        (jnp.full((8, LANES), jnp.inf, F32), jnp.full((8, LANES), -jnp.inf, F32), z8, z8))
    st8 = lax.cond((n_s - 1) % 2 == 1, lambda c: idx_body(n_s - 2, c), lambda c: c, st8)

    last = srows(n_s - 1)
    spos = (n_s - 1) * gs + lax.broadcasted_iota(I32, (gs, LANES), 0)
    tpos = bi * QBLK + lax.broadcasted_iota(I32, (gs, LANES), 1)
    sc_last = jnp.where((spos // CHUNK) <= (tpos // CHUNK), group_scores(n_s - 1), -jnp.inf)
    key_ref[last, :] = sc_last
    mn8, mx8, ge8, gt8 = stats(sc_last, st8)

    thr_ref[...] = jnp.full(thr_ref.shape, F32_LOWEST, F32)
    tie_ref[0] = 0

    @pl.when((2 * bi + 2) * CHUNK > k_sel)
    def _():
        def count_ge(cand):
            def one(g):
                sc = key_ref[srows(g), :].reshape(gs // 8, 8, LANES)
                return jnp.sum(jnp.where(sc >= cand, 1, 0), axis=0)

            def two(u, acc):
                second = jnp.where(2 * u + 1 < n_s, 1, 0)
                return acc + one(2 * u) + one(jnp.minimum(2 * u + 1, n_s - 1)) * second

            acc = lax.fori_loop(0, (n_s + 1) // 2, two, jnp.zeros((8, LANES), I32))
            return jnp.sum(acc, axis=0, keepdims=True)

        c0_ge = jnp.sum(ge8, axis=0, keepdims=True)
        c0_gt = jnp.sum(gt8, axis=0, keepdims=True)
        mn = jnp.min(mn8, axis=0, keepdims=True)
        mx = jnp.max(mx8, axis=0, keepdims=True)
        above = mx + (jnp.abs(mx) * 2.0 ** -20 + 1e-30)
        n_adm = (2 * bi + 1 + lax.broadcasted_iota(I32, (1, LANES), 1) // CHUNK) * CHUNK
        pos = c0_gt >= k_sel
        non_neg = c0_ge >= k_sel
        lo = jnp.where(non_neg, 0.0, mn)
        cnt_lo = jnp.where(non_neg, c0_ge, n_adm)
        hi = jnp.where(pos, above, 0.0)

        def midpoint(lo, hi):
            return 0.5 * lo + 0.5 * hi

        def unsettled(lo, hi, cnt_lo):
            mid = midpoint(lo, hi)
            return jnp.where(cnt_lo > k_sel, jnp.where(mid > lo, jnp.where(mid < hi, 1.0, 0.0), 0.0), 0.0)

        def cond(st):
            return jnp.logical_and(st[3] > 0.5, st[4] < BISECT_CAP)

        def halve(lo, hi, cnt_lo):
            act = unsettled(lo, hi, cnt_lo) > 0.5
            mid = midpoint(lo, hi)
            c = count_ge(mid)
            up = jnp.logical_and(act, c >= k_sel)
            dn = jnp.logical_and(act, c < k_sel)
            return jnp.where(up, mid, lo), jnp.where(dn, mid, hi), jnp.where(up, c, cnt_lo)

        def body(st):
            lo, hi, cnt_lo = halve(*halve(*st[:3]))
            return lo, hi, cnt_lo, jnp.max(unsettled(lo, hi, cnt_lo)), st[4] + 2

        lo, hi, cnt_lo = lax.fori_loop(0, BISECT_FIRST, lambda _, s3: halve(*s3), (lo, hi, cnt_lo))
        st = lax.while_loop(cond, body,
                            (lo, hi, cnt_lo, jnp.max(unsettled(lo, hi, cnt_lo)), jnp.int32(BISECT_FIRST)))
        thr_ref[...] = jnp.broadcast_to(st[0], thr_ref.shape)
        tie_ref[0] = (jnp.max(jnp.where(st[2] > k_sel, 1.0, 0.0)) > 0.5).astype(I32)

    thr = thr_ref[0:1, :]

    @pl.when(tie_ref[0] == 0)
    def _():
        def mb_body(g, carry):
            mb_ref[srows(g), :] = jnp.where(key_ref[srows(g), :] >= thr, 0.0, NEG)
            return carry

        lax.fori_loop(0, n_s, mb_body, 0)

    @pl.when(tie_ref[0] != 0)
    def _():
        def count_gt(g, acc):
            sc = key_ref[srows(g), :].reshape(gs // 8, 8, LANES)
            return acc + jnp.sum(jnp.where(sc > thr, 1, 0), axis=0)

        n_gt = jnp.sum(lax.fori_loop(0, n_s, count_gt, jnp.zeros((8, LANES), I32)), axis=0, keepdims=True)
        room = (k_sel - n_gt).astype(F32)
        r = lax.broadcasted_iota(I32, (QBLK, QBLK), 0)
        c = lax.broadcasted_iota(I32, (QBLK, QBLK), 1)
        tri = jnp.where(c <= r, 1.0, 0.0).astype(BF16)

        def mb_body(g, seen):
            for t in range(gs // QBLK):
                rs = pl.ds(pl.multiple_of(g * gs + t * QBLK, QBLK), QBLK)
                sc = key_ref[rs, :]
                eq = jnp.where(sc == thr, 1.0, 0.0)
                rank = _dot(tri, eq.astype(BF16)) + seen
                keep = jnp.where(sc > thr, 1.0, jnp.where(rank <= room, eq, 0.0))
                mb_ref[rs, :] = jnp.where(keep > 0.5, 0.0, NEG)
                seen = seen + jnp.sum(eq, axis=0, keepdims=True)
            return seen

        lax.fori_loop(0, n_s, mb_body, jnp.zeros((1, LANES), F32))

    spg = gk // QBLK
    n_g = bi // spg + 1
    for j in range(N_PAIRS):
        qbd_ref[j] = block_diag(aq_ref[j])
    m_ref[...] = jnp.full(m_ref.shape, NEG, F32)
    l_ref[...] = jnp.zeros_like(l_ref)
    acc_ref[...] = jnp.zeros_like(acc_ref)
    al_ref[...] = jnp.ones_like(al_ref)
    p_ref[1] = jnp.zeros(p_ref.shape[1:], BF16)

    def rows(g):
        return pl.ds(pl.multiple_of(g * gk, gk), gk)

    def scores(t, slot):
        mb = mb_ref[rows(t), :]
        mb2 = jnp.concatenate([mb, mb], axis=1)
        for j in range(N_PAIRS):
            s = _dot_nt(ak_ref[j, rows(t), :], qbd_ref[j]) + mb2
            s_ref[slot, j] = s
            mx_ref[slot, j] = jnp.max(s, axis=0, keepdims=True)

    def pv(t, slot):
        return [_dot(avt_ref[t, j * LANES:(j + 1) * LANES, :], p_ref[slot, j]) for j in range(N_PAIRS)]

    def step(g, cur):
        prv = 1 - cur

        @pl.when(g >= n_g - 2)
        def _():
            for j in range(N_PAIRS):
                parts = []
                for st in range(spg):
                    slot = jnp.clip(g * spg + st - bi + 2, 0, 3)
                    parts.append(jnp.concatenate([bias_ref[2 * j, slot], bias_ref[2 * j + 1, slot]], axis=1))
                s = s_ref[cur, j] + jnp.concatenate(parts, axis=0)
                s_ref[cur, j] = s
                mx_ref[cur, j] = jnp.max(s, axis=0, keepdims=True)

        al_old = [al_ref[j] for j in range(N_PAIRS)]
        for j in range(N_PAIRS):
            m_old = m_ref[j]
            m_new = jnp.maximum(m_old, mx_ref[cur, j])
            p = jnp.exp2(s_ref[cur, j] - m_new)
            alpha = jnp.exp2(m_old - m_new)
            al_ref[j] = alpha
            l_ref[j] = l_ref[j] * alpha + jnp.sum(p, axis=0, keepdims=True)
            m_ref[j] = m_new
            p_ref[cur, j] = p.astype(BF16)
        o_prev = pv(jnp.maximum(g - 1, 0), prv)
        scores(jnp.minimum(g + 1, n_g - 1), prv)
        for j in range(N_PAIRS):
            acc_ref[j] = acc_ref[j] * al_old[j] + o_prev[j]

    scores(0, 0)

    def pipe_body(u, c):
        step(2 * u, 0)

        @pl.when(2 * u + 1 < n_g)
        def _():
            step(2 * u + 1, 1)
        return c

    lax.fori_loop(0, (n_g + 1) // 2, pipe_body, 0)
    o_last = pv(n_g - 1, (n_g - 1) % 2)

    for j in range(N_PAIRS):
        acc = acc_ref[j] * al_ref[j] + o_last[j]
        l = l_ref[j]
        o0 = acc[0:ATT_DH, 0:LANES] / l[:, 0:LANES]
        o1 = acc[ATT_DH:2 * ATT_DH, LANES:2 * LANES] / l[:, LANES:2 * LANES]
        ot_ref[j * LANES:(j + 1) * LANES, :] = jnp.concatenate([o0, o1], axis=0)

    ot = ot_ref[...]
    og = og_ref[...]
    outs = []
    for hd in range(ATT_HEADS):
        oh = ot[hd * ATT_DH:(hd + 1) * ATT_DH, :]
        ms = jnp.mean(oh * oh, axis=0, keepdims=True)
        outs.append(oh * lax.rsqrt(ms + EPS) * og[hd * ATT_DH:(hd + 1) * ATT_DH, :])
    o_ref[...] = jnp.concatenate(outs, axis=0).T.astype(BF16)


def _attention(aq, ak, avt, iq, ik, iwt, rel_bias, out_g, *, gk):
    bsz, _, s, _ = aq.shape
    k_sel = min(TOPK_MAX, s // 4)
    bias = _bias_slots(rel_bias)
    qpair = pl.BlockSpec((None, N_PAIRS, QBLK, LANES), lambda b, i: (b, 0, i, 0))
    full = lambda shp: pl.BlockSpec((None,) + shp, lambda b, i: (b,) + (0,) * len(shp),
                                    pipeline_mode=pl.Buffered(1))
    return pl.pallas_call(
        functools.partial(_attn_kernel, gs=_tile(s, 512), gk=gk, k_sel=k_sel),
        grid=(bsz, s // QBLK),
        in_specs=[qpair, full((N_PAIRS, s, LANES)), full((s // gk, ATT_W, gk)), qpair,
                  full((s, LANES)), pl.BlockSpec((None, IDX_HEADS, QBLK), lambda b, i: (b, 0, i)),
                  _const_spec(bias.shape), _const_spec((ATT_W, 1))],
        out_specs=pl.BlockSpec((None, QBLK, ATT_W), lambda b, i: (b, i, 0)),
        out_shape=jax.ShapeDtypeStruct((bsz, s, ATT_W), BF16),
        scratch_shapes=[pltpu.VMEM((s, LANES), F32),
                        pltpu.VMEM((s, LANES), F32),
                        pltpu.VMEM((8, LANES), F32),
                        pltpu.VMEM((N_PAIRS, 2 * ATT_DH, 2 * LANES), F32),
                        pltpu.VMEM((N_PAIRS, 1, 2 * LANES), F32),
                        pltpu.VMEM((N_PAIRS, 1, 2 * LANES), F32),
                        pltpu.VMEM((N_PAIRS, 1, 2 * LANES), F32),
                        pltpu.VMEM((2, N_PAIRS, gk, 2 * LANES), F32),
                        pltpu.VMEM((2, N_PAIRS, 1, 2 * LANES), F32),
                        pltpu.VMEM((2, N_PAIRS, gk, 2 * LANES), BF16),
                        pltpu.VMEM((N_PAIRS, 2 * QBLK, LANES), BF16),
                        pltpu.VMEM((ATT_W, LANES), F32),
                        pltpu.SMEM((1,), I32)],
        compiler_params=_params(("parallel", "arbitrary")),
        name="attn",
    )(aq, ak, avt, iq, ik, iwt, bias, out_g.reshape(ATT_W, 1))


def _outproj_kernel(rec_ref, att_ref, x_ref, g1_ref, sc_ref, sh_ref, ng_ref, wo_ref, wr_ref, br_ref,
                    x1_ref, h2_ref, comb_ref):
    mix = _dot(rec_ref[...], wo_ref[0:REC_W, :]) + _dot(att_ref[...], wo_ref[REC_W:REC_W + ATT_W, :])
    x1 = x_ref[...] + g1_ref[...] * mix
    x1_ref[...] = x1
    ms = jnp.mean(x1 * x1, axis=-1, keepdims=True)
    h2 = (x1 * lax.rsqrt(ms + EPS) * ng_ref[...] * (1.0 + sc_ref[...]) + sh_ref[...]).astype(BF16)
    h2_ref[...] = h2

    lg = _dot_nt(wr_ref[...], h2) + br_ref[...]
    row = lax.broadcasted_iota(I32, lg.shape, 0)
    big = jnp.int32(2 * LANES)
    is_g = row < N_GROUPS
    gl = jnp.where(is_g, lg, -jnp.inf)
    gmax = jnp.max(gl, axis=0, keepdims=True)
    gate = 1.0 / jnp.sum(jnp.where(is_g, jnp.exp(lg - gmax), 0.0), axis=0, keepdims=True)
    gtop = jnp.min(jnp.where(gl == gmax, row, big), axis=0, keepdims=True)
    e_lo = ROUTE_OFF + EXPERTS_PER_GROUP * gtop
    el = jnp.where((row >= e_lo) & (row < e_lo + EXPERTS_PER_GROUP), lg, -jnp.inf)
    v1 = jnp.max(el, axis=0, keepdims=True)
    i1 = jnp.min(jnp.where(el == v1, row, big), axis=0, keepdims=True)
    el2 = jnp.where(row == i1, -jnp.inf, el)
    v2 = jnp.max(el2, axis=0, keepdims=True)
    i2 = jnp.min(jnp.where(el2 == v2, row, big), axis=0, keepdims=True)
    e2 = jnp.exp(v2 - v1)
    w1 = gate / (1.0 + e2)
    w2 = gate * e2 / (1.0 + e2)
    comb = jnp.where(row == i1, w1, 0.0) + jnp.where(row == i2, w2, 0.0)
    comb = jnp.where(row == 0, gtop.astype(F32), comb)
    pad = jnp.zeros((LANES - ROUTE_ROWS, comb.shape[1]), F32)
    comb_ref[...] = jnp.concatenate([comb, pad], axis=0).T


def _outproj(rec, att, x, g1, sc2, sh2, norm_g, w_out, w_rg, b_rg, w_re, b_re, *, tm):
    bsz, s, d = x.shape
    wr = jnp.zeros((ROUTE_ROWS, d), F32).at[:N_GROUPS].set(w_rg.T).at[ROUTE_OFF:ROUTE_OFF + N_EXPERTS].set(w_re.T)
    br = jnp.zeros((ROUTE_ROWS, 1), F32).at[:N_GROUPS, 0].set(b_rg).at[ROUTE_OFF:ROUTE_OFF + N_EXPERTS, 0].set(b_re)
    row = lambda w: pl.BlockSpec((None, tm, w), lambda b, i: (b, i, 0))
    vec = pl.BlockSpec((None, 1, d), lambda b, i: (b, 0, 0))
    return pl.pallas_call(
        _outproj_kernel,
        grid=(bsz, s // tm),
        in_specs=[row(REC_W), row(ATT_W), row(d), vec, vec, vec, _const_spec((1, d)),
                  _const_spec((REC_W + ATT_W, d)), _const_spec((ROUTE_ROWS, d)), _const_spec((ROUTE_ROWS, 1))],
        out_specs=(row(d), row(d), row(LANES)),
        out_shape=(jax.ShapeDtypeStruct((bsz, s, d), F32),
                   jax.ShapeDtypeStruct((bsz, s, d), BF16),
                   jax.ShapeDtypeStruct((bsz, s, LANES), F32)),
        compiler_params=_params(("parallel", "parallel")),
        name="outproj",
    )(rec, att, x, g1.reshape(bsz, 1, d), sc2.reshape(bsz, 1, d), sh2.reshape(bsz, 1, d),
      norm_g.reshape(1, d), w_out.astype(BF16), wr.astype(BF16), br)


MOE_ALIGN = 16
MOE_LANE_SHIFT = 32
MOE_CHUNK = 288
MOE_RB = 256


def _moe_kernel(h_ref, comb_ref, x1_ref, g2_ref, w1_ref, w3_ref, w2_ref, o_ref,
                p_ref, pt_ref, hs_ref, cs_ref, ys_ref, seg_ref, *, mc):
    e = pl.program_id(2)
    tm = h_ref.shape[0]
    npad = hs_ref.shape[0]

    @pl.when(e == 0)
    def _():
        comb = comb_ref[...]
        lane = lax.broadcasted_iota(I32, (tm, LANES), 1)
        gid = comb[:, 0:1].astype(I32)
        oh = jnp.where(lane == gid, 1.0, 0.0)
        ohb = oh.astype(BF16)
        pre = []
        for rb in range(tm // MOE_RB):
            r = rb * MOE_RB + lax.broadcasted_iota(I32, (MOE_RB, tm), 0)
            c = lax.broadcasted_iota(I32, (MOE_RB, tm), 1)
            pre.append(_dot(jnp.where(c < r, 1.0, 0.0).astype(BF16), ohb))
        prefix = jnp.concatenate(pre, axis=0)
        cnt = jnp.sum(oh, axis=0, keepdims=True).astype(I32)
        cnt_al = ((cnt + (MOE_ALIGN - 1)) // MOE_ALIGN) * MOE_ALIGN
        base = jnp.sum(jnp.where(lane < gid, cnt_al.astype(F32), 0.0), axis=-1, keepdims=True)
        rank = jnp.sum(prefix * oh, axis=-1, keepdims=True)
        pos = (base + rank).astype(I32)
        pos_row = jnp.broadcast_to(pos.astype(F32), (tm, LANES)).T[0:1, :].astype(I32)
        for rb in range(tm // MOE_RB):
            sl = slice(rb * MOE_RB, (rb + 1) * MOE_RB)
            coln = lax.broadcasted_iota(I32, (MOE_RB, npad), 1)
            pt_ref[sl, :] = jnp.where(coln == pos[sl], 1.0, 0.0).astype(BF16)
        for rb in range(npad // LANES):
            sl = slice(rb * LANES, (rb + 1) * LANES)
            rown = rb * LANES + lax.broadcasted_iota(I32, (LANES, tm), 0)
            p_ref[sl, :] = jnp.where(rown == pos_row, 1.0, 0.0).astype(BF16)
        c_hi = comb.astype(BF16)
        c_mid, c_lo = _split_bf16(comb - c_hi.astype(F32))
        packed = (c_hi.astype(F32) + pltpu.roll(c_mid.astype(F32), MOE_LANE_SHIFT, axis=1)
                  + pltpu.roll(c_lo.astype(F32), 2 * MOE_LANE_SHIFT, axis=1)).astype(BF16)
        srt = _dot(p_ref[...], jnp.concatenate([h_ref[...], packed], axis=1))
        d = h_ref.shape[1]
        hs_ref[...] = srt[:, :d].astype(BF16)
        cp = srt[:, d:]
        cs_ref[...] = (cp + pltpu.roll(cp, LANES - MOE_LANE_SHIFT, axis=1)
                       + pltpu.roll(cp, LANES - 2 * MOE_LANE_SHIFT, axis=1))
        ys_ref[...] = jnp.zeros_like(ys_ref)
        start = jnp.int32(0)
        for g in range(N_GROUPS):
            seg_ref[g] = start
            seg_ref[N_GROUPS + g] = cnt[0, g]
            start = start + cnt_al[0, g]

    grp = e // EXPERTS_PER_GROUP
    start = seg_ref[grp]
    n_rows = seg_ref[N_GROUPS + grp]

    def chunk(ci, carry):
        rs = pl.ds(pl.multiple_of(start + ci * mc, MOE_ALIGN), mc)
        hb = hs_ref[rs, :]
        cw = cs_ref[rs, :]
        lane = lax.broadcasted_iota(I32, cw.shape, 1)
        col = jnp.sum(jnp.where(lane == e + ROUTE_OFF, cw, 0.0), axis=-1, keepdims=True)
        he = _silu(_dot(hb, w1_ref[...])) * _dot(hb, w3_ref[...]) * col
        ys_ref[rs, :] += _dot(he.astype(BF16), w2_ref[...])
        return carry

    lax.fori_loop(0, (n_rows + mc - 1) // mc, chunk, 0)

    @pl.when(e == N_EXPERTS - 1)
    def _():
        o_ref[...] = x1_ref[...] + g2_ref[...] * _dot(pt_ref[...], ys_ref[...].astype(BF16))


def _moe(h2, comb, x1, g2, w1, w3, w2, *, tm):
    bsz, s, d = x1.shape
    mc = MOE_CHUNK
    npad = -(-(tm + N_GROUPS * MOE_ALIGN + mc) // LANES) * LANES
    row = lambda w, **kw: pl.BlockSpec((None, tm, w), lambda b, i, e: (b, i, 0), **kw)
    once = dict(pipeline_mode=pl.Buffered(1))
    return pl.pallas_call(
        functools.partial(_moe_kernel, mc=mc),
        grid=(bsz, s // tm, N_EXPERTS),
        in_specs=[row(d, **once), row(LANES), row(d, **once),
                  pl.BlockSpec((None, 1, d), lambda b, i, e: (b, 0, 0)),
                  pl.BlockSpec((None, d, D_EXPERT), lambda b, i, e: (e, 0, 0)),
                  pl.BlockSpec((None, d, D_EXPERT), lambda b, i, e: (e, 0, 0)),
                  pl.BlockSpec((None, D_EXPERT, d), lambda b, i, e: (e, 0, 0))],
        out_specs=row(d),
        out_shape=jax.ShapeDtypeStruct((bsz, s, d), F32),
        scratch_shapes=[pltpu.VMEM((npad, tm), BF16),
                        pltpu.VMEM((tm, npad), BF16),
                        pltpu.VMEM((npad, d), BF16),
                        pltpu.VMEM((npad, LANES), F32),
                        pltpu.VMEM((npad, d), F32),
                        pltpu.SMEM((2 * N_GROUPS,), I32)],
        compiler_params=_params(("parallel", "parallel", "arbitrary")),
        name="moe",
    )(h2, comb, x1, g2.reshape(bsz, 1, d), w1.astype(BF16), w3.astype(BF16), w2.astype(BF16))


def _tile(s, pref):
    t = min(s, pref)
    assert s % t == 0
    return t


def kernel(x, c, w_ada, b_ada, norm1_g, norm2_g, w_in, lb_logits, rec_out_g, q_norm_g, k_norm_g,
           idx_k_norm_g, idx_k_norm_b, attn_out_g, rel_bias, w_out, w_rg, b_rg, w_re, b_re, w1, w3, w2):
    bsz, s, d = x.shape
    depth = w_ada.shape[0]
    gk = _tile(s, 512)
    for l in range(depth):
        mod = _adaln(c, w_ada, b_ada[l], l)
        sh1, sc1, g1, sh2, sc2, g2 = jnp.split(mod, 6, axis=-1)
        q, f, v, g, aq, ak, avt, iq, ik, iwt = _inproj(
            x, sc1, sh1, norm1_g[l], w_in, lb_logits, q_norm_g[l], k_norm_g[l],
            idx_k_norm_g[l], idx_k_norm_b[l], layer=l, gt=gk, tm=_tile(s, 512))
        rec = _hgrn(q, f, v, g, rec_out_g[l], ts=_tile(s, 256))
        att = _attention(aq, ak, avt, iq, ik, iwt, rel_bias, attn_out_g[l], gk=gk)
        x1, h2, comb = _outproj(rec, att, x, g1, sc2, sh2, norm2_g[l], w_out[l],
                                w_rg[l], b_rg[l], w_re[l], b_re[l], tm=_tile(s, 512))
        x = _moe(h2, comb, x1, g2, w1[l], w3[l], w2[l], tm=_tile(s, 1024))
    return x
```

```python
import functools

import numpy as np
import jax
import jax.numpy as jnp
from jax import lax
from jax.experimental import pallas as pl
from jax.experimental.pallas import tpu as pltpu

F32 = jnp.float32
BF16 = jnp.bfloat16
I32 = jnp.int32

CHUNK = 64
QBLK = 128
EPS = 1e-6
REC_HEADS = 4
REC_DK = 128
REC_DV = 128
REC_W = REC_HEADS * REC_DV
ATT_HEADS = 8
ATT_DH = 64
ATT_W = ATT_HEADS * ATT_DH
IDX_HEADS = 8
IDX_DIM = 64
TOPK_MAX = 256
NUM_BUCKETS = 32
MAX_DISTANCE = 128
N_GROUPS = 4
EXPERTS_PER_GROUP = 4
N_EXPERTS = N_GROUPS * EXPERTS_PER_GROUP
D_EXPERT = 512

LANES = 128
V7X_VMEM_LIMIT = 56 * 1024 * 1024
F32_LOWEST = float(np.finfo(np.float32).min)
HOSTED_GROUPS = 8
HOSTED_CHECK = 4
BISECT_FIRST = 17
BISECT_CAP = 320
NEG = -1e30
LOG2E = float(np.log2(np.e))
N_PAIRS = ATT_HEADS // 2
ROUTE_OFF = N_GROUPS
ROUTE_ROWS = 32


def _dot(a, b):
    return jnp.dot(a, b, preferred_element_type=F32)


def _dot_nt(a, b):
    return lax.dot_general(a, b, (((1,), (1,)), ((), ())), preferred_element_type=F32)


def _dot_tn(a, b):
    return lax.dot_general(a, b, (((0,), (0,)), ((), ())), preferred_element_type=F32)


def _split_bf16(a):
    hi = a.astype(BF16)
    lo = (a - hi.astype(F32)).astype(BF16)
    return hi, lo


def _silu(a):
    return a * jax.nn.sigmoid(a)


def _const_spec(shape):
    nd = len(shape)
    return pl.BlockSpec(shape, lambda *_: (0,) * nd, pipeline_mode=pl.Buffered(1))


def _params(sem):
    return pltpu.CompilerParams(dimension_semantics=sem, vmem_limit_bytes=V7X_VMEM_LIMIT)


def _adaln_kernel(c_ref, w_ref, b_ref, o_ref):
    a_hi, a_lo = _split_bf16(_silu(c_ref[...]))
    w_hi, w_lo = _split_bf16(w_ref[...])
    o_ref[...] = _dot(a_hi, w_hi) + _dot(a_lo, w_hi) + _dot(a_hi, w_lo) + b_ref[...]


def _adaln(c, w_all, b, layer):
    bsz, d = c.shape
    n = w_all.shape[2]
    rows = 16
    bn = 1024
    cp = jnp.zeros((rows, d), F32).at[:bsz].set(c)
    out = pl.pallas_call(
        _adaln_kernel,
        grid=(n // bn,),
        in_specs=[pl.BlockSpec((rows, d), lambda i: (0, 0)),
                  pl.BlockSpec((None, d, bn), lambda i: (layer, 0, i)),
                  pl.BlockSpec((1, bn), lambda i: (0, i))],
        out_specs=pl.BlockSpec((rows, bn), lambda i: (0, i)),
        out_shape=jax.ShapeDtypeStruct((rows, n), F32),
        compiler_params=_params(("parallel",)),
        name="adaln",
    )(cp, w_all, b.reshape(1, n))
    return out[:bsz]


def _inproj_kernel(x_ref, sc_ref, sh_ref, ng_ref, wmain_ref, wtail_ref, lbl_ref,
                   qg_ref, kg_ref, ikg_ref, ikb_ref, pm_ref,
                   q_ref, f_ref, v_ref, g_ref, aq_ref, ak_ref, avt_ref, iq_ref, ik_ref, iwt_ref, w_ref,
                   *, layer, gt):
    @pl.when((pl.program_id(0) == 0) & (pl.program_id(1) == 0))
    def _():
        n_main = wmain_ref.shape[1]
        w_ref[:, 0:n_main] = wmain_ref[...].astype(BF16)
        w_ref[:, n_main:] = wtail_ref[...].astype(BF16)

    x = x_ref[...]
    tm = x.shape[0]
    ms = jnp.mean(x * x, axis=-1, keepdims=True)
    h = x * lax.rsqrt(ms + EPS) * ng_ref[...] * (1.0 + sc_ref[...]) + sh_ref[...]
    hb = h.astype(BF16)

    n_rec, n_att = 4 * REC_W, 3 * ATT_W
    zr = _dot(hb, w_ref[:, 0:n_rec])
    q_ref[...] = _silu(zr[:, 0:REC_W]).astype(BF16)
    lbl = lbl_ref[...]
    e = jnp.exp(lbl - jnp.max(lbl, axis=0, keepdims=True))
    sm = e / jnp.sum(e, axis=0, keepdims=True)
    lb = jnp.sum(sm[0:layer + 1], axis=0, keepdims=True)
    f_ref[...] = lb + (1.0 - lb) * jax.nn.sigmoid(zr[:, REC_W:2 * REC_W])
    v_ref[...] = zr[:, 2 * REC_W:3 * REC_W].astype(BF16)
    g_ref[...] = _silu(zr[:, 3 * REC_W:4 * REC_W]).astype(BF16)

    za = _dot(hb, w_ref[:, n_rec:n_rec + n_att])
    aq = za[:, 0:ATT_W]
    ak = za[:, ATT_W:2 * ATT_W]
    av = za[:, 2 * ATT_W:3 * ATT_W]
    pm = pm_ref[...]
    aqn = aq * lax.rsqrt(_dot((aq * aq).astype(BF16), pm) + EPS) * qg_ref[...]
    akn = ak * lax.rsqrt(_dot((ak * ak).astype(BF16), pm) + EPS) * kg_ref[...]
    for j in range(N_PAIRS):
        aq_ref[j] = aqn[:, j * LANES:(j + 1) * LANES].astype(BF16)
        ak_ref[j] = akn[:, j * LANES:(j + 1) * LANES].astype(BF16)
    for t in range(tm // gt):
        avt_ref[t] = av[t * gt:(t + 1) * gt, :].T.astype(BF16)

    zi = _dot(hb, w_ref[:, n_rec + n_att:])
    for j in range(N_PAIRS):
        iq_ref[j] = zi[:, j * LANES:(j + 1) * LANES].astype(BF16)
    tail = zi[:, IDX_HEADS * IDX_DIM:IDX_HEADS * IDX_DIM + LANES]
    lane = lax.broadcasted_iota(I32, tail.shape, 1)
    is_k = lane < IDX_DIM
    mu = jnp.sum(jnp.where(is_k, tail, 0.0), axis=-1, keepdims=True) * (1.0 / IDX_DIM)
    dlt = jnp.where(is_k, tail - mu, 0.0)
    var = jnp.sum(dlt * dlt, axis=-1, keepdims=True) * (1.0 / IDX_DIM)
    ikn = dlt * lax.rsqrt(var + EPS) * ikg_ref[...] + ikb_ref[...]
    ik_ref[...] = jnp.where(is_k, ikn, pltpu.roll(ikn, IDX_DIM, axis=1)).astype(BF16)
    iwt_ref[...] = tail.T[IDX_DIM:IDX_DIM + IDX_HEADS, :] * (IDX_HEADS ** -0.5 * IDX_DIM ** -0.5)


def _inproj(x, sc1, sh1, norm_g, w_in, lb_logits, q_g, k_g, ik_g, ik_b, *, layer, gt, tm):
    bsz, s, d = x.shape
    n_rec = 4 * REC_W
    n_att = 3 * ATT_W
    n_idx = IDX_HEADS * IDX_DIM + LANES
    n_all = n_rec + n_att + n_idx
    n_main = n_rec + n_att
    w_tail = jnp.zeros((d, n_idx), F32).at[:, :w_in.shape[2] - n_main].set(w_in[layer, :, n_main:])
    pm = jnp.asarray(np.kron(np.eye(ATT_HEADS), np.full((ATT_DH, ATT_DH), 1.0 / ATT_DH)), BF16)
    qg = jnp.tile(q_g, ATT_HEADS).reshape(1, ATT_W) * (ATT_DH ** -0.5 * LOG2E)
    kg = jnp.tile(k_g, ATT_HEADS).reshape(1, ATT_W)
    ikg = jnp.zeros((1, LANES), F32).at[0, :IDX_DIM].set(ik_g)
    ikb = jnp.zeros((1, LANES), F32).at[0, :IDX_DIM].set(ik_b)
    nl = lb_logits.shape[0]

    row = lambda w: pl.BlockSpec((None, tm, w), lambda b, i: (b, i, 0))
    pair = pl.BlockSpec((None, N_PAIRS, tm, LANES), lambda b, i: (b, 0, i, 0))
    vec = pl.BlockSpec((None, 1, d), lambda b, i: (b, 0, 0))
    out_shapes = (
        jax.ShapeDtypeStruct((bsz, s, REC_W), BF16),
        jax.ShapeDtypeStruct((bsz, s, REC_W), F32),
        jax.ShapeDtypeStruct((bsz, s, REC_W), BF16),
        jax.ShapeDtypeStruct((bsz, s, REC_W), BF16),
        jax.ShapeDtypeStruct((bsz, N_PAIRS, s, LANES), BF16),
        jax.ShapeDtypeStruct((bsz, N_PAIRS, s, LANES), BF16),
        jax.ShapeDtypeStruct((bsz, s // gt, ATT_W, gt), BF16),
        jax.ShapeDtypeStruct((bsz, N_PAIRS, s, LANES), BF16),
        jax.ShapeDtypeStruct((bsz, s, LANES), BF16),
        jax.ShapeDtypeStruct((bsz, IDX_HEADS, s), F32),
    )
    out_specs = (
        row(REC_W), row(REC_W), row(REC_W), row(REC_W), pair, pair,
        pl.BlockSpec((None, tm // gt, ATT_W, gt), lambda b, i: (b, i, 0, 0)),
        pair, row(LANES),
        pl.BlockSpec((None, IDX_HEADS, tm), lambda b, i: (b, 0, i)),
    )
    return pl.pallas_call(
        functools.partial(_inproj_kernel, layer=layer, gt=gt),
        grid=(bsz, s // tm),
        in_specs=[row(d), vec, vec, _const_spec((1, d)),
                  pl.BlockSpec((None, d, n_main), lambda b, i: (layer, 0, 0), pipeline_mode=pl.Buffered(1)),
                  _const_spec((d, n_idx)),
                  _const_spec((nl, REC_W)), _const_spec((1, ATT_W)), _const_spec((1, ATT_W)),
                  _const_spec((1, LANES)), _const_spec((1, LANES)), _const_spec((ATT_W, ATT_W))],
        out_specs=out_specs,
        out_shape=out_shapes,
        scratch_shapes=[pltpu.VMEM((d, n_all), BF16)],
        compiler_params=_params(("arbitrary", "arbitrary")),
        name="inproj",
    )(x, sc1.reshape(bsz, 1, d), sh1.reshape(bsz, 1, d), norm_g.reshape(1, d),
      w_in, w_tail, lb_logits, qg, kg, ikg, ikb, pm)


N_LEVELS = 6


def _hgrn_tables():
    c = CHUNK
    w = np.zeros((N_LEVELS + 2, c, c), np.float32)
    am = np.zeros((N_LEVELS + 1, c, c), np.float32)
    t = np.arange(c)
    for m in range(N_LEVELS):
        hs = 1 << m
        blk = t // (2 * hs)
        upper = (t // hs) % 2 == 1
        ref = blk * 2 * hs + hs - 1
        for i in range(c):
            if upper[i]:
                w[m, i, ref[i] + 1:i + 1] = 1.0
            else:
                w[m, i, i + 1:ref[i] + 1] = 1.0
        am[m] = (blk[:, None] == blk[None, :]) & upper[:, None] & ~upper[None, :]
    w[N_LEVELS] = np.tril(np.ones((c, c)))
    w[N_LEVELS + 1] = np.triu(np.ones((c, c)), 1)
    am[N_LEVELS] = np.eye(c)
    w = w.reshape((N_LEVELS + 2) * c, c)
    return np.concatenate([w, w], axis=1), am


HGRN_UNROLL = 4


def _hgrn_kernel(q_ref, f_ref, v_ref, g_ref, og_ref, ww_ref, am_ref, o_ref, st_ref, ex_ref, *, n_chunks):
    @pl.when(pl.program_id(1) == 0)
    def _():
        st_ref[...] = jnp.zeros_like(st_ref)

    ww = ww_ref[...]
    c = CHUNK
    tbit = lax.broadcasted_iota(I32, (c, REC_DK), 0)
    items = [(cc, hd) for cc in range(HGRN_UNROLL) for hd in range(REC_HEADS)]

    def chunks(ci, carry):
        def blk(ref, it):
            r0 = pl.multiple_of((ci * HGRN_UNROLL + it[0]) * c, c)
            return ref.at[pl.ds(r0, c), it[1] * REC_DK:(it[1] + 1) * REC_DK]

        def ex(i, part, last_row=False):
            r0 = (part + 1) * c - 1 if last_row else part * c
            return ex_ref[items[i][0], r0:(part + 1) * c, items[i][1] * REC_DK:(items[i][1] + 1) * REC_DK]

        f = [blk(f_ref, it)[...] for it in items]
        q = [blk(q_ref, it)[...].astype(F32) for it in items]
        v = [blk(v_ref, it)[...] for it in items]
        k = [1.0 - fi for fi in f]
        for cc in range(HGRN_UNROLL):
            cols = [jnp.concatenate(_split_bf16(jnp.log(f[cc * REC_HEADS + hd])), axis=0) for hd in range(REC_HEADS)]
            ex_ref[cc] = jnp.exp(_dot(ww, jnp.concatenate(cols, axis=1)))
        a = [am_ref[N_LEVELS] * _dot_nt(q[i].astype(BF16), k[i].astype(BF16)) for i in range(len(items))]
        for m in range(N_LEVELS):
            upper = ((tbit >> m) & 1) == 1
            for i in range(len(items)):
                tm_ = (ex(i, m) * jnp.where(upper, q[i], k[i])).astype(BF16)
                a[i] = a[i] + am_ref[m] * _dot_nt(tm_, tm_)
        qb = [(q[i] * ex(i, N_LEVELS)).astype(BF16) for i in range(len(items))]
        kb = [(k[i] * ex(i, N_LEVELS + 1)).astype(BF16) for i in range(len(items))]
        intra = [_dot(a[i].astype(BF16), v[i]) for i in range(len(items))]
        ut = [_dot_tn(v[i], kb[i]) for i in range(len(items))]
        for hd in range(REC_HEADS):
            st = st_ref[hd]
            for cc in range(HGRN_UNROLL):
                i = cc * REC_HEADS + hd
                o = intra[i] + _dot_nt(qb[i], st.astype(BF16))
                st = st * ex(i, N_LEVELS, last_row=True) + ut[i]
                ms = jnp.mean(o * o, axis=-1, keepdims=True)
                gate = blk(g_ref, items[i])[...].astype(F32)
                y = o * lax.rsqrt(ms + EPS) * og_ref[:, hd * REC_DV:(hd + 1) * REC_DV] * gate
                blk(o_ref, items[i])[...] = y.astype(BF16)
            st_ref[hd] = st
        return carry

    lax.fori_loop(0, n_chunks // HGRN_UNROLL, chunks, 0)


def _hgrn(q, f, v, g, out_g, *, ts):
    bsz, s, _ = q.shape
    ww_np, am_np = _hgrn_tables()
    ww = jnp.asarray(ww_np, BF16)
    am = jnp.asarray(am_np, F32)
    row = pl.BlockSpec((None, ts, REC_W), lambda b, i: (b, i, 0))
    return pl.pallas_call(
        functools.partial(_hgrn_kernel, n_chunks=ts // CHUNK),
        grid=(bsz, s // ts),
        in_specs=[row, row, row, row, _const_spec((1, REC_W)),
                  _const_spec(ww.shape), _const_spec(am.shape)],
        out_specs=row,
        out_shape=jax.ShapeDtypeStruct((bsz, s, REC_W), BF16),
        scratch_shapes=[pltpu.VMEM((REC_HEADS, REC_DV, REC_DK), F32),
                        pltpu.VMEM((HGRN_UNROLL, (N_LEVELS + 2) * CHUNK, REC_HEADS * REC_DK), F32)],
        compiler_params=_params(("parallel", "arbitrary")),
        name="hgrn",
    )(q, f, v, g, out_g.reshape(1, REC_W), ww, am)


def _t5_bucket(rel):
    nb = NUM_BUCKETS // 2
    max_exact = nb // 2
    ret = jnp.where(rel > 0, nb, 0)
    n = jnp.abs(rel)
    nf = jnp.maximum(n, 1).astype(F32)
    large = max_exact + (jnp.log(nf / max_exact) / np.log(MAX_DISTANCE / max_exact)
                         * (nb - max_exact)).astype(I32)
    large = jnp.minimum(large, nb - 1)
    return ret + jnp.where(n < max_exact, n, large)


def _bias_slots(rel_bias):
    n_rel = 3 * QBLK
    rel = jnp.arange(n_rel, dtype=I32) - 2 * QBLK
    tab = rel_bias[_t5_bucket(rel)].T
    far = rel_bias[_t5_bucket(jnp.full((1,), -2 * QBLK - 1, I32))].T
    rev = tab[:, ::-1]
    skew = jnp.tile(rev, (1, 2 * QBLK))[:, :2 * QBLK * (n_rel - 1)].reshape(ATT_HEADS, 2 * QBLK, n_rel - 1)
    near = skew[:, :, 2 * QBLK - 1:]
    nb = ((near - far[:, :, None]) * LOG2E).reshape(ATT_HEADS, 2, QBLK, QBLK)
    z = jnp.zeros((ATT_HEADS, 1, QBLK, QBLK), F32)
    return jnp.concatenate([z, nb, z], axis=1)


def _attn_kernel(aq_ref, ak_ref, avt_ref, iq_ref, iqn_ref, ik_ref, iwt_ref, iwtn_ref, bias_ref, og_ref, o_ref,
                 key_ref, mb_ref, thr_ref, bs_ref, hacc_ref, acc_ref, m_ref, l_ref, al_ref, s_ref, mx_ref, p_ref,
                 qbd_ref, ot_ref, tie_ref, cur_ref, *, gs, gk, k_sel):
    bi = pl.program_id(1)
    n_blocks = pl.num_programs(1)
    spg = gs // QBLK
    cur = bi % 2
    nxt = 1 - cur
    has_next = bi + 1 < n_blocks
    lane = lax.broadcasted_iota(I32, (QBLK, LANES), 1)

    def block_diag(xq):
        zero = jnp.zeros_like(xq)
        return jnp.concatenate([jnp.where(lane < ATT_DH, xq, zero), jnp.where(lane >= ATT_DH, xq, zero)], axis=0)

    def srows(g):
        return pl.ds(pl.multiple_of(g * gs, gs), gs)

    def n_groups(blk):
        return blk // spg + 1

    def midpoint(lo, hi):
        return 0.5 * lo + 0.5 * hi

    def unsettled(lo, hi, cnt_lo):
        mid = midpoint(lo, hi)
        return jnp.where(cnt_lo > k_sel, jnp.where(mid > lo, jnp.where(mid < hi, 1.0, 0.0), 0.0), 0.0)

    def group_count(slot, g, cand):
        sc = key_ref[slot, srows(g), :].reshape(gs // 8, 8, LANES)
        return jnp.sum(jnp.where(sc >= cand, 1, 0), axis=0)

    def update(lo, hi, cnt_lo, mid, cnt, live):
        act = jnp.logical_and(unsettled(lo, hi, cnt_lo) > 0.5, live)
        up = jnp.logical_and(act, cnt >= k_sel)
        dn = jnp.logical_and(act, cnt < k_sel)
        return jnp.where(up, mid, lo), jnp.where(dn, mid, hi), jnp.where(up, cnt, cnt_lo)

    def score_tile(blk, slot, q_ref, w_ref):
        n_s = n_groups(blk)
        for j in range(N_PAIRS):
            qbd_ref[j] = block_diag(q_ref[j])
        w = w_ref[...]

        def group_scores(g):
            sc = _dot_nt(ik_ref[srows(g), :], qbd_ref[...].reshape(N_PAIRS * 2 * QBLK, LANES))
            acc = jnp.zeros((gs, LANES), F32)
            for hd in range(IDX_HEADS):
                acc = acc + jnp.maximum(sc[:, hd * LANES:(hd + 1) * LANES], 0.0) * w[hd:hd + 1, :]
            return acc

        def stats(sc, c):
            s3 = sc.reshape(gs // 8, 8, LANES)
            return (jnp.minimum(c[0], jnp.min(jnp.where(s3 == -jnp.inf, jnp.inf, s3), axis=0)),
                    jnp.maximum(c[1], jnp.max(s3, axis=0)),
                    c[2] + jnp.sum(jnp.where(s3 >= 0.0, 1, 0), axis=0),
                    c[3] + jnp.sum(jnp.where(s3 > 0.0, 1, 0), axis=0))

        def idx_body(g, carry):
            sc = group_scores(g)
            key_ref[slot, srows(g), :] = sc
            return stats(sc, carry)

        def idx_pair(u, carry):
            return idx_body(2 * u + 1, idx_body(2 * u, carry))

        z8 = jnp.zeros((8, LANES), I32)
        st8 = lax.fori_loop(0, (n_s - 1) // 2, idx_pair,
                            (jnp.full((8, LANES), jnp.inf, F32), jnp.full((8, LANES), -jnp.inf, F32), z8, z8))
        st8 = lax.cond((n_s - 1) % 2 == 1, lambda c: idx_body(n_s - 2, c), lambda c: c, st8)

        spos = (n_s - 1) * gs + lax.broadcasted_iota(I32, (gs, LANES), 0)
        tpos = blk * QBLK + lax.broadcasted_iota(I32, (gs, LANES), 1)
        sc_last = jnp.where((spos // CHUNK) <= (tpos // CHUNK), group_scores(n_s - 1), -jnp.inf)
        key_ref[slot, srows(n_s - 1), :] = sc_last
        mn8, mx8, ge8, gt8 = stats(sc_last, st8)

        c0_ge = jnp.sum(ge8, axis=0, keepdims=True)
        c0_gt = jnp.sum(gt8, axis=0, keepdims=True)
        mn = jnp.min(mn8, axis=0, keepdims=True)
        mx = jnp.max(mx8, axis=0, keepdims=True)
        above = mx + (jnp.abs(mx) * 2.0 ** -20 + 1e-30)
        n_adm = (2 * blk + 1 + lax.broadcasted_iota(I32, (1, LANES), 1) // CHUNK) * CHUNK
        pos = c0_gt >= k_sel
        non_neg = c0_ge >= k_sel
        bs_ref[0] = jnp.broadcast_to(jnp.where(non_neg, 0.0, mn), (8, LANES))
        bs_ref[1] = jnp.broadcast_to(jnp.where(pos, above, 0.0), (8, LANES))
        bs_ref[2] = jnp.broadcast_to(jnp.where(non_neg, c0_ge, n_adm).astype(F32), (8, LANES))

    def finish_threshold(blk, slot, done):
        n_s = n_groups(blk)

        def count_ge(cand):
            def two(u, acc):
                second = jnp.where(2 * u + 1 < n_s, 1, 0)
                return (acc + group_count(slot, 2 * u, cand)
                        + group_count(slot, jnp.minimum(2 * u + 1, n_s - 1), cand) * second)

            acc = lax.fori_loop(0, (n_s + 1) // 2, two, jnp.zeros((8, LANES), I32))
            return jnp.sum(acc, axis=0, keepdims=True).astype(F32)

        def halve(lo, hi, cnt_lo):
            mid = midpoint(lo, hi)
            return update(lo, hi, cnt_lo, mid, count_ge(mid), True)

        def cond(st):
            return jnp.logical_and(st[3] > 0.5, st[4] < BISECT_CAP)

        def body(st):
            lo, hi, cnt_lo = halve(*halve(*st[:3]))
            return lo, hi, cnt_lo, jnp.max(unsettled(lo, hi, cnt_lo)), st[4] + 2

        st3 = (bs_ref[0][0:1], bs_ref[1][0:1], bs_ref[2][0:1])
        st3 = lax.fori_loop(0, jnp.maximum(BISECT_FIRST - done, 0), lambda _, s3: halve(*s3), st3)
        st = lax.while_loop(cond, body, st3 + (jnp.max(unsettled(*st3)), jnp.int32(BISECT_FIRST)))
        thr_ref[...] = jnp.broadcast_to(st[0], thr_ref.shape)
        tie_ref[0] = (jnp.max(jnp.where(st[2] > k_sel, 1.0, 0.0)) > 0.5).astype(I32)

    @pl.when(bi == 0)
    def _():
        score_tile(bi, cur, iq_ref, iwt_ref)
        finish_threshold(bi, cur, 0)

    n_s = n_groups(bi)
    thr = thr_ref[0:1, :]

    @pl.when(tie_ref[0] == 0)
    def _():
        def mb_body(g, carry):
            mb_ref[srows(g), :] = jnp.where(key_ref[cur, srows(g), :] >= thr, 0.0, NEG)
            return carry

        lax.fori_loop(0, n_s, mb_body, 0)

    @pl.when(tie_ref[0] != 0)
    def _():
        def count_gt(g, acc):
            sc = key_ref[cur, srows(g), :].reshape(gs // 8, 8, LANES)
            return acc + jnp.sum(jnp.where(sc > thr, 1, 0), axis=0)

        n_gt = jnp.sum(lax.fori_loop(0, n_s, count_gt, jnp.zeros((8, LANES), I32)), axis=0, keepdims=True)
        room = (k_sel - n_gt).astype(F32)
        r = lax.broadcasted_iota(I32, (QBLK, QBLK), 0)
        c = lax.broadcasted_iota(I32, (QBLK, QBLK), 1)
        tri = jnp.where(c <= r, 1.0, 0.0).astype(BF16)

        def mb_body(g, seen):
            for t in range(gs // QBLK):
                rs = pl.ds(pl.multiple_of(g * gs + t * QBLK, QBLK), QBLK)
                sc = key_ref[cur, rs, :]
                eq = jnp.where(sc == thr, 1.0, 0.0)
                rank = _dot(tri, eq.astype(BF16)) + seen
                keep = jnp.where(sc > thr, 1.0, jnp.where(rank <= room, eq, 0.0))
                mb_ref[rs, :] = jnp.where(keep > 0.5, 0.0, NEG)
                seen = seen + jnp.sum(eq, axis=0, keepdims=True)
            return seen

        lax.fori_loop(0, n_s, mb_body, jnp.zeros((1, LANES), F32))

    @pl.when(has_next)
    def _():
        score_tile(bi + 1, nxt, iqn_ref, iwtn_ref)

    @pl.when(jnp.logical_not(has_next))
    def _():
        bs_ref[...] = jnp.zeros_like(bs_ref)

    n_h = n_groups(jnp.minimum(bi + 1, n_blocks - 1))
    hacc_ref[...] = jnp.zeros_like(hacc_ref)
    cur_ref[0] = 0
    cur_ref[1] = 0

    def hosted_counts():
        for _ in range(HOSTED_GROUPS // HOSTED_CHECK):
            lo, hi, cnt_lo = bs_ref[0][0:1], bs_ref[1][0:1], bs_ref[2][0:1]
            mid = midpoint(lo, hi)
            c0 = cur_ref[0]
            acc = hacc_ref[...]
            for t in range(HOSTED_CHECK):
                g = c0 + t
                acc = acc + group_count(nxt, jnp.minimum(g, n_h - 1), mid) * jnp.where(g < n_h, 1, 0)
            c1 = c0 + HOSTED_CHECK
            full = c1 >= n_h
            cnt = jnp.sum(acc, axis=0, keepdims=True).astype(F32)
            lo, hi, cnt_lo = update(lo, hi, cnt_lo, mid, cnt, full)
            bs_ref[0] = jnp.broadcast_to(lo, (8, LANES))
            bs_ref[1] = jnp.broadcast_to(hi, (8, LANES))
            bs_ref[2] = jnp.broadcast_to(cnt_lo, (8, LANES))
            hacc_ref[...] = jnp.where(full, jnp.zeros_like(acc), acc)
            cur_ref[0] = jnp.where(full, 0, c1)
            cur_ref[1] = cur_ref[1] + jnp.where(full, 1, 0)

    n_g = bi // (gk // QBLK) + 1
    for j in range(N_PAIRS):
        qbd_ref[j] = block_diag(aq_ref[j])
    m_ref[...] = jnp.full(m_ref.shape, NEG, F32)
    l_ref[...] = jnp.zeros_like(l_ref)
    acc_ref[...] = jnp.zeros_like(acc_ref)
    al_ref[...] = jnp.ones_like(al_ref)
    p_ref[1] = jnp.zeros(p_ref.shape[1:], BF16)

    def rows(g):
        return pl.ds(pl.multiple_of(g * gk, gk), gk)

    def scores(t, slot):
        mb = mb_ref[rows(t), :]
        mb2 = jnp.concatenate([mb, mb], axis=1)
        for j in range(N_PAIRS):
            s = _dot_nt(ak_ref[j, rows(t), :], qbd_ref[j]) + mb2
            s_ref[slot, j] = s
            mx_ref[slot, j] = jnp.max(s, axis=0, keepdims=True)

    def pv(t, slot):
        return [_dot(avt_ref[t, j * LANES:(j + 1) * LANES, :], p_ref[slot, j]) for j in range(N_PAIRS)]

    def step(g, sl):
        prv = 1 - sl

        @pl.when(g >= n_g - 2)
        def _():
            for j in range(N_PAIRS):
                parts = []
                for st in range(gk // QBLK):
                    slot = jnp.clip(g * (gk // QBLK) + st - bi + 2, 0, 3)
                    parts.append(jnp.concatenate([bias_ref[2 * j, slot], bias_ref[2 * j + 1, slot]], axis=1))
                s = s_ref[sl, j] + jnp.concatenate(parts, axis=0)
                s_ref[sl, j] = s
                mx_ref[sl, j] = jnp.max(s, axis=0, keepdims=True)

        hosted_counts()
        al_old = [al_ref[j] for j in range(N_PAIRS)]
        for j in range(N_PAIRS):
            m_old = m_ref[j]
            m_new = jnp.maximum(m_old, mx_ref[sl, j])
            p = jnp.exp2(s_ref[sl, j] - m_new)
            alpha = jnp.exp2(m_old - m_new)
            al_ref[j] = alpha
            l_ref[j] = l_ref[j] * alpha + jnp.sum(p, axis=0, keepdims=True)
            m_ref[j] = m_new
            p_ref[sl, j] = p.astype(BF16)
        o_prev = pv(jnp.maximum(g - 1, 0), prv)
        scores(jnp.minimum(g + 1, n_g - 1), prv)
        for j in range(N_PAIRS):
            acc_ref[j] = acc_ref[j] * al_old[j] + o_prev[j]

    scores(0, 0)

    def pipe_body(u, c):
        step(2 * u, 0)

        @pl.when(2 * u + 1 < n_g)
        def _():
            step(2 * u + 1, 1)
        return c

    lax.fori_loop(0, (n_g + 1) // 2, pipe_body, 0)
    o_last = pv(n_g - 1, (n_g - 1) % 2)

    for j in range(N_PAIRS):
        acc = acc_ref[j] * al_ref[j] + o_last[j]
        l = l_ref[j]
        o0 = acc[0:ATT_DH, 0:LANES] / l[:, 0:LANES]
        o1 = acc[ATT_DH:2 * ATT_DH, LANES:2 * LANES] / l[:, LANES:2 * LANES]
        ot_ref[j * LANES:(j + 1) * LANES, :] = jnp.concatenate([o0, o1], axis=0)

    ot = ot_ref[...]
    og = og_ref[...]
    outs = []
    for hd in range(ATT_HEADS):
        oh = ot[hd * ATT_DH:(hd + 1) * ATT_DH, :]
        ms = jnp.mean(oh * oh, axis=0, keepdims=True)
        outs.append(oh * lax.rsqrt(ms + EPS) * og[hd * ATT_DH:(hd + 1) * ATT_DH, :])
    o_ref[...] = jnp.concatenate(outs, axis=0).T.astype(BF16)

    @pl.when(has_next)
    def _():
        finish_threshold(bi + 1, nxt, cur_ref[1])


def _attention(aq, ak, avt, iq, ik, iwt, rel_bias, out_g, *, gk):
    bsz, _, s, _ = aq.shape
    k_sel = min(TOPK_MAX, s // 4)
    n_blocks = s // QBLK
    bias = _bias_slots(rel_bias)
    qpair = pl.BlockSpec((None, N_PAIRS, QBLK, LANES), lambda b, i: (b, 0, i, 0))
    qnext = pl.BlockSpec((None, N_PAIRS, QBLK, LANES), lambda b, i: (b, 0, jnp.minimum(i + 1, n_blocks - 1), 0))
    wcur = pl.BlockSpec((None, IDX_HEADS, QBLK), lambda b, i: (b, 0, i))
    wnext = pl.BlockSpec((None, IDX_HEADS, QBLK), lambda b, i: (b, 0, jnp.minimum(i + 1, n_blocks - 1)))
    full = lambda shp: pl.BlockSpec((None,) + shp, lambda b, i: (b,) + (0,) * len(shp),
                                    pipeline_mode=pl.Buffered(1))
    return pl.pallas_call(
        functools.partial(_attn_kernel, gs=_tile(s, 512), gk=gk, k_sel=k_sel),
        grid=(bsz, n_blocks),
        in_specs=[qpair, full((N_PAIRS, s, LANES)), full((s // gk, ATT_W, gk)), qpair, qnext,
                  full((s, LANES)), wcur, wnext, _const_spec(bias.shape), _const_spec((ATT_W, 1))],
        out_specs=pl.BlockSpec((None, QBLK, ATT_W), lambda b, i: (b, i, 0)),
        out_shape=jax.ShapeDtypeStruct((bsz, s, ATT_W), BF16),
        scratch_shapes=[pltpu.VMEM((2, s, LANES), F32),
                        pltpu.VMEM((s, LANES), F32),
                        pltpu.VMEM((8, LANES), F32),
                        pltpu.VMEM((3, 8, LANES), F32),
                        pltpu.VMEM((8, LANES), I32),
                        pltpu.VMEM((N_PAIRS, 2 * ATT_DH, 2 * LANES), F32),
                        pltpu.VMEM((N_PAIRS, 1, 2 * LANES), F32),
                        pltpu.VMEM((N_PAIRS, 1, 2 * LANES), F32),
                        pltpu.VMEM((N_PAIRS, 1, 2 * LANES), F32),
                        pltpu.VMEM((2, N_PAIRS, gk, 2 * LANES), F32),
                        pltpu.VMEM((2, N_PAIRS, 1, 2 * LANES), F32),
                        pltpu.VMEM((2, N_PAIRS, gk, 2 * LANES), BF16),
                        pltpu.VMEM((N_PAIRS, 2 * QBLK, LANES), BF16),
                        pltpu.VMEM((ATT_W, LANES), F32),
                        pltpu.SMEM((1,), I32),
                        pltpu.SMEM((2,), I32)],
        compiler_params=_params(("parallel", "arbitrary")),
        name="attn",
    )(aq, ak, avt, iq, iq, ik, iwt, iwt, bias, out_g.reshape(ATT_W, 1))


def _outproj_kernel(rec_ref, att_ref, x_ref, g1_ref, sc_ref, sh_ref, ng_ref, wo_ref, wr_ref, br_ref,
                    x1_ref, h2_ref, comb_ref):
    mix = _dot(rec_ref[...], wo_ref[0:REC_W, :]) + _dot(att_ref[...], wo_ref[REC_W:REC_W + ATT_W, :])
    x1 = x_ref[...] + g1_ref[...] * mix
    x1_ref[...] = x1
    ms = jnp.mean(x1 * x1, axis=-1, keepdims=True)
    h2 = (x1 * lax.rsqrt(ms + EPS) * ng_ref[...] * (1.0 + sc_ref[...]) + sh_ref[...]).astype(BF16)
    h2_ref[...] = h2

    lg = _dot_nt(wr_ref[...], h2) + br_ref[...]
    row = lax.broadcasted_iota(I32, lg.shape, 0)
    big = jnp.int32(2 * LANES)
    is_g = row < N_GROUPS
    gl = jnp.where(is_g, lg, -jnp.inf)
    gmax = jnp.max(gl, axis=0, keepdims=True)
    gate = 1.0 / jnp.sum(jnp.where(is_g, jnp.exp(lg - gmax), 0.0), axis=0, keepdims=True)
    gtop = jnp.min(jnp.where(gl == gmax, row, big), axis=0, keepdims=True)
    e_lo = ROUTE_OFF + EXPERTS_PER_GROUP * gtop
    el = jnp.where((row >= e_lo) & (row < e_lo + EXPERTS_PER_GROUP), lg, -jnp.inf)
    v1 = jnp.max(el, axis=0, keepdims=True)
    i1 = jnp.min(jnp.where(el == v1, row, big), axis=0, keepdims=True)
    el2 = jnp.where(row == i1, -jnp.inf, el)
    v2 = jnp.max(el2, axis=0, keepdims=True)
    i2 = jnp.min(jnp.where(el2 == v2, row, big), axis=0, keepdims=True)
    e2 = jnp.exp(v2 - v1)
    w1 = gate / (1.0 + e2)
    w2 = gate * e2 / (1.0 + e2)
    comb = jnp.where(row == i1, w1, 0.0) + jnp.where(row == i2, w2, 0.0)
    comb = jnp.where(row == 0, gtop.astype(F32), comb)
    pad = jnp.zeros((LANES - ROUTE_ROWS, comb.shape[1]), F32)
    comb_ref[...] = jnp.concatenate([comb, pad], axis=0).T


def _outproj(rec, att, x, g1, sc2, sh2, norm_g, w_out, w_rg, b_rg, w_re, b_re, *, tm):
    bsz, s, d = x.shape
    wr = jnp.zeros((ROUTE_ROWS, d), F32).at[:N_GROUPS].set(w_rg.T).at[ROUTE_OFF:ROUTE_OFF + N_EXPERTS].set(w_re.T)
    br = jnp.zeros((ROUTE_ROWS, 1), F32).at[:N_GROUPS, 0].set(b_rg).at[ROUTE_OFF:ROUTE_OFF + N_EXPERTS, 0].set(b_re)
    row = lambda w: pl.BlockSpec((None, tm, w), lambda b, i: (b, i, 0))
    vec = pl.BlockSpec((None, 1, d), lambda b, i: (b, 0, 0))
    return pl.pallas_call(
        _outproj_kernel,
        grid=(bsz, s // tm),
        in_specs=[row(REC_W), row(ATT_W), row(d), vec, vec, vec, _const_spec((1, d)),
                  _const_spec((REC_W + ATT_W, d)), _const_spec((ROUTE_ROWS, d)), _const_spec((ROUTE_ROWS, 1))],
        out_specs=(row(d), row(d), row(LANES)),
        out_shape=(jax.ShapeDtypeStruct((bsz, s, d), F32),
                   jax.ShapeDtypeStruct((bsz, s, d), BF16),
                   jax.ShapeDtypeStruct((bsz, s, LANES), F32)),
        compiler_params=_params(("parallel", "parallel")),
        name="outproj",
    )(rec, att, x, g1.reshape(bsz, 1, d), sc2.reshape(bsz, 1, d), sh2.reshape(bsz, 1, d),
      norm_g.reshape(1, d), w_out.astype(BF16), wr.astype(BF16), br)


MOE_ALIGN = 16
MOE_LANE_SHIFT = 32
MOE_CHUNK = 288
MOE_RB = 256


def _moe_kernel(h_ref, comb_ref, x1_ref, g2_ref, w1_ref, w3_ref, w2_ref, o_ref,
                p_ref, pt_ref, hs_ref, cs_ref, ys_ref, seg_ref, *, mc):
    e = pl.program_id(2)
    tm = h_ref.shape[0]
    npad = hs_ref.shape[0]

    @pl.when(e == 0)
    def _():
        comb = comb_ref[...]
        lane = lax.broadcasted_iota(I32, (tm, LANES), 1)
        gid = comb[:, 0:1].astype(I32)
        oh = jnp.where(lane == gid, 1.0, 0.0)
        ohb = oh.astype(BF16)
        pre = []
        for rb in range(tm // MOE_RB):
            r = rb * MOE_RB + lax.broadcasted_iota(I32, (MOE_RB, tm), 0)
            c = lax.broadcasted_iota(I32, (MOE_RB, tm), 1)
            pre.append(_dot(jnp.where(c < r, 1.0, 0.0).astype(BF16), ohb))
        prefix = jnp.concatenate(pre, axis=0)
        cnt = jnp.sum(oh, axis=0, keepdims=True).astype(I32)
        cnt_al = ((cnt + (MOE_ALIGN - 1)) // MOE_ALIGN) * MOE_ALIGN
        base = jnp.sum(jnp.where(lane < gid, cnt_al.astype(F32), 0.0), axis=-1, keepdims=True)
        rank = jnp.sum(prefix * oh, axis=-1, keepdims=True)
        pos = (base + rank).astype(I32)
        pos_row = jnp.broadcast_to(pos.astype(F32), (tm, LANES)).T[0:1, :].astype(I32)
        for rb in range(tm // MOE_RB):
            sl = slice(rb * MOE_RB, (rb + 1) * MOE_RB)
            coln = lax.broadcasted_iota(I32, (MOE_RB, npad), 1)
            pt_ref[sl, :] = jnp.where(coln == pos[sl], 1.0, 0.0).astype(BF16)
        for rb in range(npad // LANES):
            sl = slice(rb * LANES, (rb + 1) * LANES)
            rown = rb * LANES + lax.broadcasted_iota(I32, (LANES, tm), 0)
            p_ref[sl, :] = jnp.where(rown == pos_row, 1.0, 0.0).astype(BF16)
        c_hi = comb.astype(BF16)
        c_mid, c_lo = _split_bf16(comb - c_hi.astype(F32))
        packed = (c_hi.astype(F32) + pltpu.roll(c_mid.astype(F32), MOE_LANE_SHIFT, axis=1)
                  + pltpu.roll(c_lo.astype(F32), 2 * MOE_LANE_SHIFT, axis=1)).astype(BF16)
        srt = _dot(p_ref[...], jnp.concatenate([h_ref[...], packed], axis=1))
        d = h_ref.shape[1]
        hs_ref[...] = srt[:, :d].astype(BF16)
        cp = srt[:, d:]
        cs_ref[...] = (cp + pltpu.roll(cp, LANES - MOE_LANE_SHIFT, axis=1)
                       + pltpu.roll(cp, LANES - 2 * MOE_LANE_SHIFT, axis=1))
        ys_ref[...] = jnp.zeros_like(ys_ref)
        start = jnp.int32(0)
        for g in range(N_GROUPS):
            seg_ref[g] = start
            seg_ref[N_GROUPS + g] = cnt[0, g]
            start = start + cnt_al[0, g]

    grp = e // EXPERTS_PER_GROUP
    start = seg_ref[grp]
    n_rows = seg_ref[N_GROUPS + grp]

    def chunk(ci, carry):
        rs = pl.ds(pl.multiple_of(start + ci * mc, MOE_ALIGN), mc)
        hb = hs_ref[rs, :]
        cw = cs_ref[rs, :]
        lane = lax.broadcasted_iota(I32, cw.shape, 1)
        col = jnp.sum(jnp.where(lane == e + ROUTE_OFF, cw, 0.0), axis=-1, keepdims=True)
        he = _silu(_dot(hb, w1_ref[...])) * _dot(hb, w3_ref[...]) * col
        ys_ref[rs, :] += _dot(he.astype(BF16), w2_ref[...])
        return carry

    lax.fori_loop(0, (n_rows + mc - 1) // mc, chunk, 0)

    @pl.when(e == N_EXPERTS - 1)
    def _():
        o_ref[...] = x1_ref[...] + g2_ref[...] * _dot(pt_ref[...], ys_ref[...].astype(BF16))


def _moe(h2, comb, x1, g2, w1, w3, w2, *, tm):
    bsz, s, d = x1.shape
    mc = MOE_CHUNK
    npad = -(-(tm + N_GROUPS * MOE_ALIGN + mc) // LANES) * LANES
    row = lambda w, **kw: pl.BlockSpec((None, tm, w), lambda b, i, e: (b, i, 0), **kw)
    once = dict(pipeline_mode=pl.Buffered(1))
    return pl.pallas_call(
        functools.partial(_moe_kernel, mc=mc),
        grid=(bsz, s // tm, N_EXPERTS),
        in_specs=[row(d, **once), row(LANES), row(d, **once),
                  pl.BlockSpec((None, 1, d), lambda b, i, e: (b, 0, 0)),
                  pl.BlockSpec((None, d, D_EXPERT), lambda b, i, e: (e, 0, 0)),
                  pl.BlockSpec((None, d, D_EXPERT), lambda b, i, e: (e, 0, 0)),
                  pl.BlockSpec((None, D_EXPERT, d), lambda b, i, e: (e, 0, 0))],
        out_specs=row(d),
        out_shape=jax.ShapeDtypeStruct((bsz, s, d), F32),
        scratch_shapes=[pltpu.VMEM((npad, tm), BF16),
                        pltpu.VMEM((tm, npad), BF16),
                        pltpu.VMEM((npad, d), BF16),
                        pltpu.VMEM((npad, LANES), F32),
                        pltpu.VMEM((npad, d), F32),
                        pltpu.SMEM((2 * N_GROUPS,), I32)],
        compiler_params=_params(("parallel", "parallel", "arbitrary")),
        name="moe",
    )(h2, comb, x1, g2.reshape(bsz, 1, d), w1.astype(BF16), w3.astype(BF16), w2.astype(BF16))


def _tile(s, pref):
    t = min(s, pref)
    assert s % t == 0
    return t


def kernel(x, c, w_ada, b_ada, norm1_g, norm2_g, w_in, lb_logits, rec_out_g, q_norm_g, k_norm_g,
           idx_k_norm_g, idx_k_norm_b, attn_out_g, rel_bias, w_out, w_rg, b_rg, w_re, b_re, w1, w3, w2):
    bsz, s, d = x.shape
    depth = w_ada.shape[0]
    gk = _tile(s, 512)
    for l in range(depth):
        mod = _adaln(c, w_ada, b_ada[l], l)
        sh1, sc1, g1, sh2, sc2, g2 = jnp.split(mod, 6, axis=-1)
        q, f, v, g, aq, ak, avt, iq, ik, iwt = _inproj(
            x, sc1, sh1, norm1_g[l], w_in, lb_logits, q_norm_g[l], k_norm_g[l],
            idx_k_norm_g[l], idx_k_norm_b[l], layer=l, gt=gk, tm=_tile(s, 512))
        rec = _hgrn(q, f, v, g, rec_out_g[l], ts=_tile(s, 256))
        att = _attention(aq, ak, avt, iq, ik, iwt, rel_bias, attn_out_g[l], gk=gk)
        x1, h2, comb = _outproj(rec, att, x, g1, sc2, sh2, norm2_g[l], w_out[l],
                                w_rg[l], b_rg[l], w_re[l], b_re[l], tm=_tile(s, 512))
        x = _moe(h2, comb, x1, g2, w1[l], w3[l], w2[l], tm=_tile(s, 1024))
    return x
```

```python
import functools

import numpy as np
import jax
import jax.numpy as jnp
from jax import lax
from jax.experimental import pallas as pl
from jax.experimental.pallas import tpu as pltpu

F32 = jnp.float32
BF16 = jnp.bfloat16
I32 = jnp.int32

CHUNK = 64
QBLK = 128
EPS = 1e-6
REC_HEADS = 4
REC_DK = 128
REC_DV = 128
REC_W = REC_HEADS * REC_DV
ATT_HEADS = 8
ATT_DH = 64
ATT_W = ATT_HEADS * ATT_DH
IDX_HEADS = 8
IDX_DIM = 64
TOPK_MAX = 256
NUM_BUCKETS = 32
MAX_DISTANCE = 128
N_GROUPS = 4
EXPERTS_PER_GROUP = 4
N_EXPERTS = N_GROUPS * EXPERTS_PER_GROUP
D_EXPERT = 512

LANES = 128
V7X_VMEM_LIMIT = 56 * 1024 * 1024
F32_LOWEST = float(np.finfo(np.float32).min)
BISECT_FIRST = 17
BISECT_CAP = 320
NEG = -1e30
LOG2E = float(np.log2(np.e))
N_PAIRS = ATT_HEADS // 2
ROUTE_OFF = N_GROUPS
ROUTE_ROWS = 32


def _dot(a, b):
    return jnp.dot(a, b, preferred_element_type=F32)


def _dot_nt(a, b):
    return lax.dot_general(a, b, (((1,), (1,)), ((), ())), preferred_element_type=F32)


def _dot_tn(a, b):
    return lax.dot_general(a, b, (((0,), (0,)), ((), ())), preferred_element_type=F32)


def _split_bf16(a):
    hi = a.astype(BF16)
    lo = (a - hi.astype(F32)).astype(BF16)
    return hi, lo


def _silu(a):
    return a * jax.nn.sigmoid(a)


def _const_spec(shape):
    nd = len(shape)
    return pl.BlockSpec(shape, lambda *_: (0,) * nd, pipeline_mode=pl.Buffered(1))


def _params(sem):
    return pltpu.CompilerParams(dimension_semantics=sem, vmem_limit_bytes=V7X_VMEM_LIMIT)


def _adaln_kernel(c_ref, w_ref, b_ref, o_ref):
    a_hi, a_lo = _split_bf16(_silu(c_ref[...]))
    w_hi, w_lo = _split_bf16(w_ref[...])
    o_ref[...] = _dot(a_hi, w_hi) + _dot(a_lo, w_hi) + _dot(a_hi, w_lo) + b_ref[...]


def _adaln(c, w_all, b, layer):
    bsz, d = c.shape
    n = w_all.shape[2]
    rows = 16
    bn = 1024
    cp = jnp.zeros((rows, d), F32).at[:bsz].set(c)
    out = pl.pallas_call(
        _adaln_kernel,
        grid=(n // bn,),
        in_specs=[pl.BlockSpec((rows, d), lambda i: (0, 0)),
                  pl.BlockSpec((None, d, bn), lambda i: (layer, 0, i)),
                  pl.BlockSpec((1, bn), lambda i: (0, i))],
        out_specs=pl.BlockSpec((rows, bn), lambda i: (0, i)),
        out_shape=jax.ShapeDtypeStruct((rows, n), F32),
        compiler_params=_params(("parallel",)),
        name="adaln",
    )(cp, w_all, b.reshape(1, n))
    return out[:bsz]


def _inproj_kernel(x_ref, sc_ref, sh_ref, ng_ref, wmain_ref, wtail_ref, lbl_ref,
                   qg_ref, kg_ref, ikg_ref, ikb_ref, pm_ref,
                   q_ref, f_ref, v_ref, g_ref, aq_ref, ak_ref, avt_ref, iq_ref, ik_ref, iwt_ref, w_ref,
                   *, layer, gt):
    @pl.when((pl.program_id(0) == 0) & (pl.program_id(1) == 0))
    def _():
        n_main = wmain_ref.shape[1]
        w_ref[:, 0:n_main] = wmain_ref[...].astype(BF16)
        w_ref[:, n_main:] = wtail_ref[...].astype(BF16)

    x = x_ref[...]
    tm = x.shape[0]
    ms = jnp.mean(x * x, axis=-1, keepdims=True)
    h = x * lax.rsqrt(ms + EPS) * ng_ref[...] * (1.0 + sc_ref[...]) + sh_ref[...]
    hb = h.astype(BF16)

    n_rec, n_att = 4 * REC_W, 3 * ATT_W
    zr = _dot(hb, w_ref[:, 0:n_rec])
    q_ref[...] = _silu(zr[:, 0:REC_W]).astype(BF16)
    lbl = lbl_ref[...]
    e = jnp.exp(lbl - jnp.max(lbl, axis=0, keepdims=True))
    sm = e / jnp.sum(e, axis=0, keepdims=True)
    lb = jnp.sum(sm[0:layer + 1], axis=0, keepdims=True)
    f_ref[...] = lb + (1.0 - lb) * jax.nn.sigmoid(zr[:, REC_W:2 * REC_W])
    v_ref[...] = zr[:, 2 * REC_W:3 * REC_W].astype(BF16)
    g_ref[...] = _silu(zr[:, 3 * REC_W:4 * REC_W]).astype(BF16)

    za = _dot(hb, w_ref[:, n_rec:n_rec + n_att])
    aq = za[:, 0:ATT_W]
    ak = za[:, ATT_W:2 * ATT_W]
    av = za[:, 2 * ATT_W:3 * ATT_W]
    pm = pm_ref[...]
    aqn = aq * lax.rsqrt(_dot((aq * aq).astype(BF16), pm) + EPS) * qg_ref[...]
    akn = ak * lax.rsqrt(_dot((ak * ak).astype(BF16), pm) + EPS) * kg_ref[...]
    for j in range(N_PAIRS):
        aq_ref[j] = aqn[:, j * LANES:(j + 1) * LANES].astype(BF16)
        ak_ref[j] = akn[:, j * LANES:(j + 1) * LANES].astype(BF16)
    for t in range(tm // gt):
        avt_ref[t] = av[t * gt:(t + 1) * gt, :].T.astype(BF16)

    zi = _dot(hb, w_ref[:, n_rec + n_att:])
    for j in range(N_PAIRS):
        iq_ref[j] = zi[:, j * LANES:(j + 1) * LANES].astype(BF16)
    tail = zi[:, IDX_HEADS * IDX_DIM:IDX_HEADS * IDX_DIM + LANES]
    lane = lax.broadcasted_iota(I32, tail.shape, 1)
    is_k = lane < IDX_DIM
    mu = jnp.sum(jnp.where(is_k, tail, 0.0), axis=-1, keepdims=True) * (1.0 / IDX_DIM)
    dlt = jnp.where(is_k, tail - mu, 0.0)
    var = jnp.sum(dlt * dlt, axis=-1, keepdims=True) * (1.0 / IDX_DIM)
    ikn = dlt * lax.rsqrt(var + EPS) * ikg_ref[...] + ikb_ref[...]
    ik_ref[...] = jnp.where(is_k, ikn, pltpu.roll(ikn, IDX_DIM, axis=1)).astype(BF16)
    iwt_ref[...] = tail.T[IDX_DIM:IDX_DIM + IDX_HEADS, :] * (IDX_HEADS ** -0.5 * IDX_DIM ** -0.5)


def _inproj(x, sc1, sh1, norm_g, w_in, lb_logits, q_g, k_g, ik_g, ik_b, *, layer, gt, tm):
    bsz, s, d = x.shape
    n_rec = 4 * REC_W
    n_att = 3 * ATT_W
    n_idx = IDX_HEADS * IDX_DIM + LANES
    n_all = n_rec + n_att + n_idx
    n_main = n_rec + n_att
    w_tail = jnp.zeros((d, n_idx), F32).at[:, :w_in.shape[2] - n_main].set(w_in[layer, :, n_main:])
    pm = jnp.asarray(np.kron(np.eye(ATT_HEADS), np.full((ATT_DH, ATT_DH), 1.0 / ATT_DH)), BF16)
    qg = jnp.tile(q_g, ATT_HEADS).reshape(1, ATT_W) * (ATT_DH ** -0.5 * LOG2E)
    kg = jnp.tile(k_g, ATT_HEADS).reshape(1, ATT_W)
    ikg = jnp.zeros((1, LANES), F32).at[0, :IDX_DIM].set(ik_g)
    ikb = jnp.zeros((1, LANES), F32).at[0, :IDX_DIM].set(ik_b)
    nl = lb_logits.shape[0]

    row = lambda w: pl.BlockSpec((None, tm, w), lambda b, i: (b, i, 0))
    pair = pl.BlockSpec((None, N_PAIRS, tm, LANES), lambda b, i: (b, 0, i, 0))
    vec = pl.BlockSpec((None, 1, d), lambda b, i: (b, 0, 0))
    out_shapes = (
        jax.ShapeDtypeStruct((bsz, s, REC_W), BF16),
        jax.ShapeDtypeStruct((bsz, s, REC_W), F32),
        jax.ShapeDtypeStruct((bsz, s, REC_W), BF16),
        jax.ShapeDtypeStruct((bsz, s, REC_W), BF16),
        jax.ShapeDtypeStruct((bsz, N_PAIRS, s, LANES), BF16),
        jax.ShapeDtypeStruct((bsz, N_PAIRS, s, LANES), BF16),
        jax.ShapeDtypeStruct((bsz, s // gt, ATT_W, gt), BF16),
        jax.ShapeDtypeStruct((bsz, N_PAIRS, s, LANES), BF16),
        jax.ShapeDtypeStruct((bsz, s, LANES), BF16),
        jax.ShapeDtypeStruct((bsz, IDX_HEADS, s), F32),
    )
    out_specs = (
        row(REC_W), row(REC_W), row(REC_W), row(REC_W), pair, pair,
        pl.BlockSpec((None, tm // gt, ATT_W, gt), lambda b, i: (b, i, 0, 0)),
        pair, row(LANES),
        pl.BlockSpec((None, IDX_HEADS, tm), lambda b, i: (b, 0, i)),
    )
    return pl.pallas_call(
        functools.partial(_inproj_kernel, layer=layer, gt=gt),
        grid=(bsz, s // tm),
        in_specs=[row(d), vec, vec, _const_spec((1, d)),
                  pl.BlockSpec((None, d, n_main), lambda b, i: (layer, 0, 0), pipeline_mode=pl.Buffered(1)),
                  _const_spec((d, n_idx)),
                  _const_spec((nl, REC_W)), _const_spec((1, ATT_W)), _const_spec((1, ATT_W)),
                  _const_spec((1, LANES)), _const_spec((1, LANES)), _const_spec((ATT_W, ATT_W))],
        out_specs=out_specs,
        out_shape=out_shapes,
        scratch_shapes=[pltpu.VMEM((d, n_all), BF16)],
        compiler_params=_params(("arbitrary", "arbitrary")),
        name="inproj",
    )(x, sc1.reshape(bsz, 1, d), sh1.reshape(bsz, 1, d), norm_g.reshape(1, d),
      w_in, w_tail, lb_logits, qg, kg, ikg, ikb, pm)


N_LEVELS = 6


def _hgrn_tables():
    c = CHUNK
    w = np.zeros((N_LEVELS + 2, c, c), np.float32)
    am = np.zeros((N_LEVELS + 1, c, c), np.float32)
    t = np.arange(c)
    for m in range(N_LEVELS):
        hs = 1 << m
        blk = t // (2 * hs)
        upper = (t // hs) % 2 == 1
        ref = blk * 2 * hs + hs - 1
        for i in range(c):
            if upper[i]:
                w[m, i, ref[i] + 1:i + 1] = 1.0
            else:
                w[m, i, i + 1:ref[i] + 1] = 1.0
        am[m] = (blk[:, None] == blk[None, :]) & upper[:, None] & ~upper[None, :]
    w[N_LEVELS] = np.tril(np.ones((c, c)))
    w[N_LEVELS + 1] = np.triu(np.ones((c, c)), 1)
    am[N_LEVELS] = np.eye(c)
    w = w.reshape((N_LEVELS + 2) * c, c)
    return np.concatenate([w, w], axis=1), am


HGRN_UNROLL = 4


def _hgrn_kernel(q_ref, f_ref, v_ref, g_ref, og_ref, ww_ref, am_ref, o_ref, st_ref, ex_ref, *, n_chunks):
    @pl.when(pl.program_id(1) == 0)
    def _():
        st_ref[...] = jnp.zeros_like(st_ref)

    ww = ww_ref[...]
    c = CHUNK
    tbit = lax.broadcasted_iota(I32, (c, REC_DK), 0)
    items = [(cc, hd) for cc in range(HGRN_UNROLL) for hd in range(REC_HEADS)]

    def chunks(ci, carry):
        def blk(ref, it):
            r0 = pl.multiple_of((ci * HGRN_UNROLL + it[0]) * c, c)
            return ref.at[pl.ds(r0, c), it[1] * REC_DK:(it[1] + 1) * REC_DK]

        def ex(i, part, last_row=False):
            r0 = (part + 1) * c - 1 if last_row else part * c
            return ex_ref[items[i][0], r0:(part + 1) * c, items[i][1] * REC_DK:(items[i][1] + 1) * REC_DK]

        f = [blk(f_ref, it)[...] for it in items]
        q = [blk(q_ref, it)[...].astype(F32) for it in items]
        v = [blk(v_ref, it)[...] for it in items]
        k = [1.0 - fi for fi in f]
        for cc in range(HGRN_UNROLL):
            cols = [jnp.concatenate(_split_bf16(jnp.log(f[cc * REC_HEADS + hd])), axis=0) for hd in range(REC_HEADS)]
            ex_ref[cc] = jnp.exp(_dot(ww, jnp.concatenate(cols, axis=1)))
        a = [am_ref[N_LEVELS] * _dot_nt(q[i].astype(BF16), k[i].astype(BF16)) for i in range(len(items))]
        for m in range(N_LEVELS):
            upper = ((tbit >> m) & 1) == 1
            for i in range(len(items)):
                tm_ = (ex(i, m) * jnp.where(upper, q[i], k[i])).astype(BF16)
                a[i] = a[i] + am_ref[m] * _dot_nt(tm_, tm_)
        qb = [(q[i] * ex(i, N_LEVELS)).astype(BF16) for i in range(len(items))]
        kb = [(k[i] * ex(i, N_LEVELS + 1)).astype(BF16) for i in range(len(items))]
        intra = [_dot(a[i].astype(BF16), v[i]) for i in range(len(items))]
        ut = [_dot_tn(v[i], kb[i]) for i in range(len(items))]
        for hd in range(REC_HEADS):
            st = st_ref[hd]
            for cc in range(HGRN_UNROLL):
                i = cc * REC_HEADS + hd
                o = intra[i] + _dot_nt(qb[i], st.astype(BF16))
                st = st * ex(i, N_LEVELS, last_row=True) + ut[i]
                ms = jnp.mean(o * o, axis=-1, keepdims=True)
                gate = blk(g_ref, items[i])[...].astype(F32)
                y = o * lax.rsqrt(ms + EPS) * og_ref[:, hd * REC_DV:(hd + 1) * REC_DV] * gate
                blk(o_ref, items[i])[...] = y.astype(BF16)
            st_ref[hd] = st
        return carry

    lax.fori_loop(0, n_chunks // HGRN_UNROLL, chunks, 0)


def _hgrn(q, f, v, g, out_g, *, ts):
    bsz, s, _ = q.shape
    ww_np, am_np = _hgrn_tables()
    ww = jnp.asarray(ww_np, BF16)
    am = jnp.asarray(am_np, F32)
    row = pl.BlockSpec((None, ts, REC_W), lambda b, i: (b, i, 0))
    return pl.pallas_call(
        functools.partial(_hgrn_kernel, n_chunks=ts // CHUNK),
        grid=(bsz, s // ts),
        in_specs=[row, row, row, row, _const_spec((1, REC_W)),
                  _const_spec(ww.shape), _const_spec(am.shape)],
        out_specs=row,
        out_shape=jax.ShapeDtypeStruct((bsz, s, REC_W), BF16),
        scratch_shapes=[pltpu.VMEM((REC_HEADS, REC_DV, REC_DK), F32),
                        pltpu.VMEM((HGRN_UNROLL, (N_LEVELS + 2) * CHUNK, REC_HEADS * REC_DK), F32)],
        compiler_params=_params(("parallel", "arbitrary")),
        name="hgrn",
    )(q, f, v, g, out_g.reshape(1, REC_W), ww, am)


def _t5_bucket(rel):
    nb = NUM_BUCKETS // 2
    max_exact = nb // 2
    ret = jnp.where(rel > 0, nb, 0)
    n = jnp.abs(rel)
    nf = jnp.maximum(n, 1).astype(F32)
    large = max_exact + (jnp.log(nf / max_exact) / np.log(MAX_DISTANCE / max_exact)
                         * (nb - max_exact)).astype(I32)
    large = jnp.minimum(large, nb - 1)
    return ret + jnp.where(n < max_exact, n, large)


def _bias_slots(rel_bias):
    n_rel = 3 * QBLK
    rel = jnp.arange(n_rel, dtype=I32) - 2 * QBLK
    tab = rel_bias[_t5_bucket(rel)].T
    far = rel_bias[_t5_bucket(jnp.full((1,), -2 * QBLK - 1, I32))].T
    rev = tab[:, ::-1]
    skew = jnp.tile(rev, (1, 2 * QBLK))[:, :2 * QBLK * (n_rel - 1)].reshape(ATT_HEADS, 2 * QBLK, n_rel - 1)
    near = skew[:, :, 2 * QBLK - 1:]
    nb = ((near - far[:, :, None]) * LOG2E).reshape(ATT_HEADS, 2, QBLK, QBLK)
    z = jnp.zeros((ATT_HEADS, 1, QBLK, QBLK), F32)
    return jnp.concatenate([z, nb, z], axis=1)


def _attn_kernel(aq_ref, ak_ref, avt_ref, iq_ref, ik_ref, iwt_ref, bias_ref, og_ref, o_ref,
                 key_ref, mb_ref, thr_ref, acc_ref, m_ref, l_ref, al_ref, s_ref, mx_ref, p_ref, qbd_ref, ot_ref, tie_ref,
                 *, gs, gk, k_sel):
    bi = pl.program_id(1)
    n_s = bi // (gs // QBLK) + 1
    lane = lax.broadcasted_iota(I32, (QBLK, LANES), 1)

    def block_diag(xq):
        zero = jnp.zeros_like(xq)
        return jnp.concatenate([jnp.where(lane < ATT_DH, xq, zero), jnp.where(lane >= ATT_DH, xq, zero)], axis=0)

    def srows(g):
        return pl.ds(pl.multiple_of(g * gs, gs), gs)

    for j in range(N_PAIRS):
        qbd_ref[j] = block_diag(iq_ref[j])
    w = iwt_ref[...]

    def group_scores(g):
        sc = _dot_nt(ik_ref[srows(g), :], qbd_ref[...].reshape(N_PAIRS * 2 * QBLK, LANES))
        acc = jnp.zeros((gs, LANES), F32)
        for hd in range(IDX_HEADS):
            acc = acc + jnp.maximum(sc[:, hd * LANES:(hd + 1) * LANES], 0.0) * w[hd:hd + 1, :]
        return acc

    def stats(sc, c):
        s3 = sc.reshape(gs // 8, 8, LANES)
        return (jnp.minimum(c[0], jnp.min(jnp.where(s3 == -jnp.inf, jnp.inf, s3), axis=0)),
                jnp.maximum(c[1], jnp.max(s3, axis=0)),
                c[2] + jnp.sum(jnp.where(s3 >= 0.0, 1, 0), axis=0),
                c[3] + jnp.sum(jnp.where(s3 > 0.0, 1, 0), axis=0))

    def idx_body(g, carry):
        sc = group_scores(g)
        key_ref[srows(g), :] = sc
        return stats(sc, carry)

    def idx_pair(u, carry):
        return idx_body(2 * u + 1, idx_body(2 * u, carry))

    z8 = jnp.zeros((8, LANES), I32)
    st8 = lax.fori_loop(0, (n_s - 1) // 2, idx_pair,
                        (jnp.full((8, LANES), jnp.inf, F32), jnp.full((8, LANES), -jnp.inf, F32), z8, z8))
    st8 = lax.cond((n_s - 1) % 2 == 1, lambda c: idx_body(n_s - 2, c), lambda c: c, st8)

    last = srows(n_s - 1)
    spos = (n_s - 1) * gs + lax.broadcasted_iota(I32, (gs, LANES), 0)
    tpos = bi * QBLK + lax.broadcasted_iota(I32, (gs, LANES), 1)
    sc_last = jnp.where((spos // CHUNK) <= (tpos // CHUNK), group_scores(n_s - 1), -jnp.inf)
    key_ref[last, :] = sc_last
    mn8, mx8, ge8, gt8 = stats(sc_last, st8)

    thr_ref[...] = jnp.full(thr_ref.shape, F32_LOWEST, F32)
    tie_ref[0] = 0

    @pl.when((2 * bi + 2) * CHUNK > k_sel)
    def _():
        def count_ge(cand):
            def one(g):
                sc = key_ref[srows(g), :].reshape(gs // 8, 8, LANES)
                return jnp.sum(jnp.where(sc >= cand, 1, 0), axis=0)

            def two(u, acc):
                second = jnp.where(2 * u + 1 < n_s, 1, 0)
                return acc + one(2 * u) + one(jnp.minimum(2 * u + 1, n_s - 1)) * second

            acc = lax.fori_loop(0, (n_s + 1) // 2, two, jnp.zeros((8, LANES), I32))
            return jnp.sum(acc, axis=0, keepdims=True)

        c0_ge = jnp.sum(ge8, axis=0, keepdims=True)
        c0_gt = jnp.sum(gt8, axis=0, keepdims=True)
        mn = jnp.min(mn8, axis=0, keepdims=True)
        mx = jnp.max(mx8, axis=0, keepdims=True)
        above = mx + (jnp.abs(mx) * 2.0 ** -20 + 1e-30)
        n_adm = (2 * bi + 1 + lax.broadcasted_iota(I32, (1, LANES), 1) // CHUNK) * CHUNK
        pos = c0_gt >= k_sel
        non_neg = c0_ge >= k_sel
        lo = jnp.where(non_neg, 0.0, mn)
        cnt_lo = jnp.where(non_neg, c0_ge, n_adm)
        hi = jnp.where(pos, above, 0.0)

        def midpoint(lo, hi):
            return 0.5 * lo + 0.5 * hi

        def unsettled(lo, hi, cnt_lo):
            mid = midpoint(lo, hi)
            return jnp.where(cnt_lo > k_sel, jnp.where(mid > lo, jnp.where(mid < hi, 1.0, 0.0), 0.0), 0.0)

        def cond(st):
            return jnp.logical_and(st[3] > 0.5, st[4] < BISECT_CAP)

        def halve(lo, hi, cnt_lo):
            act = unsettled(lo, hi, cnt_lo) > 0.5
            mid = midpoint(lo, hi)
            c = count_ge(mid)
            up = jnp.logical_and(act, c >= k_sel)
            dn = jnp.logical_and(act, c < k_sel)
            return jnp.where(up, mid, lo), jnp.where(dn, mid, hi), jnp.where(up, c, cnt_lo)

        def body(st):
            lo, hi, cnt_lo = halve(*halve(*st[:3]))
            return lo, hi, cnt_lo, jnp.max(unsettled(lo, hi, cnt_lo)), st[4] + 2

        lo, hi, cnt_lo = lax.fori_loop(0, BISECT_FIRST, lambda _, s3: halve(*s3), (lo, hi, cnt_lo))
        st = lax.while_loop(cond, body,
                            (lo, hi, cnt_lo, jnp.max(unsettled(lo, hi, cnt_lo)), jnp.int32(BISECT_FIRST)))
        thr_ref[...] = jnp.broadcast_to(st[0], thr_ref.shape)
        tie_ref[0] = (jnp.max(jnp.where(st[2] > k_sel, 1.0, 0.0)) > 0.5).astype(I32)

    thr = thr_ref[0:1, :]

    @pl.when(tie_ref[0] == 0)
    def _():
        def mb_body(g, carry):
            mb_ref[srows(g), :] = jnp.where(key_ref[srows(g), :] >= thr, 0.0, NEG)
            return carry

        lax.fori_loop(0, n_s, mb_body, 0)

    @pl.when(tie_ref[0] != 0)
    def _():
        def count_gt(g, acc):
            sc = key_ref[srows(g), :].reshape(gs // 8, 8, LANES)
            return acc + jnp.sum(jnp.where(sc > thr, 1, 0), axis=0)

        n_gt = jnp.sum(lax.fori_loop(0, n_s, count_gt, jnp.zeros((8, LANES), I32)), axis=0, keepdims=True)
        room = (k_sel - n_gt).astype(F32)
        r = lax.broadcasted_iota(I32, (QBLK, QBLK), 0)
        c = lax.broadcasted_iota(I32, (QBLK, QBLK), 1)
        tri = jnp.where(c <= r, 1.0, 0.0).astype(BF16)

        def mb_body(g, seen):
            for t in range(gs // QBLK):
                rs = pl.ds(pl.multiple_of(g * gs + t * QBLK, QBLK), QBLK)
                sc = key_ref[rs, :]
                eq = jnp.where(sc == thr, 1.0, 0.0)
                rank = _dot(tri, eq.astype(BF16)) + seen
                keep = jnp.where(sc > thr, 1.0, jnp.where(rank <= room, eq, 0.0))
                mb_ref[rs, :] = jnp.where(keep > 0.5, 0.0, NEG)
                seen = seen + jnp.sum(eq, axis=0, keepdims=True)
            return seen

        lax.fori_loop(0, n_s, mb_body, jnp.zeros((1, LANES), F32))

    spg = gk // QBLK
    n_g = bi // spg + 1
    for j in range(N_PAIRS):
        qbd_ref[j] = block_diag(aq_ref[j])
    m_ref[...] = jnp.full(m_ref.shape, NEG, F32)
    l_ref[...] = jnp.zeros_like(l_ref)
    acc_ref[...] = jnp.zeros_like(acc_ref)
    al_ref[...] = jnp.ones_like(al_ref)
    p_ref[1] = jnp.zeros(p_ref.shape[1:], BF16)

    def rows(g):
        return pl.ds(pl.multiple_of(g * gk, gk), gk)

    def scores(t, slot):
        mb = mb_ref[rows(t), :]
        mb2 = jnp.concatenate([mb, mb], axis=1)
        for j in range(N_PAIRS):
            s = _dot_nt(ak_ref[j, rows(t), :], qbd_ref[j]) + mb2
            s_ref[slot, j] = s
            mx_ref[slot, j] = jnp.max(s, axis=0, keepdims=True)

    def pv(t, slot):
        return [_dot(avt_ref[t, j * LANES:(j + 1) * LANES, :], p_ref[slot, j]) for j in range(N_PAIRS)]

    def step(g, cur):
        prv = 1 - cur

        @pl.when(g >= n_g - 2)
        def _():
            for j in range(N_PAIRS):
                parts = []
                for st in range(spg):
                    slot = jnp.clip(g * spg + st - bi + 2, 0, 3)
                    parts.append(jnp.concatenate([bias_ref[2 * j, slot], bias_ref[2 * j + 1, slot]], axis=1))
                s = s_ref[cur, j] + jnp.concatenate(parts, axis=0)
                s_ref[cur, j] = s
                mx_ref[cur, j] = jnp.max(s, axis=0, keepdims=True)

        al_old = [al_ref[j] for j in range(N_PAIRS)]
        for j in range(N_PAIRS):
            m_old = m_ref[j]
            m_new = jnp.maximum(m_old, mx_ref[cur, j])
            p = jnp.exp2(s_ref[cur, j] - m_new)
            alpha = jnp.exp2(m_old - m_new)
            al_ref[j] = alpha
            l_ref[j] = l_ref[j] * alpha + jnp.sum(p, axis=0, keepdims=True)
            m_ref[j] = m_new
            p_ref[cur, j] = p.astype(BF16)
        o_prev = pv(jnp.maximum(g - 1, 0), prv)
        scores(jnp.minimum(g + 1, n_g - 1), prv)
        for j in range(N_PAIRS):
            acc_ref[j] = acc_ref[j] * al_old[j] + o_prev[j]

    scores(0, 0)

    def pipe_body(u, c):
        step(2 * u, 0)

        @pl.when(2 * u + 1 < n_g)
        def _():
            step(2 * u + 1, 1)
        return c

    lax.fori_loop(0, (n_g + 1) // 2, pipe_body, 0)
    o_last = pv(n_g - 1, (n_g - 1) % 2)

    for j in range(N_PAIRS):
        acc = acc_ref[j] * al_ref[j] + o_last[j]
        l = l_ref[j]
        o0 = acc[0:ATT_DH, 0:LANES] / l[:, 0:LANES]
        o1 = acc[ATT_DH:2 * ATT_DH, LANES:2 * LANES] / l[:, LANES:2 * LANES]
        ot_ref[j * LANES:(j + 1) * LANES, :] = jnp.concatenate([o0, o1], axis=0)

    ot = ot_ref[...]
    og = og_ref[...]
    outs = []
    for hd in range(ATT_HEADS):
        oh = ot[hd * ATT_DH:(hd + 1) * ATT_DH, :]
        ms = jnp.mean(oh * oh, axis=0, keepdims=True)
        outs.append(oh * lax.rsqrt(ms + EPS) * og[hd * ATT_DH:(hd + 1) * ATT_DH, :])
    o_ref[...] = jnp.concatenate(outs, axis=0).T.astype(BF16)


def _attention(aq, ak, avt, iq, ik, iwt, rel_bias, out_g, *, gk):
    bsz, _, s, _ = aq.shape
    k_sel = min(TOPK_MAX, s // 4)
    bias = _bias_slots(rel_bias)
    qpair = pl.BlockSpec((None, N_PAIRS, QBLK, LANES), lambda b, i: (b, 0, i, 0))
    full = lambda shp: pl.BlockSpec((None,) + shp, lambda b, i: (b,) + (0,) * len(shp),
                                    pipeline_mode=pl.Buffered(1))
    return pl.pallas_call(
        functools.partial(_attn_kernel, gs=_tile(s, 512), gk=gk, k_sel=k_sel),
        grid=(bsz, s // QBLK),
        in_specs=[qpair, full((N_PAIRS, s, LANES)), full((s // gk, ATT_W, gk)), qpair,
                  full((s, LANES)), pl.BlockSpec((None, IDX_HEADS, QBLK), lambda b, i: (b, 0, i)),
                  _const_spec(bias.shape), _const_spec((ATT_W, 1))],
        out_specs=pl.BlockSpec((None, QBLK, ATT_W), lambda b, i: (b, i, 0)),
        out_shape=jax.ShapeDtypeStruct((bsz, s, ATT_W), BF16),
        scratch_shapes=[pltpu.VMEM((s, LANES), F32),
                        pltpu.VMEM((s, LANES), F32),
                        pltpu.VMEM((8, LANES), F32),
                        pltpu.VMEM((N_PAIRS, 2 * ATT_DH, 2 * LANES), F32),
                        pltpu.VMEM((N_PAIRS, 1, 2 * LANES), F32),
                        pltpu.VMEM((N_PAIRS, 1, 2 * LANES), F32),
                        pltpu.VMEM((N_PAIRS, 1, 2 * LANES), F32),
                        pltpu.VMEM((2, N_PAIRS, gk, 2 * LANES), F32),
                        pltpu.VMEM((2, N_PAIRS, 1, 2 * LANES), F32),
                        pltpu.VMEM((2, N_PAIRS, gk, 2 * LANES), BF16),
                        pltpu.VMEM((N_PAIRS, 2 * QBLK, LANES), BF16),
                        pltpu.VMEM((ATT_W, LANES), F32),
                        pltpu.SMEM((1,), I32)],
        compiler_params=_params(("parallel", "arbitrary")),
        name="attn",
    )(aq, ak, avt, iq, ik, iwt, bias, out_g.reshape(ATT_W, 1))


def _outproj_kernel(rec_ref, att_ref, x_ref, g1_ref, sc_ref, sh_ref, ng_ref, wo_ref, wr_ref, br_ref,
                    x1_ref, h2_ref, comb_ref):
    mix = _dot(rec_ref[...], wo_ref[0:REC_W, :]) + _dot(att_ref[...], wo_ref[REC_W:REC_W + ATT_W, :])
    x1 = x_ref[...] + g1_ref[...] * mix
    x1_ref[...] = x1
    ms = jnp.mean(x1 * x1, axis=-1, keepdims=True)
    h2 = (x1 * lax.rsqrt(ms + EPS) * ng_ref[...] * (1.0 + sc_ref[...]) + sh_ref[...]).astype(BF16)
    h2_ref[...] = h2

    lg = _dot_nt(wr_ref[...], h2) + br_ref[...]
    row = lax.broadcasted_iota(I32, lg.shape, 0)
    big = jnp.int32(2 * LANES)
    is_g = row < N_GROUPS
    gl = jnp.where(is_g, lg, -jnp.inf)
    gmax = jnp.max(gl, axis=0, keepdims=True)
    gate = 1.0 / jnp.sum(jnp.where(is_g, jnp.exp(lg - gmax), 0.0), axis=0, keepdims=True)
    gtop = jnp.min(jnp.where(gl == gmax, row, big), axis=0, keepdims=True)
    e_lo = ROUTE_OFF + EXPERTS_PER_GROUP * gtop
    el = jnp.where((row >= e_lo) & (row < e_lo + EXPERTS_PER_GROUP), lg, -jnp.inf)
    v1 = jnp.max(el, axis=0, keepdims=True)
    i1 = jnp.min(jnp.where(el == v1, row, big), axis=0, keepdims=True)
    el2 = jnp.where(row == i1, -jnp.inf, el)
    v2 = jnp.max(el2, axis=0, keepdims=True)
    i2 = jnp.min(jnp.where(el2 == v2, row, big), axis=0, keepdims=True)
    e2 = jnp.exp(v2 - v1)
    w1 = gate / (1.0 + e2)
    w2 = gate * e2 / (1.0 + e2)
    comb = jnp.where(row == i1, w1, 0.0) + jnp.where(row == i2, w2, 0.0)
    comb = jnp.where(row == 0, gtop.astype(F32), comb)
    pad = jnp.zeros((LANES - ROUTE_ROWS, comb.shape[1]), F32)
    comb_ref[...] = jnp.concatenate([comb, pad], axis=0).T


def _outproj(rec, att, x, g1, sc2, sh2, norm_g, w_out, w_rg, b_rg, w_re, b_re, *, tm):
    bsz, s, d = x.shape
    wr = jnp.zeros((ROUTE_ROWS, d), F32).at[:N_GROUPS].set(w_rg.T).at[ROUTE_OFF:ROUTE_OFF + N_EXPERTS].set(w_re.T)
    br = jnp.zeros((ROUTE_ROWS, 1), F32).at[:N_GROUPS, 0].set(b_rg).at[ROUTE_OFF:ROUTE_OFF + N_EXPERTS, 0].set(b_re)
    row = lambda w: pl.BlockSpec((None, tm, w), lambda b, i: (b, i, 0))
    vec = pl.BlockSpec((None, 1, d), lambda b, i: (b, 0, 0))
    return pl.pallas_call(
        _outproj_kernel,
        grid=(bsz, s // tm),
        in_specs=[row(REC_W), row(ATT_W), row(d), vec, vec, vec, _const_spec((1, d)),
                  _const_spec((REC_W + ATT_W, d)), _const_spec((ROUTE_ROWS, d)), _const_spec((ROUTE_ROWS, 1))],
        out_specs=(row(d), row(d), row(LANES)),
        out_shape=(jax.ShapeDtypeStruct((bsz, s, d), F32),
                   jax.ShapeDtypeStruct((bsz, s, d), BF16),
                   jax.ShapeDtypeStruct((bsz, s, LANES), F32)),
        compiler_params=_params(("parallel", "parallel")),
        name="outproj",
    )(rec, att, x, g1.reshape(bsz, 1, d), sc2.reshape(bsz, 1, d), sh2.reshape(bsz, 1, d),
      norm_g.reshape(1, d), w_out.astype(BF16), wr.astype(BF16), br)


MOE_ALIGN = 16
MOE_LANE_SHIFT = 32
MOE_CHUNK = 288
MOE_RB = 256


def _moe_kernel(h_ref, comb_ref, x1_ref, g2_ref, w1_ref, w3_ref, w2_ref, o_ref,
                p_ref, pt_ref, hs_ref, cs_ref, ys_ref, seg_ref, *, mc):
    e = pl.program_id(2)
    tm = h_ref.shape[0]
    npad = hs_ref.shape[0]

    @pl.when(e == 0)
    def _():
        comb = comb_ref[...]
        lane = lax.broadcasted_iota(I32, (tm, LANES), 1)
        gid = comb[:, 0:1].astype(I32)
        oh = jnp.where(lane == gid, 1.0, 0.0)
        ohb = oh.astype(BF16)
        pre = []
        for rb in range(tm // MOE_RB):
            r = rb * MOE_RB + lax.broadcasted_iota(I32, (MOE_RB, tm), 0)
            c = lax.broadcasted_iota(I32, (MOE_RB, tm), 1)
            pre.append(_dot(jnp.where(c < r, 1.0, 0.0).astype(BF16), ohb))
        prefix = jnp.concatenate(pre, axis=0)
        cnt = jnp.sum(oh, axis=0, keepdims=True).astype(I32)
        cnt_al = ((cnt + (MOE_ALIGN - 1)) // MOE_ALIGN) * MOE_ALIGN
        base = jnp.sum(jnp.where(lane < gid, cnt_al.astype(F32), 0.0), axis=-1, keepdims=True)
        rank = jnp.sum(prefix * oh, axis=-1, keepdims=True)
        pos = (base + rank).astype(I32)
        pos_row = jnp.broadcast_to(pos.astype(F32), (tm, LANES)).T[0:1, :].astype(I32)
        for rb in range(tm // MOE_RB):
            sl = slice(rb * MOE_RB, (rb + 1) * MOE_RB)
            coln = lax.broadcasted_iota(I32, (MOE_RB, npad), 1)
            pt_ref[sl, :] = jnp.where(coln == pos[sl], 1.0, 0.0).astype(BF16)
        for rb in range(npad // LANES):
            sl = slice(rb * LANES, (rb + 1) * LANES)
            rown = rb * LANES + lax.broadcasted_iota(I32, (LANES, tm), 0)
            p_ref[sl, :] = jnp.where(rown == pos_row, 1.0, 0.0).astype(BF16)
        c_hi = comb.astype(BF16)
        c_mid, c_lo = _split_bf16(comb - c_hi.astype(F32))
        packed = (c_hi.astype(F32) + pltpu.roll(c_mid.astype(F32), MOE_LANE_SHIFT, axis=1)
                  + pltpu.roll(c_lo.astype(F32), 2 * MOE_LANE_SHIFT, axis=1)).astype(BF16)
        srt = _dot(p_ref[...], jnp.concatenate([h_ref[...], packed], axis=1))
        d = h_ref.shape[1]
        hs_ref[...] = srt[:, :d].astype(BF16)
        cp = srt[:, d:]
        cs_ref[...] = (cp + pltpu.roll(cp, LANES - MOE_LANE_SHIFT, axis=1)
                       + pltpu.roll(cp, LANES - 2 * MOE_LANE_SHIFT, axis=1))
        ys_ref[...] = jnp.zeros_like(ys_ref)
        start = jnp.int32(0)
        for g in range(N_GROUPS):
            seg_ref[g] = start
            seg_ref[N_GROUPS + g] = cnt[0, g]
            start = start + cnt_al[0, g]

    grp = e // EXPERTS_PER_GROUP
    start = seg_ref[grp]
    n_rows = seg_ref[N_GROUPS + grp]

    def chunk(ci, carry):
        rs = pl.ds(pl.multiple_of(start + ci * mc, MOE_ALIGN), mc)
        hb = hs_ref[rs, :]
        cw = cs_ref[rs, :]
        lane = lax.broadcasted_iota(I32, cw.shape, 1)
        col = jnp.sum(jnp.where(lane == e + ROUTE_OFF, cw, 0.0), axis=-1, keepdims=True)
        he = _silu(_dot(hb, w1_ref[...])) * _dot(hb, w3_ref[...]) * col
        ys_ref[rs, :] += _dot(he.astype(BF16), w2_ref[...])
        return carry

    lax.fori_loop(0, (n_rows + mc - 1) // mc, chunk, 0)

    @pl.when(e == N_EXPERTS - 1)
    def _():
        o_ref[...] = x1_ref[...] + g2_ref[...] * _dot(pt_ref[...], ys_ref[...].astype(BF16))


def _moe(h2, comb, x1, g2, w1, w3, w2, *, tm):
    bsz, s, d = x1.shape
    mc = MOE_CHUNK
    npad = -(-(tm + N_GROUPS * MOE_ALIGN + mc) // LANES) * LANES
    row = lambda w, **kw: pl.BlockSpec((None, tm, w), lambda b, i, e: (b, i, 0), **kw)
    once = dict(pipeline_mode=pl.Buffered(1))
    return pl.pallas_call(
        functools.partial(_moe_kernel, mc=mc),
        grid=(bsz, s // tm, N_EXPERTS),
        in_specs=[row(d, **once), row(LANES), row(d, **once),
                  pl.BlockSpec((None, 1, d), lambda b, i, e: (b, 0, 0)),
                  pl.BlockSpec((None, d, D_EXPERT), lambda b, i, e: (e, 0, 0)),
                  pl.BlockSpec((None, d, D_EXPERT), lambda b, i, e: (e, 0, 0)),
                  pl.BlockSpec((None, D_EXPERT, d), lambda b, i, e: (e, 0, 0))],
        out_specs=row(d),
        out_shape=jax.ShapeDtypeStruct((bsz, s, d), F32),
        scratch_shapes=[pltpu.VMEM((npad, tm), BF16),
                        pltpu.VMEM((tm, npad), BF16),
                        pltpu.VMEM((npad, d), BF16),
                        pltpu.VMEM((npad, LANES), F32),
                        pltpu.VMEM((npad, d), F32),
                        pltpu.SMEM((2 * N_GROUPS,), I32)],
        compiler_params=_params(("parallel", "parallel", "arbitrary")),
        name="moe",
    )(h2, comb, x1, g2.reshape(bsz, 1, d), w1.astype(BF16), w3.astype(BF16), w2.astype(BF16))


def _tile(s, pref):
    t = min(s, pref)
    assert s % t == 0
    return t


def kernel(x, c, w_ada, b_ada, norm1_g, norm2_g, w_in, lb_logits, rec_out_g, q_norm_g, k_norm_g,
           idx_k_norm_g, idx_k_norm_b, attn_out_g, rel_bias, w_out, w_rg, b_rg, w_re, b_re, w1, w3, w2):
    bsz, s, d = x.shape
    depth = w_ada.shape[0]
    gk = _tile(s, 512)
    for l in range(depth):
        mod = _adaln(c, w_ada, b_ada[l], l)
        sh1, sc1, g1, sh2, sc2, g2 = jnp.split(mod, 6, axis=-1)
        q, f, v, g, aq, ak, avt, iq, ik, iwt = _inproj(
            x, sc1, sh1, norm1_g[l], w_in, lb_logits, q_norm_g[l], k_norm_g[l],
            idx_k_norm_g[l], idx_k_norm_b[l], layer=l, gt=gk, tm=_tile(s, 512))
        rec = _hgrn(q, f, v, g, rec_out_g[l], ts=_tile(s, 1024))
        att = _attention(aq, ak, avt, iq, ik, iwt, rel_bias, attn_out_g[l], gk=gk)
        x1, h2, comb = _outproj(rec, att, x, g1, sc2, sh2, norm2_g[l], w_out[l],
                                w_rg[l], b_rg[l], w_re[l], b_re[l], tm=_tile(s, 512))
        x = _moe(h2, comb, x1, g2, w1[l], w3[l], w2[l], tm=_tile(s, 1024))
    return x
```

```python
import functools

import numpy as np
import jax
import jax.numpy as jnp
from jax import lax
from jax.experimental import pallas as pl
from jax.experimental.pallas import tpu as pltpu

F32 = jnp.float32
BF16 = jnp.bfloat16
I32 = jnp.int32

CHUNK = 64
QBLK = 128
EPS = 1e-6
REC_HEADS = 4
REC_DK = 128
REC_DV = 128
REC_W = REC_HEADS * REC_DV
ATT_HEADS = 8
ATT_DH = 64
ATT_W = ATT_HEADS * ATT_DH
IDX_HEADS = 8
IDX_DIM = 64
TOPK_MAX = 256
NUM_BUCKETS = 32
MAX_DISTANCE = 128
N_GROUPS = 4
EXPERTS_PER_GROUP = 4
N_EXPERTS = N_GROUPS * EXPERTS_PER_GROUP
D_EXPERT = 512

LANES = 128
V7X_VMEM_LIMIT = 56 * 1024 * 1024
F32_LOWEST = float(np.finfo(np.float32).min)
BISECT_FIRST = 17
BISECT_CAP = 320
NEG = -(2.0 ** 100)
LOG2E = float(np.log2(np.e))
N_PAIRS = ATT_HEADS // 2
ROUTE_OFF = N_GROUPS
ROUTE_ROWS = 32


def _dot(a, b):
    return jnp.dot(a, b, preferred_element_type=F32)


def _dot_nt(a, b):
    return lax.dot_general(a, b, (((1,), (1,)), ((), ())), preferred_element_type=F32)


def _dot_tn(a, b):
    return lax.dot_general(a, b, (((0,), (0,)), ((), ())), preferred_element_type=F32)


def _split_bf16(a):
    hi = a.astype(BF16)
    lo = (a - hi.astype(F32)).astype(BF16)
    return hi, lo


def _silu(a):
    return a * jax.nn.sigmoid(a)


def _const_spec(shape):
    nd = len(shape)
    return pl.BlockSpec(shape, lambda *_: (0,) * nd, pipeline_mode=pl.Buffered(1))


def _params(sem):
    return pltpu.CompilerParams(dimension_semantics=sem, vmem_limit_bytes=V7X_VMEM_LIMIT)


def _adaln_kernel(c_ref, w_ref, b_ref, o_ref):
    a_hi, a_lo = _split_bf16(_silu(c_ref[...]))
    w_hi, w_lo = _split_bf16(w_ref[...])
    o_ref[...] = _dot(a_hi, w_hi) + _dot(a_lo, w_hi) + _dot(a_hi, w_lo) + b_ref[...]


def _adaln(c, w_all, b, layer):
    bsz, d = c.shape
    n = w_all.shape[2]
    rows = 16
    bn = 1024
    cp = jnp.zeros((rows, d), F32).at[:bsz].set(c)
    out = pl.pallas_call(
        _adaln_kernel,
        grid=(n // bn,),
        in_specs=[pl.BlockSpec((rows, d), lambda i: (0, 0)),
                  pl.BlockSpec((None, d, bn), lambda i: (layer, 0, i)),
                  pl.BlockSpec((1, bn), lambda i: (0, i))],
        out_specs=pl.BlockSpec((rows, bn), lambda i: (0, i)),
        out_shape=jax.ShapeDtypeStruct((rows, n), F32),
        compiler_params=_params(("parallel",)),
        name="adaln",
    )(cp, w_all, b.reshape(1, n))
    return out[:bsz]


def _inproj_kernel(x_ref, sc_ref, sh_ref, ng_ref, wmain_ref, wtail_ref, lbl_ref,
                   qg_ref, kg_ref, ikg_ref, ikb_ref, pm_ref,
                   q_ref, f_ref, v_ref, g_ref, aq_ref, ak_ref, avt_ref, iq_ref, ik_ref, iwt_ref, w_ref,
                   *, layer, gt):
    @pl.when((pl.program_id(0) == 0) & (pl.program_id(1) == 0))
    def _():
        n_main = wmain_ref.shape[1]
        w_ref[:, 0:n_main] = wmain_ref[...].astype(BF16)
        w_ref[:, n_main:] = wtail_ref[...].astype(BF16)

    x = x_ref[...]
    tm = x.shape[0]
    ms = jnp.mean(x * x, axis=-1, keepdims=True)
    h = x * lax.rsqrt(ms + EPS) * ng_ref[...] * (1.0 + sc_ref[...]) + sh_ref[...]
    hb = h.astype(BF16)

    n_rec, n_att = 4 * REC_W, 3 * ATT_W
    zr = _dot(hb, w_ref[:, 0:n_rec])
    q_ref[...] = _silu(zr[:, 0:REC_W]).astype(BF16)
    lbl = lbl_ref[...]
    e = jnp.exp(lbl - jnp.max(lbl, axis=0, keepdims=True))
    sm = e / jnp.sum(e, axis=0, keepdims=True)
    lb = jnp.sum(sm[0:layer + 1], axis=0, keepdims=True)
    f_ref[...] = lb + (1.0 - lb) * jax.nn.sigmoid(zr[:, REC_W:2 * REC_W])
    v_ref[...] = zr[:, 2 * REC_W:3 * REC_W].astype(BF16)
    g_ref[...] = _silu(zr[:, 3 * REC_W:4 * REC_W]).astype(BF16)

    za = _dot(hb, w_ref[:, n_rec:n_rec + n_att])
    aq = za[:, 0:ATT_W]
    ak = za[:, ATT_W:2 * ATT_W]
    av = za[:, 2 * ATT_W:3 * ATT_W]
    pm = pm_ref[...]
    aqn = aq * lax.rsqrt(_dot((aq * aq).astype(BF16), pm) + EPS) * qg_ref[...]
    akn = ak * lax.rsqrt(_dot((ak * ak).astype(BF16), pm) + EPS) * kg_ref[...]
    for j in range(N_PAIRS):
        aq_ref[j] = aqn[:, j * LANES:(j + 1) * LANES].astype(BF16)
        ak_ref[j] = akn[:, j * LANES:(j + 1) * LANES].astype(BF16)
    for t in range(tm // gt):
        avt_ref[t] = av[t * gt:(t + 1) * gt, :].T.astype(BF16)

    zi = _dot(hb, w_ref[:, n_rec + n_att:])
    for j in range(N_PAIRS):
        iq_ref[j] = zi[:, j * LANES:(j + 1) * LANES].astype(BF16)
    tail = zi[:, IDX_HEADS * IDX_DIM:IDX_HEADS * IDX_DIM + LANES]
    lane = lax.broadcasted_iota(I32, tail.shape, 1)
    is_k = lane < IDX_DIM
    mu = jnp.sum(jnp.where(is_k, tail, 0.0), axis=-1, keepdims=True) * (1.0 / IDX_DIM)
    dlt = jnp.where(is_k, tail - mu, 0.0)
    var = jnp.sum(dlt * dlt, axis=-1, keepdims=True) * (1.0 / IDX_DIM)
    ikn = dlt * lax.rsqrt(var + EPS) * ikg_ref[...] + ikb_ref[...]
    ik_ref[...] = jnp.where(is_k, ikn, pltpu.roll(ikn, IDX_DIM, axis=1)).astype(BF16)
    iwt_ref[...] = tail.T[IDX_DIM:IDX_DIM + IDX_HEADS, :] * (IDX_HEADS ** -0.5 * IDX_DIM ** -0.5)


def _inproj(x, sc1, sh1, norm_g, w_in, lb_logits, q_g, k_g, ik_g, ik_b, *, layer, gt, tm):
    bsz, s, d = x.shape
    n_rec = 4 * REC_W
    n_att = 3 * ATT_W
    n_idx = IDX_HEADS * IDX_DIM + LANES
    n_all = n_rec + n_att + n_idx
    n_main = n_rec + n_att
    w_tail = jnp.zeros((d, n_idx), F32).at[:, :w_in.shape[2] - n_main].set(w_in[layer, :, n_main:])
    pm = jnp.asarray(np.kron(np.eye(ATT_HEADS), np.full((ATT_DH, ATT_DH), 1.0 / ATT_DH)), BF16)
    qg = jnp.tile(q_g, ATT_HEADS).reshape(1, ATT_W) * (ATT_DH ** -0.5 * LOG2E)
    kg = jnp.tile(k_g, ATT_HEADS).reshape(1, ATT_W)
    ikg = jnp.zeros((1, LANES), F32).at[0, :IDX_DIM].set(ik_g)
    ikb = jnp.zeros((1, LANES), F32).at[0, :IDX_DIM].set(ik_b)
    nl = lb_logits.shape[0]

    row = lambda w: pl.BlockSpec((None, tm, w), lambda b, i: (b, i, 0))
    pair = pl.BlockSpec((None, N_PAIRS, tm, LANES), lambda b, i: (b, 0, i, 0))
    vec = pl.BlockSpec((None, 1, d), lambda b, i: (b, 0, 0))
    out_shapes = (
        jax.ShapeDtypeStruct((bsz, s, REC_W), BF16),
        jax.ShapeDtypeStruct((bsz, s, REC_W), F32),
        jax.ShapeDtypeStruct((bsz, s, REC_W), BF16),
        jax.ShapeDtypeStruct((bsz, s, REC_W), BF16),
        jax.ShapeDtypeStruct((bsz, N_PAIRS, s, LANES), BF16),
        jax.ShapeDtypeStruct((bsz, N_PAIRS, s, LANES), BF16),
        jax.ShapeDtypeStruct((bsz, s // gt, ATT_W, gt), BF16),
        jax.ShapeDtypeStruct((bsz, N_PAIRS, s, LANES), BF16),
        jax.ShapeDtypeStruct((bsz, s, LANES), BF16),
        jax.ShapeDtypeStruct((bsz, IDX_HEADS, s), F32),
    )
    out_specs = (
        row(REC_W), row(REC_W), row(REC_W), row(REC_W), pair, pair,
        pl.BlockSpec((None, tm // gt, ATT_W, gt), lambda b, i: (b, i, 0, 0)),
        pair, row(LANES),
        pl.BlockSpec((None, IDX_HEADS, tm), lambda b, i: (b, 0, i)),
    )
    return pl.pallas_call(
        functools.partial(_inproj_kernel, layer=layer, gt=gt),
        grid=(bsz, s // tm),
        in_specs=[row(d), vec, vec, _const_spec((1, d)),
                  pl.BlockSpec((None, d, n_main), lambda b, i: (layer, 0, 0), pipeline_mode=pl.Buffered(1)),
                  _const_spec((d, n_idx)),
                  _const_spec((nl, REC_W)), _const_spec((1, ATT_W)), _const_spec((1, ATT_W)),
                  _const_spec((1, LANES)), _const_spec((1, LANES)), _const_spec((ATT_W, ATT_W))],
        out_specs=out_specs,
        out_shape=out_shapes,
        scratch_shapes=[pltpu.VMEM((d, n_all), BF16)],
        compiler_params=_params(("arbitrary", "arbitrary")),
        name="inproj",
    )(x, sc1.reshape(bsz, 1, d), sh1.reshape(bsz, 1, d), norm_g.reshape(1, d),
      w_in, w_tail, lb_logits, qg, kg, ikg, ikb, pm)


N_LEVELS = 6


def _hgrn_tables():
    c = CHUNK
    w = np.zeros((N_LEVELS + 2, c, c), np.float32)
    am = np.zeros((N_LEVELS + 1, c, c), np.float32)
    t = np.arange(c)
    for m in range(N_LEVELS):
        hs = 1 << m
        blk = t // (2 * hs)
        upper = (t // hs) % 2 == 1
        ref = blk * 2 * hs + hs - 1
        for i in range(c):
            if upper[i]:
                w[m, i, ref[i] + 1:i + 1] = 1.0
            else:
                w[m, i, i + 1:ref[i] + 1] = 1.0
        am[m] = (blk[:, None] == blk[None, :]) & upper[:, None] & ~upper[None, :]
    w[N_LEVELS] = np.tril(np.ones((c, c)))
    w[N_LEVELS + 1] = np.triu(np.ones((c, c)), 1)
    am[N_LEVELS] = np.eye(c)
    w = w.reshape((N_LEVELS + 2) * c, c)
    return np.concatenate([w, w], axis=1), am


HGRN_UNROLL = 4


def _hgrn_kernel(q_ref, f_ref, v_ref, g_ref, og_ref, ww_ref, am_ref, o_ref, st_ref, ex_ref, *, n_chunks):
    @pl.when(pl.program_id(1) == 0)
    def _():
        st_ref[...] = jnp.zeros_like(st_ref)

    ww = ww_ref[...]
    c = CHUNK
    tbit = lax.broadcasted_iota(I32, (c, REC_DK), 0)
    items = [(cc, hd) for cc in range(HGRN_UNROLL) for hd in range(REC_HEADS)]

    def chunks(ci, carry):
        def blk(ref, it):
            r0 = pl.multiple_of((ci * HGRN_UNROLL + it[0]) * c, c)
            return ref.at[pl.ds(r0, c), it[1] * REC_DK:(it[1] + 1) * REC_DK]

        def ex(i, part, last_row=False):
            r0 = (part + 1) * c - 1 if last_row else part * c
            return ex_ref[items[i][0], r0:(part + 1) * c, items[i][1] * REC_DK:(items[i][1] + 1) * REC_DK]

        f = [blk(f_ref, it)[...] for it in items]
        q = [blk(q_ref, it)[...].astype(F32) for it in items]
        v = [blk(v_ref, it)[...] for it in items]
        k = [1.0 - fi for fi in f]
        for cc in range(HGRN_UNROLL):
            cols = [jnp.concatenate(_split_bf16(jnp.log(f[cc * REC_HEADS + hd])), axis=0) for hd in range(REC_HEADS)]
            ex_ref[cc] = jnp.exp(_dot(ww, jnp.concatenate(cols, axis=1)))
        a = [am_ref[N_LEVELS] * _dot_nt(q[i].astype(BF16), k[i].astype(BF16)) for i in range(len(items))]
        for m in range(N_LEVELS):
            upper = ((tbit >> m) & 1) == 1
            for i in range(len(items)):
                tm_ = (ex(i, m) * jnp.where(upper, q[i], k[i])).astype(BF16)
                a[i] = a[i] + am_ref[m] * _dot_nt(tm_, tm_)
        qb = [(q[i] * ex(i, N_LEVELS)).astype(BF16) for i in range(len(items))]
        kb = [(k[i] * ex(i, N_LEVELS + 1)).astype(BF16) for i in range(len(items))]
        intra = [_dot(a[i].astype(BF16), v[i]) for i in range(len(items))]
        ut = [_dot_tn(v[i], kb[i]) for i in range(len(items))]
        for hd in range(REC_HEADS):
            st = st_ref[hd]
            for cc in range(HGRN_UNROLL):
                i = cc * REC_HEADS + hd
                o = intra[i] + _dot_nt(qb[i], st.astype(BF16))
                st = st * ex(i, N_LEVELS, last_row=True) + ut[i]
                ms = jnp.mean(o * o, axis=-1, keepdims=True)
                gate = blk(g_ref, items[i])[...].astype(F32)
                y = o * lax.rsqrt(ms + EPS) * og_ref[:, hd * REC_DV:(hd + 1) * REC_DV] * gate
                blk(o_ref, items[i])[...] = y.astype(BF16)
            st_ref[hd] = st
        return carry

    lax.fori_loop(0, n_chunks // HGRN_UNROLL, chunks, 0)


def _hgrn(q, f, v, g, out_g, *, ts):
    bsz, s, _ = q.shape
    ww_np, am_np = _hgrn_tables()
    ww = jnp.asarray(ww_np, BF16)
    am = jnp.asarray(am_np, F32)
    row = pl.BlockSpec((None, ts, REC_W), lambda b, i: (b, i, 0))
    return pl.pallas_call(
        functools.partial(_hgrn_kernel, n_chunks=ts // CHUNK),
        grid=(bsz, s // ts),
        in_specs=[row, row, row, row, _const_spec((1, REC_W)),
                  _const_spec(ww.shape), _const_spec(am.shape)],
        out_specs=row,
        out_shape=jax.ShapeDtypeStruct((bsz, s, REC_W), BF16),
        scratch_shapes=[pltpu.VMEM((REC_HEADS, REC_DV, REC_DK), F32),
                        pltpu.VMEM((HGRN_UNROLL, (N_LEVELS + 2) * CHUNK, REC_HEADS * REC_DK), F32)],
        compiler_params=_params(("parallel", "arbitrary")),
        name="hgrn",
    )(q, f, v, g, out_g.reshape(1, REC_W), ww, am)


def _t5_bucket(rel):
    nb = NUM_BUCKETS // 2
    max_exact = nb // 2
    ret = jnp.where(rel > 0, nb, 0)
    n = jnp.abs(rel)
    nf = jnp.maximum(n, 1).astype(F32)
    large = max_exact + (jnp.log(nf / max_exact) / np.log(MAX_DISTANCE / max_exact)
                         * (nb - max_exact)).astype(I32)
    large = jnp.minimum(large, nb - 1)
    return ret + jnp.where(n < max_exact, n, large)


def _bias_slots(rel_bias):
    n_rel = 3 * QBLK
    rel = jnp.arange(n_rel, dtype=I32) - 2 * QBLK
    tab = rel_bias[_t5_bucket(rel)].T
    far = rel_bias[_t5_bucket(jnp.full((1,), -2 * QBLK - 1, I32))].T
    rev = tab[:, ::-1]
    skew = jnp.tile(rev, (1, 2 * QBLK))[:, :2 * QBLK * (n_rel - 1)].reshape(ATT_HEADS, 2 * QBLK, n_rel - 1)
    near = skew[:, :, 2 * QBLK - 1:]
    nb = ((near - far[:, :, None]) * LOG2E).reshape(ATT_HEADS, 2, QBLK, QBLK)
    z = jnp.zeros((ATT_HEADS, 1, QBLK, QBLK), F32)
    return jnp.concatenate([z, nb, z], axis=1)


def _attn_kernel(aq_ref, ak_ref, avt_ref, iq_ref, ik_ref, iwt_ref, bias_ref, og_ref, o_ref,
                 key_ref, mb_ref, thr_ref, acc_ref, m_ref, l_ref, al_ref, s_ref, mx_ref, p_ref, qbd_ref, ot_ref, tie_ref,
                 *, gs, gk, k_sel):
    bi = pl.program_id(1)
    n_s = bi // (gs // QBLK) + 1
    lane = lax.broadcasted_iota(I32, (QBLK, LANES), 1)

    def block_diag(xq):
        zero = jnp.zeros_like(xq)
        return jnp.concatenate([jnp.where(lane < ATT_DH, xq, zero), jnp.where(lane >= ATT_DH, xq, zero)], axis=0)

    def srows(g):
        return pl.ds(pl.multiple_of(g * gs, gs), gs)

    for j in range(N_PAIRS):
        qbd_ref[j] = block_diag(iq_ref[j])
    w = iwt_ref[...]

    def group_scores(g):
        sc = _dot_nt(ik_ref[srows(g), :], qbd_ref[...].reshape(N_PAIRS * 2 * QBLK, LANES))
        acc = jnp.zeros((gs, LANES), F32)
        for hd in range(IDX_HEADS):
            acc = acc + jnp.maximum(sc[:, hd * LANES:(hd + 1) * LANES], 0.0) * w[hd:hd + 1, :]
        return acc

    def stats(sc, c):
        s3 = sc.reshape(gs // 8, 8, LANES)
        return (jnp.minimum(c[0], jnp.min(jnp.where(s3 == -jnp.inf, jnp.inf, s3), axis=0)),
                jnp.maximum(c[1], jnp.max(s3, axis=0)),
                c[2] + jnp.sum(jnp.where(s3 >= 0.0, 1, 0), axis=0),
                c[3] + jnp.sum(jnp.where(s3 > 0.0, 1, 0), axis=0))

    def idx_body(g, carry):
        sc = group_scores(g)
        key_ref[srows(g), :] = sc
        return stats(sc, carry)

    def idx_pair(u, carry):
        return idx_body(2 * u + 1, idx_body(2 * u, carry))

    z8 = jnp.zeros((8, LANES), I32)
    st8 = lax.fori_loop(0, (n_s - 1) // 2, idx_pair,
                        (jnp.full((8, LANES), jnp.inf, F32), jnp.full((8, LANES), -jnp.inf, F32), z8, z8))
    st8 = lax.cond((n_s - 1) % 2 == 1, lambda c: idx_body(n_s - 2, c), lambda c: c, st8)

    last = srows(n_s - 1)
    spos = (n_s - 1) * gs + lax.broadcasted_iota(I32, (gs, LANES), 0)
    tpos = bi * QBLK + lax.broadcasted_iota(I32, (gs, LANES), 1)
    sc_last = jnp.where((spos // CHUNK) <= (tpos // CHUNK), group_scores(n_s - 1), -jnp.inf)
    key_ref[last, :] = sc_last
    mn8, mx8, ge8, gt8 = stats(sc_last, st8)

    thr_ref[...] = jnp.full(thr_ref.shape, F32_LOWEST, F32)
    tie_ref[0] = 0

    @pl.when((2 * bi + 2) * CHUNK > k_sel)
    def _():
        def count_ge(cand):
            def one(g):
                sc = key_ref[srows(g), :].reshape(gs // 8, 8, LANES)
                return jnp.sum(jnp.where(sc >= cand, 1, 0), axis=0)

            def two(u, acc):
                second = jnp.where(2 * u + 1 < n_s, 1, 0)
                return acc + one(2 * u) + one(jnp.minimum(2 * u + 1, n_s - 1)) * second

            acc = lax.fori_loop(0, (n_s + 1) // 2, two, jnp.zeros((8, LANES), I32))
            return jnp.sum(acc, axis=0, keepdims=True)

        c0_ge = jnp.sum(ge8, axis=0, keepdims=True)
        c0_gt = jnp.sum(gt8, axis=0, keepdims=True)
        mn = jnp.min(mn8, axis=0, keepdims=True)
        mx = jnp.max(mx8, axis=0, keepdims=True)
        above = mx + (jnp.abs(mx) * 2.0 ** -20 + 1e-30)
        n_adm = (2 * bi + 1 + lax.broadcasted_iota(I32, (1, LANES), 1) // CHUNK) * CHUNK
        pos = c0_gt >= k_sel
        non_neg = c0_ge >= k_sel
        lo = jnp.where(non_neg, 0.0, mn)
        cnt_lo = jnp.where(non_neg, c0_ge, n_adm)
        hi = jnp.where(pos, above, 0.0)

        def midpoint(lo, hi):
            return 0.5 * lo + 0.5 * hi

        def unsettled(lo, hi, cnt_lo):
            mid = midpoint(lo, hi)
            return jnp.where(cnt_lo > k_sel, jnp.where(mid > lo, jnp.where(mid < hi, 1.0, 0.0), 0.0), 0.0)

        def cond(st):
            return jnp.logical_and(st[3] > 0.5, st[4] < BISECT_CAP)

        def halve(lo, hi, cnt_lo):
            act = unsettled(lo, hi, cnt_lo) > 0.5
            mid = midpoint(lo, hi)
            c = count_ge(mid)
            up = jnp.logical_and(act, c >= k_sel)
            dn = jnp.logical_and(act, c < k_sel)
            return jnp.where(up, mid, lo), jnp.where(dn, mid, hi), jnp.where(up, c, cnt_lo)

        def body(st):
            lo, hi, cnt_lo = halve(*halve(*st[:3]))
            return lo, hi, cnt_lo, jnp.max(unsettled(lo, hi, cnt_lo)), st[4] + 2

        lo, hi, cnt_lo = lax.fori_loop(0, BISECT_FIRST, lambda _, s3: halve(*s3), (lo, hi, cnt_lo))
        st = lax.while_loop(cond, body,
                            (lo, hi, cnt_lo, jnp.max(unsettled(lo, hi, cnt_lo)), jnp.int32(BISECT_FIRST)))
        thr_ref[...] = jnp.broadcast_to(st[0], thr_ref.shape)
        tie_ref[0] = (jnp.max(jnp.where(st[2] > k_sel, 1.0, 0.0)) > 0.5).astype(I32)

    thr = thr_ref[0:1, :]

    @pl.when(tie_ref[0] == 0)
    def _():
        def mb_body(g, carry):
            mb_ref[srows(g), :] = jnp.where(key_ref[srows(g), :] >= thr, 0.0, NEG).astype(BF16)
            return carry

        lax.fori_loop(0, n_s, mb_body, 0)

    @pl.when(tie_ref[0] != 0)
    def _():
        def count_gt(g, acc):
            sc = key_ref[srows(g), :].reshape(gs // 8, 8, LANES)
            return acc + jnp.sum(jnp.where(sc > thr, 1, 0), axis=0)

        n_gt = jnp.sum(lax.fori_loop(0, n_s, count_gt, jnp.zeros((8, LANES), I32)), axis=0, keepdims=True)
        room = (k_sel - n_gt).astype(F32)
        r = lax.broadcasted_iota(I32, (QBLK, QBLK), 0)
        c = lax.broadcasted_iota(I32, (QBLK, QBLK), 1)
        tri = jnp.where(c <= r, 1.0, 0.0).astype(BF16)

        def mb_body(g, seen):
            for t in range(gs // QBLK):
                rs = pl.ds(pl.multiple_of(g * gs + t * QBLK, QBLK), QBLK)
                sc = key_ref[rs, :]
                eq = jnp.where(sc == thr, 1.0, 0.0)
                rank = _dot(tri, eq.astype(BF16)) + seen
                keep = jnp.where(sc > thr, 1.0, jnp.where(rank <= room, eq, 0.0))
                mb_ref[rs, :] = jnp.where(keep > 0.5, 0.0, NEG).astype(BF16)
                seen = seen + jnp.sum(eq, axis=0, keepdims=True)
            return seen

        lax.fori_loop(0, n_s, mb_body, jnp.zeros((1, LANES), F32))

    spg = gk // QBLK
    n_g = bi // spg + 1
    for j in range(N_PAIRS):
        qbd_ref[j] = block_diag(aq_ref[j])
    m_ref[...] = jnp.full(m_ref.shape, NEG, F32)
    l_ref[...] = jnp.zeros_like(l_ref)
    acc_ref[...] = jnp.zeros_like(acc_ref)
    al_ref[...] = jnp.ones_like(al_ref)
    p_ref[1] = jnp.zeros(p_ref.shape[1:], BF16)

    def rows(g):
        return pl.ds(pl.multiple_of(g * gk, gk), gk)

    def scores(t, slot):
        mb = mb_ref[rows(t), :]
        mb2 = jnp.concatenate([mb, mb], axis=1)
        for j in range(N_PAIRS):
            s = _dot_nt(ak_ref[j, rows(t), :], qbd_ref[j]).astype(BF16) + mb2
            s_ref[slot, j] = s
            mx_ref[slot, j] = jnp.max(s, axis=0, keepdims=True).astype(F32)

    def pv(t, slot):
        return [_dot(avt_ref[t, j * LANES:(j + 1) * LANES, :], p_ref[slot, j]) for j in range(N_PAIRS)]

    def step(g, cur):
        prv = 1 - cur

        @pl.when(g >= n_g - 2)
        def _():
            for j in range(N_PAIRS):
                parts = []
                for st in range(spg):
                    slot = jnp.clip(g * spg + st - bi + 2, 0, 3)
                    parts.append(jnp.concatenate([bias_ref[2 * j, slot], bias_ref[2 * j + 1, slot]], axis=1))
                s = (s_ref[cur, j].astype(F32) + jnp.concatenate(parts, axis=0)).astype(BF16)
                s_ref[cur, j] = s
                mx_ref[cur, j] = jnp.max(s, axis=0, keepdims=True).astype(F32)

        al_old = [al_ref[j] for j in range(N_PAIRS)]
        for j in range(N_PAIRS):
            m_old = m_ref[j]
            m_new = jnp.maximum(m_old, mx_ref[cur, j])
            p = jnp.exp2(s_ref[cur, j] - m_new.astype(BF16))
            alpha = jnp.exp2(m_old - m_new)
            al_ref[j] = alpha
            p4 = p.reshape(4, gk // 4, 2 * LANES)
            psum = ((p4[0] + p4[1]) + (p4[2] + p4[3])).astype(F32)
            l_ref[j] = l_ref[j] * alpha + jnp.sum(psum, axis=0, keepdims=True)
            m_ref[j] = m_new
            p_ref[cur, j] = p
        o_prev = pv(jnp.maximum(g - 1, 0), prv)
        scores(jnp.minimum(g + 1, n_g - 1), prv)
        for j in range(N_PAIRS):
            acc_ref[j] = acc_ref[j] * al_old[j] + o_prev[j]

    scores(0, 0)

    def pipe_body(u, c):
        step(2 * u, 0)

        @pl.when(2 * u + 1 < n_g)
        def _():
            step(2 * u + 1, 1)
        return c

    lax.fori_loop(0, (n_g + 1) // 2, pipe_body, 0)
    o_last = pv(n_g - 1, (n_g - 1) % 2)

    for j in range(N_PAIRS):
        acc = acc_ref[j] * al_ref[j] + o_last[j]
        l = l_ref[j]
        o0 = acc[0:ATT_DH, 0:LANES] / l[:, 0:LANES]
        o1 = acc[ATT_DH:2 * ATT_DH, LANES:2 * LANES] / l[:, LANES:2 * LANES]
        ot_ref[j * LANES:(j + 1) * LANES, :] = jnp.concatenate([o0, o1], axis=0)

    ot = ot_ref[...]
    og = og_ref[...]
    outs = []
    for hd in range(ATT_HEADS):
        oh = ot[hd * ATT_DH:(hd + 1) * ATT_DH, :]
        ms = jnp.mean(oh * oh, axis=0, keepdims=True)
        outs.append(oh * lax.rsqrt(ms + EPS) * og[hd * ATT_DH:(hd + 1) * ATT_DH, :])
    o_ref[...] = jnp.concatenate(outs, axis=0).T.astype(BF16)


def _attention(aq, ak, avt, iq, ik, iwt, rel_bias, out_g, *, gk):
    bsz, _, s, _ = aq.shape
    k_sel = min(TOPK_MAX, s // 4)
    bias = _bias_slots(rel_bias)
    qpair = pl.BlockSpec((None, N_PAIRS, QBLK, LANES), lambda b, i: (b, 0, i, 0))
    full = lambda shp: pl.BlockSpec((None,) + shp, lambda b, i: (b,) + (0,) * len(shp),
                                    pipeline_mode=pl.Buffered(1))
    return pl.pallas_call(
        functools.partial(_attn_kernel, gs=_tile(s, 512), gk=gk, k_sel=k_sel),
        grid=(bsz, s // QBLK),
        in_specs=[qpair, full((N_PAIRS, s, LANES)), full((s // gk, ATT_W, gk)), qpair,
                  full((s, LANES)), pl.BlockSpec((None, IDX_HEADS, QBLK), lambda b, i: (b, 0, i)),
                  _const_spec(bias.shape), _const_spec((ATT_W, 1))],
        out_specs=pl.BlockSpec((None, QBLK, ATT_W), lambda b, i: (b, i, 0)),
        out_shape=jax.ShapeDtypeStruct((bsz, s, ATT_W), BF16),
        scratch_shapes=[pltpu.VMEM((s, LANES), F32),
                        pltpu.VMEM((s, LANES), BF16),
                        pltpu.VMEM((8, LANES), F32),
                        pltpu.VMEM((N_PAIRS, 2 * ATT_DH, 2 * LANES), F32),
                        pltpu.VMEM((N_PAIRS, 1, 2 * LANES), F32),
                        pltpu.VMEM((N_PAIRS, 1, 2 * LANES), F32),
                        pltpu.VMEM((N_PAIRS, 1, 2 * LANES), F32),
                        pltpu.VMEM((2, N_PAIRS, gk, 2 * LANES), BF16),
                        pltpu.VMEM((2, N_PAIRS, 1, 2 * LANES), F32),
                        pltpu.VMEM((2, N_PAIRS, gk, 2 * LANES), BF16),
                        pltpu.VMEM((N_PAIRS, 2 * QBLK, LANES), BF16),
                        pltpu.VMEM((ATT_W, LANES), F32),
                        pltpu.SMEM((1,), I32)],
        compiler_params=_params(("parallel", "arbitrary")),
        name="attn",
    )(aq, ak, avt, iq, ik, iwt, bias, out_g.reshape(ATT_W, 1))


def _outproj_kernel(rec_ref, att_ref, x_ref, g1_ref, sc_ref, sh_ref, ng_ref, wo_ref, wr_ref, br_ref,
                    x1_ref, h2_ref, comb_ref):
    mix = _dot(rec_ref[...], wo_ref[0:REC_W, :]) + _dot(att_ref[...], wo_ref[REC_W:REC_W + ATT_W, :])
    x1 = x_ref[...] + g1_ref[...] * mix
    x1_ref[...] = x1
    ms = jnp.mean(x1 * x1, axis=-1, keepdims=True)
    h2 = (x1 * lax.rsqrt(ms + EPS) * ng_ref[...] * (1.0 + sc_ref[...]) + sh_ref[...]).astype(BF16)
    h2_ref[...] = h2

    lg = _dot_nt(wr_ref[...], h2) + br_ref[...]
    row = lax.broadcasted_iota(I32, lg.shape, 0)
    big = jnp.int32(2 * LANES)
    is_g = row < N_GROUPS
    gl = jnp.where(is_g, lg, -jnp.inf)
    gmax = jnp.max(gl, axis=0, keepdims=True)
    gate = 1.0 / jnp.sum(jnp.where(is_g, jnp.exp(lg - gmax), 0.0), axis=0, keepdims=True)
    gtop = jnp.min(jnp.where(gl == gmax, row, big), axis=0, keepdims=True)
    e_lo = ROUTE_OFF + EXPERTS_PER_GROUP * gtop
    el = jnp.where((row >= e_lo) & (row < e_lo + EXPERTS_PER_GROUP), lg, -jnp.inf)
    v1 = jnp.max(el, axis=0, keepdims=True)
    i1 = jnp.min(jnp.where(el == v1, row, big), axis=0, keepdims=True)
    el2 = jnp.where(row == i1, -jnp.inf, el)
    v2 = jnp.max(el2, axis=0, keepdims=True)
    i2 = jnp.min(jnp.where(el2 == v2, row, big), axis=0, keepdims=True)
    e2 = jnp.exp(v2 - v1)
    w1 = gate / (1.0 + e2)
    w2 = gate * e2 / (1.0 + e2)
    comb = jnp.where(row == i1, w1, 0.0) + jnp.where(row == i2, w2, 0.0)
    comb = jnp.where(row == 0, gtop.astype(F32), comb)
    pad = jnp.zeros((LANES - ROUTE_ROWS, comb.shape[1]), F32)
    comb_ref[...] = jnp.concatenate([comb, pad], axis=0).T


def _outproj(rec, att, x, g1, sc2, sh2, norm_g, w_out, w_rg, b_rg, w_re, b_re, *, tm):
    bsz, s, d = x.shape
    wr = jnp.zeros((ROUTE_ROWS, d), F32).at[:N_GROUPS].set(w_rg.T).at[ROUTE_OFF:ROUTE_OFF + N_EXPERTS].set(w_re.T)
    br = jnp.zeros((ROUTE_ROWS, 1), F32).at[:N_GROUPS, 0].set(b_rg).at[ROUTE_OFF:ROUTE_OFF + N_EXPERTS, 0].set(b_re)
    row = lambda w: pl.BlockSpec((None, tm, w), lambda b, i: (b, i, 0))
    vec = pl.BlockSpec((None, 1, d), lambda b, i: (b, 0, 0))
    return pl.pallas_call(
        _outproj_kernel,
        grid=(bsz, s // tm),
        in_specs=[row(REC_W), row(ATT_W), row(d), vec, vec, vec, _const_spec((1, d)),
                  _const_spec((REC_W + ATT_W, d)), _const_spec((ROUTE_ROWS, d)), _const_spec((ROUTE_ROWS, 1))],
        out_specs=(row(d), row(d), row(LANES)),
        out_shape=(jax.ShapeDtypeStruct((bsz, s, d), F32),
                   jax.ShapeDtypeStruct((bsz, s, d), BF16),
                   jax.ShapeDtypeStruct((bsz, s, LANES), F32)),
        compiler_params=_params(("parallel", "parallel")),
        name="outproj",
    )(rec, att, x, g1.reshape(bsz, 1, d), sc2.reshape(bsz, 1, d), sh2.reshape(bsz, 1, d),
      norm_g.reshape(1, d), w_out.astype(BF16), wr.astype(BF16), br)


MOE_ALIGN = 16
MOE_LANE_SHIFT = 32
MOE_CHUNK = 288
MOE_RB = 256


def _moe_kernel(h_ref, comb_ref, x1_ref, g2_ref, w1_ref, w3_ref, w2_ref, o_ref,
                p_ref, pt_ref, hs_ref, cs_ref, ys_ref, seg_ref, *, mc):
    e = pl.program_id(2)
    tm = h_ref.shape[0]
    npad = hs_ref.shape[0]

    @pl.when(e == 0)
    def _():
        comb = comb_ref[...]
        lane = lax.broadcasted_iota(I32, (tm, LANES), 1)
        gid = comb[:, 0:1].astype(I32)
        oh = jnp.where(lane == gid, 1.0, 0.0)
        ohb = oh.astype(BF16)
        pre = []
        for rb in range(tm // MOE_RB):
            r = rb * MOE_RB + lax.broadcasted_iota(I32, (MOE_RB, tm), 0)
            c = lax.broadcasted_iota(I32, (MOE_RB, tm), 1)
            pre.append(_dot(jnp.where(c < r, 1.0, 0.0).astype(BF16), ohb))
        prefix = jnp.concatenate(pre, axis=0)
        cnt = jnp.sum(oh, axis=0, keepdims=True).astype(I32)
        cnt_al = ((cnt + (MOE_ALIGN - 1)) // MOE_ALIGN) * MOE_ALIGN
        base = jnp.sum(jnp.where(lane < gid, cnt_al.astype(F32), 0.0), axis=-1, keepdims=True)
        rank = jnp.sum(prefix * oh, axis=-1, keepdims=True)
        pos = (base + rank).astype(I32)
        pos_row = jnp.broadcast_to(pos.astype(F32), (tm, LANES)).T[0:1, :].astype(I32)
        for rb in range(tm // MOE_RB):
            sl = slice(rb * MOE_RB, (rb + 1) * MOE_RB)
            coln = lax.broadcasted_iota(I32, (MOE_RB, npad), 1)
            pt_ref[sl, :] = jnp.where(coln == pos[sl], 1.0, 0.0).astype(BF16)
        for rb in range(npad // LANES):
            sl = slice(rb * LANES, (rb + 1) * LANES)
            rown = rb * LANES + lax.broadcasted_iota(I32, (LANES, tm), 0)
            p_ref[sl, :] = jnp.where(rown == pos_row, 1.0, 0.0).astype(BF16)
        c_hi = comb.astype(BF16)
        c_mid, c_lo = _split_bf16(comb - c_hi.astype(F32))
        packed = (c_hi.astype(F32) + pltpu.roll(c_mid.astype(F32), MOE_LANE_SHIFT, axis=1)
                  + pltpu.roll(c_lo.astype(F32), 2 * MOE_LANE_SHIFT, axis=1)).astype(BF16)
        srt = _dot(p_ref[...], jnp.concatenate([h_ref[...], packed], axis=1))
        d = h_ref.shape[1]
        hs_ref[...] = srt[:, :d].astype(BF16)
        cp = srt[:, d:]
        cs_ref[...] = (cp + pltpu.roll(cp, LANES - MOE_LANE_SHIFT, axis=1)
                       + pltpu.roll(cp, LANES - 2 * MOE_LANE_SHIFT, axis=1))
        ys_ref[...] = jnp.zeros_like(ys_ref)
        start = jnp.int32(0)
        for g in range(N_GROUPS):
            seg_ref[g] = start
            seg_ref[N_GROUPS + g] = cnt[0, g]
            start = start + cnt_al[0, g]

    grp = e // EXPERTS_PER_GROUP
    start = seg_ref[grp]
    n_rows = seg_ref[N_GROUPS + grp]

    def chunk(ci, carry):
        rs = pl.ds(pl.multiple_of(start + ci * mc, MOE_ALIGN), mc)
        hb = hs_ref[rs, :]
        cw = cs_ref[rs, :]
        lane = lax.broadcasted_iota(I32, cw.shape, 1)
        col = jnp.sum(jnp.where(lane == e + ROUTE_OFF, cw, 0.0), axis=-1, keepdims=True)
        he = _silu(_dot(hb, w1_ref[...])) * _dot(hb, w3_ref[...]) * col
        ys_ref[rs, :] += _dot(he.astype(BF16), w2_ref[...])
        return carry

    lax.fori_loop(0, (n_rows + mc - 1) // mc, chunk, 0)

    @pl.when(e == N_EXPERTS - 1)
    def _():
        o_ref[...] = x1_ref[...] + g2_ref[...] * _dot(pt_ref[...], ys_ref[...].astype(BF16))


def _moe(h2, comb, x1, g2, w1, w3, w2, *, tm):
    bsz, s, d = x1.shape
    mc = MOE_CHUNK
    npad = -(-(tm + N_GROUPS * MOE_ALIGN + mc) // LANES) * LANES
    row = lambda w, **kw: pl.BlockSpec((None, tm, w), lambda b, i, e: (b, i, 0), **kw)
    once = dict(pipeline_mode=pl.Buffered(1))
    return pl.pallas_call(
        functools.partial(_moe_kernel, mc=mc),
        grid=(bsz, s // tm, N_EXPERTS),
        in_specs=[row(d, **once), row(LANES), row(d, **once),
                  pl.BlockSpec((None, 1, d), lambda b, i, e: (b, 0, 0)),
                  pl.BlockSpec((None, d, D_EXPERT), lambda b, i, e: (e, 0, 0)),
                  pl.BlockSpec((None, d, D_EXPERT), lambda b, i, e: (e, 0, 0)),
                  pl.BlockSpec((None, D_EXPERT, d), lambda b, i, e: (e, 0, 0))],
        out_specs=row(d),
        out_shape=jax.ShapeDtypeStruct((bsz, s, d), F32),
        scratch_shapes=[pltpu.VMEM((npad, tm), BF16),
                        pltpu.VMEM((tm, npad), BF16),
                        pltpu.VMEM((npad, d), BF16),
                        pltpu.VMEM((npad, LANES), F32),
                        pltpu.VMEM((npad, d), F32),
                        pltpu.SMEM((2 * N_GROUPS,), I32)],
        compiler_params=_params(("parallel", "parallel", "arbitrary")),
        name="moe",
    )(h2, comb, x1, g2.reshape(bsz, 1, d), w1.astype(BF16), w3.astype(BF16), w2.astype(BF16))


def _tile(s, pref):
    t = min(s, pref)
    assert s % t == 0
    return t


def kernel(x, c, w_ada, b_ada, norm1_g, norm2_g, w_in, lb_logits, rec_out_g, q_norm_g, k_norm_g,
           idx_k_norm_g, idx_k_norm_b, attn_out_g, rel_bias, w_out, w_rg, b_rg, w_re, b_re, w1, w3, w2):
    bsz, s, d = x.shape
    depth = w_ada.shape[0]
    gk = _tile(s, 512)
    for l in range(depth):
        mod = _adaln(c, w_ada, b_ada[l], l)
        sh1, sc1, g1, sh2, sc2, g2 = jnp.split(mod, 6, axis=-1)
        q, f, v, g, aq, ak, avt, iq, ik, iwt = _inproj(
            x, sc1, sh1, norm1_g[l], w_in, lb_logits, q_norm_g[l], k_norm_g[l],
            idx_k_norm_g[l], idx_k_norm_b[l], layer=l, gt=gk, tm=_tile(s, 512))
        rec = _hgrn(q, f, v, g, rec_out_g[l], ts=_tile(s, 1024))
        att = _attention(aq, ak, avt, iq, ik, iwt, rel_bias, attn_out_g[l], gk=gk)
        x1, h2, comb = _outproj(rec, att, x, g1, sc2, sh2, norm2_g[l], w_out[l],
                                w_rg[l], b_rg[l], w_re[l], b_re[l], tm=_tile(s, 512))
        x = _moe(h2, comb, x1, g2, w1[l], w3[l], w2[l], tm=_tile(s, 1024))
    return x
```
